```python
import jax, jax.numpy as jnp
from jax import lax
import numpy as np

D_MODEL = 1024
BATCH = 8
SEQ = 8192
DEPTH = 4

N_MIXERS = 3
N_META = 16
D_FF = 2816
SC_WIDTH = 3
CONF_WIDTH = 31
N_HEADS = 16
HEAD_DIM = D_MODEL // N_HEADS
BLOCK = 128
EPS = 1e-6
MASK_VALUE = -1e30
N_A = (DEPTH + 2) // 3
N_B = (DEPTH + 1) // 3
N_C = DEPTH // 3

kernel_name = "hybrid_shortconv_conformer_fox_macaron"


def rms_norm(x, g):
    xf = x.astype(jnp.float32)
    y = xf * lax.rsqrt(jnp.mean(xf * xf, axis=-1, keepdims=True) + EPS)
    return (y * g.astype(jnp.float32)).astype(x.dtype)


def layer_norm(x, g, b):
    xf = x.astype(jnp.float32)
    mu = jnp.mean(xf, axis=-1, keepdims=True)
    xc = xf - mu
    var = jnp.mean(xc * xc, axis=-1, keepdims=True)
    y = xc * lax.rsqrt(var + EPS) * g.astype(jnp.float32) + b.astype(jnp.float32)
    return y.astype(x.dtype)


def causal_depthwise_conv(x, w):
    k = w.shape[0]
    return lax.conv_general_dilated(
        x, w[:, None, :].astype(x.dtype), window_strides=(1,), padding=[(k - 1, 0)],
        dimension_numbers=("NWC", "WIO", "NWC"), feature_group_count=x.shape[-1])


def swiglu(h, w_gate, w_up, w_down):
    return (jax.nn.silu(h @ w_gate) * (h @ w_up)) @ w_down


def short_conv_mixer(h, w_in, conv_w, w_out):
    b_gate, c_gate, v = jnp.split(h @ w_in, 3, axis=-1)
    y = b_gate * causal_depthwise_conv(c_gate * v, conv_w)
    return y @ w_out


def conformer_conv_mixer(h, w_in, conv_w, conv_b, ln_g, ln_b, w_out):
    a, g = jnp.split(h @ w_in, 2, axis=-1)
    u = a * jax.nn.sigmoid(g)
    u = causal_depthwise_conv(u, conv_w) + conv_b
    u = jax.nn.silu(layer_norm(u, ln_g, ln_b))
    return u @ w_out


def forgetting_attention(h, w_in, b_f, q_g, k_g, w_out):
    bsz, L, _ = h.shape
    proj = h @ w_in
    q = proj[..., :D_MODEL].reshape(bsz, L, N_HEADS, HEAD_DIM)
    k = proj[..., D_MODEL:2 * D_MODEL].reshape(bsz, L, N_HEADS, HEAD_DIM)
    v = proj[..., 2 * D_MODEL:3 * D_MODEL].reshape(bsz, L, N_HEADS, HEAD_DIM)
    f_logit = proj[..., 3 * D_MODEL:] + b_f
    q = rms_norm(q, q_g)
    k = rms_norm(k, k_g)
    log_f = jax.nn.log_sigmoid(f_logit.astype(jnp.float32))
    cum = jnp.cumsum(log_f, axis=1)
    pad = (-L) % BLOCK
    lp = L + pad
    n_blocks = lp // BLOCK
    pad4 = ((0, 0), (pad, 0), (0, 0), (0, 0))
    q = jnp.pad(q, pad4).transpose(0, 2, 1, 3)
    k = jnp.pad(k, pad4).transpose(0, 2, 1, 3)
    v = jnp.pad(v, pad4).transpose(0, 2, 1, 3)
    cum = jnp.pad(cum, ((0, 0), (pad, 0), (0, 0))).transpose(0, 2, 1)
    kpos = jnp.arange(lp)
    scale = HEAD_DIM ** -0.5

    def one_block(i):
        start = i * BLOCK
        qb = lax.dynamic_slice_in_dim(q, start, BLOCK, axis=2)
        cq = lax.dynamic_slice_in_dim(cum, start, BLOCK, axis=2)
        s = jnp.einsum("bhqd,bhkd->bhqk", qb, k, preferred_element_type=jnp.float32) * scale
        s = s + cq[..., :, None] - cum[..., None, :]
        qpos = start + jnp.arange(BLOCK)
        mask = (kpos[None, :] <= qpos[:, None]) & (kpos[None, :] >= pad)
        s = jnp.where(mask, s, MASK_VALUE)
        p = jax.nn.softmax(s, axis=-1).astype(v.dtype)
        return jnp.einsum("bhqk,bhkd->bhqd", p, v)

    o = lax.map(one_block, jnp.arange(n_blocks))
    o = o.transpose(1, 0, 3, 2, 4).reshape(bsz, lp, D_MODEL)[:, pad:]
    return o @ w_out


def _fwd_setup_inputs(seed: int = 0) -> dict:
    key = jax.random.key(seed)
    ks = jax.random.split(key, 24)
    D, F, H = D_MODEL, D_FF, N_HEADS
    nrm = lambda k, shape, fan: jax.random.normal(k, shape, jnp.float32) * (fan ** -0.5)
    gain = lambda k, shape: 1.0 + 0.05 * jax.random.normal(k, shape, jnp.float32)
    small = lambda k, shape: 0.02 * jax.random.normal(k, shape, jnp.float32)
    return {
        "x": jax.random.normal(ks[0], (BATCH, SEQ, D), jnp.float32),
        "meta": jax.random.normal(ks[1], (N_META, D), jnp.float32),
        "ffn_norm": gain(ks[2], (DEPTH, 2, D)),
        "ffn_w_gate": nrm(ks[3], (DEPTH, 2, D, F), D),
        "ffn_w_up": nrm(ks[4], (DEPTH, 2, D, F), D),
        "ffn_w_down": nrm(ks[5], (DEPTH, 2, F, D), F),
        "mix_norm": gain(ks[6], (DEPTH, D)),
        "a_w_in": nrm(ks[7], (N_A, D, 3 * D), D),
        "a_conv": nrm(ks[8], (N_A, SC_WIDTH, D), SC_WIDTH),
        "a_w_out": nrm(ks[9], (N_A, D, D), D),
        "b_w_in": nrm(ks[10], (N_B, D, 2 * D), D),
        "b_conv": nrm(ks[11], (N_B, CONF_WIDTH, D), CONF_WIDTH),
        "b_conv_bias": small(ks[12], (N_B, D)),
        "b_ln_g": gain(ks[13], (N_B, D)),
        "b_ln_b": small(ks[14], (N_B, D)),
        "b_w_out": nrm(ks[15], (N_B, D, D), D),
        "c_w_in": nrm(ks[16], (N_C, D, 3 * D + H), D),
        "c_b_f": jax.random.uniform(ks[17], (N_C, H), jnp.float32, 1.0, 4.0),
        "c_q_norm": gain(ks[18], (N_C, HEAD_DIM)),
        "c_k_norm": gain(ks[19], (N_C, HEAD_DIM)),
        "c_w_out": nrm(ks[20], (N_C, D, D), D),
    }


def _fwd_reference(x, meta, ffn_norm, ffn_w_gate, ffn_w_up, ffn_w_down, mix_norm,
              a_w_in, a_conv, a_w_out,
              b_w_in, b_conv, b_conv_bias, b_ln_g, b_ln_b, b_w_out,
              c_w_in, c_b_f, c_q_norm, c_k_norm, c_w_out):
    bsz = x.shape[0]
    meta_b = jnp.broadcast_to(meta[None].astype(x.dtype), (bsz, N_META, D_MODEL))
    h = jnp.concatenate([meta_b, x], axis=1)
    for i in range(DEPTH):
        m, j = i % N_MIXERS, i // N_MIXERS
        h = h + 0.5 * swiglu(rms_norm(h, ffn_norm[i, 0]), ffn_w_gate[i, 0], ffn_w_up[i, 0], ffn_w_down[i, 0])
        u = rms_norm(h, mix_norm[i])
        if m == 0:
            mix = short_conv_mixer(u, a_w_in[j], a_conv[j], a_w_out[j])
        elif m == 1:
            mix = conformer_conv_mixer(u, b_w_in[j], b_conv[j], b_conv_bias[j], b_ln_g[j], b_ln_b[j], b_w_out[j])
        else:
            mix = forgetting_attention(u, c_w_in[j], c_b_f[j], c_q_norm[j], c_k_norm[j], c_w_out[j])
        h = h + mix
        h = h + 0.5 * swiglu(rms_norm(h, ffn_norm[i, 1]), ffn_w_gate[i, 1], ffn_w_up[i, 1], ffn_w_down[i, 1])
    return h[:, N_META:]


import jax as _jax
import jax.numpy as _jnp

TWIN_FORMAT = 'train_step'
FWD_PARAMS = ['x', 'meta', 'ffn_norm', 'ffn_w_gate', 'ffn_w_up', 'ffn_w_down', 'mix_norm', 'a_w_in', 'a_conv', 'a_w_out', 'b_w_in', 'b_conv', 'b_conv_bias', 'b_ln_g', 'b_ln_b', 'b_w_out', 'c_w_in', 'c_b_f', 'c_q_norm', 'c_k_norm', 'c_w_out']
TWIN_WEIGHTS = ['meta', 'ffn_norm', 'ffn_w_gate', 'ffn_w_up', 'ffn_w_down', 'mix_norm', 'a_w_in', 'a_conv', 'a_w_out', 'b_w_in', 'b_conv', 'b_conv_bias', 'b_ln_g', 'b_ln_b', 'b_w_out', 'c_w_in', 'c_b_f', 'c_q_norm', 'c_k_norm', 'c_w_out']
TWIN_DIFF_INPUT = 'x'
TWIN_INPUTS = ['x', 'meta', 'ffn_norm', 'ffn_w_gate', 'ffn_w_up', 'ffn_w_down', 'mix_norm', 'a_w_in', 'a_conv', 'a_w_out', 'b_w_in', 'b_conv', 'b_conv_bias', 'b_ln_g', 'b_ln_b', 'b_w_out', 'c_w_in', 'c_b_f', 'c_q_norm', 'c_k_norm', 'c_w_out', 'loss_target', 'm_meta', 'm_ffn_norm', 'm_ffn_w_gate', 'm_ffn_w_up', 'm_ffn_w_down', 'm_mix_norm', 'm_a_w_in', 'm_a_conv', 'm_a_w_out', 'm_b_w_in', 'm_b_conv', 'm_b_conv_bias', 'm_b_ln_g', 'm_b_ln_b', 'm_b_w_out', 'm_c_w_in', 'm_c_b_f', 'm_c_q_norm', 'm_c_k_norm', 'm_c_w_out', 'v_meta', 'v_ffn_norm', 'v_ffn_w_gate', 'v_ffn_w_up', 'v_ffn_w_down', 'v_mix_norm', 'v_a_w_in', 'v_a_conv', 'v_a_w_out', 'v_b_w_in', 'v_b_conv', 'v_b_conv_bias', 'v_b_ln_g', 'v_b_ln_b', 'v_b_w_out', 'v_c_w_in', 'v_c_b_f', 'v_c_q_norm', 'v_c_k_norm', 'v_c_w_out']
TWIN_OUTPUTS = ['loss', 'grad_x', 'grad_meta', 'grad_ffn_norm', 'grad_ffn_w_gate', 'grad_ffn_w_up', 'grad_ffn_w_down', 'grad_mix_norm', 'grad_a_w_in', 'grad_a_conv', 'grad_a_w_out', 'grad_b_w_in', 'grad_b_conv', 'grad_b_conv_bias', 'grad_b_ln_g', 'grad_b_ln_b', 'grad_b_w_out', 'grad_c_w_in', 'grad_c_b_f', 'grad_c_q_norm', 'grad_c_k_norm', 'grad_c_w_out', 'delta_meta', 'delta_ffn_norm', 'delta_ffn_w_gate', 'delta_ffn_w_up', 'delta_ffn_w_down', 'delta_mix_norm', 'delta_a_w_in', 'delta_a_conv', 'delta_a_w_out', 'delta_b_w_in', 'delta_b_conv', 'delta_b_conv_bias', 'delta_b_ln_g', 'delta_b_ln_b', 'delta_b_w_out', 'delta_c_w_in', 'delta_c_b_f', 'delta_c_q_norm', 'delta_c_k_norm', 'delta_c_w_out', 'new_m_meta', 'new_m_ffn_norm', 'new_m_ffn_w_gate', 'new_m_ffn_w_up', 'new_m_ffn_w_down', 'new_m_mix_norm', 'new_m_a_w_in', 'new_m_a_conv', 'new_m_a_w_out', 'new_m_b_w_in', 'new_m_b_conv', 'new_m_b_conv_bias', 'new_m_b_ln_g', 'new_m_b_ln_b', 'new_m_b_w_out', 'new_m_c_w_in', 'new_m_c_b_f', 'new_m_c_q_norm', 'new_m_c_k_norm', 'new_m_c_w_out', 'new_v_meta', 'new_v_ffn_norm', 'new_v_ffn_w_gate', 'new_v_ffn_w_up', 'new_v_ffn_w_down', 'new_v_mix_norm', 'new_v_a_w_in', 'new_v_a_conv', 'new_v_a_w_out', 'new_v_b_w_in', 'new_v_b_conv', 'new_v_b_conv_bias', 'new_v_b_ln_g', 'new_v_b_ln_b', 'new_v_b_w_out', 'new_v_c_w_in', 'new_v_c_b_f', 'new_v_c_q_norm', 'new_v_c_k_norm', 'new_v_c_w_out']
TWIN_LEAF_KINDS = {'loss': 'loss', 'grad_x': 'grad_x', 'grad_meta': 'grad_w', 'grad_ffn_norm': 'grad_w', 'grad_ffn_w_gate': 'grad_w', 'grad_ffn_w_up': 'grad_w', 'grad_ffn_w_down': 'grad_w', 'grad_mix_norm': 'grad_w', 'grad_a_w_in': 'grad_w', 'grad_a_conv': 'grad_w', 'grad_a_w_out': 'grad_w', 'grad_b_w_in': 'grad_w', 'grad_b_conv': 'grad_w', 'grad_b_conv_bias': 'grad_w', 'grad_b_ln_g': 'grad_w', 'grad_b_ln_b': 'grad_w', 'grad_b_w_out': 'grad_w', 'grad_c_w_in': 'grad_w', 'grad_c_b_f': 'grad_w', 'grad_c_q_norm': 'grad_w', 'grad_c_k_norm': 'grad_w', 'grad_c_w_out': 'grad_w', 'delta_meta': 'delta_w', 'delta_ffn_norm': 'delta_w', 'delta_ffn_w_gate': 'delta_w', 'delta_ffn_w_up': 'delta_w', 'delta_ffn_w_down': 'delta_w', 'delta_mix_norm': 'delta_w', 'delta_a_w_in': 'delta_w', 'delta_a_conv': 'delta_w', 'delta_a_w_out': 'delta_w', 'delta_b_w_in': 'delta_w', 'delta_b_conv': 'delta_w', 'delta_b_conv_bias': 'delta_w', 'delta_b_ln_g': 'delta_w', 'delta_b_ln_b': 'delta_w', 'delta_b_w_out': 'delta_w', 'delta_c_w_in': 'delta_w', 'delta_c_b_f': 'delta_w', 'delta_c_q_norm': 'delta_w', 'delta_c_k_norm': 'delta_w', 'delta_c_w_out': 'delta_w', 'new_m_meta': 'new_m', 'new_m_ffn_norm': 'new_m', 'new_m_ffn_w_gate': 'new_m', 'new_m_ffn_w_up': 'new_m', 'new_m_ffn_w_down': 'new_m', 'new_m_mix_norm': 'new_m', 'new_m_a_w_in': 'new_m', 'new_m_a_conv': 'new_m', 'new_m_a_w_out': 'new_m', 'new_m_b_w_in': 'new_m', 'new_m_b_conv': 'new_m', 'new_m_b_conv_bias': 'new_m', 'new_m_b_ln_g': 'new_m', 'new_m_b_ln_b': 'new_m', 'new_m_b_w_out': 'new_m', 'new_m_c_w_in': 'new_m', 'new_m_c_b_f': 'new_m', 'new_m_c_q_norm': 'new_m', 'new_m_c_k_norm': 'new_m', 'new_m_c_w_out': 'new_m', 'new_v_meta': 'new_v', 'new_v_ffn_norm': 'new_v', 'new_v_ffn_w_gate': 'new_v', 'new_v_ffn_w_up': 'new_v', 'new_v_ffn_w_down': 'new_v', 'new_v_mix_norm': 'new_v', 'new_v_a_w_in': 'new_v', 'new_v_a_conv': 'new_v', 'new_v_a_w_out': 'new_v', 'new_v_b_w_in': 'new_v', 'new_v_b_conv': 'new_v', 'new_v_b_conv_bias': 'new_v', 'new_v_b_ln_g': 'new_v', 'new_v_b_ln_b': 'new_v', 'new_v_b_w_out': 'new_v', 'new_v_c_w_in': 'new_v', 'new_v_c_b_f': 'new_v', 'new_v_c_q_norm': 'new_v', 'new_v_c_k_norm': 'new_v', 'new_v_c_w_out': 'new_v'}


def _forward(args):
    return _fwd_reference(*[args[k] for k in FWD_PARAMS])


def _output_shape():
    def fwd():
        inp = _fwd_setup_inputs(0)
        return _fwd_reference(*[inp[k] for k in FWD_PARAMS])
    out = _jax.eval_shape(fwd)
    return out.shape, out.dtype

N_MICROBATCH = 1
ADAM_LR = 0.001
ADAM_B1 = 0.9
ADAM_B2 = 0.999
ADAM_EPS = 1e-08
ADAM_WD = 0.01
ADAM_STEP = 10
PER_EXAMPLE_BATCH_AXIS = {'x': 0, 'loss_target': 0}
SHARED_INPUTS = []
_WEIGHT_DTYPES = {'meta': _jnp.float32, 'ffn_norm': _jnp.float32, 'ffn_w_gate': _jnp.float32, 'ffn_w_up': _jnp.float32, 'ffn_w_down': _jnp.float32, 'mix_norm': _jnp.float32, 'a_w_in': _jnp.float32, 'a_conv': _jnp.float32, 'a_w_out': _jnp.float32, 'b_w_in': _jnp.float32, 'b_conv': _jnp.float32, 'b_conv_bias': _jnp.float32, 'b_ln_g': _jnp.float32, 'b_ln_b': _jnp.float32, 'b_w_out': _jnp.float32, 'c_w_in': _jnp.float32, 'c_b_f': _jnp.float32, 'c_q_norm': _jnp.float32, 'c_k_norm': _jnp.float32, 'c_w_out': _jnp.float32}
MOMENT_SCALE = {'meta': 1.322058e-01, 'ffn_norm': 1.173590e+01, 'ffn_w_gate': 3.530702e-01, 'ffn_w_up': 3.752604e-01, 'ffn_w_down': 6.219594e-01, 'mix_norm': 1.319444e+02, 'a_w_in': 2.241680e+00, 'a_conv': 3.590065e+01, 'a_w_out': 2.119759e+00, 'b_w_in': 7.070129e-01, 'b_conv': 1.479824e+00, 'b_conv_bias': 2.680191e+01, 'b_ln_g': 3.103364e+01, 'b_ln_b': 2.431046e+01, 'b_w_out': 5.350049e+00, 'c_w_in': 1.923665e+00, 'c_b_f': 1.803413e+02, 'c_q_norm': 5.349962e+01, 'c_k_norm': 5.408333e+01, 'c_w_out': 2.081052e+00}


def _to_microbatches(a, axis):
    t = _jnp.moveaxis(a, axis, 0)
    t = t.reshape((N_MICROBATCH, t.shape[0] // N_MICROBATCH) + t.shape[1:])
    return _jnp.moveaxis(t, 1, axis + 1)


def setup_inputs(seed: int = 0) -> dict:
    inp = _fwd_setup_inputs(seed)
    key = _jax.random.fold_in(_jax.random.key(seed), 7919)
    shape, _ = _output_shape()
    out = dict(inp)
    out["loss_target"] = _jax.random.normal(_jax.random.fold_in(key, 0), shape, _jnp.float32)
    for i, name in enumerate(TWIN_WEIGHTS):
        w = inp[name].astype(_jnp.float32)
        if MOMENT_SCALE is None:
            s = _jnp.sqrt(_jnp.mean(_jnp.square(w)) + 1e-30)
        else:
            s = MOMENT_SCALE[name]
        km, kv = _jax.random.split(_jax.random.fold_in(key, i + 1))
        out[name] = w
        out["m_" + name] = s * _jax.random.normal(km, w.shape, _jnp.float32)
        out["v_" + name] = (s * s) * _jax.random.uniform(kv, w.shape, _jnp.float32, 0.5, 1.5)
    if N_MICROBATCH > 1:
        for name, axis in PER_EXAMPLE_BATCH_AXIS.items():
            out[name] = _to_microbatches(out[name], axis)
    return {'x': out['x'], 'meta': out['meta'], 'ffn_norm': out['ffn_norm'], 'ffn_w_gate': out['ffn_w_gate'], 'ffn_w_up': out['ffn_w_up'], 'ffn_w_down': out['ffn_w_down'], 'mix_norm': out['mix_norm'], 'a_w_in': out['a_w_in'], 'a_conv': out['a_conv'], 'a_w_out': out['a_w_out'], 'b_w_in': out['b_w_in'], 'b_conv': out['b_conv'], 'b_conv_bias': out['b_conv_bias'], 'b_ln_g': out['b_ln_g'], 'b_ln_b': out['b_ln_b'], 'b_w_out': out['b_w_out'], 'c_w_in': out['c_w_in'], 'c_b_f': out['c_b_f'], 'c_q_norm': out['c_q_norm'], 'c_k_norm': out['c_k_norm'], 'c_w_out': out['c_w_out'], 'loss_target': out['loss_target'], 'm_meta': out['m_meta'], 'm_ffn_norm': out['m_ffn_norm'], 'm_ffn_w_gate': out['m_ffn_w_gate'], 'm_ffn_w_up': out['m_ffn_w_up'], 'm_ffn_w_down': out['m_ffn_w_down'], 'm_mix_norm': out['m_mix_norm'], 'm_a_w_in': out['m_a_w_in'], 'm_a_conv': out['m_a_conv'], 'm_a_w_out': out['m_a_w_out'], 'm_b_w_in': out['m_b_w_in'], 'm_b_conv': out['m_b_conv'], 'm_b_conv_bias': out['m_b_conv_bias'], 'm_b_ln_g': out['m_b_ln_g'], 'm_b_ln_b': out['m_b_ln_b'], 'm_b_w_out': out['m_b_w_out'], 'm_c_w_in': out['m_c_w_in'], 'm_c_b_f': out['m_c_b_f'], 'm_c_q_norm': out['m_c_q_norm'], 'm_c_k_norm': out['m_c_k_norm'], 'm_c_w_out': out['m_c_w_out'], 'v_meta': out['v_meta'], 'v_ffn_norm': out['v_ffn_norm'], 'v_ffn_w_gate': out['v_ffn_w_gate'], 'v_ffn_w_up': out['v_ffn_w_up'], 'v_ffn_w_down': out['v_ffn_w_down'], 'v_mix_norm': out['v_mix_norm'], 'v_a_w_in': out['v_a_w_in'], 'v_a_conv': out['v_a_conv'], 'v_a_w_out': out['v_a_w_out'], 'v_b_w_in': out['v_b_w_in'], 'v_b_conv': out['v_b_conv'], 'v_b_conv_bias': out['v_b_conv_bias'], 'v_b_ln_g': out['v_b_ln_g'], 'v_b_ln_b': out['v_b_ln_b'], 'v_b_w_out': out['v_b_w_out'], 'v_c_w_in': out['v_c_w_in'], 'v_c_b_f': out['v_c_b_f'], 'v_c_q_norm': out['v_c_q_norm'], 'v_c_k_norm': out['v_c_k_norm'], 'v_c_w_out': out['v_c_w_out']}


def _loss(weights, diff, rest, loss_target):
    with _jax.named_scope("forward"):
        args = {**rest, TWIN_DIFF_INPUT: diff, **{k: w.astype(_WEIGHT_DTYPES[k]) for k, w in weights.items()}}
        y = _forward(args)
    with _jax.named_scope("loss_head"):
        err = _jnp.square(y.astype(_jnp.float32) - loss_target)
        return 0.5 * _jnp.sum(_jnp.mean(err, axis=-1)) if err.ndim else 0.5 * err


def _adamw(w, g, m, v):
    m = ADAM_B1 * m + (1.0 - ADAM_B1) * g
    v = ADAM_B2 * v + (1.0 - ADAM_B2) * _jnp.square(g)
    m_hat = m / (1.0 - ADAM_B1 ** ADAM_STEP)
    v_hat = v / (1.0 - ADAM_B2 ** ADAM_STEP)
    delta = -ADAM_LR * (m_hat / (_jnp.sqrt(v_hat) + ADAM_EPS) + ADAM_WD * w)
    return delta, m, v


def reference(x, meta, ffn_norm, ffn_w_gate, ffn_w_up, ffn_w_down, mix_norm, a_w_in, a_conv, a_w_out, b_w_in, b_conv, b_conv_bias, b_ln_g, b_ln_b, b_w_out, c_w_in, c_b_f, c_q_norm, c_k_norm, c_w_out, loss_target, m_meta, m_ffn_norm, m_ffn_w_gate, m_ffn_w_up, m_ffn_w_down, m_mix_norm, m_a_w_in, m_a_conv, m_a_w_out, m_b_w_in, m_b_conv, m_b_conv_bias, m_b_ln_g, m_b_ln_b, m_b_w_out, m_c_w_in, m_c_b_f, m_c_q_norm, m_c_k_norm, m_c_w_out, v_meta, v_ffn_norm, v_ffn_w_gate, v_ffn_w_up, v_ffn_w_down, v_mix_norm, v_a_w_in, v_a_conv, v_a_w_out, v_b_w_in, v_b_conv, v_b_conv_bias, v_b_ln_g, v_b_ln_b, v_b_w_out, v_c_w_in, v_c_b_f, v_c_q_norm, v_c_k_norm, v_c_w_out):
    given = dict(x=x, meta=meta, ffn_norm=ffn_norm, ffn_w_gate=ffn_w_gate, ffn_w_up=ffn_w_up, ffn_w_down=ffn_w_down, mix_norm=mix_norm, a_w_in=a_w_in, a_conv=a_conv, a_w_out=a_w_out, b_w_in=b_w_in, b_conv=b_conv, b_conv_bias=b_conv_bias, b_ln_g=b_ln_g, b_ln_b=b_ln_b, b_w_out=b_w_out, c_w_in=c_w_in, c_b_f=c_b_f, c_q_norm=c_q_norm, c_k_norm=c_k_norm, c_w_out=c_w_out, loss_target=loss_target, m_meta=m_meta, m_ffn_norm=m_ffn_norm, m_ffn_w_gate=m_ffn_w_gate, m_ffn_w_up=m_ffn_w_up, m_ffn_w_down=m_ffn_w_down, m_mix_norm=m_mix_norm, m_a_w_in=m_a_w_in, m_a_conv=m_a_conv, m_a_w_out=m_a_w_out, m_b_w_in=m_b_w_in, m_b_conv=m_b_conv, m_b_conv_bias=m_b_conv_bias, m_b_ln_g=m_b_ln_g, m_b_ln_b=m_b_ln_b, m_b_w_out=m_b_w_out, m_c_w_in=m_c_w_in, m_c_b_f=m_c_b_f, m_c_q_norm=m_c_q_norm, m_c_k_norm=m_c_k_norm, m_c_w_out=m_c_w_out, v_meta=v_meta, v_ffn_norm=v_ffn_norm, v_ffn_w_gate=v_ffn_w_gate, v_ffn_w_up=v_ffn_w_up, v_ffn_w_down=v_ffn_w_down, v_mix_norm=v_mix_norm, v_a_w_in=v_a_w_in, v_a_conv=v_a_conv, v_a_w_out=v_a_w_out, v_b_w_in=v_b_w_in, v_b_conv=v_b_conv, v_b_conv_bias=v_b_conv_bias, v_b_ln_g=v_b_ln_g, v_b_ln_b=v_b_ln_b, v_b_w_out=v_b_w_out, v_c_w_in=v_c_w_in, v_c_b_f=v_c_b_f, v_c_q_norm=v_c_q_norm, v_c_k_norm=v_c_k_norm, v_c_w_out=v_c_w_out)
    weights = {n: given[n] for n in TWIN_WEIGHTS}
    shared = {n: given[n] for n in SHARED_INPUTS}
    per_example = {n: given[n] for n in ['x']}
    grad_fn = _jax.value_and_grad(_loss, argnums=(0, 1))

    def one_microbatch(ex, loss_target):
        ex = dict(ex)
        diff = ex.pop(TWIN_DIFF_INPUT)
        return grad_fn(weights, diff, {**shared, **ex}, loss_target)

    if N_MICROBATCH == 1:
        loss, (grad_w, grad_x) = one_microbatch(per_example, given["loss_target"])
    else:
        def body(carry, xs):
            loss_sum, grad_sum = carry
            l_k, (gw_k, gx_k) = one_microbatch(xs[0], xs[1])
            with _jax.named_scope("update"):
                return (loss_sum + l_k, _jax.tree.map(_jnp.add, grad_sum, gw_k)), gx_k

        init = (_jnp.zeros((), _jnp.float32), _jax.tree.map(_jnp.zeros_like, weights))
        (loss, grad_w), grad_x = _jax.lax.scan(body, init, (per_example, given["loss_target"]))
    with _jax.named_scope("update"):
        delta_w, new_m, new_v = {}, {}, {}
        for n in TWIN_WEIGHTS:
            delta_w[n], new_m[n], new_v[n] = _adamw(weights[n], grad_w[n], given["m_" + n], given["v_" + n])
    return (loss, grad_x, *[grad_w[n] for n in TWIN_WEIGHTS], *[delta_w[n] for n in TWIN_WEIGHTS],
            *[new_m[n] for n in TWIN_WEIGHTS], *[new_v[n] for n in TWIN_WEIGHTS])
```

```python
import functools

import jax
import jax.numpy as jnp
from jax import lax
from jax.experimental import pallas as pl
from jax.experimental.pallas import tpu as pltpu

F32 = jnp.float32
BF16 = jnp.bfloat16
HI = lax.Precision.HIGHEST
EPS = 1e-6
NEG = -1e30
N_META = 16
HEAD_DIM = 64
LANES = 128
N_DEV = 8
ROW_ALIGN = 256
VMEM_LIMIT = 56 * 1024 * 1024
ADAM_LR, ADAM_B1, ADAM_B2, ADAM_EPS, ADAM_WD, ADAM_STEP = 0.001, 0.9, 0.999, 1e-08, 0.01, 10
MESH = pl.DeviceIdType.MESH
NT = (((1,), (1,)), ((), ()))
TN = (((0,), (0,)), ((), ()))


def _cparams(n_axes):
    return pltpu.CompilerParams(dimension_semantics=("arbitrary",) * n_axes, vmem_limit_bytes=VMEM_LIMIT)


def _pick(n, cands):
    for c in cands:
        if n % c == 0:
            return c
    raise ValueError(f"no tile for {n} among {cands}")


def _sigmoid(x):
    return 1.0 / (1.0 + jnp.exp(-x))


def _rms(x):
    r = lax.rsqrt(jnp.mean(x * x, axis=-1, keepdims=True) + EPS)
    return x * r, r


def _rms_bwd(xhat, r, g, dy):
    gy = dy * g
    dx = r * (gy - xhat * jnp.mean(gy * xhat, axis=-1, keepdims=True))
    return dx, jnp.sum(dy * xhat, axis=0, keepdims=True)


def _mesh_pos():
    return lax.axis_index("x"), lax.axis_index("y"), lax.axis_index("c")


def all_gather(arrs):
    n = len(arrs)
    hbm = pl.BlockSpec(memory_space=pltpu.HBM)

    def body(*refs):
        ins, outs = refs[:n], refs[n:2 * n]
        send_sems, recv_sems, local_sems = refs[2 * n:]
        x, y, c = _mesh_pos()
        me, sibling = (x, y, c), (x, y, 1 - c)
        chips = [(1 - x, y), (x, 1 - y), (1 - x, 1 - y)]

        def slot(a, px, py, pc):
            return outs[a].at[4 * px + 2 * py + pc]

        def copy(a, k, block, to, src=None):
            return pltpu.make_async_remote_copy(
                src_ref=slot(a, *block) if src is None else src, dst_ref=slot(a, *block),
                send_sem=send_sems.at[7 * a + k], recv_sem=recv_sems.at[7 * a + k],
                device_id=to, device_id_type=MESH)

        own, first, passed = [], [], []
        for a in range(n):
            cp = pltpu.make_async_copy(ins[a], slot(a, *me), local_sems.at[a])
            cp.start()
            own.append(cp)
            first.append(copy(a, 0, me, sibling, src=ins[a]))
            first += [copy(a, 1 + j, me, (*chip, c), src=ins[a]) for j, chip in enumerate(chips)]
        for cp in first:
            cp.start()
        for j, chip in enumerate(chips):
            for a in range(n):
                copy(a, 1 + j, (*chip, c), me).wait_recv()
                cp = copy(a, 4 + j, (*chip, c), sibling)
                cp.start()
                passed.append(cp)
        for a in range(n):
            copy(a, 0, sibling, me).wait_recv()
            for j, chip in enumerate(chips):
                copy(a, 4 + j, (*chip, 1 - c), me).wait_recv()
        for cp in first + passed:
            cp.wait_send()
        for cp in own:
            cp.wait()

    return pl.pallas_call(
        body, name="all_gather",
        out_shape=[jax.ShapeDtypeStruct((N_DEV,) + a.shape, a.dtype) for a in arrs],
        in_specs=[hbm] * n, out_specs=[hbm] * n,
        scratch_shapes=[pltpu.SemaphoreType.DMA((7 * n,)), pltpu.SemaphoreType.DMA((7 * n,)),
                        pltpu.SemaphoreType.DMA((n,))],
    )(*arrs)


def exchange(arrs):
    n = len(arrs)
    hbm = pl.BlockSpec(memory_space=pltpu.HBM)

    def body(*refs):
        ins, outs = refs[:n], refs[n:2 * n]
        send_sems, recv_sems, local_sems = refs[2 * n:]
        x, y, c = _mesh_pos()
        me = 4 * x + 2 * y + c
        own, sent = [], []
        for a in range(n):
            cp = pltpu.make_async_copy(ins[a].at[me], outs[a].at[me], local_sems.at[a])
            cp.start()
            own.append(cp)
        for k in range(1, N_DEV):
            px = 1 - x if k & 4 else x
            py = 1 - y if k & 2 else y
            pc = 1 - c if k & 1 else c
            peer = 4 * px + 2 * py + pc
            for a in range(n):
                cp = pltpu.make_async_remote_copy(
                    src_ref=ins[a].at[peer], dst_ref=outs[a].at[me],
                    send_sem=send_sems.at[7 * a + k - 1], recv_sem=recv_sems.at[7 * a + k - 1],
                    device_id=(px, py, pc), device_id_type=MESH)
                cp.start()
                sent.append((cp, a, k, peer))
        for cp, a, k, peer in sent:
            pltpu.make_async_remote_copy(
                src_ref=ins[a].at[peer], dst_ref=outs[a].at[peer],
                send_sem=send_sems.at[7 * a + k - 1], recv_sem=recv_sems.at[7 * a + k - 1],
                device_id=(x, y, c), device_id_type=MESH).wait_recv()
        for cp, a, k, peer in sent:
            cp.wait_send()
        for cp in own:
            cp.wait()

    return pl.pallas_call(
        body, name="exchange",
        out_shape=[jax.ShapeDtypeStruct(a.shape, a.dtype) for a in arrs],
        in_specs=[hbm] * n, out_specs=[hbm] * n,
        scratch_shapes=[pltpu.SemaphoreType.DMA((7 * n,)), pltpu.SemaphoreType.DMA((7 * n,)),
                        pltpu.SemaphoreType.DMA((n,))],
    )(*arrs)


def _resident(shape):
    return pl.BlockSpec(shape, lambda i: (0,) * len(shape), pipeline_mode=pl.Buffered(1))


def ffn_fwd(h, g, wg, wu, wd):
    tp, d = h.shape
    f = wg.shape[1]
    tm = _pick(tp, (384, 256))
    tf = _pick(f, (1408, 1024, 512, 256, 128))

    def body(h_ref, g_ref, wg_ref, wu_ref, wd_ref, ho_ref, gate_ref, up_ref):
        x = h_ref[...]
        xhat, _ = _rms(x)
        hn = (xhat * g_ref[...]).astype(BF16)
        acc = jnp.zeros((tm, d), F32)
        for c0 in range(0, f, tf):
            gate = jnp.dot(hn, wg_ref[:, c0:c0 + tf], preferred_element_type=F32)
            up = jnp.dot(hn, wu_ref[:, c0:c0 + tf], preferred_element_type=F32)
            gate_ref[:, c0:c0 + tf] = gate.astype(BF16)
            up_ref[:, c0:c0 + tf] = up.astype(BF16)
            act = (gate * _sigmoid(gate) * up).astype(BF16)
            acc = acc + jnp.dot(act, wd_ref[c0:c0 + tf, :], preferred_element_type=F32)
        ho_ref[...] = x + 0.5 * acc

    row = lambda n: pl.BlockSpec((tm, n), lambda i: (i, 0))
    return pl.pallas_call(
        body, name="ffn_fwd", grid=(tp // tm,),
        out_shape=[jax.ShapeDtypeStruct((tp, d), F32), jax.ShapeDtypeStruct((tp, f), BF16),
                   jax.ShapeDtypeStruct((tp, f), BF16)],
        in_specs=[row(d), _resident((1, d)), _resident((d, f)), _resident((d, f)), _resident((f, d))],
        out_specs=[row(d), row(f), row(f)],
        compiler_params=_cparams(1),
    )(h, g, wg, wu, wd)


def ffn_bwd(dho, h, g, gate, up, wg, wu, wd):
    tp, d = h.shape
    f = wg.shape[1]
    tm = _pick(tp, (256,))
    tf = _pick(f, (1408, 1024, 512, 256, 128))

    def body(dho_ref, h_ref, g_ref, gate_ref, up_ref, wg_ref, wu_ref, wd_ref,
             dhi_ref, hn_ref, dgate_ref, dup_ref, act_ref, dg_ref):
        @pl.when(pl.program_id(0) == 0)
        def _():
            dg_ref[...] = jnp.zeros_like(dg_ref)

        dho_ = dho_ref[...]
        dout = (0.5 * dho_).astype(BF16)
        dhn = jnp.zeros((tm, d), F32)
        for c0 in range(0, f, tf):
            dact = lax.dot_general(dout, wd_ref[c0:c0 + tf, :], NT, preferred_element_type=F32)
            gt = gate_ref[:, c0:c0 + tf].astype(F32)
            u = up_ref[:, c0:c0 + tf].astype(F32)
            sig = _sigmoid(gt)
            silu = gt * sig
            act_ref[:, c0:c0 + tf] = (silu * u).astype(BF16)
            dup = (dact * silu).astype(BF16)
            dgate = (dact * u * (sig * (1.0 + gt * (1.0 - sig)))).astype(BF16)
            dup_ref[:, c0:c0 + tf] = dup
            dgate_ref[:, c0:c0 + tf] = dgate
            dhn = dhn + (lax.dot_general(dgate, wg_ref[:, c0:c0 + tf], NT, preferred_element_type=F32)
                         + lax.dot_general(dup, wu_ref[:, c0:c0 + tf], NT, preferred_element_type=F32))
        xhat, r = _rms(h_ref[...])
        gg = g_ref[...]
        dx, dgp = _rms_bwd(xhat, r, gg, dhn)
        dg_ref[...] += dgp
        dhi_ref[...] = dho_ + dx
        hn_ref[...] = (xhat * gg).astype(BF16)

    row = lambda n: pl.BlockSpec((tm, n), lambda i: (i, 0))
    return pl.pallas_call(
        body, name="ffn_bwd", grid=(tp // tm,),
        out_shape=[jax.ShapeDtypeStruct((tp, d), F32), jax.ShapeDtypeStruct((tp, d), BF16),
                   jax.ShapeDtypeStruct((tp, f), BF16), jax.ShapeDtypeStruct((tp, f), BF16),
                   jax.ShapeDtypeStruct((tp, f), BF16), jax.ShapeDtypeStruct((1, d), F32)],
        in_specs=[row(d), row(d), _resident((1, d)), row(f), row(f),
                  _resident((d, f)), _resident((d, f)), _resident((f, d))],
        out_specs=[row(d), row(d), row(f), row(f), row(f), pl.BlockSpec((1, d), lambda i: (0, 0))],
        compiler_params=_cparams(1),
    )(dho, h, g, gate, up, wg, wu, wd)


def atb(a, b, scale=1.0):
    tp, m = a.shape
    n = b.shape[1]
    tmm = _pick(m, (1024, 1408, 640, 512, 256, 128))
    tn = _pick(n, (1024, 1408, 640, 512, 256, 128))
    tk = _pick(tp, (768, 512, 256))
    nk = tp // tk

    def body(a_ref, b_ref, o_ref):
        k = pl.program_id(2)

        @pl.when(k == 0)
        def _():
            o_ref[...] = jnp.zeros_like(o_ref)

        o_ref[...] += lax.dot_general(a_ref[...].astype(BF16), b_ref[...].astype(BF16), TN,
                                      preferred_element_type=F32)

        if scale != 1.0:
            @pl.when(k == nk - 1)
            def _():
                o_ref[...] = o_ref[...] * scale

    return pl.pallas_call(
        body, name="atb", grid=(m // tmm, n // tn, nk),
        out_shape=jax.ShapeDtypeStruct((m, n), F32),
        in_specs=[pl.BlockSpec((tk, tmm), lambda i, j, k: (k, i)), pl.BlockSpec((tk, tn), lambda i, j, k: (k, j))],
        out_specs=pl.BlockSpec((tmm, tn), lambda i, j, k: (i, j)),
        compiler_params=_cparams(3),
    )(a, b)


def norm_matmul(h, g, w, out_dtype, emit_u):
    tp, d = h.shape
    n = w.shape[1]
    tm = _pick(tp, (768, 512, 256))
    tn = _pick(n, (1024, 768, 640, 512, 256, 128))

    def body(h_ref, g_ref, w_ref, o_ref, *rest):
        u_sc = rest[-1]

        @pl.when(pl.program_id(1) == 0)
        def _():
            xhat, _ = _rms(h_ref[...])
            u_sc[...] = (xhat * g_ref[...]).astype(BF16)
            if emit_u:
                rest[0][...] = u_sc[...]

        o_ref[...] = jnp.dot(u_sc[...], w_ref[...], preferred_element_type=F32).astype(out_dtype)

    out_shape = [jax.ShapeDtypeStruct((tp, n), out_dtype)]
    out_specs = [pl.BlockSpec((tm, tn), lambda i, j: (i, j))]
    if emit_u:
        out_shape.append(jax.ShapeDtypeStruct((tp, d), BF16))
        out_specs.append(pl.BlockSpec((tm, d), lambda i, j: (i, 0)))
    return pl.pallas_call(
        body, name="norm_matmul", grid=(tp // tm, n // tn), out_shape=out_shape,
        in_specs=[pl.BlockSpec((tm, d), lambda i, j: (i, 0)), pl.BlockSpec((1, d), lambda i, j: (0, 0)),
                  pl.BlockSpec((d, tn), lambda i, j: (0, j))],
        out_specs=out_specs, scratch_shapes=[pltpu.VMEM((tm, d), BF16)],
        compiler_params=_cparams(2),
    )(h, g, w)


def res_matmul(h, y, w):
    tp, d = h.shape
    k = y.shape[1]
    tm = _pick(tp, (768, 512, 256))

    def body(h_ref, y_ref, w_ref, o_ref):
        o_ref[...] = h_ref[...] + jnp.dot(y_ref[...], w_ref[...], preferred_element_type=F32)

    return pl.pallas_call(
        body, name="res_matmul", grid=(tp // tm,), out_shape=jax.ShapeDtypeStruct((tp, d), F32),
        in_specs=[pl.BlockSpec((tm, d), lambda i: (i, 0)), pl.BlockSpec((tm, k), lambda i: (i, 0)),
                  pl.BlockSpec((k, d), lambda i: (0, 0))],
        out_specs=pl.BlockSpec((tm, d), lambda i: (i, 0)),
        compiler_params=_cparams(1),
    )(h, y, w)


def matmul_nt(a, w):
    tp, d = a.shape
    k = w.shape[0]
    tm = _pick(tp, (768, 512, 256))

    def body(a_ref, w_ref, o_ref):
        o_ref[...] = lax.dot_general(a_ref[...].astype(BF16), w_ref[...], NT,
                                     preferred_element_type=F32).astype(BF16)

    return pl.pallas_call(
        body, name="matmul_nt", grid=(tp // tm,), out_shape=jax.ShapeDtypeStruct((tp, k), BF16),
        in_specs=[pl.BlockSpec((tm, d), lambda i: (i, 0)), pl.BlockSpec((k, d), lambda i: (0, 0))],
        out_specs=pl.BlockSpec((tm, k), lambda i: (i, 0)),
        compiler_params=_cparams(1),
    )(a, w)


def matmul_nt_rms_bwd(dp, w, h, g, dres):
    tp, d = h.shape
    n = w.shape[1]
    tm = _pick(tp, (384, 256))

    def body(dp_ref, w_ref, h_ref, g_ref, dres_ref, dh_ref, dg_ref):
        @pl.when(pl.program_id(0) == 0)
        def _():
            dg_ref[...] = jnp.zeros_like(dg_ref)

        du = lax.dot_general(dp_ref[...], w_ref[...], NT, preferred_element_type=F32)
        xhat, r = _rms(h_ref[...])
        dx, dgp = _rms_bwd(xhat, r, g_ref[...], du)
        dg_ref[...] += dgp
        dh_ref[...] = dres_ref[...] + dx

    return pl.pallas_call(
        body, name="matmul_nt_rms_bwd", grid=(tp // tm,),
        out_shape=[jax.ShapeDtypeStruct((tp, d), F32), jax.ShapeDtypeStruct((1, d), F32)],
        in_specs=[pl.BlockSpec((tm, n), lambda i: (i, 0)), pl.BlockSpec((d, n), lambda i: (0, 0)),
                  pl.BlockSpec((tm, d), lambda i: (i, 0)), pl.BlockSpec((1, d), lambda i: (0, 0)),
                  pl.BlockSpec((tm, d), lambda i: (i, 0))],
        out_specs=[pl.BlockSpec((tm, d), lambda i: (i, 0)), pl.BlockSpec((1, d), lambda i: (0, 0))],
        compiler_params=_cparams(1),
    )(dp, w, h, g, dres)


CONV_TILE = (256,)


def _halo_before(tm, hb, col=0):
    return lambda i: (jnp.maximum(i * (tm // hb) - 1, 0), col)


def _halo_after(tm, hb, nblk, col=0):
    return lambda i: (jnp.minimum((i + 1) * (tm // hb), nblk - 1), col)


def a_elem_fwd(p, w):
    tp, d3 = p.shape
    d = d3 // 3
    kw = w.shape[0]
    tm = _pick(tp, CONV_TILE)
    hb = 16

    def body(p_ref, ph_ref, w_ref, y_ref, ext):
        i = pl.program_id(0)
        halo = ph_ref[:, d:2 * d].astype(F32) * ph_ref[:, 2 * d:].astype(F32)
        ext[0:hb, :] = jnp.where(i > 0, halo, 0.0)
        ext[hb:, :] = p_ref[:, d:2 * d].astype(F32) * p_ref[:, 2 * d:].astype(F32)
        z = jnp.zeros((tm, d), F32)
        for k in range(kw):
            z = z + w_ref[k:k + 1, :] * ext[pl.ds(hb - (kw - 1) + k, tm), :]
        y_ref[...] = (p_ref[:, 0:d].astype(F32) * z).astype(BF16)

    return pl.pallas_call(
        body, name="a_elem_fwd", grid=(tp // tm,), out_shape=jax.ShapeDtypeStruct((tp, d), BF16),
        in_specs=[pl.BlockSpec((tm, d3), lambda i: (i, 0)), pl.BlockSpec((hb, d3), _halo_before(tm, hb)),
                  pl.BlockSpec((kw, d), lambda i: (0, 0))],
        out_specs=pl.BlockSpec((tm, d), lambda i: (i, 0)),
        scratch_shapes=[pltpu.VMEM((tm + hb, d), F32)],
        compiler_params=_cparams(1),
    )(p, p, w)


def a_elem_bwd(p, dy, w):
    tp, d3 = p.shape
    d = d3 // 3
    kw = w.shape[0]
    tm = _pick(tp, CONV_TILE)
    hb = 16
    n = tp // tm

    def body(p_ref, ph_ref, pa_ref, dy_ref, dya_ref, w_ref, dp_ref, dw_ref, ext, dzext):
        i = pl.program_id(0)

        @pl.when(i == 0)
        def _():
            dw_ref[...] = jnp.zeros_like(dw_ref)

        b = p_ref[:, 0:d].astype(F32)
        c = p_ref[:, d:2 * d].astype(F32)
        v = p_ref[:, 2 * d:].astype(F32)
        dy_ = dy_ref[...].astype(F32)
        halo = ph_ref[:, d:2 * d].astype(F32) * ph_ref[:, 2 * d:].astype(F32)
        ext[0:hb, :] = jnp.where(i > 0, halo, 0.0)
        ext[hb:, :] = c * v
        dz = dy_ * b
        dzext[0:tm, :] = dz
        dzext[tm:, :] = jnp.where(i < n - 1, dya_ref[...].astype(F32) * pa_ref[:, 0:d].astype(F32), 0.0)
        z = jnp.zeros((tm, d), F32)
        dcv = jnp.zeros((tm, d), F32)
        for k in range(kw):
            sh = ext[pl.ds(hb - (kw - 1) + k, tm), :]
            z = z + w_ref[k:k + 1, :] * sh
            dw_ref[k:k + 1, :] += jnp.sum(dz * sh, axis=0, keepdims=True)
            dcv = dcv + w_ref[k:k + 1, :] * dzext[pl.ds(kw - 1 - k, tm), :]
        dp_ref[:, 0:d] = (dy_ * z).astype(BF16)
        dp_ref[:, d:2 * d] = (dcv * v).astype(BF16)
        dp_ref[:, 2 * d:] = (dcv * c).astype(BF16)

    nb = tp // hb
    return pl.pallas_call(
        body, name="a_elem_bwd", grid=(n,),
        out_shape=[jax.ShapeDtypeStruct((tp, d3), BF16), jax.ShapeDtypeStruct((8, d), F32)],
        in_specs=[pl.BlockSpec((tm, d3), lambda i: (i, 0)), pl.BlockSpec((hb, d3), _halo_before(tm, hb)),
                  pl.BlockSpec((hb, d3), _halo_after(tm, hb, nb)),
                  pl.BlockSpec((tm, d), lambda i: (i, 0)), pl.BlockSpec((hb, d), _halo_after(tm, hb, nb)),
                  pl.BlockSpec((kw, d), lambda i: (0, 0))],
        out_specs=[pl.BlockSpec((tm, d3), lambda i: (i, 0)), pl.BlockSpec((8, d), lambda i: (0, 0))],
        scratch_shapes=[pltpu.VMEM((tm + hb, d), F32), pltpu.VMEM((tm + hb, d), F32)],
        compiler_params=_cparams(1),
    )(p, p, p, dy, dy, w)


def _b_u2(p_ref, ph_ref, w_ref, bias_ref, ext, i, tm, d, kw, hb):
    a = p_ref[:, 0:d].astype(F32)
    sg = _sigmoid(p_ref[:, d:].astype(F32))
    halo = ph_ref[:, 0:d].astype(F32) * _sigmoid(ph_ref[:, d:].astype(F32))
    ext[0:hb, :] = jnp.where(i > 0, halo, 0.0)
    ext[hb:, :] = a * sg
    u2 = jnp.zeros((tm, d), F32) + bias_ref[...]
    for k in range(kw):
        u2 = u2 + w_ref[k:k + 1, :] * ext[pl.ds(hb - (kw - 1) + k, tm), :]
    return a, sg, u2


def _ln(u2):
    mu = jnp.mean(u2, axis=-1, keepdims=True)
    xc = u2 - mu
    rstd = lax.rsqrt(jnp.mean(xc * xc, axis=-1, keepdims=True) + EPS)
    return xc * rstd, rstd


def b_elem_fwd(p, w, bias, ln_g, ln_b):
    tp, d2 = p.shape
    d = d2 // 2
    kw = w.shape[0]
    tm = _pick(tp, CONV_TILE)
    hb = 32

    def body(p_ref, ph_ref, w_ref, bias_ref, g_ref, b_ref, y_ref, ext):
        i = pl.program_id(0)
        _, _, u2 = _b_u2(p_ref, ph_ref, w_ref, bias_ref, ext, i, tm, d, kw, hb)
        xhat, _ = _ln(u2)
        u3 = xhat * g_ref[...] + b_ref[...]
        y_ref[...] = (u3 * _sigmoid(u3)).astype(BF16)

    vec = pl.BlockSpec((1, d), lambda i: (0, 0))
    return pl.pallas_call(
        body, name="b_elem_fwd", grid=(tp // tm,), out_shape=jax.ShapeDtypeStruct((tp, d), BF16),
        in_specs=[pl.BlockSpec((tm, d2), lambda i: (i, 0)), pl.BlockSpec((hb, d2), _halo_before(tm, hb)),
                  pl.BlockSpec((kw, d), lambda i: (0, 0)), vec, vec, vec],
        out_specs=pl.BlockSpec((tm, d), lambda i: (i, 0)),
        scratch_shapes=[pltpu.VMEM((tm + hb, d), F32)],
        compiler_params=_cparams(1),
    )(p, p, w, bias, ln_g, ln_b)


def b_elem_bwd1(p, dy, w, bias, ln_g, ln_b):
    tp, d2 = p.shape
    d = d2 // 2
    kw = w.shape[0]
    tm = _pick(tp, CONV_TILE)
    hb = 32

    def body(p_ref, ph_ref, dy_ref, w_ref, bias_ref, g_ref, b_ref, du2_ref, st_ref, ext):
        i = pl.program_id(0)

        @pl.when(i == 0)
        def _():
            st_ref[...] = jnp.zeros_like(st_ref)

        _, _, u2 = _b_u2(p_ref, ph_ref, w_ref, bias_ref, ext, i, tm, d, kw, hb)
        xhat, rstd = _ln(u2)
        u3 = xhat * g_ref[...] + b_ref[...]
        s3 = _sigmoid(u3)
        du3 = dy_ref[...].astype(F32) * (s3 * (1.0 + u3 * (1.0 - s3)))
        dxh = du3 * g_ref[...]
        du2 = rstd * (dxh - jnp.mean(dxh, axis=-1, keepdims=True)
                      - xhat * jnp.mean(dxh * xhat, axis=-1, keepdims=True))
        du2_ref[...] = du2
        st_ref[0:1, :] += jnp.sum(du3 * xhat, axis=0, keepdims=True)
        st_ref[1:2, :] += jnp.sum(du3, axis=0, keepdims=True)
        st_ref[2:3, :] += jnp.sum(du2, axis=0, keepdims=True)

    vec = pl.BlockSpec((1, d), lambda i: (0, 0))
    return pl.pallas_call(
        body, name="b_elem_bwd1", grid=(tp // tm,),
        out_shape=[jax.ShapeDtypeStruct((tp, d), F32), jax.ShapeDtypeStruct((8, d), F32)],
        in_specs=[pl.BlockSpec((tm, d2), lambda i: (i, 0)), pl.BlockSpec((hb, d2), _halo_before(tm, hb)),
                  pl.BlockSpec((tm, d), lambda i: (i, 0)), pl.BlockSpec((kw, d), lambda i: (0, 0)), vec, vec, vec],
        out_specs=[pl.BlockSpec((tm, d), lambda i: (i, 0)), pl.BlockSpec((8, d), lambda i: (0, 0))],
        scratch_shapes=[pltpu.VMEM((tm + hb, d), F32)],
        compiler_params=_cparams(1),
    )(p, p, dy, w, bias, ln_g, ln_b)


def b_elem_bwd2(p, du2, w):
    tp, d2 = p.shape
    d = d2 // 2
    kw = w.shape[0]
    tm = _pick(tp, CONV_TILE)
    hb = 32
    n = tp // tm

    def body(p_ref, ph_ref, du2_ref, du2a_ref, w_ref, dp_ref, dw_ref, ext, dext):
        i = pl.program_id(0)

        @pl.when(i == 0)
        def _():
            dw_ref[...] = jnp.zeros_like(dw_ref)

        a = p_ref[:, 0:d].astype(F32)
        sg = _sigmoid(p_ref[:, d:].astype(F32))
        halo = ph_ref[:, 0:d].astype(F32) * _sigmoid(ph_ref[:, d:].astype(F32))
        ext[0:hb, :] = jnp.where(i > 0, halo, 0.0)
        ext[hb:, :] = a * sg
        du2_ = du2_ref[...]
        dext[0:tm, :] = du2_
        dext[tm:, :] = jnp.where(i < n - 1, du2a_ref[...], 0.0)
        du1 = jnp.zeros((tm, d), F32)
        for k in range(kw):
            dw_ref[k:k + 1, :] += jnp.sum(du2_ * ext[pl.ds(hb - (kw - 1) + k, tm), :], axis=0, keepdims=True)
            du1 = du1 + w_ref[k:k + 1, :] * dext[pl.ds(kw - 1 - k, tm), :]
        dp_ref[:, 0:d] = (du1 * sg).astype(BF16)
        dp_ref[:, d:] = (du1 * a * sg * (1.0 - sg)).astype(BF16)

    nb = tp // hb
    return pl.pallas_call(
        body, name="b_elem_bwd2", grid=(n,),
        out_shape=[jax.ShapeDtypeStruct((tp, d2), BF16), jax.ShapeDtypeStruct((32, d), F32)],
        in_specs=[pl.BlockSpec((tm, d2), lambda i: (i, 0)), pl.BlockSpec((hb, d2), _halo_before(tm, hb)),
                  pl.BlockSpec((tm, d), lambda i: (i, 0)), pl.BlockSpec((hb, d), _halo_after(tm, hb, nb)),
                  pl.BlockSpec((kw, d), lambda i: (0, 0))],
        out_specs=[pl.BlockSpec((tm, d2), lambda i: (i, 0)), pl.BlockSpec((32, d), lambda i: (0, 0))],
        scratch_shapes=[pltpu.VMEM((tm + hb, d), F32), pltpu.VMEM((tm + hb, d), F32)],
        compiler_params=_cparams(1),
    )(p, p, du2, du2, w)


ATT_TILE = 256


def _head_masks(d):
    c = lax.broadcasted_iota(jnp.int32, (d, LANES), 0)
    h = lax.broadcasted_iota(jnp.int32, (d, LANES), 1)
    seg = (c // HEAD_DIM == h).astype(F32)
    fold = (c % HEAD_DIM == h).astype(F32)
    return seg, seg.T, fold


def c_prep(p, b_f, qg, kg):
    tp = p.shape[0]
    d = qg.shape[1]
    tm = ATT_TILE
    seg, seg_t, _ = _head_masks(d)

    def body(p_ref, pf_ref, bf_ref, qg_ref, kg_ref, seg_ref, segt_ref, q_ref, k_ref, v_ref, cum_ref, carry):
        @pl.when(pl.program_id(0) == 0)
        def _():
            carry[...] = jnp.zeros_like(carry)

        def norm(x, g):
            ms = jnp.dot(x * x, seg_ref[...], precision=HI, preferred_element_type=F32) * (1.0 / HEAD_DIM)
            r = jnp.dot(lax.rsqrt(ms + EPS), segt_ref[...], precision=HI, preferred_element_type=F32)
            return x * r * g

        q_ref[...] = (norm(p_ref[:, 0:d], qg_ref[...]) * (HEAD_DIM ** -0.5)).astype(BF16)
        k_ref[...] = norm(p_ref[:, d:2 * d], kg_ref[...]).astype(BF16)
        v_ref[...] = p_ref[:, 2 * d:].astype(BF16)
        xf = pf_ref[...] + bf_ref[...]
        logf = jnp.minimum(xf, 0.0) - jnp.log(1.0 + jnp.exp(-jnp.abs(xf)))
        r_ = lax.broadcasted_iota(jnp.int32, (tm, tm), 0)
        c_ = lax.broadcasted_iota(jnp.int32, (tm, tm), 1)
        tri = (c_ <= r_).astype(F32)
        cum_ref[...] = jnp.dot(tri, logf, precision=HI, preferred_element_type=F32) + carry[0:1, :]
        carry[0:1, :] += jnp.sum(logf, axis=0, keepdims=True)

    vec = pl.BlockSpec((1, d), lambda i: (0, 0))
    rowd = pl.BlockSpec((tm, d), lambda i: (i, 0))
    return pl.pallas_call(
        body, name="c_prep", grid=(tp // tm,),
        out_shape=[jax.ShapeDtypeStruct((tp, d), BF16)] * 3 + [jax.ShapeDtypeStruct((tp, LANES), F32)],
        in_specs=[pl.BlockSpec((tm, 3 * d), lambda i: (i, 0)), pl.BlockSpec((tm, LANES), lambda i: (i, 3 * d // LANES)),
                  pl.BlockSpec((1, LANES), lambda i: (0, 0)), vec, vec,
                  pl.BlockSpec((d, LANES), lambda i: (0, 0)), pl.BlockSpec((LANES, d), lambda i: (0, 0))],
        out_specs=[rowd, rowd, rowd, pl.BlockSpec((tm, LANES), lambda i: (i, 0))],
        scratch_shapes=[pltpu.VMEM((8, LANES), F32)],
        compiler_params=_cparams(1),
    )(p, p, b_f, qg, kg, seg, seg_t)


def _pair_split(x2, lo):
    return jnp.where(lo, x2, jnp.zeros_like(x2)), jnp.where(lo, jnp.zeros_like(x2), x2)


def _lo_mask():
    return lax.broadcasted_iota(jnp.int32, (1, LANES), 1) < HEAD_DIM


def _scores(qh, k2, cqh, ckh, causal):
    s = lax.dot_general(qh, k2, NT, preferred_element_type=F32)
    return jnp.where(causal, s + cqh - ckh, NEG)


def attn_fwd(q, k, v, cqp, ckr):
    tp, d = q.shape
    t = ATT_TILE
    nb = tp // t
    npair = d // LANES

    def body(q_ref, k_ref, v_ref, cq_ref, ck_ref, o_ref, lse_ref):
        qi = pl.program_id(1)
        lo = _lo_mask()
        qa, qb = _pair_split(q_ref[...], lo)
        cq = cq_ref[0]
        cqa, cqb = cq[:, 0:1], cq[:, HEAD_DIM:HEAD_DIM + 1]
        rows = qi * t + lax.broadcasted_iota(jnp.int32, (t, t), 0)
        cols0 = lax.broadcasted_iota(jnp.int32, (t, t), 1)

        def step(kj, carry):
            off = pl.multiple_of(kj * t, t)
            k2 = k_ref[pl.ds(off, t), :]
            v2 = v_ref[pl.ds(off, t), :]
            ck = ck_ref[0, kj]
            causal = cols0 + off <= rows
            out = []
            for hd, (qh, cqh) in enumerate(((qa, cqa), (qb, cqb))):
                m, l, acc = carry[3 * hd:3 * hd + 3]
                s = _scores(qh, k2, cqh, ck[hd:hd + 1, :], causal)
                m_new = jnp.maximum(m, jnp.max(s, axis=-1, keepdims=True))
                alpha = jnp.exp(m - m_new)
                pr = jnp.exp(s - m_new)
                l = alpha * l + jnp.sum(pr, axis=-1, keepdims=True)
                acc = alpha * acc + jnp.dot(pr.astype(BF16), v2, preferred_element_type=F32)
                out += [m_new, l, acc]
            return tuple(out)

        init = (jnp.full((t, 1), NEG, F32), jnp.zeros((t, 1), F32), jnp.zeros((t, LANES), F32)) * 2
        ma, la, acca, mb, lb, accb = lax.fori_loop(0, qi + 1, step, init)
        o_ref[...] = jnp.where(lo, acca / la, accb / lb).astype(BF16)
        lse_ref[0] = jnp.where(lo, ma + jnp.log(la), mb + jnp.log(lb))

    return pl.pallas_call(
        body, name="attn_fwd", grid=(npair, nb),
        out_shape=[jax.ShapeDtypeStruct((tp, d), BF16), jax.ShapeDtypeStruct((npair, tp, LANES), F32)],
        in_specs=[pl.BlockSpec((t, LANES), lambda h, i: (i, h)), pl.BlockSpec((tp, LANES), lambda h, i: (0, h)),
                  pl.BlockSpec((tp, LANES), lambda h, i: (0, h)),
                  pl.BlockSpec((1, t, LANES), lambda h, i: (h, i, 0)),
                  pl.BlockSpec((1, nb, 2, t), lambda h, i: (h, 0, 0, 0))],
        out_specs=[pl.BlockSpec((t, LANES), lambda h, i: (i, h)), pl.BlockSpec((1, t, LANES), lambda h, i: (h, i, 0))],
        compiler_params=_cparams(2),
    )(q, k, v, cqp, ckr)


def attn_bwd_dq(q, k, v, cqp, ckr, do, lse):
    tp, d = q.shape
    t = ATT_TILE
    nb = tp // t
    npair = d // LANES

    def body(q_ref, k_ref, v_ref, cq_ref, ck_ref, do_ref, lse_ref, dq_ref, delta_ref):
        qi = pl.program_id(1)
        lo = _lo_mask()
        qa, qb = _pair_split(q_ref[...], lo)
        doa, dob = _pair_split(do_ref[...], lo)
        cq = cq_ref[0]
        lse2 = lse_ref[0]
        heads = ((qa, doa, cq[:, 0:1], lse2[:, 0:1]),
                 (qb, dob, cq[:, HEAD_DIM:HEAD_DIM + 1], lse2[:, HEAD_DIM:HEAD_DIM + 1]))
        rows = qi * t + lax.broadcasted_iota(jnp.int32, (t, t), 0)
        cols0 = lax.broadcasted_iota(jnp.int32, (t, t), 1)

        def p_dp(kj, hd):
            qh, doh, cqh, lseh = heads[hd]
            off = pl.multiple_of(kj * t, t)
            k2 = k_ref[pl.ds(off, t), :]
            ck = ck_ref[0, kj]
            pr = jnp.exp(_scores(qh, k2, cqh, ck[hd:hd + 1, :], cols0 + off <= rows) - lseh)
            dpr = lax.dot_general(doh, v_ref[pl.ds(off, t), :], NT, preferred_element_type=F32)
            return pr, dpr, k2

        def delta_step(kj, carry):
            out = []
            for hd in range(2):
                pr, dpr, _ = p_dp(kj, hd)
                out.append(carry[hd] + jnp.sum(pr * dpr, axis=-1, keepdims=True))
            return tuple(out)

        dl = lax.fori_loop(0, qi + 1, delta_step, (jnp.zeros((t, 1), F32),) * 2)
        delta_ref[0] = jnp.where(lo, dl[0], dl[1])

        def step(kj, carry):
            out = []
            for hd in range(2):
                pr, dpr, k2 = p_dp(kj, hd)
                ds = (pr * (dpr - dl[hd])).astype(BF16)
                out.append(carry[hd] + jnp.dot(ds, k2, preferred_element_type=F32))
            return tuple(out)

        dqa, dqb = lax.fori_loop(0, qi + 1, step, (jnp.zeros((t, LANES), F32),) * 2)
        dq_ref[...] = jnp.where(lo, dqa, dqb)

    pair_rows = pl.BlockSpec((1, t, LANES), lambda h, i: (h, i, 0))
    blk = pl.BlockSpec((t, LANES), lambda h, i: (i, h))
    full = pl.BlockSpec((tp, LANES), lambda h, i: (0, h))
    return pl.pallas_call(
        body, name="attn_bwd_dq", grid=(npair, nb),
        out_shape=[jax.ShapeDtypeStruct((tp, d), F32), jax.ShapeDtypeStruct((npair, tp, LANES), F32)],
        in_specs=[blk, full, full, pair_rows, pl.BlockSpec((1, nb, 2, t), lambda h, i: (h, 0, 0, 0)), blk, pair_rows],
        out_specs=[blk, pair_rows],
        compiler_params=_cparams(2),
    )(q, k, v, cqp, ckr, do, lse)


def attn_bwd_dkv(q, k, v, cqp, ckr, do, lse, delta):
    tp, d = q.shape
    t = ATT_TILE
    nb = tp // t
    npair = d // LANES

    def body(q_ref, k_ref, v_ref, cq_ref, ck_ref, do_ref, lse_ref, delta_ref, dk_ref, dv_ref, dck_ref):
        kj = pl.program_id(1)
        lo = _lo_mask()
        k2 = k_ref[...]
        v2 = v_ref[...]
        ck = ck_ref[0, 0]
        cols = kj * t + lax.broadcasted_iota(jnp.int32, (t, t), 1)
        rows0 = lax.broadcasted_iota(jnp.int32, (t, t), 0)

        def step(qi, carry):
            dk, dv, dca, dcb = carry
            off = pl.multiple_of(qi * t, t)
            qa, qb = _pair_split(q_ref[pl.ds(off, t), :], lo)
            doa, dob = _pair_split(do_ref[pl.ds(off, t), :], lo)
            cq = cq_ref[0, pl.ds(off, t), :]
            lse2 = lse_ref[0, pl.ds(off, t), :]
            dl2 = delta_ref[0, pl.ds(off, t), :]
            causal = cols <= rows0 + off
            dcs = []
            for hd, (qh, doh) in enumerate(((qa, doa), (qb, dob))):
                c0 = hd * HEAD_DIM
                pr = jnp.exp(_scores(qh, k2, cq[:, c0:c0 + 1], ck[hd:hd + 1, :], causal) - lse2[:, c0:c0 + 1])
                dpr = lax.dot_general(doh, v2, NT, preferred_element_type=F32)
                ds = pr * (dpr - dl2[:, c0:c0 + 1])
                dv = dv + lax.dot_general(pr.astype(BF16), doh, TN, preferred_element_type=F32)
                dk = dk + lax.dot_general(ds.astype(BF16), qh, TN, preferred_element_type=F32)
                dcs.append(jnp.sum(ds, axis=0, keepdims=True))
            return dk, dv, dca - dcs[0], dcb - dcs[1]

        zero = jnp.zeros((t, LANES), F32)
        zrow = jnp.zeros((1, t), F32)
        dk, dv, dca, dcb = lax.fori_loop(kj, nb, step, (zero, zero, zrow, zrow))
        dk_ref[...] = dk
        dv_ref[...] = dv.astype(BF16)
        dck_ref[0, 0, 0:1, :] = dca
        dck_ref[0, 0, 1:2, :] = dcb

    full = pl.BlockSpec((tp, LANES), lambda h, j: (0, h))
    pair_full = pl.BlockSpec((1, tp, LANES), lambda h, j: (h, 0, 0))
    blk = pl.BlockSpec((t, LANES), lambda h, j: (j, h))
    ckblk = pl.BlockSpec((1, 1, 2, t), lambda h, j: (h, j, 0, 0))
    return pl.pallas_call(
        body, name="attn_bwd_dkv", grid=(npair, nb),
        out_shape=[jax.ShapeDtypeStruct((tp, d), F32), jax.ShapeDtypeStruct((tp, d), BF16),
                   jax.ShapeDtypeStruct((npair, nb, 2, t), F32)],
        in_specs=[full, blk, blk, pair_full, ckblk, full, pair_full, pair_full],
        out_specs=[blk, blk, ckblk],
        compiler_params=_cparams(2),
    )(q, k, v, cqp, ckr, do, lse, delta)


def rev_cumsum_rows(x):
    r, tp = x.shape
    t = ATT_TILE
    nb = tp // t

    def body(x_ref, o_ref, carry):
        @pl.when(pl.program_id(0) == 0)
        def _():
            carry[...] = jnp.zeros_like(carry)

        xv = x_ref[...]
        r_ = lax.broadcasted_iota(jnp.int32, (t, t), 0)
        c_ = lax.broadcasted_iota(jnp.int32, (t, t), 1)
        o_ref[...] = jnp.dot(xv, (r_ >= c_).astype(F32), precision=HI, preferred_element_type=F32) + carry[:, 0:1]
        carry[...] += jnp.sum(xv, axis=1, keepdims=True)

    return pl.pallas_call(
        body, name="rev_cumsum_rows", grid=(nb,), out_shape=jax.ShapeDtypeStruct((r, tp), F32),
        in_specs=[pl.BlockSpec((r, t), lambda i: (0, nb - 1 - i))],
        out_specs=pl.BlockSpec((r, t), lambda i: (0, nb - 1 - i)),
        scratch_shapes=[pltpu.VMEM((r, LANES), F32)],
        compiler_params=_cparams(1),
    )(x)


def c_elem_bwd(p, b_f, qg, kg, dq, dk, dv, dlogf):
    tp, n_out = p.shape
    d = qg.shape[1]
    tm = ATT_TILE
    n = tp // tm
    seg, seg_t, fold = _head_masks(d)

    def body(p_ref, pf_ref, bf_ref, qg_ref, kg_ref, dq_ref, dk_ref, dv_ref, dlf_ref, seg_ref, segt_ref, fold_ref,
             dp_ref, sm_ref, accq, acck, accf):
        i = pl.program_id(0)

        @pl.when(i == 0)
        def _():
            accq[...] = jnp.zeros_like(accq)
            acck[...] = jnp.zeros_like(acck)
            accf[...] = jnp.zeros_like(accf)

        def norm_bwd(x, g, dy, acc):
            ms = jnp.dot(x * x, seg_ref[...], precision=HI, preferred_element_type=F32) * (1.0 / HEAD_DIM)
            r = jnp.dot(lax.rsqrt(ms + EPS), segt_ref[...], precision=HI, preferred_element_type=F32)
            xhat = x * r
            acc[0:1, :] += jnp.sum(dy * xhat, axis=0, keepdims=True)
            gy = dy * g
            mean = jnp.dot(jnp.dot(gy * xhat, seg_ref[...], precision=HI, preferred_element_type=F32),
                           segt_ref[...], precision=HI, preferred_element_type=F32) * (1.0 / HEAD_DIM)
            return r * (gy - xhat * mean)

        dp_ref[:, 0:d] = norm_bwd(p_ref[:, 0:d], qg_ref[...], dq_ref[...] * (HEAD_DIM ** -0.5), accq).astype(BF16)
        dp_ref[:, d:2 * d] = norm_bwd(p_ref[:, d:2 * d], kg_ref[...], dk_ref[...], acck).astype(BF16)
        dp_ref[:, 2 * d:3 * d] = dv_ref[...]
        df = dlf_ref[...] * _sigmoid(-(pf_ref[...] + bf_ref[...]))
        accf[0:1, :] += jnp.sum(df, axis=0, keepdims=True)
        dp_ref[:, 3 * d:] = df.astype(BF16)

        @pl.when(i == n - 1)
        def _():
            sm_ref[...] = jnp.zeros_like(sm_ref)
            sm_ref[0:1, :] = jnp.dot(accq[0:1, :], fold_ref[...], precision=HI, preferred_element_type=F32)
            sm_ref[1:2, :] = jnp.dot(acck[0:1, :], fold_ref[...], precision=HI, preferred_element_type=F32)
            sm_ref[2:3, :] = accf[0:1, :]

    vec = pl.BlockSpec((1, d), lambda i: (0, 0))
    rowd = pl.BlockSpec((tm, d), lambda i: (i, 0))
    rowl = pl.BlockSpec((tm, LANES), lambda i: (i, 0))
    return pl.pallas_call(
        body, name="c_elem_bwd", grid=(n,),
        out_shape=[jax.ShapeDtypeStruct((tp, n_out), BF16), jax.ShapeDtypeStruct((8, LANES), F32)],
        in_specs=[pl.BlockSpec((tm, 2 * d), lambda i: (i, 0)), pl.BlockSpec((tm, LANES), lambda i: (i, 3 * d // LANES)),
                  pl.BlockSpec((1, LANES), lambda i: (0, 0)), vec, vec, rowd, rowd, rowd, rowl,
                  pl.BlockSpec((d, LANES), lambda i: (0, 0)), pl.BlockSpec((LANES, d), lambda i: (0, 0)),
                  pl.BlockSpec((d, LANES), lambda i: (0, 0))],
        out_specs=[pl.BlockSpec((tm, n_out), lambda i: (i, 0)), pl.BlockSpec((8, LANES), lambda i: (0, 0))],
        scratch_shapes=[pltpu.VMEM((8, d), F32), pltpu.VMEM((8, d), F32), pltpu.VMEM((8, LANES), F32)],
        compiler_params=_cparams(1),
    )(p, p, b_f, qg, kg, dq, dk, dv, dlogf, seg, seg_t, fold)


def loss_head(h, tgt, t_real):
    tp, d = h.shape
    tm = _pick(tp, (768, 512, 256))

    def body(h_ref, t_ref, dh_ref, l_ref):
        i = pl.program_id(0)

        @pl.when(i == 0)
        def _():
            l_ref[...] = jnp.zeros_like(l_ref)

        row = i * tm + lax.broadcasted_iota(jnp.int32, (tm, 1), 0)
        valid = (row >= N_META) & (row < t_real)
        e = jnp.where(valid, h_ref[...] - t_ref[...], 0.0)
        dh_ref[...] = e * (1.0 / d)
        per_row = jnp.sum(e * e, axis=-1, keepdims=True) * (1.0 / d)
        l_ref[...] += 0.5 * jnp.sum(per_row, axis=0, keepdims=True)

    return pl.pallas_call(
        body, name="loss_head", grid=(tp // tm,),
        out_shape=[jax.ShapeDtypeStruct((tp, d), F32), jax.ShapeDtypeStruct((8, LANES), F32)],
        in_specs=[pl.BlockSpec((tm, d), lambda i: (i, 0)), pl.BlockSpec((tm, d), lambda i: (i, 0))],
        out_specs=[pl.BlockSpec((tm, d), lambda i: (i, 0)), pl.BlockSpec((8, LANES), lambda i: (0, 0))],
        compiler_params=_cparams(1),
    )(h, tgt)


def sum_devices(x):
    _, r, c = x.shape

    def body(x_ref, o_ref):
        acc = x_ref[0]
        for dev in range(1, N_DEV):
            acc = acc + x_ref[dev]
        o_ref[...] = acc

    return pl.pallas_call(
        body, name="sum_devices", out_shape=jax.ShapeDtypeStruct((r, c), F32),
        in_specs=[pl.BlockSpec(memory_space=pltpu.VMEM)], out_specs=pl.BlockSpec(memory_space=pltpu.VMEM),
    )(x)


def _adamw_math(w, g, m, v):
    m = ADAM_B1 * m + (1.0 - ADAM_B1) * g
    v = ADAM_B2 * v + (1.0 - ADAM_B2) * (g * g)
    m_hat = m / (1.0 - ADAM_B1 ** ADAM_STEP)
    v_hat = v / (1.0 - ADAM_B2 ** ADAM_STEP)
    delta = -ADAM_LR * (m_hat / (jnp.sqrt(v_hat) + ADAM_EPS) + ADAM_WD * w)
    return delta, m, v


def adamw_small(w, g, m, v):
    def body(w_ref, g_ref, m_ref, v_ref, d_ref, nm_ref, nv_ref):
        d_ref[...], nm_ref[...], nv_ref[...] = _adamw_math(w_ref[...], g_ref[...], m_ref[...], v_ref[...])

    vm = pl.BlockSpec(memory_space=pltpu.VMEM)
    return pl.pallas_call(
        body, name="adamw_small", out_shape=[jax.ShapeDtypeStruct(w.shape, F32)] * 3,
        in_specs=[vm] * 4, out_specs=[vm] * 3,
    )(w, g, m, v)


def adamw_reduce(w, m, v, parts):
    r, c = w.shape
    tr = _pick(r, (512, 384, 352, 256, 128, 8))

    def body(w_ref, m_ref, v_ref, p_ref, g_ref, d_ref, nm_ref, nv_ref):
        g = p_ref[0].astype(F32)
        for dev in range(1, N_DEV):
            g = g + p_ref[dev].astype(F32)
        g_ref[...] = g
        d_ref[...], nm_ref[...], nv_ref[...] = _adamw_math(w_ref[...], g, m_ref[...], v_ref[...])

    blk = pl.BlockSpec((tr, c), lambda i: (i, 0))
    return pl.pallas_call(
        body, name="adamw_reduce", grid=(r // tr,), out_shape=[jax.ShapeDtypeStruct((r, c), F32)] * 4,
        in_specs=[blk, blk, blk, pl.BlockSpec((N_DEV, tr, c), lambda i: (0, i, 0))],
        out_specs=[blk] * 4, compiler_params=_cparams(1),
    )(w, m, v, parts)


def _unshard(g, axis):
    g = jnp.moveaxis(g, 0, axis)
    return g.reshape(g.shape[:axis] + (g.shape[axis] * g.shape[axis + 1],) + g.shape[axis + 2:])


def _shard(full, axis):
    s = full.shape
    g = full.reshape(s[:axis] + (N_DEV, s[axis] // N_DEV) + s[axis + 1:])
    return jnp.moveaxis(g, axis, 0)


def _pad_lanes(a, n=LANES):
    flat = a.reshape(-1)
    pad = (-flat.shape[0]) % n
    return jnp.pad(flat, (0, pad)).reshape(-1, n)


BIG = ("ffn_w_gate", "ffn_w_up", "ffn_w_down", "a_w_in", "a_w_out", "b_w_in", "b_w_out", "c_w_in", "c_w_out")
SHARD_AXIS = {"ffn_w_gate": 3, "ffn_w_up": 3, "ffn_w_down": 2, "a_w_in": 2, "a_w_out": 1, "b_w_in": 2,
              "b_w_out": 1, "c_w_in": 2, "c_w_out": 1, "meta": 1, "ffn_norm": 2, "a_conv": 2, "b_conv": 2}
SMALL_SHARDED = ("meta", "ffn_norm", "a_conv", "b_conv")
SMALL_REPL = ("mix_norm", "b_conv_bias", "b_ln_g", "b_ln_b", "c_b_f", "c_q_norm", "c_k_norm")
WEIGHTS = ("meta", "ffn_norm", "ffn_w_gate", "ffn_w_up", "ffn_w_down", "mix_norm", "a_w_in", "a_conv", "a_w_out",
           "b_w_in", "b_conv", "b_conv_bias", "b_ln_g", "b_ln_b", "b_w_out", "c_w_in", "c_b_f", "c_q_norm",
           "c_k_norm", "c_w_out")
N_MIXERS = 3


def kernel(x, meta, ffn_norm, ffn_w_gate, ffn_w_up, ffn_w_down, mix_norm, a_w_in, a_conv, a_w_out, b_w_in, b_conv, b_conv_bias, b_ln_g, b_ln_b, b_w_out, c_w_in, c_b_f, c_q_norm, c_k_norm, c_w_out, loss_target, m_meta, m_ffn_norm, m_ffn_w_gate, m_ffn_w_up, m_ffn_w_down, m_mix_norm, m_a_w_in, m_a_conv, m_a_w_out, m_b_w_in, m_b_conv, m_b_conv_bias, m_b_ln_g, m_b_ln_b, m_b_w_out, m_c_w_in, m_c_b_f, m_c_q_norm, m_c_k_norm, m_c_w_out, v_meta, v_ffn_norm, v_ffn_w_gate, v_ffn_w_up, v_ffn_w_down, v_mix_norm, v_a_w_in, v_a_conv, v_a_w_out, v_b_w_in, v_b_conv, v_b_conv_bias, v_b_ln_g, v_b_ln_b, v_b_w_out, v_c_w_in, v_c_b_f, v_c_q_norm, v_c_k_norm, v_c_w_out):
    local = dict(locals())
    w = {n: local[n] for n in WEIGHTS}
    mom = {n: local["m_" + n] for n in WEIGHTS}
    var = {n: local["v_" + n] for n in WEIGHTS}
    d = x.shape[-1]
    depth = ffn_norm.shape[0]
    n_heads = d // HEAD_DIM
    seq = x.shape[1]
    t_real = N_META + seq
    tp = -(-t_real // ROW_ALIGN) * ROW_ALIGN
    nb = tp // ATT_TILE
    npair = d // LANES

    names = BIG + SMALL_SHARDED
    gathered = all_gather([w[n].astype(BF16) for n in BIG] + [w[n] for n in SMALL_SHARDED])
    full = {n: _unshard(g, SHARD_AXIS[n]) for n, g in zip(names, gathered)}
    n_c = full["c_w_in"].shape[-1]
    n_cp = 3 * d + LANES
    c_w_qkv = jnp.pad(full["c_w_in"], ((0, 0), (0, 0), (0, n_cp - n_c)))
    b_f_pad = jnp.pad(c_b_f, ((0, 0), (0, LANES - n_heads)))
    qg_t = jnp.tile(c_q_norm, (1, n_heads))
    kg_t = jnp.tile(c_k_norm, (1, n_heads))

    h = jnp.concatenate([full["meta"], x[0], jnp.zeros((tp - t_real, d), F32)], axis=0)
    saved = []
    for i in range(depth):
        mixer, j = i % N_MIXERS, i // N_MIXERS
        s = {"h0": h}
        h, s["gate_a"], s["up_a"] = ffn_fwd(h, full["ffn_norm"][i, 0:1], full["ffn_w_gate"][i, 0],
                                            full["ffn_w_up"][i, 0], full["ffn_w_down"][i, 0])
        s["h1"] = h
        g_mix = mix_norm[i:i + 1]
        if mixer == 0:
            s["p"], s["u"] = norm_matmul(h, g_mix, full["a_w_in"][j], BF16, True)
            s["y"] = a_elem_fwd(s["p"], full["a_conv"][j])
            h = res_matmul(h, s["y"], full["a_w_out"][j])
        elif mixer == 1:
            s["p"], s["u"] = norm_matmul(h, g_mix, full["b_w_in"][j], BF16, True)
            s["y"] = b_elem_fwd(s["p"], full["b_conv"][j], b_conv_bias[j:j + 1], b_ln_g[j:j + 1], b_ln_b[j:j + 1])
            h = res_matmul(h, s["y"], full["b_w_out"][j])
        else:
            s["p"], s["u"] = norm_matmul(h, g_mix, c_w_qkv[j], F32, True)
            s["q"], s["k"], s["v"], cum = c_prep(s["p"], b_f_pad[j:j + 1], qg_t[j:j + 1], kg_t[j:j + 1])
            cum_h = cum[:, :n_heads]
            s["cqp"] = jnp.repeat(cum_h.reshape(tp, npair, 2).transpose(1, 0, 2), HEAD_DIM, axis=2)
            s["ckr"] = cum_h.reshape(nb, ATT_TILE, npair, 2).transpose(2, 0, 3, 1)
            s["y"], s["lse"] = attn_fwd(s["q"], s["k"], s["v"], s["cqp"], s["ckr"])
            h = res_matmul(h, s["y"], full["c_w_out"][j])
        s["h2"] = h
        h, s["gate_b"], s["up_b"] = ffn_fwd(h, full["ffn_norm"][i, 1:2], full["ffn_w_gate"][i, 1],
                                            full["ffn_w_up"][i, 1], full["ffn_w_down"][i, 1])
        saved.append(s)

    tgt = jnp.concatenate([jnp.zeros((N_META, d), F32), loss_target[0], jnp.zeros((tp - t_real, d), F32)], axis=0)
    dh, loss_part = loss_head(h, tgt, t_real)

    gfull = {n: [None] * full[n].shape[0] for n in ("a_w_in", "a_w_out", "b_w_in", "b_w_out", "c_w_in", "c_w_out")}
    gffn = {n: [[None, None] for _ in range(depth)] for n in ("ffn_w_gate", "ffn_w_up", "ffn_w_down")}
    g_ffn_norm = [[None, None] for _ in range(depth)]
    g_mix_norm = [None] * depth
    g_a_conv = [None] * a_conv.shape[0]
    g_b_conv = [None] * b_conv.shape[0]
    g_b_stats = [None] * b_conv.shape[0]
    g_c_small = [None] * c_b_f.shape[0]

    def ffn_backward(dh_out, i, half, h_in, gate, up):
        dh_in, hn, dgate, dup, act, dg = ffn_bwd(dh_out, h_in, full["ffn_norm"][i, half:half + 1], gate, up,
                                                 full["ffn_w_gate"][i, half], full["ffn_w_up"][i, half],
                                                 full["ffn_w_down"][i, half])
        gffn["ffn_w_gate"][i][half] = atb(hn, dgate)
        gffn["ffn_w_up"][i][half] = atb(hn, dup)
        gffn["ffn_w_down"][i][half] = atb(act, dh_out, 0.5)
        g_ffn_norm[i][half] = dg
        return dh_in

    for i in reversed(range(depth)):
        mixer, j = i % N_MIXERS, i // N_MIXERS
        s = saved[i]
        g_mix = mix_norm[i:i + 1]
        dh = ffn_backward(dh, i, 1, s["h2"], s["gate_b"], s["up_b"])
        if mixer == 0:
            dy = matmul_nt(dh, full["a_w_out"][j])
            dp, dwc = a_elem_bwd(s["p"], dy, full["a_conv"][j])
            g_a_conv[j] = dwc[:a_conv.shape[1]]
            gfull["a_w_out"][j] = atb(s["y"], dh)
            gfull["a_w_in"][j] = atb(s["u"], dp)
            dh, g_mix_norm[i] = matmul_nt_rms_bwd(dp, full["a_w_in"][j], s["h1"], g_mix, dh)
        elif mixer == 1:
            dy = matmul_nt(dh, full["b_w_out"][j])
            du2, g_b_stats[j] = b_elem_bwd1(s["p"], dy, full["b_conv"][j], b_conv_bias[j:j + 1],
                                            b_ln_g[j:j + 1], b_ln_b[j:j + 1])
            dp, dwc = b_elem_bwd2(s["p"], du2, full["b_conv"][j])
            g_b_conv[j] = dwc[:b_conv.shape[1]]
            gfull["b_w_out"][j] = atb(s["y"], dh)
            gfull["b_w_in"][j] = atb(s["u"], dp)
            dh, g_mix_norm[i] = matmul_nt_rms_bwd(dp, full["b_w_in"][j], s["h1"], g_mix, dh)
        else:
            do = matmul_nt(dh, full["c_w_out"][j])
            dq, delta = attn_bwd_dq(s["q"], s["k"], s["v"], s["cqp"], s["ckr"], do, s["lse"])
            dk, dv, dck = attn_bwd_dkv(s["q"], s["k"], s["v"], s["cqp"], s["ckr"], do, s["lse"], delta)
            dck_rows = dck.transpose(0, 2, 1, 3).reshape(n_heads, tp)
            dlogf = rev_cumsum_rows(dck_rows)
            dlogf = jnp.pad(dlogf.T, ((0, 0), (0, LANES - n_heads)))
            dp, g_c_small[j] = c_elem_bwd(s["p"], b_f_pad[j:j + 1], qg_t[j:j + 1], kg_t[j:j + 1], dq, dk, dv, dlogf)
            gfull["c_w_out"][j] = atb(s["y"], dh)
            gfull["c_w_in"][j] = atb(s["u"], dp)[:, :n_c]
            dh, g_mix_norm[i] = matmul_nt_rms_bwd(dp, c_w_qkv[j], s["h1"], g_mix, dh)
        dh = ffn_backward(dh, i, 0, s["h0"], s["gate_a"], s["up_a"])

    grad_x = dh[N_META:t_real][None]

    gbig = {n: jnp.stack([jnp.stack(r) for r in gffn[n]]) for n in gffn}
    gbig.update({n: jnp.stack(gfull[n]) for n in gfull})
    parts = exchange([_shard(gbig[n], SHARD_AXIS[n]).astype(BF16) for n in BIG])
    out_g, out_d, out_m, out_v = {}, {}, {}, {}
    for n, prt in zip(BIG, parts):
        shp = w[n].shape
        flat = lambda a: a.reshape(-1, shp[-1])
        res = adamw_reduce(flat(w[n]), flat(mom[n]), flat(var[n]), prt.reshape(N_DEV, -1, shp[-1]))
        out_g[n], out_d[n], out_m[n], out_v[n] = [r.reshape(shp) for r in res]

    def rows(a):
        return a.reshape(-1, d)

    def lane_rows(a):
        return jnp.pad(a.reshape(1, -1), ((0, 0), (0, d - a.size)))

    c_small = jnp.stack(g_c_small)
    pieces = [("meta", dh[:N_META]),
              ("ffn_norm", rows(jnp.stack([jnp.stack(r) for r in g_ffn_norm]))),
              ("mix_norm", rows(jnp.stack(g_mix_norm))),
              ("a_conv", rows(jnp.stack(g_a_conv))),
              ("b_conv", rows(jnp.stack(g_b_conv))),
              ("b_ln_g", rows(jnp.stack([st[0] for st in g_b_stats]))),
              ("b_ln_b", rows(jnp.stack([st[1] for st in g_b_stats]))),
              ("b_conv_bias", rows(jnp.stack([st[2] for st in g_b_stats]))),
              ("c_q_norm", lane_rows(c_small[:, 0, :HEAD_DIM])),
              ("c_k_norm", lane_rows(c_small[:, 1, :HEAD_DIM])),
              ("c_b_f", lane_rows(c_small[:, 2, :n_heads])),
              ("loss", lane_rows(loss_part[0:1, 0:1]))]
    packed = jnp.concatenate([pc for _, pc in pieces], axis=0)
    n_rows = packed.shape[0]
    packed = jnp.pad(packed, ((0, (-n_rows) % 8), (0, 0)))
    total = sum_devices(all_gather([packed])[0])
    small_g, r0 = {}, 0
    for n, pc in pieces:
        small_g[n] = total[r0:r0 + pc.shape[0]]
        r0 += pc.shape[0]
    loss = small_g.pop("loss")[0, 0]
    me = 4 * lax.axis_index("x") + 2 * lax.axis_index("y") + lax.axis_index("c")
    for n in SMALL_SHARDED:
        cols = w[n].shape[-1]
        g = lax.dynamic_slice_in_dim(small_g[n], me * cols, cols, axis=1)
        out_g[n] = g.reshape(w[n].shape)
    for n in SMALL_REPL:
        out_g[n] = small_g[n].reshape(-1)[:w[n].size].reshape(w[n].shape)
    small = SMALL_SHARDED + SMALL_REPL
    pack = lambda dct: jnp.concatenate([_pad_lanes(dct[n]) for n in small], axis=0)
    res = adamw_small(pack(w), pack(out_g), pack(mom), pack(var))
    r0 = 0
    for n in small:
        nr = -(-w[n].size // LANES)
        for dct, arr in zip((out_d, out_m, out_v), res):
            dct[n] = arr[r0:r0 + nr].reshape(-1)[:w[n].size].reshape(w[n].shape)
        r0 += nr

    return (loss, grad_x, *[out_g[n] for n in WEIGHTS], *[out_d[n] for n in WEIGHTS],
            *[out_m[n] for n in WEIGHTS], *[out_v[n] for n in WEIGHTS])
```

```python
import functools

import jax
import jax.numpy as jnp
from jax import lax
from jax.experimental import pallas as pl
from jax.experimental.pallas import tpu as pltpu

F32 = jnp.float32
BF16 = jnp.bfloat16
HI = lax.Precision.HIGHEST
EPS = 1e-6
NEG = -1e30
N_META = 16
HEAD_DIM = 64
LANES = 128
N_DEV = 8
ROW_ALIGN = 256
VMEM_LIMIT = 56 * 1024 * 1024
ADAM_LR, ADAM_B1, ADAM_B2, ADAM_EPS, ADAM_WD, ADAM_STEP = 0.001, 0.9, 0.999, 1e-08, 0.01, 10
MESH = pl.DeviceIdType.MESH
NT = (((1,), (1,)), ((), ()))
TN = (((0,), (0,)), ((), ()))


def _cparams(n_axes):
    return pltpu.CompilerParams(dimension_semantics=("arbitrary",) * n_axes, vmem_limit_bytes=VMEM_LIMIT)


def _pick(n, cands):
    for c in cands:
        if n % c == 0:
            return c
    raise ValueError(f"no tile for {n} among {cands}")


def _sigmoid(x):
    return 1.0 / (1.0 + jnp.exp(-x))


def _rms(x):
    r = lax.rsqrt(jnp.mean(x * x, axis=-1, keepdims=True) + EPS)
    return x * r, r


def _rms_bwd(xhat, r, g, dy):
    gy = dy * g
    dx = r * (gy - xhat * jnp.mean(gy * xhat, axis=-1, keepdims=True))
    return dx, jnp.sum(dy * xhat, axis=0, keepdims=True)


def _mesh_pos():
    return lax.axis_index("x"), lax.axis_index("y"), lax.axis_index("c")


def all_gather(arrs):
    n = len(arrs)
    hbm = pl.BlockSpec(memory_space=pltpu.HBM)

    def body(*refs):
        ins, outs = refs[:n], refs[n:2 * n]
        send_sems, recv_sems, local_sems = refs[2 * n:]
        x, y, c = _mesh_pos()
        me, sibling = (x, y, c), (x, y, 1 - c)
        chips = [(1 - x, y), (x, 1 - y), (1 - x, 1 - y)]

        def slot(a, px, py, pc):
            return outs[a].at[4 * px + 2 * py + pc]

        def copy(a, k, block, to, src=None):
            return pltpu.make_async_remote_copy(
                src_ref=slot(a, *block) if src is None else src, dst_ref=slot(a, *block),
                send_sem=send_sems.at[7 * a + k], recv_sem=recv_sems.at[7 * a + k],
                device_id=to, device_id_type=MESH)

        own, first, passed = [], [], []
        for a in range(n):
            cp = pltpu.make_async_copy(ins[a], slot(a, *me), local_sems.at[a])
            cp.start()
            own.append(cp)
            first.append(copy(a, 0, me, sibling, src=ins[a]))
            first += [copy(a, 1 + j, me, (*chip, c), src=ins[a]) for j, chip in enumerate(chips)]
        for cp in first:
            cp.start()
        for j, chip in enumerate(chips):
            for a in range(n):
                copy(a, 1 + j, (*chip, c), me).wait_recv()
                cp = copy(a, 4 + j, (*chip, c), sibling)
                cp.start()
                passed.append(cp)
        for a in range(n):
            copy(a, 0, sibling, me).wait_recv()
            for j, chip in enumerate(chips):
                copy(a, 4 + j, (*chip, 1 - c), me).wait_recv()
        for cp in first + passed:
            cp.wait_send()
        for cp in own:
            cp.wait()

    return pl.pallas_call(
        body, name="all_gather",
        out_shape=[jax.ShapeDtypeStruct((N_DEV,) + a.shape, a.dtype) for a in arrs],
        in_specs=[hbm] * n, out_specs=[hbm] * n,
        scratch_shapes=[pltpu.SemaphoreType.DMA((7 * n,)), pltpu.SemaphoreType.DMA((7 * n,)),
                        pltpu.SemaphoreType.DMA((n,))],
    )(*arrs)


def exchange(arrs):
    n = len(arrs)
    hbm = pl.BlockSpec(memory_space=pltpu.HBM)

    def body(*refs):
        ins, outs = refs[:n], refs[n:2 * n]
        send_sems, recv_sems, local_sems = refs[2 * n:]
        x, y, c = _mesh_pos()
        me = 4 * x + 2 * y + c
        own, sent = [], []
        for a in range(n):
            cp = pltpu.make_async_copy(ins[a].at[me], outs[a].at[me], local_sems.at[a])
            cp.start()
            own.append(cp)
        for k in range(1, N_DEV):
            px = 1 - x if k & 4 else x
            py = 1 - y if k & 2 else y
            pc = 1 - c if k & 1 else c
            peer = 4 * px + 2 * py + pc
            for a in range(n):
                cp = pltpu.make_async_remote_copy(
                    src_ref=ins[a].at[peer], dst_ref=outs[a].at[me],
                    send_sem=send_sems.at[7 * a + k - 1], recv_sem=recv_sems.at[7 * a + k - 1],
                    device_id=(px, py, pc), device_id_type=MESH)
                cp.start()
                sent.append((cp, a, k, peer))
        for cp, a, k, peer in sent:
            pltpu.make_async_remote_copy(
                src_ref=ins[a].at[peer], dst_ref=outs[a].at[peer],
                send_sem=send_sems.at[7 * a + k - 1], recv_sem=recv_sems.at[7 * a + k - 1],
                device_id=(x, y, c), device_id_type=MESH).wait_recv()
        for cp, a, k, peer in sent:
            cp.wait_send()
        for cp in own:
            cp.wait()

    return pl.pallas_call(
        body, name="exchange",
        out_shape=[jax.ShapeDtypeStruct(a.shape, a.dtype) for a in arrs],
        in_specs=[hbm] * n, out_specs=[hbm] * n,
        scratch_shapes=[pltpu.SemaphoreType.DMA((7 * n,)), pltpu.SemaphoreType.DMA((7 * n,)),
                        pltpu.SemaphoreType.DMA((n,))],
    )(*arrs)


def _resident(shape):
    return pl.BlockSpec(shape, lambda i: (0,) * len(shape), pipeline_mode=pl.Buffered(1))


def ffn_fwd(h, g, wg, wu, wd):
    tp, d = h.shape
    f = wg.shape[1]
    tm = _pick(tp, (384, 256))
    tf = _pick(f, (1408, 1024, 512, 256, 128))

    def body(h_ref, g_ref, wg_ref, wu_ref, wd_ref, ho_ref, gate_ref, up_ref):
        x = h_ref[...]
        xhat, _ = _rms(x)
        hn = (xhat * g_ref[...]).astype(BF16)
        acc = jnp.zeros((tm, d), F32)
        for c0 in range(0, f, tf):
            gate = jnp.dot(hn, wg_ref[:, c0:c0 + tf], preferred_element_type=F32)
            up = jnp.dot(hn, wu_ref[:, c0:c0 + tf], preferred_element_type=F32)
            gate_ref[:, c0:c0 + tf] = gate.astype(BF16)
            up_ref[:, c0:c0 + tf] = up.astype(BF16)
            act = (gate * _sigmoid(gate) * up).astype(BF16)
            acc = acc + jnp.dot(act, wd_ref[c0:c0 + tf, :], preferred_element_type=F32)
        ho_ref[...] = x + 0.5 * acc

    row = lambda n: pl.BlockSpec((tm, n), lambda i: (i, 0))
    return pl.pallas_call(
        body, name="ffn_fwd", grid=(tp // tm,),
        out_shape=[jax.ShapeDtypeStruct((tp, d), F32), jax.ShapeDtypeStruct((tp, f), BF16),
                   jax.ShapeDtypeStruct((tp, f), BF16)],
        in_specs=[row(d), _resident((1, d)), _resident((d, f)), _resident((d, f)), _resident((f, d))],
        out_specs=[row(d), row(f), row(f)],
        compiler_params=_cparams(1),
    )(h, g, wg, wu, wd)


def ffn_bwd(dho, h, g, gate, up, wg, wu, wd):
    tp, d = h.shape
    f = wg.shape[1]
    tm = _pick(tp, (256,))
    tf = _pick(f, (1408, 1024, 512, 256, 128))

    def body(dho_ref, h_ref, g_ref, gate_ref, up_ref, wg_ref, wu_ref, wd_ref,
             dhi_ref, hn_ref, dgate_ref, dup_ref, act_ref, dg_ref):
        @pl.when(pl.program_id(0) == 0)
        def _():
            dg_ref[...] = jnp.zeros_like(dg_ref)

        dho_ = dho_ref[...]
        dout = (0.5 * dho_).astype(BF16)
        dhn = jnp.zeros((tm, d), F32)
        for c0 in range(0, f, tf):
            dact = lax.dot_general(dout, wd_ref[c0:c0 + tf, :], NT, preferred_element_type=F32)
            gt = gate_ref[:, c0:c0 + tf].astype(F32)
            u = up_ref[:, c0:c0 + tf].astype(F32)
            sig = _sigmoid(gt)
            silu = gt * sig
            act_ref[:, c0:c0 + tf] = (silu * u).astype(BF16)
            dup = (dact * silu).astype(BF16)
            dgate = (dact * u * (sig * (1.0 + gt * (1.0 - sig)))).astype(BF16)
            dup_ref[:, c0:c0 + tf] = dup
            dgate_ref[:, c0:c0 + tf] = dgate
            dhn = dhn + (lax.dot_general(dgate, wg_ref[:, c0:c0 + tf], NT, preferred_element_type=F32)
                         + lax.dot_general(dup, wu_ref[:, c0:c0 + tf], NT, preferred_element_type=F32))
        xhat, r = _rms(h_ref[...])
        gg = g_ref[...]
        dx, dgp = _rms_bwd(xhat, r, gg, dhn)
        dg_ref[...] += dgp
        dhi_ref[...] = dho_ + dx
        hn_ref[...] = (xhat * gg).astype(BF16)

    row = lambda n: pl.BlockSpec((tm, n), lambda i: (i, 0))
    return pl.pallas_call(
        body, name="ffn_bwd", grid=(tp // tm,),
        out_shape=[jax.ShapeDtypeStruct((tp, d), F32), jax.ShapeDtypeStruct((tp, d), BF16),
                   jax.ShapeDtypeStruct((tp, f), BF16), jax.ShapeDtypeStruct((tp, f), BF16),
                   jax.ShapeDtypeStruct((tp, f), BF16), jax.ShapeDtypeStruct((1, d), F32)],
        in_specs=[row(d), row(d), _resident((1, d)), row(f), row(f),
                  _resident((d, f)), _resident((d, f)), _resident((f, d))],
        out_specs=[row(d), row(d), row(f), row(f), row(f), pl.BlockSpec((1, d), lambda i: (0, 0))],
        compiler_params=_cparams(1),
    )(dho, h, g, gate, up, wg, wu, wd)


def atb(a, b, scale=1.0):
    tp, m = a.shape
    n = b.shape[1]
    tmm = _pick(m, (1024, 1408, 640, 512, 256, 128))
    tn = _pick(n, (1024, 1408, 640, 512, 256, 128))
    tk = _pick(tp, (768, 512, 256))
    nk = tp // tk

    def body(a_ref, b_ref, o_ref):
        k = pl.program_id(2)

        @pl.when(k == 0)
        def _():
            o_ref[...] = jnp.zeros_like(o_ref)

        o_ref[...] += lax.dot_general(a_ref[...].astype(BF16), b_ref[...].astype(BF16), TN,
                                      preferred_element_type=F32)

        if scale != 1.0:
            @pl.when(k == nk - 1)
            def _():
                o_ref[...] = o_ref[...] * scale

    return pl.pallas_call(
        body, name="atb", grid=(m // tmm, n // tn, nk),
        out_shape=jax.ShapeDtypeStruct((m, n), F32),
        in_specs=[pl.BlockSpec((tk, tmm), lambda i, j, k: (k, i)), pl.BlockSpec((tk, tn), lambda i, j, k: (k, j))],
        out_specs=pl.BlockSpec((tmm, tn), lambda i, j, k: (i, j)),
        compiler_params=_cparams(3),
    )(a, b)


def norm_matmul(h, g, w, out_dtype, emit_u):
    tp, d = h.shape
    n = w.shape[1]
    tm = _pick(tp, (768, 512, 256))
    tn = _pick(n, (1024, 768, 640, 512, 256, 128))

    def body(h_ref, g_ref, w_ref, o_ref, *rest):
        u_sc = rest[-1]

        @pl.when(pl.program_id(1) == 0)
        def _():
            xhat, _ = _rms(h_ref[...])
            u_sc[...] = (xhat * g_ref[...]).astype(BF16)
            if emit_u:
                rest[0][...] = u_sc[...]

        o_ref[...] = jnp.dot(u_sc[...], w_ref[...], preferred_element_type=F32).astype(out_dtype)

    out_shape = [jax.ShapeDtypeStruct((tp, n), out_dtype)]
    out_specs = [pl.BlockSpec((tm, tn), lambda i, j: (i, j))]
    if emit_u:
        out_shape.append(jax.ShapeDtypeStruct((tp, d), BF16))
        out_specs.append(pl.BlockSpec((tm, d), lambda i, j: (i, 0)))
    return pl.pallas_call(
        body, name="norm_matmul", grid=(tp // tm, n // tn), out_shape=out_shape,
        in_specs=[pl.BlockSpec((tm, d), lambda i, j: (i, 0)), pl.BlockSpec((1, d), lambda i, j: (0, 0)),
                  pl.BlockSpec((d, tn), lambda i, j: (0, j))],
        out_specs=out_specs, scratch_shapes=[pltpu.VMEM((tm, d), BF16)],
        compiler_params=_cparams(2),
    )(h, g, w)


def res_matmul(h, y, w):
    tp, d = h.shape
    k = y.shape[1]
    tm = _pick(tp, (768, 512, 256))

    def body(h_ref, y_ref, w_ref, o_ref):
        o_ref[...] = h_ref[...] + jnp.dot(y_ref[...], w_ref[...], preferred_element_type=F32)

    return pl.pallas_call(
        body, name="res_matmul", grid=(tp // tm,), out_shape=jax.ShapeDtypeStruct((tp, d), F32),
        in_specs=[pl.BlockSpec((tm, d), lambda i: (i, 0)), pl.BlockSpec((tm, k), lambda i: (i, 0)),
                  pl.BlockSpec((k, d), lambda i: (0, 0))],
        out_specs=pl.BlockSpec((tm, d), lambda i: (i, 0)),
        compiler_params=_cparams(1),
    )(h, y, w)


def matmul_nt(a, w):
    tp, d = a.shape
    k = w.shape[0]
    tm = _pick(tp, (768, 512, 256))

    def body(a_ref, w_ref, o_ref):
        o_ref[...] = lax.dot_general(a_ref[...].astype(BF16), w_ref[...], NT,
                                     preferred_element_type=F32).astype(BF16)

    return pl.pallas_call(
        body, name="matmul_nt", grid=(tp // tm,), out_shape=jax.ShapeDtypeStruct((tp, k), BF16),
        in_specs=[pl.BlockSpec((tm, d), lambda i: (i, 0)), pl.BlockSpec((k, d), lambda i: (0, 0))],
        out_specs=pl.BlockSpec((tm, k), lambda i: (i, 0)),
        compiler_params=_cparams(1),
    )(a, w)


def matmul_nt_rms_bwd(dp, w, h, g, dres):
    tp, d = h.shape
    n = w.shape[1]
    tm = _pick(tp, (384, 256))

    def body(dp_ref, w_ref, h_ref, g_ref, dres_ref, dh_ref, dg_ref):
        @pl.when(pl.program_id(0) == 0)
        def _():
            dg_ref[...] = jnp.zeros_like(dg_ref)

        du = lax.dot_general(dp_ref[...], w_ref[...], NT, preferred_element_type=F32)
        xhat, r = _rms(h_ref[...])
        dx, dgp = _rms_bwd(xhat, r, g_ref[...], du)
        dg_ref[...] += dgp
        dh_ref[...] = dres_ref[...] + dx

    return pl.pallas_call(
        body, name="matmul_nt_rms_bwd", grid=(tp // tm,),
        out_shape=[jax.ShapeDtypeStruct((tp, d), F32), jax.ShapeDtypeStruct((1, d), F32)],
        in_specs=[pl.BlockSpec((tm, n), lambda i: (i, 0)), pl.BlockSpec((d, n), lambda i: (0, 0)),
                  pl.BlockSpec((tm, d), lambda i: (i, 0)), pl.BlockSpec((1, d), lambda i: (0, 0)),
                  pl.BlockSpec((tm, d), lambda i: (i, 0))],
        out_specs=[pl.BlockSpec((tm, d), lambda i: (i, 0)), pl.BlockSpec((1, d), lambda i: (0, 0))],
        compiler_params=_cparams(1),
    )(dp, w, h, g, dres)


CONV_TILE = (256,)


def _halo_before(tm, hb, col=0):
    return lambda i: (jnp.maximum(i * (tm // hb) - 1, 0), col)


def _halo_after(tm, hb, nblk, col=0):
    return lambda i: (jnp.minimum((i + 1) * (tm // hb), nblk - 1), col)


def a_elem_fwd(p, w):
    tp, d3 = p.shape
    d = d3 // 3
    kw = w.shape[0]
    tm = _pick(tp, CONV_TILE)
    hb = 16

    def body(p_ref, ph_ref, w_ref, y_ref, ext):
        i = pl.program_id(0)
        halo = ph_ref[:, d:2 * d].astype(F32) * ph_ref[:, 2 * d:].astype(F32)
        ext[0:hb, :] = jnp.where(i > 0, halo, 0.0)
        ext[hb:, :] = p_ref[:, d:2 * d].astype(F32) * p_ref[:, 2 * d:].astype(F32)
        z = jnp.zeros((tm, d), F32)
        for k in range(kw):
            z = z + w_ref[k:k + 1, :] * ext[pl.ds(hb - (kw - 1) + k, tm), :]
        y_ref[...] = (p_ref[:, 0:d].astype(F32) * z).astype(BF16)

    return pl.pallas_call(
        body, name="a_elem_fwd", grid=(tp // tm,), out_shape=jax.ShapeDtypeStruct((tp, d), BF16),
        in_specs=[pl.BlockSpec((tm, d3), lambda i: (i, 0)), pl.BlockSpec((hb, d3), _halo_before(tm, hb)),
                  pl.BlockSpec((kw, d), lambda i: (0, 0))],
        out_specs=pl.BlockSpec((tm, d), lambda i: (i, 0)),
        scratch_shapes=[pltpu.VMEM((tm + hb, d), F32)],
        compiler_params=_cparams(1),
    )(p, p, w)


def a_elem_bwd(p, dy, w):
    tp, d3 = p.shape
    d = d3 // 3
    kw = w.shape[0]
    tm = _pick(tp, CONV_TILE)
    hb = 16
    n = tp // tm

    def body(p_ref, ph_ref, pa_ref, dy_ref, dya_ref, w_ref, dp_ref, dw_ref, ext, dzext):
        i = pl.program_id(0)

        @pl.when(i == 0)
        def _():
            dw_ref[...] = jnp.zeros_like(dw_ref)

        b = p_ref[:, 0:d].astype(F32)
        c = p_ref[:, d:2 * d].astype(F32)
        v = p_ref[:, 2 * d:].astype(F32)
        dy_ = dy_ref[...].astype(F32)
        halo = ph_ref[:, d:2 * d].astype(F32) * ph_ref[:, 2 * d:].astype(F32)
        ext[0:hb, :] = jnp.where(i > 0, halo, 0.0)
        ext[hb:, :] = c * v
        dz = dy_ * b
        dzext[0:tm, :] = dz
        dzext[tm:, :] = jnp.where(i < n - 1, dya_ref[...].astype(F32) * pa_ref[:, 0:d].astype(F32), 0.0)
        z = jnp.zeros((tm, d), F32)
        dcv = jnp.zeros((tm, d), F32)
        for k in range(kw):
            sh = ext[pl.ds(hb - (kw - 1) + k, tm), :]
            z = z + w_ref[k:k + 1, :] * sh
            dw_ref[k:k + 1, :] += jnp.sum(dz * sh, axis=0, keepdims=True)
            dcv = dcv + w_ref[k:k + 1, :] * dzext[pl.ds(kw - 1 - k, tm), :]
        dp_ref[:, 0:d] = (dy_ * z).astype(BF16)
        dp_ref[:, d:2 * d] = (dcv * v).astype(BF16)
        dp_ref[:, 2 * d:] = (dcv * c).astype(BF16)

    nb = tp // hb
    return pl.pallas_call(
        body, name="a_elem_bwd", grid=(n,),
        out_shape=[jax.ShapeDtypeStruct((tp, d3), BF16), jax.ShapeDtypeStruct((8, d), F32)],
        in_specs=[pl.BlockSpec((tm, d3), lambda i: (i, 0)), pl.BlockSpec((hb, d3), _halo_before(tm, hb)),
                  pl.BlockSpec((hb, d3), _halo_after(tm, hb, nb)),
                  pl.BlockSpec((tm, d), lambda i: (i, 0)), pl.BlockSpec((hb, d), _halo_after(tm, hb, nb)),
                  pl.BlockSpec((kw, d), lambda i: (0, 0))],
        out_specs=[pl.BlockSpec((tm, d3), lambda i: (i, 0)), pl.BlockSpec((8, d), lambda i: (0, 0))],
        scratch_shapes=[pltpu.VMEM((tm + hb, d), F32), pltpu.VMEM((tm + hb, d), F32)],
        compiler_params=_cparams(1),
    )(p, p, p, dy, dy, w)


def _b_u2(p_ref, ph_ref, w_ref, bias_ref, ext, i, tm, d, kw, hb):
    a = p_ref[:, 0:d].astype(F32)
    sg = _sigmoid(p_ref[:, d:].astype(F32))
    halo = ph_ref[:, 0:d].astype(F32) * _sigmoid(ph_ref[:, d:].astype(F32))
    ext[0:hb, :] = jnp.where(i > 0, halo, 0.0)
    ext[hb:, :] = a * sg
    u2 = jnp.zeros((tm, d), F32) + bias_ref[...]
    for k in range(kw):
        u2 = u2 + w_ref[k:k + 1, :] * ext[pl.ds(hb - (kw - 1) + k, tm), :]
    return a, sg, u2


def _ln(u2):
    mu = jnp.mean(u2, axis=-1, keepdims=True)
    xc = u2 - mu
    rstd = lax.rsqrt(jnp.mean(xc * xc, axis=-1, keepdims=True) + EPS)
    return xc * rstd, rstd


def b_elem_fwd(p, w, bias, ln_g, ln_b):
    tp, d2 = p.shape
    d = d2 // 2
    kw = w.shape[0]
    tm = _pick(tp, CONV_TILE)
    hb = 32

    def body(p_ref, ph_ref, w_ref, bias_ref, g_ref, b_ref, y_ref, ext):
        i = pl.program_id(0)
        _, _, u2 = _b_u2(p_ref, ph_ref, w_ref, bias_ref, ext, i, tm, d, kw, hb)
        xhat, _ = _ln(u2)
        u3 = xhat * g_ref[...] + b_ref[...]
        y_ref[...] = (u3 * _sigmoid(u3)).astype(BF16)

    vec = pl.BlockSpec((1, d), lambda i: (0, 0))
    return pl.pallas_call(
        body, name="b_elem_fwd", grid=(tp // tm,), out_shape=jax.ShapeDtypeStruct((tp, d), BF16),
        in_specs=[pl.BlockSpec((tm, d2), lambda i: (i, 0)), pl.BlockSpec((hb, d2), _halo_before(tm, hb)),
                  pl.BlockSpec((kw, d), lambda i: (0, 0)), vec, vec, vec],
        out_specs=pl.BlockSpec((tm, d), lambda i: (i, 0)),
        scratch_shapes=[pltpu.VMEM((tm + hb, d), F32)],
        compiler_params=_cparams(1),
    )(p, p, w, bias, ln_g, ln_b)


def b_elem_bwd1(p, dy, w, bias, ln_g, ln_b):
    tp, d2 = p.shape
    d = d2 // 2
    kw = w.shape[0]
    tm = _pick(tp, CONV_TILE)
    hb = 32

    def body(p_ref, ph_ref, dy_ref, w_ref, bias_ref, g_ref, b_ref, du2_ref, st_ref, ext):
        i = pl.program_id(0)

        @pl.when(i == 0)
        def _():
            st_ref[...] = jnp.zeros_like(st_ref)

        _, _, u2 = _b_u2(p_ref, ph_ref, w_ref, bias_ref, ext, i, tm, d, kw, hb)
        xhat, rstd = _ln(u2)
        u3 = xhat * g_ref[...] + b_ref[...]
        s3 = _sigmoid(u3)
        du3 = dy_ref[...].astype(F32) * (s3 * (1.0 + u3 * (1.0 - s3)))
        dxh = du3 * g_ref[...]
        du2 = rstd * (dxh - jnp.mean(dxh, axis=-1, keepdims=True)
                      - xhat * jnp.mean(dxh * xhat, axis=-1, keepdims=True))
        du2_ref[...] = du2
        st_ref[0:1, :] += jnp.sum(du3 * xhat, axis=0, keepdims=True)
        st_ref[1:2, :] += jnp.sum(du3, axis=0, keepdims=True)
        st_ref[2:3, :] += jnp.sum(du2, axis=0, keepdims=True)

    vec = pl.BlockSpec((1, d), lambda i: (0, 0))
    return pl.pallas_call(
        body, name="b_elem_bwd1", grid=(tp // tm,),
        out_shape=[jax.ShapeDtypeStruct((tp, d), F32), jax.ShapeDtypeStruct((8, d), F32)],
        in_specs=[pl.BlockSpec((tm, d2), lambda i: (i, 0)), pl.BlockSpec((hb, d2), _halo_before(tm, hb)),
                  pl.BlockSpec((tm, d), lambda i: (i, 0)), pl.BlockSpec((kw, d), lambda i: (0, 0)), vec, vec, vec],
        out_specs=[pl.BlockSpec((tm, d), lambda i: (i, 0)), pl.BlockSpec((8, d), lambda i: (0, 0))],
        scratch_shapes=[pltpu.VMEM((tm + hb, d), F32)],
        compiler_params=_cparams(1),
    )(p, p, dy, w, bias, ln_g, ln_b)


def b_elem_bwd2(p, du2, w):
    tp, d2 = p.shape
    d = d2 // 2
    kw = w.shape[0]
    tm = _pick(tp, CONV_TILE)
    hb = 32
    n = tp // tm

    def body(p_ref, ph_ref, du2_ref, du2a_ref, w_ref, dp_ref, dw_ref, ext, dext):
        i = pl.program_id(0)

        @pl.when(i == 0)
        def _():
            dw_ref[...] = jnp.zeros_like(dw_ref)

        a = p_ref[:, 0:d].astype(F32)
        sg = _sigmoid(p_ref[:, d:].astype(F32))
        halo = ph_ref[:, 0:d].astype(F32) * _sigmoid(ph_ref[:, d:].astype(F32))
        ext[0:hb, :] = jnp.where(i > 0, halo, 0.0)
        ext[hb:, :] = a * sg
        du2_ = du2_ref[...]
        dext[0:tm, :] = du2_
        dext[tm:, :] = jnp.where(i < n - 1, du2a_ref[...], 0.0)
        du1 = jnp.zeros((tm, d), F32)
        for k in range(kw):
            dw_ref[k:k + 1, :] += jnp.sum(du2_ * ext[pl.ds(hb - (kw - 1) + k, tm), :], axis=0, keepdims=True)
            du1 = du1 + w_ref[k:k + 1, :] * dext[pl.ds(kw - 1 - k, tm), :]
        dp_ref[:, 0:d] = (du1 * sg).astype(BF16)
        dp_ref[:, d:] = (du1 * a * sg * (1.0 - sg)).astype(BF16)

    nb = tp // hb
    return pl.pallas_call(
        body, name="b_elem_bwd2", grid=(n,),
        out_shape=[jax.ShapeDtypeStruct((tp, d2), BF16), jax.ShapeDtypeStruct((32, d), F32)],
        in_specs=[pl.BlockSpec((tm, d2), lambda i: (i, 0)), pl.BlockSpec((hb, d2), _halo_before(tm, hb)),
                  pl.BlockSpec((tm, d), lambda i: (i, 0)), pl.BlockSpec((hb, d), _halo_after(tm, hb, nb)),
                  pl.BlockSpec((kw, d), lambda i: (0, 0))],
        out_specs=[pl.BlockSpec((tm, d2), lambda i: (i, 0)), pl.BlockSpec((32, d), lambda i: (0, 0))],
        scratch_shapes=[pltpu.VMEM((tm + hb, d), F32), pltpu.VMEM((tm + hb, d), F32)],
        compiler_params=_cparams(1),
    )(p, p, du2, du2, w)


ATT_TILE = 256


def _head_masks(d):
    c = lax.broadcasted_iota(jnp.int32, (d, LANES), 0)
    h = lax.broadcasted_iota(jnp.int32, (d, LANES), 1)
    seg = (c // HEAD_DIM == h).astype(F32)
    fold = (c % HEAD_DIM == h).astype(F32)
    return seg, seg.T, fold


def c_prep(p, b_f, qg, kg):
    tp = p.shape[0]
    d = qg.shape[1]
    tm = ATT_TILE
    seg, seg_t, _ = _head_masks(d)

    def body(p_ref, pf_ref, bf_ref, qg_ref, kg_ref, seg_ref, segt_ref, q_ref, k_ref, v_ref, cum_ref, carry):
        @pl.when(pl.program_id(0) == 0)
        def _():
            carry[...] = jnp.zeros_like(carry)

        def norm(x, g):
            ms = jnp.dot(x * x, seg_ref[...], precision=HI, preferred_element_type=F32) * (1.0 / HEAD_DIM)
            r = jnp.dot(lax.rsqrt(ms + EPS), segt_ref[...], precision=HI, preferred_element_type=F32)
            return x * r * g

        q_ref[...] = (norm(p_ref[:, 0:d], qg_ref[...]) * (HEAD_DIM ** -0.5)).astype(BF16)
        k_ref[...] = norm(p_ref[:, d:2 * d], kg_ref[...]).astype(BF16)
        v_ref[...] = p_ref[:, 2 * d:].astype(BF16)
        xf = pf_ref[...] + bf_ref[...]
        logf = jnp.minimum(xf, 0.0) - jnp.log(1.0 + jnp.exp(-jnp.abs(xf)))
        r_ = lax.broadcasted_iota(jnp.int32, (tm, tm), 0)
        c_ = lax.broadcasted_iota(jnp.int32, (tm, tm), 1)
        tri = (c_ <= r_).astype(F32)
        cum_ref[...] = jnp.dot(tri, logf, precision=HI, preferred_element_type=F32) + carry[0:1, :]
        carry[0:1, :] += jnp.sum(logf, axis=0, keepdims=True)

    vec = pl.BlockSpec((1, d), lambda i: (0, 0))
    rowd = pl.BlockSpec((tm, d), lambda i: (i, 0))
    return pl.pallas_call(
        body, name="c_prep", grid=(tp // tm,),
        out_shape=[jax.ShapeDtypeStruct((tp, d), BF16)] * 3 + [jax.ShapeDtypeStruct((tp, LANES), F32)],
        in_specs=[pl.BlockSpec((tm, 3 * d), lambda i: (i, 0)), pl.BlockSpec((tm, LANES), lambda i: (i, 3 * d // LANES)),
                  pl.BlockSpec((1, LANES), lambda i: (0, 0)), vec, vec,
                  pl.BlockSpec((d, LANES), lambda i: (0, 0)), pl.BlockSpec((LANES, d), lambda i: (0, 0))],
        out_specs=[rowd, rowd, rowd, pl.BlockSpec((tm, LANES), lambda i: (i, 0))],
        scratch_shapes=[pltpu.VMEM((8, LANES), F32)],
        compiler_params=_cparams(1),
    )(p, p, b_f, qg, kg, seg, seg_t)


def _pair_split(x2, lo):
    return jnp.where(lo, x2, jnp.zeros_like(x2)), jnp.where(lo, jnp.zeros_like(x2), x2)


def _lo_mask():
    return lax.broadcasted_iota(jnp.int32, (1, LANES), 1) < HEAD_DIM


def _scores(qh, k2, cqh, ckh, causal):
    s = lax.dot_general(qh, k2, NT, preferred_element_type=F32)
    return jnp.where(causal, s + cqh - ckh, NEG)


def attn_fwd(q, k, v, cqp, ckr):
    tp, d = q.shape
    t = ATT_TILE
    nb = tp // t
    npair = d // LANES

    def body(q_ref, k_ref, v_ref, cq_ref, ck_ref, o_ref, lse_ref):
        qi = pl.program_id(1)
        lo = _lo_mask()
        qa, qb = _pair_split(q_ref[...], lo)
        cq = cq_ref[0]
        cqa, cqb = cq[:, 0:1], cq[:, HEAD_DIM:HEAD_DIM + 1]
        rows = qi * t + lax.broadcasted_iota(jnp.int32, (t, t), 0)
        cols0 = lax.broadcasted_iota(jnp.int32, (t, t), 1)

        def step(kj, carry):
            off = pl.multiple_of(kj * t, t)
            k2 = k_ref[pl.ds(off, t), :]
            v2 = v_ref[pl.ds(off, t), :]
            ck = ck_ref[0, kj]
            causal = cols0 + off <= rows
            out = []
            for hd, (qh, cqh) in enumerate(((qa, cqa), (qb, cqb))):
                m, l, acc = carry[3 * hd:3 * hd + 3]
                s = _scores(qh, k2, cqh, ck[hd:hd + 1, :], causal)
                m_new = jnp.maximum(m, jnp.max(s, axis=-1, keepdims=True))
                alpha = jnp.exp(m - m_new)
                pr = jnp.exp(s - m_new)
                l = alpha * l + jnp.sum(pr, axis=-1, keepdims=True)
                acc = alpha * acc + jnp.dot(pr.astype(BF16), v2, preferred_element_type=F32)
                out += [m_new, l, acc]
            return tuple(out)

        init = (jnp.full((t, 1), NEG, F32), jnp.zeros((t, 1), F32), jnp.zeros((t, LANES), F32)) * 2
        ma, la, acca, mb, lb, accb = lax.fori_loop(0, qi + 1, step, init)
        o_ref[...] = jnp.where(lo, acca / la, accb / lb).astype(BF16)
        lse_ref[0] = jnp.where(lo, ma + jnp.log(la), mb + jnp.log(lb))

    return pl.pallas_call(
        body, name="attn_fwd", grid=(npair, nb),
        out_shape=[jax.ShapeDtypeStruct((tp, d), BF16), jax.ShapeDtypeStruct((npair, tp, LANES), F32)],
        in_specs=[pl.BlockSpec((t, LANES), lambda h, i: (i, h)), pl.BlockSpec((tp, LANES), lambda h, i: (0, h)),
                  pl.BlockSpec((tp, LANES), lambda h, i: (0, h)),
                  pl.BlockSpec((1, t, LANES), lambda h, i: (h, i, 0)),
                  pl.BlockSpec((1, nb, 2, t), lambda h, i: (h, 0, 0, 0))],
        out_specs=[pl.BlockSpec((t, LANES), lambda h, i: (i, h)), pl.BlockSpec((1, t, LANES), lambda h, i: (h, i, 0))],
        compiler_params=_cparams(2),
    )(q, k, v, cqp, ckr)


def attn_bwd_dq(q, k, v, cqp, ckr, do, lse):
    tp, d = q.shape
    t = ATT_TILE
    nb = tp // t
    npair = d // LANES

    def body(q_ref, k_ref, v_ref, cq_ref, ck_ref, do_ref, lse_ref, dq_ref, delta_ref):
        qi = pl.program_id(1)
        lo = _lo_mask()
        qa, qb = _pair_split(q_ref[...], lo)
        doa, dob = _pair_split(do_ref[...], lo)
        cq = cq_ref[0]
        lse2 = lse_ref[0]
        heads = ((qa, doa, cq[:, 0:1], lse2[:, 0:1]),
                 (qb, dob, cq[:, HEAD_DIM:HEAD_DIM + 1], lse2[:, HEAD_DIM:HEAD_DIM + 1]))
        rows = qi * t + lax.broadcasted_iota(jnp.int32, (t, t), 0)
        cols0 = lax.broadcasted_iota(jnp.int32, (t, t), 1)

        def p_dp(kj, hd):
            qh, doh, cqh, lseh = heads[hd]
            off = pl.multiple_of(kj * t, t)
            k2 = k_ref[pl.ds(off, t), :]
            ck = ck_ref[0, kj]
            pr = jnp.exp(_scores(qh, k2, cqh, ck[hd:hd + 1, :], cols0 + off <= rows) - lseh)
            dpr = lax.dot_general(doh, v_ref[pl.ds(off, t), :], NT, preferred_element_type=F32)
            return pr, dpr, k2

        def delta_step(kj, carry):
            out = []
            for hd in range(2):
                pr, dpr, _ = p_dp(kj, hd)
                out.append(carry[hd] + jnp.sum(pr * dpr, axis=-1, keepdims=True))
            return tuple(out)

        dl = lax.fori_loop(0, qi + 1, delta_step, (jnp.zeros((t, 1), F32),) * 2)
        delta_ref[0] = jnp.where(lo, dl[0], dl[1])

        def step(kj, carry):
            out = []
            for hd in range(2):
                pr, dpr, k2 = p_dp(kj, hd)
                ds = (pr * (dpr - dl[hd])).astype(BF16)
                out.append(carry[hd] + jnp.dot(ds, k2, preferred_element_type=F32))
            return tuple(out)

        dqa, dqb = lax.fori_loop(0, qi + 1, step, (jnp.zeros((t, LANES), F32),) * 2)
        dq_ref[...] = jnp.where(lo, dqa, dqb)

    pair_rows = pl.BlockSpec((1, t, LANES), lambda h, i: (h, i, 0))
    blk = pl.BlockSpec((t, LANES), lambda h, i: (i, h))
    full = pl.BlockSpec((tp, LANES), lambda h, i: (0, h))
    return pl.pallas_call(
        body, name="attn_bwd_dq", grid=(npair, nb),
        out_shape=[jax.ShapeDtypeStruct((tp, d), F32), jax.ShapeDtypeStruct((npair, tp, LANES), F32)],
        in_specs=[blk, full, full, pair_rows, pl.BlockSpec((1, nb, 2, t), lambda h, i: (h, 0, 0, 0)), blk, pair_rows],
        out_specs=[blk, pair_rows],
        compiler_params=_cparams(2),
    )(q, k, v, cqp, ckr, do, lse)


def attn_bwd_dkv(q, k, v, cqp, ckr, do, lse, delta):
    tp, d = q.shape
    t = ATT_TILE
    nb = tp // t
    npair = d // LANES

    def body(q_ref, k_ref, v_ref, cq_ref, ck_ref, do_ref, lse_ref, delta_ref, dk_ref, dv_ref, dck_ref):
        kj = pl.program_id(1)
        lo = _lo_mask()
        k2 = k_ref[...]
        v2 = v_ref[...]
        ck = ck_ref[0, 0]
        cols = kj * t + lax.broadcasted_iota(jnp.int32, (t, t), 1)
        rows0 = lax.broadcasted_iota(jnp.int32, (t, t), 0)

        def step(qi, carry):
            dk, dv, dca, dcb = carry
            off = pl.multiple_of(qi * t, t)
            qa, qb = _pair_split(q_ref[pl.ds(off, t), :], lo)
            doa, dob = _pair_split(do_ref[pl.ds(off, t), :], lo)
            cq = cq_ref[0, pl.ds(off, t), :]
            lse2 = lse_ref[0, pl.ds(off, t), :]
            dl2 = delta_ref[0, pl.ds(off, t), :]
            causal = cols <= rows0 + off
            dcs = []
            for hd, (qh, doh) in enumerate(((qa, doa), (qb, dob))):
                c0 = hd * HEAD_DIM
                pr = jnp.exp(_scores(qh, k2, cq[:, c0:c0 + 1], ck[hd:hd + 1, :], causal) - lse2[:, c0:c0 + 1])
                dpr = lax.dot_general(doh, v2, NT, preferred_element_type=F32)
                ds = pr * (dpr - dl2[:, c0:c0 + 1])
                dv = dv + lax.dot_general(pr.astype(BF16), doh, TN, preferred_element_type=F32)
                dk = dk + lax.dot_general(ds.astype(BF16), qh, TN, preferred_element_type=F32)
                dcs.append(jnp.sum(ds, axis=0, keepdims=True))
            return dk, dv, dca - dcs[0], dcb - dcs[1]

        zero = jnp.zeros((t, LANES), F32)
        zrow = jnp.zeros((1, t), F32)
        dk, dv, dca, dcb = lax.fori_loop(kj, nb, step, (zero, zero, zrow, zrow))
        dk_ref[...] = dk
        dv_ref[...] = dv.astype(BF16)
        dck_ref[0, 0, 0:1, :] = dca
        dck_ref[0, 0, 1:2, :] = dcb

    full = pl.BlockSpec((tp, LANES), lambda h, j: (0, h))
    pair_full = pl.BlockSpec((1, tp, LANES), lambda h, j: (h, 0, 0))
    blk = pl.BlockSpec((t, LANES), lambda h, j: (j, h))
    ckblk = pl.BlockSpec((1, 1, 2, t), lambda h, j: (h, j, 0, 0))
    return pl.pallas_call(
        body, name="attn_bwd_dkv", grid=(npair, nb),
        out_shape=[jax.ShapeDtypeStruct((tp, d), F32), jax.ShapeDtypeStruct((tp, d), BF16),
                   jax.ShapeDtypeStruct((npair, nb, 2, t), F32)],
        in_specs=[full, blk, blk, pair_full, ckblk, full, pair_full, pair_full],
        out_specs=[blk, blk, ckblk],
        compiler_params=_cparams(2),
    )(q, k, v, cqp, ckr, do, lse, delta)


L2E = 1.4426950408889634
LN2 = 0.6931471805599453
AUG = HEAD_DIM


def _split3(r):
    r1 = r.astype(BF16).astype(F32)
    r2 = (r - r1).astype(BF16).astype(F32)
    r3 = (r - r1 - r2).astype(BF16).astype(F32)
    return r1, r2, r3


def c_prep_slots(p, b_f, qg, kg):
    tp = p.shape[0]
    d = qg.shape[1]
    n_heads = d // HEAD_DIM
    tm = ATT_TILE
    seg, seg_t, _ = _head_masks(d)

    def body(p_ref, pf_ref, bf_ref, qg_ref, kg_ref, seg_ref, segt_ref, q_ref, k_ref, v_ref, cum_ref, carry):
        @pl.when(pl.program_id(0) == 0)
        def _():
            carry[...] = jnp.zeros_like(carry)

        def norm(x, g):
            ms = jnp.dot(x * x, seg_ref[...], precision=HI, preferred_element_type=F32) * (1.0 / HEAD_DIM)
            r = jnp.dot(lax.rsqrt(ms + EPS), segt_ref[...], precision=HI, preferred_element_type=F32)
            return x * r * g

        qn = norm(p_ref[:, 0:d], qg_ref[...]) * (HEAD_DIM ** -0.5 * L2E)
        kn = norm(p_ref[:, d:2 * d], kg_ref[...])
        v_ref[...] = p_ref[:, 2 * d:].astype(BF16)
        xf = pf_ref[...] + bf_ref[...]
        logf = jnp.minimum(xf, 0.0) - jnp.log(1.0 + jnp.exp(-jnp.abs(xf)))
        r_ = lax.broadcasted_iota(jnp.int32, (tm, tm), 0)
        c_ = lax.broadcasted_iota(jnp.int32, (tm, tm), 1)
        cum = jnp.dot((c_ <= r_).astype(F32), logf, precision=HI, preferred_element_type=F32) + carry[0:1, :]
        cum_ref[...] = cum
        carry[0:1, :] += jnp.sum(logf, axis=0, keepdims=True)

        lane = lax.broadcasted_iota(jnp.int32, (1, LANES), 1)
        ones_q = jnp.where((lane >= AUG + 3) & (lane < AUG + 6), 1.0, 0.0)
        ones_k = jnp.where((lane >= AUG) & (lane < AUG + 3), 1.0, 0.0)
        for h in range(n_heads):
            c0 = LANES * (h // 2)
            src_q, src_k = qn[:, c0:c0 + LANES], kn[:, c0:c0 + LANES]
            if h % 2:
                src_q = pltpu.roll(src_q, HEAD_DIM, axis=1)
                src_k = pltpu.roll(src_k, HEAD_DIM, axis=1)
            c = cum[:, h:h + 1] * L2E
            a1, a2, a3 = _split3(c)
            b1, b2, b3 = _split3(-c)
            aug_q = jnp.where(lane == AUG, a1, jnp.where(lane == AUG + 1, a2, jnp.where(lane == AUG + 2, a3, ones_q)))
            aug_k = jnp.where(lane == AUG + 3, b1,
                              jnp.where(lane == AUG + 4, b2, jnp.where(lane == AUG + 5, b3, ones_k)))
            q_ref[:, LANES * h:LANES * (h + 1)] = jnp.where(lane < HEAD_DIM, src_q, aug_q).astype(BF16)
            k_ref[:, LANES * h:LANES * (h + 1)] = jnp.where(lane < HEAD_DIM, src_k, aug_k).astype(BF16)

    vec = pl.BlockSpec((1, d), lambda i: (0, 0))
    rowd = pl.BlockSpec((tm, d), lambda i: (i, 0))
    slots = pl.BlockSpec((tm, n_heads * LANES), lambda i: (i, 0))
    return pl.pallas_call(
        body, name="c_prep_slots", grid=(tp // tm,),
        out_shape=[jax.ShapeDtypeStruct((tp, n_heads * LANES), BF16)] * 2
        + [jax.ShapeDtypeStruct((tp, d), BF16), jax.ShapeDtypeStruct((tp, LANES), F32)],
        in_specs=[pl.BlockSpec((tm, 3 * d), lambda i: (i, 0)), pl.BlockSpec((tm, LANES), lambda i: (i, 3 * d // LANES)),
                  pl.BlockSpec((1, LANES), lambda i: (0, 0)), vec, vec,
                  pl.BlockSpec((d, LANES), lambda i: (0, 0)), pl.BlockSpec((LANES, d), lambda i: (0, 0))],
        out_specs=[slots, slots, rowd, pl.BlockSpec((tm, LANES), lambda i: (i, 0))],
        scratch_shapes=[pltpu.VMEM((8, LANES), F32)],
        compiler_params=_cparams(1),
    )(p, p, b_f, qg, kg, seg, seg_t)


def _diag_mask(t):
    return lax.broadcasted_iota(jnp.int32, (t, t), 0) <= lax.broadcasted_iota(jnp.int32, (t, t), 1)


def attn_fwd_t(qpt, kp, vt4):
    tp = kp.shape[0]
    t = ATT_TILE
    nb = tp // t
    npair = vt4.shape[0]

    def body(q_ref, k_ref, v_ref, o_ref, lse_ref):
        qi = pl.program_id(1)
        qts = (q_ref[0:LANES, :], q_ref[LANES:2 * LANES, :])
        mask = _diag_mask(t)

        def block(kj, carry, masked):
            off = pl.multiple_of(kj * t, t)
            v2 = v_ref[0, kj]
            out = []
            for hd in range(2):
                m, l, acc = carry[3 * hd:3 * hd + 3]
                st = jnp.dot(k_ref[pl.ds(off, t), LANES * hd:LANES * (hd + 1)], qts[hd], preferred_element_type=F32)
                if masked:
                    st = jnp.where(mask, st, NEG)
                m_new = jnp.maximum(m, jnp.max(st, axis=0, keepdims=True))
                pt = jnp.exp2(st - m_new)
                alpha = jnp.exp2(m - m_new)
                l = alpha * l + jnp.sum(pt, axis=0, keepdims=True)
                acc = alpha * acc + jnp.dot(v2, pt.astype(BF16), preferred_element_type=F32)
                out += [m_new, l, acc]
            return tuple(out)

        init = (jnp.full((1, t), NEG, F32), jnp.zeros((1, t), F32), jnp.zeros((LANES, t), F32)) * 2
        carry = lax.fori_loop(0, qi, lambda kj, c: block(kj, c, False), init)
        ma, la, acca, mb, lb, accb = block(qi, carry, True)
        row = lax.broadcasted_iota(jnp.int32, (LANES, 1), 0)
        o_ref[...] = jnp.where(row < HEAD_DIM, acca / la, accb / lb).astype(BF16)
        lse_ref[0, 0:1, :] = ma + jnp.log(la) * L2E
        lse_ref[0, 1:2, :] = mb + jnp.log(lb) * L2E

    return pl.pallas_call(
        body, name="attn_fwd_t", grid=(npair, nb),
        out_shape=[jax.ShapeDtypeStruct((npair * LANES, tp), BF16), jax.ShapeDtypeStruct((npair, 2, tp), F32)],
        in_specs=[pl.BlockSpec((2 * LANES, t), lambda h, i: (h, i)), pl.BlockSpec((tp, 2 * LANES), lambda h, i: (0, h)),
                  pl.BlockSpec((1, nb, LANES, t), lambda h, i: (h, 0, 0, 0))],
        out_specs=[pl.BlockSpec((LANES, t), lambda h, i: (h, i)), pl.BlockSpec((1, 2, t), lambda h, i: (h, 0, i))],
        compiler_params=_cparams(2),
    )(qpt, kp, vt4)


def attn_bwd_t(qpt, qp, kp, kpt4, v, do, dot_, lse, cqt):
    tp = kp.shape[0]
    t = ATT_TILE
    nb = tp // t
    n_heads = kpt4.shape[0]
    d = v.shape[1]

    def body(qt_ref, q_ref, k_ref, kt_ref, v_ref, do_ref, dot_ref, lse_ref, cq_ref, dq_ref, dk_ref, dv_ref,
             p_sc, dp_sc, dc_sc):
        hd = pl.program_id(0) % 2
        qi = pl.program_id(1)

        @pl.when(qi == 0)
        def _():
            dk_ref[...] = jnp.zeros_like(dk_ref)
            dc_sc[...] = jnp.zeros_like(dc_sc)

        @pl.when((qi == 0) & (hd == 0))
        def _():
            dv_ref[...] = jnp.zeros_like(dv_ref)

        row = lax.broadcasted_iota(jnp.int32, (LANES, 1), 0)
        lane = lax.broadcasted_iota(jnp.int32, (1, LANES), 1)
        dot_h = jnp.where(row // HEAD_DIM == hd, dot_ref[...], jnp.zeros_like(dot_ref))
        do_h = jnp.where(lane // HEAD_DIM == hd, do_ref[...], jnp.zeros_like(do_ref))
        rr = cq_ref[0] - lse_ref[0]
        r1, r2, r3 = _split3(jnp.where(hd == 0, rr[0:1, :], rr[1:2, :]))
        qt = qt_ref[...].astype(F32)
        qt = jnp.where(row == AUG, r1, jnp.where(row == AUG + 1, r2, jnp.where(row == AUG + 2, r3, qt))).astype(BF16)
        mask = _diag_mask(t)

        def pass1(kj, delta, masked):
            off = pl.multiple_of(kj * t, t)
            st = jnp.dot(k_ref[pl.ds(off, t), :], qt, preferred_element_type=F32)
            if masked:
                st = jnp.where(mask, st, NEG)
            pt = jnp.exp2(st)
            dpt = jnp.dot(v_ref[pl.ds(off, t), :], dot_h, preferred_element_type=F32)
            p_sc[kj] = pt
            dp_sc[kj] = dpt
            return delta + jnp.sum(pt * dpt, axis=0, keepdims=True)

        delta = lax.fori_loop(0, qi, lambda kj, dl: pass1(kj, dl, False), jnp.zeros((1, t), F32))
        delta = pass1(qi, delta, True)

        def pass2(kj, dq):
            off = pl.multiple_of(kj * t, t)
            pt = p_sc[kj]
            ds32 = pt * (dp_sc[kj] - delta)
            ds = ds32.astype(BF16)
            dc_sc[pl.ds(off, t), :] += ds32[:, 0:LANES] + ds32[:, LANES:2 * LANES]
            dk_ref[pl.ds(off, t), :] += jnp.dot(ds, q_ref[...], preferred_element_type=F32)
            dv_ref[pl.ds(off, t), :] += jnp.dot(pt.astype(BF16), do_h, preferred_element_type=F32)
            return dq + jnp.dot(kt_ref[0, kj], ds, preferred_element_type=F32)

        dq_ref[...] = lax.fori_loop(0, qi + 1, pass2, jnp.zeros((LANES, t), F32))

        @pl.when(qi == nb - 1)
        def _():
            dk_ref[:, AUG + 3:AUG + 4] = jnp.sum(dc_sc[...], axis=1, keepdims=True)

    once = dict(pipeline_mode=pl.Buffered(1))
    pair_rows = pl.BlockSpec((1, 2, t), lambda h, i: (h // 2, 0, i))
    return pl.pallas_call(
        body, name="attn_bwd_t", grid=(n_heads, nb),
        out_shape=[jax.ShapeDtypeStruct((n_heads * LANES, tp), F32), jax.ShapeDtypeStruct((tp, n_heads * LANES), F32),
                   jax.ShapeDtypeStruct((tp, d), F32)],
        in_specs=[pl.BlockSpec((LANES, t), lambda h, i: (h, i)), pl.BlockSpec((t, LANES), lambda h, i: (i, h)),
                  pl.BlockSpec((tp, LANES), lambda h, i: (0, h), **once),
                  pl.BlockSpec((1, nb, LANES, t), lambda h, i: (h, 0, 0, 0), **once),
                  pl.BlockSpec((tp, LANES), lambda h, i: (0, h // 2), **once),
                  pl.BlockSpec((t, LANES), lambda h, i: (i, h // 2)), pl.BlockSpec((LANES, t), lambda h, i: (h // 2, i)),
                  pair_rows, pair_rows],
        out_specs=[pl.BlockSpec((LANES, t), lambda h, i: (h, i)), pl.BlockSpec((tp, LANES), lambda h, i: (0, h)),
                   pl.BlockSpec((tp, LANES), lambda h, i: (0, h // 2))],
        scratch_shapes=[pltpu.VMEM((nb, t, t), F32), pltpu.VMEM((nb, t, t), F32), pltpu.VMEM((tp, LANES), F32)],
        compiler_params=_cparams(2),
    )(qpt, qp, kp, kpt4, v, do, dot_, lse, cqt)


def rev_cumsum_rows(x):
    r, tp = x.shape
    t = ATT_TILE
    nb = tp // t

    def body(x_ref, o_ref, carry):
        @pl.when(pl.program_id(0) == 0)
        def _():
            carry[...] = jnp.zeros_like(carry)

        xv = x_ref[...]
        r_ = lax.broadcasted_iota(jnp.int32, (t, t), 0)
        c_ = lax.broadcasted_iota(jnp.int32, (t, t), 1)
        o_ref[...] = jnp.dot(xv, (r_ >= c_).astype(F32), precision=HI, preferred_element_type=F32) + carry[:, 0:1]
        carry[...] += jnp.sum(xv, axis=1, keepdims=True)

    return pl.pallas_call(
        body, name="rev_cumsum_rows", grid=(nb,), out_shape=jax.ShapeDtypeStruct((r, tp), F32),
        in_specs=[pl.BlockSpec((r, t), lambda i: (0, nb - 1 - i))],
        out_specs=pl.BlockSpec((r, t), lambda i: (0, nb - 1 - i)),
        scratch_shapes=[pltpu.VMEM((r, LANES), F32)],
        compiler_params=_cparams(1),
    )(x)


def c_elem_bwd(p, b_f, qg, kg, dq, dk, dv, dlogf):
    tp, n_out = p.shape
    d = qg.shape[1]
    tm = ATT_TILE
    n = tp // tm
    seg, seg_t, fold = _head_masks(d)

    def body(p_ref, pf_ref, bf_ref, qg_ref, kg_ref, dq_ref, dk_ref, dv_ref, dlf_ref, seg_ref, segt_ref, fold_ref,
             dp_ref, sm_ref, accq, acck, accf):
        i = pl.program_id(0)

        @pl.when(i == 0)
        def _():
            accq[...] = jnp.zeros_like(accq)
            acck[...] = jnp.zeros_like(acck)
            accf[...] = jnp.zeros_like(accf)

        def norm_bwd(x, g, dy, acc):
            ms = jnp.dot(x * x, seg_ref[...], precision=HI, preferred_element_type=F32) * (1.0 / HEAD_DIM)
            r = jnp.dot(lax.rsqrt(ms + EPS), segt_ref[...], precision=HI, preferred_element_type=F32)
            xhat = x * r
            acc[0:1, :] += jnp.sum(dy * xhat, axis=0, keepdims=True)
            gy = dy * g
            mean = jnp.dot(jnp.dot(gy * xhat, seg_ref[...], precision=HI, preferred_element_type=F32),
                           segt_ref[...], precision=HI, preferred_element_type=F32) * (1.0 / HEAD_DIM)
            return r * (gy - xhat * mean)

        dp_ref[:, 0:d] = norm_bwd(p_ref[:, 0:d], qg_ref[...], dq_ref[...] * (HEAD_DIM ** -0.5), accq).astype(BF16)
        dp_ref[:, d:2 * d] = norm_bwd(p_ref[:, d:2 * d], kg_ref[...], dk_ref[...] * LN2, acck).astype(BF16)
        dp_ref[:, 2 * d:3 * d] = dv_ref[...].astype(BF16)
        df = dlf_ref[...] * _sigmoid(-(pf_ref[...] + bf_ref[...]))
        accf[0:1, :] += jnp.sum(df, axis=0, keepdims=True)
        dp_ref[:, 3 * d:] = df.astype(BF16)

        @pl.when(i == n - 1)
        def _():
            sm_ref[...] = jnp.zeros_like(sm_ref)
            sm_ref[0:1, :] = jnp.dot(accq[0:1, :], fold_ref[...], precision=HI, preferred_element_type=F32)
            sm_ref[1:2, :] = jnp.dot(acck[0:1, :], fold_ref[...], precision=HI, preferred_element_type=F32)
            sm_ref[2:3, :] = accf[0:1, :]

    vec = pl.BlockSpec((1, d), lambda i: (0, 0))
    rowd = pl.BlockSpec((tm, d), lambda i: (i, 0))
    rowl = pl.BlockSpec((tm, LANES), lambda i: (i, 0))
    return pl.pallas_call(
        body, name="c_elem_bwd", grid=(n,),
        out_shape=[jax.ShapeDtypeStruct((tp, n_out), BF16), jax.ShapeDtypeStruct((8, LANES), F32)],
        in_specs=[pl.BlockSpec((tm, 2 * d), lambda i: (i, 0)), pl.BlockSpec((tm, LANES), lambda i: (i, 3 * d // LANES)),
                  pl.BlockSpec((1, LANES), lambda i: (0, 0)), vec, vec, rowd, rowd, rowd, rowl,
                  pl.BlockSpec((d, LANES), lambda i: (0, 0)), pl.BlockSpec((LANES, d), lambda i: (0, 0)),
                  pl.BlockSpec((d, LANES), lambda i: (0, 0))],
        out_specs=[pl.BlockSpec((tm, n_out), lambda i: (i, 0)), pl.BlockSpec((8, LANES), lambda i: (0, 0))],
        scratch_shapes=[pltpu.VMEM((8, d), F32), pltpu.VMEM((8, d), F32), pltpu.VMEM((8, LANES), F32)],
        compiler_params=_cparams(1),
    )(p, p, b_f, qg, kg, dq, dk, dv, dlogf, seg, seg_t, fold)


def loss_head(h, tgt, t_real):
    tp, d = h.shape
    tm = _pick(tp, (768, 512, 256))

    def body(h_ref, t_ref, dh_ref, l_ref):
        i = pl.program_id(0)

        @pl.when(i == 0)
        def _():
            l_ref[...] = jnp.zeros_like(l_ref)

        row = i * tm + lax.broadcasted_iota(jnp.int32, (tm, 1), 0)
        valid = (row >= N_META) & (row < t_real)
        e = jnp.where(valid, h_ref[...] - t_ref[...], 0.0)
        dh_ref[...] = e * (1.0 / d)
        per_row = jnp.sum(e * e, axis=-1, keepdims=True) * (1.0 / d)
        l_ref[...] += 0.5 * jnp.sum(per_row, axis=0, keepdims=True)

    return pl.pallas_call(
        body, name="loss_head", grid=(tp // tm,),
        out_shape=[jax.ShapeDtypeStruct((tp, d), F32), jax.ShapeDtypeStruct((8, LANES), F32)],
        in_specs=[pl.BlockSpec((tm, d), lambda i: (i, 0)), pl.BlockSpec((tm, d), lambda i: (i, 0))],
        out_specs=[pl.BlockSpec((tm, d), lambda i: (i, 0)), pl.BlockSpec((8, LANES), lambda i: (0, 0))],
        compiler_params=_cparams(1),
    )(h, tgt)


def sum_devices(x):
    _, r, c = x.shape

    def body(x_ref, o_ref):
        acc = x_ref[0]
        for dev in range(1, N_DEV):
            acc = acc + x_ref[dev]
        o_ref[...] = acc

    return pl.pallas_call(
        body, name="sum_devices", out_shape=jax.ShapeDtypeStruct((r, c), F32),
        in_specs=[pl.BlockSpec(memory_space=pltpu.VMEM)], out_specs=pl.BlockSpec(memory_space=pltpu.VMEM),
    )(x)


def _adamw_math(w, g, m, v):
    m = ADAM_B1 * m + (1.0 - ADAM_B1) * g
    v = ADAM_B2 * v + (1.0 - ADAM_B2) * (g * g)
    m_hat = m / (1.0 - ADAM_B1 ** ADAM_STEP)
    v_hat = v / (1.0 - ADAM_B2 ** ADAM_STEP)
    delta = -ADAM_LR * (m_hat / (jnp.sqrt(v_hat) + ADAM_EPS) + ADAM_WD * w)
    return delta, m, v


def adamw_small(w, g, m, v):
    def body(w_ref, g_ref, m_ref, v_ref, d_ref, nm_ref, nv_ref):
        d_ref[...], nm_ref[...], nv_ref[...] = _adamw_math(w_ref[...], g_ref[...], m_ref[...], v_ref[...])

    vm = pl.BlockSpec(memory_space=pltpu.VMEM)
    return pl.pallas_call(
        body, name="adamw_small", out_shape=[jax.ShapeDtypeStruct(w.shape, F32)] * 3,
        in_specs=[vm] * 4, out_specs=[vm] * 3,
    )(w, g, m, v)


def adamw_reduce(w, m, v, parts):
    r, c = w.shape
    tr = _pick(r, (512, 384, 352, 256, 128, 8))

    def body(w_ref, m_ref, v_ref, p_ref, g_ref, d_ref, nm_ref, nv_ref):
        g = p_ref[0].astype(F32)
        for dev in range(1, N_DEV):
            g = g + p_ref[dev].astype(F32)
        g_ref[...] = g
        d_ref[...], nm_ref[...], nv_ref[...] = _adamw_math(w_ref[...], g, m_ref[...], v_ref[...])

    blk = pl.BlockSpec((tr, c), lambda i: (i, 0))
    return pl.pallas_call(
        body, name="adamw_reduce", grid=(r // tr,), out_shape=[jax.ShapeDtypeStruct((r, c), F32)] * 4,
        in_specs=[blk, blk, blk, pl.BlockSpec((N_DEV, tr, c), lambda i: (0, i, 0))],
        out_specs=[blk] * 4, compiler_params=_cparams(1),
    )(w, m, v, parts)


def _unshard(g, axis):
    g = jnp.moveaxis(g, 0, axis)
    return g.reshape(g.shape[:axis] + (g.shape[axis] * g.shape[axis + 1],) + g.shape[axis + 2:])


def _shard(full, axis):
    s = full.shape
    g = full.reshape(s[:axis] + (N_DEV, s[axis] // N_DEV) + s[axis + 1:])
    return jnp.moveaxis(g, axis, 0)


def _pad_lanes(a, n=LANES):
    flat = a.reshape(-1)
    pad = (-flat.shape[0]) % n
    return jnp.pad(flat, (0, pad)).reshape(-1, n)


BIG = ("ffn_w_gate", "ffn_w_up", "ffn_w_down", "a_w_in", "a_w_out", "b_w_in", "b_w_out", "c_w_in", "c_w_out")
SHARD_AXIS = {"ffn_w_gate": 3, "ffn_w_up": 3, "ffn_w_down": 2, "a_w_in": 2, "a_w_out": 1, "b_w_in": 2,
              "b_w_out": 1, "c_w_in": 2, "c_w_out": 1, "meta": 1, "ffn_norm": 2, "a_conv": 2, "b_conv": 2}
SMALL_SHARDED = ("meta", "ffn_norm", "a_conv", "b_conv")
SMALL_REPL = ("mix_norm", "b_conv_bias", "b_ln_g", "b_ln_b", "c_b_f", "c_q_norm", "c_k_norm")
WEIGHTS = ("meta", "ffn_norm", "ffn_w_gate", "ffn_w_up", "ffn_w_down", "mix_norm", "a_w_in", "a_conv", "a_w_out",
           "b_w_in", "b_conv", "b_conv_bias", "b_ln_g", "b_ln_b", "b_w_out", "c_w_in", "c_b_f", "c_q_norm",
           "c_k_norm", "c_w_out")
N_MIXERS = 3


def kernel(x, meta, ffn_norm, ffn_w_gate, ffn_w_up, ffn_w_down, mix_norm, a_w_in, a_conv, a_w_out, b_w_in, b_conv, b_conv_bias, b_ln_g, b_ln_b, b_w_out, c_w_in, c_b_f, c_q_norm, c_k_norm, c_w_out, loss_target, m_meta, m_ffn_norm, m_ffn_w_gate, m_ffn_w_up, m_ffn_w_down, m_mix_norm, m_a_w_in, m_a_conv, m_a_w_out, m_b_w_in, m_b_conv, m_b_conv_bias, m_b_ln_g, m_b_ln_b, m_b_w_out, m_c_w_in, m_c_b_f, m_c_q_norm, m_c_k_norm, m_c_w_out, v_meta, v_ffn_norm, v_ffn_w_gate, v_ffn_w_up, v_ffn_w_down, v_mix_norm, v_a_w_in, v_a_conv, v_a_w_out, v_b_w_in, v_b_conv, v_b_conv_bias, v_b_ln_g, v_b_ln_b, v_b_w_out, v_c_w_in, v_c_b_f, v_c_q_norm, v_c_k_norm, v_c_w_out):
    local = dict(locals())
    w = {n: local[n] for n in WEIGHTS}
    mom = {n: local["m_" + n] for n in WEIGHTS}
    var = {n: local["v_" + n] for n in WEIGHTS}
    d = x.shape[-1]
    depth = ffn_norm.shape[0]
    n_heads = d // HEAD_DIM
    seq = x.shape[1]
    t_real = N_META + seq
    tp = -(-t_real // ROW_ALIGN) * ROW_ALIGN
    nb = tp // ATT_TILE
    npair = d // LANES

    names = BIG + SMALL_SHARDED
    gathered = all_gather([w[n].astype(BF16) for n in BIG] + [w[n] for n in SMALL_SHARDED])
    full = {n: _unshard(g, SHARD_AXIS[n]) for n, g in zip(names, gathered)}
    n_c = full["c_w_in"].shape[-1]
    n_cp = 3 * d + LANES
    c_w_qkv = jnp.pad(full["c_w_in"], ((0, 0), (0, 0), (0, n_cp - n_c)))
    b_f_pad = jnp.pad(c_b_f, ((0, 0), (0, LANES - n_heads)))
    qg_t = jnp.tile(c_q_norm, (1, n_heads))
    kg_t = jnp.tile(c_k_norm, (1, n_heads))

    h = jnp.concatenate([full["meta"], x[0], jnp.zeros((tp - t_real, d), F32)], axis=0)
    saved = []
    for i in range(depth):
        mixer, j = i % N_MIXERS, i // N_MIXERS
        s = {"h0": h}
        h, s["gate_a"], s["up_a"] = ffn_fwd(h, full["ffn_norm"][i, 0:1], full["ffn_w_gate"][i, 0],
                                            full["ffn_w_up"][i, 0], full["ffn_w_down"][i, 0])
        s["h1"] = h
        g_mix = mix_norm[i:i + 1]
        if mixer == 0:
            s["p"], s["u"] = norm_matmul(h, g_mix, full["a_w_in"][j], BF16, True)
            s["y"] = a_elem_fwd(s["p"], full["a_conv"][j])
            h = res_matmul(h, s["y"], full["a_w_out"][j])
        elif mixer == 1:
            s["p"], s["u"] = norm_matmul(h, g_mix, full["b_w_in"][j], BF16, True)
            s["y"] = b_elem_fwd(s["p"], full["b_conv"][j], b_conv_bias[j:j + 1], b_ln_g[j:j + 1], b_ln_b[j:j + 1])
            h = res_matmul(h, s["y"], full["b_w_out"][j])
        else:
            s["p"], s["u"] = norm_matmul(h, g_mix, c_w_qkv[j], F32, True)
            s["qp"], s["kp"], s["v"], cum = c_prep_slots(s["p"], b_f_pad[j:j + 1], qg_t[j:j + 1], kg_t[j:j + 1])
            s["cqt"] = (cum[:, :n_heads].T * L2E).reshape(npair, 2, tp)
            s["qpt"] = s["qp"].T
            s["kpt4"] = s["kp"].T.reshape(n_heads, LANES, nb, ATT_TILE).transpose(0, 2, 1, 3)
            vt4 = s["v"].T.reshape(npair, LANES, nb, ATT_TILE).transpose(0, 2, 1, 3)
            ot, s["lse"] = attn_fwd_t(s["qpt"], s["kp"], vt4)
            s["y"] = ot.T
            h = res_matmul(h, s["y"], full["c_w_out"][j])
        s["h2"] = h
        h, s["gate_b"], s["up_b"] = ffn_fwd(h, full["ffn_norm"][i, 1:2], full["ffn_w_gate"][i, 1],
                                            full["ffn_w_up"][i, 1], full["ffn_w_down"][i, 1])
        saved.append(s)

    tgt = jnp.concatenate([jnp.zeros((N_META, d), F32), loss_target[0], jnp.zeros((tp - t_real, d), F32)], axis=0)
    dh, loss_part = loss_head(h, tgt, t_real)

    gfull = {n: [None] * full[n].shape[0] for n in ("a_w_in", "a_w_out", "b_w_in", "b_w_out", "c_w_in", "c_w_out")}
    gffn = {n: [[None, None] for _ in range(depth)] for n in ("ffn_w_gate", "ffn_w_up", "ffn_w_down")}
    g_ffn_norm = [[None, None] for _ in range(depth)]
    g_mix_norm = [None] * depth
    g_a_conv = [None] * a_conv.shape[0]
    g_b_conv = [None] * b_conv.shape[0]
    g_b_stats = [None] * b_conv.shape[0]
    g_c_small = [None] * c_b_f.shape[0]

    def ffn_backward(dh_out, i, half, h_in, gate, up):
        dh_in, hn, dgate, dup, act, dg = ffn_bwd(dh_out, h_in, full["ffn_norm"][i, half:half + 1], gate, up,
                                                 full["ffn_w_gate"][i, half], full["ffn_w_up"][i, half],
                                                 full["ffn_w_down"][i, half])
        gffn["ffn_w_gate"][i][half] = atb(hn, dgate)
        gffn["ffn_w_up"][i][half] = atb(hn, dup)
        gffn["ffn_w_down"][i][half] = atb(act, dh_out, 0.5)
        g_ffn_norm[i][half] = dg
        return dh_in

    for i in reversed(range(depth)):
        mixer, j = i % N_MIXERS, i // N_MIXERS
        s = saved[i]
        g_mix = mix_norm[i:i + 1]
        dh = ffn_backward(dh, i, 1, s["h2"], s["gate_b"], s["up_b"])
        if mixer == 0:
            dy = matmul_nt(dh, full["a_w_out"][j])
            dp, dwc = a_elem_bwd(s["p"], dy, full["a_conv"][j])
            g_a_conv[j] = dwc[:a_conv.shape[1]]
            gfull["a_w_out"][j] = atb(s["y"], dh)
            gfull["a_w_in"][j] = atb(s["u"], dp)
            dh, g_mix_norm[i] = matmul_nt_rms_bwd(dp, full["a_w_in"][j], s["h1"], g_mix, dh)
        elif mixer == 1:
            dy = matmul_nt(dh, full["b_w_out"][j])
            du2, g_b_stats[j] = b_elem_bwd1(s["p"], dy, full["b_conv"][j], b_conv_bias[j:j + 1],
                                            b_ln_g[j:j + 1], b_ln_b[j:j + 1])
            dp, dwc = b_elem_bwd2(s["p"], du2, full["b_conv"][j])
            g_b_conv[j] = dwc[:b_conv.shape[1]]
            gfull["b_w_out"][j] = atb(s["y"], dh)
            gfull["b_w_in"][j] = atb(s["u"], dp)
            dh, g_mix_norm[i] = matmul_nt_rms_bwd(dp, full["b_w_in"][j], s["h1"], g_mix, dh)
        else:
            do = matmul_nt(dh, full["c_w_out"][j])
            dqt, dkp, dv = attn_bwd_t(s["qpt"], s["qp"], s["kp"], s["kpt4"], s["v"], do, do.T, s["lse"], s["cqt"])
            dq = dqt.reshape(n_heads, LANES, tp)[:, :HEAD_DIM].reshape(d, tp).T
            dkp = dkp.reshape(tp, n_heads, LANES)
            dk = dkp[:, :, :HEAD_DIM].reshape(tp, d)
            dlogf = rev_cumsum_rows(-dkp[:, :, AUG + 3].T)
            dlogf = jnp.pad(dlogf.T, ((0, 0), (0, LANES - n_heads)))
            dp, g_c_small[j] = c_elem_bwd(s["p"], b_f_pad[j:j + 1], qg_t[j:j + 1], kg_t[j:j + 1], dq, dk, dv, dlogf)
            gfull["c_w_out"][j] = atb(s["y"], dh)
            gfull["c_w_in"][j] = atb(s["u"], dp)[:, :n_c]
            dh, g_mix_norm[i] = matmul_nt_rms_bwd(dp, c_w_qkv[j], s["h1"], g_mix, dh)
        dh = ffn_backward(dh, i, 0, s["h0"], s["gate_a"], s["up_a"])

    grad_x = dh[N_META:t_real][None]

    gbig = {n: jnp.stack([jnp.stack(r) for r in gffn[n]]) for n in gffn}
    gbig.update({n: jnp.stack(gfull[n]) for n in gfull})
    parts = exchange([_shard(gbig[n], SHARD_AXIS[n]).astype(BF16) for n in BIG])
    out_g, out_d, out_m, out_v = {}, {}, {}, {}
    for n, prt in zip(BIG, parts):
        shp = w[n].shape
        flat = lambda a: a.reshape(-1, shp[-1])
        res = adamw_reduce(flat(w[n]), flat(mom[n]), flat(var[n]), prt.reshape(N_DEV, -1, shp[-1]))
        out_g[n], out_d[n], out_m[n], out_v[n] = [r.reshape(shp) for r in res]

    def rows(a):
        return a.reshape(-1, d)

    def lane_rows(a):
        return jnp.pad(a.reshape(1, -1), ((0, 0), (0, d - a.size)))

    c_small = jnp.stack(g_c_small)
    pieces = [("meta", dh[:N_META]),
              ("ffn_norm", rows(jnp.stack([jnp.stack(r) for r in g_ffn_norm]))),
              ("mix_norm", rows(jnp.stack(g_mix_norm))),
              ("a_conv", rows(jnp.stack(g_a_conv))),
              ("b_conv", rows(jnp.stack(g_b_conv))),
              ("b_ln_g", rows(jnp.stack([st[0] for st in g_b_stats]))),
              ("b_ln_b", rows(jnp.stack([st[1] for st in g_b_stats]))),
              ("b_conv_bias", rows(jnp.stack([st[2] for st in g_b_stats]))),
              ("c_q_norm", lane_rows(c_small[:, 0, :HEAD_DIM])),
              ("c_k_norm", lane_rows(c_small[:, 1, :HEAD_DIM])),
              ("c_b_f", lane_rows(c_small[:, 2, :n_heads])),
              ("loss", lane_rows(loss_part[0:1, 0:1]))]
    packed = jnp.concatenate([pc for _, pc in pieces], axis=0)
    n_rows = packed.shape[0]
    packed = jnp.pad(packed, ((0, (-n_rows) % 8), (0, 0)))
    total = sum_devices(all_gather([packed])[0])
    small_g, r0 = {}, 0
    for n, pc in pieces:
        small_g[n] = total[r0:r0 + pc.shape[0]]
        r0 += pc.shape[0]
    loss = small_g.pop("loss")[0, 0]
    me = 4 * lax.axis_index("x") + 2 * lax.axis_index("y") + lax.axis_index("c")
    for n in SMALL_SHARDED:
        cols = w[n].shape[-1]
        g = lax.dynamic_slice_in_dim(small_g[n], me * cols, cols, axis=1)
        out_g[n] = g.reshape(w[n].shape)
    for n in SMALL_REPL:
        out_g[n] = small_g[n].reshape(-1)[:w[n].size].reshape(w[n].shape)
    small = SMALL_SHARDED + SMALL_REPL
    pack = lambda dct: jnp.concatenate([_pad_lanes(dct[n]) for n in small], axis=0)
    res = adamw_small(pack(w), pack(out_g), pack(mom), pack(var))
    r0 = 0
    for n in small:
        nr = -(-w[n].size // LANES)
        for dct, arr in zip((out_d, out_m, out_v), res):
            dct[n] = arr[r0:r0 + nr].reshape(-1)[:w[n].size].reshape(w[n].shape)
        r0 += nr

    return (loss, grad_x, *[out_g[n] for n in WEIGHTS], *[out_d[n] for n in WEIGHTS],
            *[out_m[n] for n in WEIGHTS], *[out_v[n] for n in WEIGHTS])
```

```python
import functools

import jax
import jax.numpy as jnp
from jax import lax
from jax.experimental import pallas as pl
from jax.experimental.pallas import tpu as pltpu

F32 = jnp.float32
BF16 = jnp.bfloat16
HI = lax.Precision.HIGHEST
EPS = 1e-6
NEG = -1e30
N_META = 16
HEAD_DIM = 64
LANES = 128
N_DEV = 8
ROW_ALIGN = 256
VMEM_LIMIT = 56 * 1024 * 1024
ADAM_LR, ADAM_B1, ADAM_B2, ADAM_EPS, ADAM_WD, ADAM_STEP = 0.001, 0.9, 0.999, 1e-08, 0.01, 10
MESH = pl.DeviceIdType.MESH
NT = (((1,), (1,)), ((), ()))
TN = (((0,), (0,)), ((), ()))


def _cparams(n_axes):
    return pltpu.CompilerParams(dimension_semantics=("arbitrary",) * n_axes, vmem_limit_bytes=VMEM_LIMIT)


def _pick(n, cands):
    for c in cands:
        if n % c == 0:
            return c
    raise ValueError(f"no tile for {n} among {cands}")


def _sigmoid(x):
    return 1.0 / (1.0 + jnp.exp(-x))


def _rms(x):
    r = lax.rsqrt(jnp.mean(x * x, axis=-1, keepdims=True) + EPS)
    return x * r, r


def _rms_bwd(xhat, r, g, dy):
    gy = dy * g
    dx = r * (gy - xhat * jnp.mean(gy * xhat, axis=-1, keepdims=True))
    return dx, jnp.sum(dy * xhat, axis=0, keepdims=True)


def _mesh_pos():
    return lax.axis_index("x"), lax.axis_index("y"), lax.axis_index("c")


def all_gather(arrs):
    n = len(arrs)
    hbm = pl.BlockSpec(memory_space=pltpu.HBM)

    def body(*refs):
        ins, outs = refs[:n], refs[n:2 * n]
        send_sems, recv_sems, local_sems = refs[2 * n:]
        x, y, c = _mesh_pos()
        me, sibling = (x, y, c), (x, y, 1 - c)
        chips = [(1 - x, y), (x, 1 - y), (1 - x, 1 - y)]

        def slot(a, px, py, pc):
            return outs[a].at[4 * px + 2 * py + pc]

        def copy(a, k, block, to, src=None):
            return pltpu.make_async_remote_copy(
                src_ref=slot(a, *block) if src is None else src, dst_ref=slot(a, *block),
                send_sem=send_sems.at[7 * a + k], recv_sem=recv_sems.at[7 * a + k],
                device_id=to, device_id_type=MESH)

        own, first, passed = [], [], []
        for a in range(n):
            cp = pltpu.make_async_copy(ins[a], slot(a, *me), local_sems.at[a])
            cp.start()
            own.append(cp)
            first.append(copy(a, 0, me, sibling, src=ins[a]))
            first += [copy(a, 1 + j, me, (*chip, c), src=ins[a]) for j, chip in enumerate(chips)]
        for cp in first:
            cp.start()
        for j, chip in enumerate(chips):
            for a in range(n):
                copy(a, 1 + j, (*chip, c), me).wait_recv()
                cp = copy(a, 4 + j, (*chip, c), sibling)
                cp.start()
                passed.append(cp)
        for a in range(n):
            copy(a, 0, sibling, me).wait_recv()
            for j, chip in enumerate(chips):
                copy(a, 4 + j, (*chip, 1 - c), me).wait_recv()
        for cp in first + passed:
            cp.wait_send()
        for cp in own:
            cp.wait()

    return pl.pallas_call(
        body, name="all_gather",
        out_shape=[jax.ShapeDtypeStruct((N_DEV,) + a.shape, a.dtype) for a in arrs],
        in_specs=[hbm] * n, out_specs=[hbm] * n,
        scratch_shapes=[pltpu.SemaphoreType.DMA((7 * n,)), pltpu.SemaphoreType.DMA((7 * n,)),
                        pltpu.SemaphoreType.DMA((n,))],
    )(*arrs)


def exchange(arrs):
    n = len(arrs)
    hbm = pl.BlockSpec(memory_space=pltpu.HBM)

    def body(*refs):
        ins, outs = refs[:n], refs[n:2 * n]
        send_sems, recv_sems, local_sems = refs[2 * n:]
        x, y, c = _mesh_pos()
        me = 4 * x + 2 * y + c
        own, sent = [], []
        for a in range(n):
            cp = pltpu.make_async_copy(ins[a].at[me], outs[a].at[me], local_sems.at[a])
            cp.start()
            own.append(cp)
        for k in range(1, N_DEV):
            px = 1 - x if k & 4 else x
            py = 1 - y if k & 2 else y
            pc = 1 - c if k & 1 else c
            peer = 4 * px + 2 * py + pc
            for a in range(n):
                cp = pltpu.make_async_remote_copy(
                    src_ref=ins[a].at[peer], dst_ref=outs[a].at[me],
                    send_sem=send_sems.at[7 * a + k - 1], recv_sem=recv_sems.at[7 * a + k - 1],
                    device_id=(px, py, pc), device_id_type=MESH)
                cp.start()
                sent.append((cp, a, k, peer))
        for cp, a, k, peer in sent:
            pltpu.make_async_remote_copy(
                src_ref=ins[a].at[peer], dst_ref=outs[a].at[peer],
                send_sem=send_sems.at[7 * a + k - 1], recv_sem=recv_sems.at[7 * a + k - 1],
                device_id=(x, y, c), device_id_type=MESH).wait_recv()
        for cp, a, k, peer in sent:
            cp.wait_send()
        for cp in own:
            cp.wait()

    return pl.pallas_call(
        body, name="exchange",
        out_shape=[jax.ShapeDtypeStruct(a.shape, a.dtype) for a in arrs],
        in_specs=[hbm] * n, out_specs=[hbm] * n,
        scratch_shapes=[pltpu.SemaphoreType.DMA((7 * n,)), pltpu.SemaphoreType.DMA((7 * n,)),
                        pltpu.SemaphoreType.DMA((n,))],
    )(*arrs)


def _resident(shape):
    return pl.BlockSpec(shape, lambda i: (0,) * len(shape), pipeline_mode=pl.Buffered(1))


def ffn_fwd(h, g, wg, wu, wd):
    tp, d = h.shape
    f = wg.shape[1]
    tm = _pick(tp, (384, 256))
    tf = _pick(f, (1408, 1024, 512, 256, 128))

    def body(h_ref, g_ref, wg_ref, wu_ref, wd_ref, ho_ref, gate_ref, up_ref):
        x = h_ref[...]
        xhat, _ = _rms(x)
        hn = (xhat * g_ref[...]).astype(BF16)
        acc = jnp.zeros((tm, d), F32)
        for c0 in range(0, f, tf):
            gate = jnp.dot(hn, wg_ref[:, c0:c0 + tf], preferred_element_type=F32)
            up = jnp.dot(hn, wu_ref[:, c0:c0 + tf], preferred_element_type=F32)
            gate_ref[:, c0:c0 + tf] = gate.astype(BF16)
            up_ref[:, c0:c0 + tf] = up.astype(BF16)
            act = (gate * _sigmoid(gate) * up).astype(BF16)
            acc = acc + jnp.dot(act, wd_ref[c0:c0 + tf, :], preferred_element_type=F32)
        ho_ref[...] = x + 0.5 * acc

    row = lambda n: pl.BlockSpec((tm, n), lambda i: (i, 0))
    return pl.pallas_call(
        body, name="ffn_fwd", grid=(tp // tm,),
        out_shape=[jax.ShapeDtypeStruct((tp, d), F32), jax.ShapeDtypeStruct((tp, f), BF16),
                   jax.ShapeDtypeStruct((tp, f), BF16)],
        in_specs=[row(d), _resident((1, d)), _resident((d, f)), _resident((d, f)), _resident((f, d))],
        out_specs=[row(d), row(f), row(f)],
        compiler_params=_cparams(1),
    )(h, g, wg, wu, wd)


def ffn_bwd(dho, h, g, gate, up, wg, wu, wd):
    tp, d = h.shape
    f = wg.shape[1]
    tm = _pick(tp, (256,))
    tf = _pick(f, (1408, 1024, 512, 256, 128))

    def body(dho_ref, h_ref, g_ref, gate_ref, up_ref, wg_ref, wu_ref, wd_ref,
             dhi_ref, hn_ref, dgate_ref, dup_ref, act_ref, dg_ref):
        @pl.when(pl.program_id(0) == 0)
        def _():
            dg_ref[...] = jnp.zeros_like(dg_ref)

        dho_ = dho_ref[...]
        dout = (0.5 * dho_).astype(BF16)
        dhn = jnp.zeros((tm, d), F32)
        for c0 in range(0, f, tf):
            dact = lax.dot_general(dout, wd_ref[c0:c0 + tf, :], NT, preferred_element_type=F32)
            gt = gate_ref[:, c0:c0 + tf].astype(F32)
            u = up_ref[:, c0:c0 + tf].astype(F32)
            sig = _sigmoid(gt)
            silu = gt * sig
            act_ref[:, c0:c0 + tf] = (silu * u).astype(BF16)
            dup = (dact * silu).astype(BF16)
            dgate = (dact * u * (sig * (1.0 + gt * (1.0 - sig)))).astype(BF16)
            dup_ref[:, c0:c0 + tf] = dup
            dgate_ref[:, c0:c0 + tf] = dgate
            dhn = dhn + (lax.dot_general(dgate, wg_ref[:, c0:c0 + tf], NT, preferred_element_type=F32)
                         + lax.dot_general(dup, wu_ref[:, c0:c0 + tf], NT, preferred_element_type=F32))
        xhat, r = _rms(h_ref[...])
        gg = g_ref[...]
        dx, dgp = _rms_bwd(xhat, r, gg, dhn)
        dg_ref[...] += dgp
        dhi_ref[...] = dho_ + dx
        hn_ref[...] = (xhat * gg).astype(BF16)

    row = lambda n: pl.BlockSpec((tm, n), lambda i: (i, 0))
    return pl.pallas_call(
        body, name="ffn_bwd", grid=(tp // tm,),
        out_shape=[jax.ShapeDtypeStruct((tp, d), F32), jax.ShapeDtypeStruct((tp, d), BF16),
                   jax.ShapeDtypeStruct((tp, f), BF16), jax.ShapeDtypeStruct((tp, f), BF16),
                   jax.ShapeDtypeStruct((tp, f), BF16), jax.ShapeDtypeStruct((1, d), F32)],
        in_specs=[row(d), row(d), _resident((1, d)), row(f), row(f),
                  _resident((d, f)), _resident((d, f)), _resident((f, d))],
        out_specs=[row(d), row(d), row(f), row(f), row(f), pl.BlockSpec((1, d), lambda i: (0, 0))],
        compiler_params=_cparams(1),
    )(dho, h, g, gate, up, wg, wu, wd)


def atb(a, b, scale=1.0):
    tp, m = a.shape
    n = b.shape[1]
    tmm = _pick(m, (1024, 1408, 640, 512, 256, 128))
    tn = _pick(n, (1024, 1408, 640, 512, 256, 128))
    tk = _pick(tp, (768, 512, 256))
    nk = tp // tk

    def body(a_ref, b_ref, o_ref):
        k = pl.program_id(2)

        @pl.when(k == 0)
        def _():
            o_ref[...] = jnp.zeros_like(o_ref)

        o_ref[...] += lax.dot_general(a_ref[...].astype(BF16), b_ref[...].astype(BF16), TN,
                                      preferred_element_type=F32)

        if scale != 1.0:
            @pl.when(k == nk - 1)
            def _():
                o_ref[...] = o_ref[...] * scale

    return pl.pallas_call(
        body, name="atb", grid=(m // tmm, n // tn, nk),
        out_shape=jax.ShapeDtypeStruct((m, n), F32),
        in_specs=[pl.BlockSpec((tk, tmm), lambda i, j, k: (k, i)), pl.BlockSpec((tk, tn), lambda i, j, k: (k, j))],
        out_specs=pl.BlockSpec((tmm, tn), lambda i, j, k: (i, j)),
        compiler_params=_cparams(3),
    )(a, b)


def norm_matmul(h, g, w, out_dtype, emit_u):
    tp, d = h.shape
    n = w.shape[1]
    tm = _pick(tp, (768, 512, 256))
    tn = _pick(n, (1024, 768, 640, 512, 256, 128))

    def body(h_ref, g_ref, w_ref, o_ref, *rest):
        u_sc = rest[-1]

        @pl.when(pl.program_id(1) == 0)
        def _():
            xhat, _ = _rms(h_ref[...])
            u_sc[...] = (xhat * g_ref[...]).astype(BF16)
            if emit_u:
                rest[0][...] = u_sc[...]

        o_ref[...] = jnp.dot(u_sc[...], w_ref[...], preferred_element_type=F32).astype(out_dtype)

    out_shape = [jax.ShapeDtypeStruct((tp, n), out_dtype)]
    out_specs = [pl.BlockSpec((tm, tn), lambda i, j: (i, j))]
    if emit_u:
        out_shape.append(jax.ShapeDtypeStruct((tp, d), BF16))
        out_specs.append(pl.BlockSpec((tm, d), lambda i, j: (i, 0)))
    return pl.pallas_call(
        body, name="norm_matmul", grid=(tp // tm, n // tn), out_shape=out_shape,
        in_specs=[pl.BlockSpec((tm, d), lambda i, j: (i, 0)), pl.BlockSpec((1, d), lambda i, j: (0, 0)),
                  pl.BlockSpec((d, tn), lambda i, j: (0, j))],
        out_specs=out_specs, scratch_shapes=[pltpu.VMEM((tm, d), BF16)],
        compiler_params=_cparams(2),
    )(h, g, w)


def res_matmul(h, y, w):
    tp, d = h.shape
    k = y.shape[1]
    tm = _pick(tp, (768, 512, 256))

    def body(h_ref, y_ref, w_ref, o_ref):
        o_ref[...] = h_ref[...] + jnp.dot(y_ref[...], w_ref[...], preferred_element_type=F32)

    return pl.pallas_call(
        body, name="res_matmul", grid=(tp // tm,), out_shape=jax.ShapeDtypeStruct((tp, d), F32),
        in_specs=[pl.BlockSpec((tm, d), lambda i: (i, 0)), pl.BlockSpec((tm, k), lambda i: (i, 0)),
                  pl.BlockSpec((k, d), lambda i: (0, 0))],
        out_specs=pl.BlockSpec((tm, d), lambda i: (i, 0)),
        compiler_params=_cparams(1),
    )(h, y, w)


def matmul_nt(a, w):
    tp, d = a.shape
    k = w.shape[0]
    tm = _pick(tp, (768, 512, 256))

    def body(a_ref, w_ref, o_ref):
        o_ref[...] = lax.dot_general(a_ref[...].astype(BF16), w_ref[...], NT,
                                     preferred_element_type=F32).astype(BF16)

    return pl.pallas_call(
        body, name="matmul_nt", grid=(tp // tm,), out_shape=jax.ShapeDtypeStruct((tp, k), BF16),
        in_specs=[pl.BlockSpec((tm, d), lambda i: (i, 0)), pl.BlockSpec((k, d), lambda i: (0, 0))],
        out_specs=pl.BlockSpec((tm, k), lambda i: (i, 0)),
        compiler_params=_cparams(1),
    )(a, w)


def matmul_nt_rms_bwd(dp, w, h, g, dres):
    tp, d = h.shape
    n = w.shape[1]
    tm = _pick(tp, (384, 256))

    def body(dp_ref, w_ref, h_ref, g_ref, dres_ref, dh_ref, dg_ref):
        @pl.when(pl.program_id(0) == 0)
        def _():
            dg_ref[...] = jnp.zeros_like(dg_ref)

        du = lax.dot_general(dp_ref[...], w_ref[...], NT, preferred_element_type=F32)
        xhat, r = _rms(h_ref[...])
        dx, dgp = _rms_bwd(xhat, r, g_ref[...], du)
        dg_ref[...] += dgp
        dh_ref[...] = dres_ref[...] + dx

    return pl.pallas_call(
        body, name="matmul_nt_rms_bwd", grid=(tp // tm,),
        out_shape=[jax.ShapeDtypeStruct((tp, d), F32), jax.ShapeDtypeStruct((1, d), F32)],
        in_specs=[pl.BlockSpec((tm, n), lambda i: (i, 0)), pl.BlockSpec((d, n), lambda i: (0, 0)),
                  pl.BlockSpec((tm, d), lambda i: (i, 0)), pl.BlockSpec((1, d), lambda i: (0, 0)),
                  pl.BlockSpec((tm, d), lambda i: (i, 0))],
        out_specs=[pl.BlockSpec((tm, d), lambda i: (i, 0)), pl.BlockSpec((1, d), lambda i: (0, 0))],
        compiler_params=_cparams(1),
    )(dp, w, h, g, dres)


CONV_TILE = (256,)


def _halo_before(tm, hb, col=0):
    return lambda i: (jnp.maximum(i * (tm // hb) - 1, 0), col)


def _halo_after(tm, hb, nblk, col=0):
    return lambda i: (jnp.minimum((i + 1) * (tm // hb), nblk - 1), col)


def a_elem_fwd(p, w):
    tp, d3 = p.shape
    d = d3 // 3
    kw = w.shape[0]
    tm = _pick(tp, CONV_TILE)
    hb = 16

    def body(p_ref, ph_ref, w_ref, y_ref, ext):
        i = pl.program_id(0)
        halo = ph_ref[:, d:2 * d].astype(F32) * ph_ref[:, 2 * d:].astype(F32)
        ext[0:hb, :] = jnp.where(i > 0, halo, 0.0)
        ext[hb:, :] = p_ref[:, d:2 * d].astype(F32) * p_ref[:, 2 * d:].astype(F32)
        z = jnp.zeros((tm, d), F32)
        for k in range(kw):
            z = z + w_ref[k:k + 1, :] * ext[pl.ds(hb - (kw - 1) + k, tm), :]
        y_ref[...] = (p_ref[:, 0:d].astype(F32) * z).astype(BF16)

    return pl.pallas_call(
        body, name="a_elem_fwd", grid=(tp // tm,), out_shape=jax.ShapeDtypeStruct((tp, d), BF16),
        in_specs=[pl.BlockSpec((tm, d3), lambda i: (i, 0)), pl.BlockSpec((hb, d3), _halo_before(tm, hb)),
                  pl.BlockSpec((kw, d), lambda i: (0, 0))],
        out_specs=pl.BlockSpec((tm, d), lambda i: (i, 0)),
        scratch_shapes=[pltpu.VMEM((tm + hb, d), F32)],
        compiler_params=_cparams(1),
    )(p, p, w)


def a_elem_bwd(p, dy, w):
    tp, d3 = p.shape
    d = d3 // 3
    kw = w.shape[0]
    tm = _pick(tp, CONV_TILE)
    hb = 16
    n = tp // tm

    def body(p_ref, ph_ref, pa_ref, dy_ref, dya_ref, w_ref, dp_ref, dw_ref, ext, dzext):
        i = pl.program_id(0)

        @pl.when(i == 0)
        def _():
            dw_ref[...] = jnp.zeros_like(dw_ref)

        b = p_ref[:, 0:d].astype(F32)
        c = p_ref[:, d:2 * d].astype(F32)
        v = p_ref[:, 2 * d:].astype(F32)
        dy_ = dy_ref[...].astype(F32)
        halo = ph_ref[:, d:2 * d].astype(F32) * ph_ref[:, 2 * d:].astype(F32)
        ext[0:hb, :] = jnp.where(i > 0, halo, 0.0)
        ext[hb:, :] = c * v
        dz = dy_ * b
        dzext[0:tm, :] = dz
        dzext[tm:, :] = jnp.where(i < n - 1, dya_ref[...].astype(F32) * pa_ref[:, 0:d].astype(F32), 0.0)
        z = jnp.zeros((tm, d), F32)
        dcv = jnp.zeros((tm, d), F32)
        for k in range(kw):
            sh = ext[pl.ds(hb - (kw - 1) + k, tm), :]
            z = z + w_ref[k:k + 1, :] * sh
            dw_ref[k:k + 1, :] += jnp.sum(dz * sh, axis=0, keepdims=True)
            dcv = dcv + w_ref[k:k + 1, :] * dzext[pl.ds(kw - 1 - k, tm), :]
        dp_ref[:, 0:d] = (dy_ * z).astype(BF16)
        dp_ref[:, d:2 * d] = (dcv * v).astype(BF16)
        dp_ref[:, 2 * d:] = (dcv * c).astype(BF16)

    nb = tp // hb
    return pl.pallas_call(
        body, name="a_elem_bwd", grid=(n,),
        out_shape=[jax.ShapeDtypeStruct((tp, d3), BF16), jax.ShapeDtypeStruct((8, d), F32)],
        in_specs=[pl.BlockSpec((tm, d3), lambda i: (i, 0)), pl.BlockSpec((hb, d3), _halo_before(tm, hb)),
                  pl.BlockSpec((hb, d3), _halo_after(tm, hb, nb)),
                  pl.BlockSpec((tm, d), lambda i: (i, 0)), pl.BlockSpec((hb, d), _halo_after(tm, hb, nb)),
                  pl.BlockSpec((kw, d), lambda i: (0, 0))],
        out_specs=[pl.BlockSpec((tm, d3), lambda i: (i, 0)), pl.BlockSpec((8, d), lambda i: (0, 0))],
        scratch_shapes=[pltpu.VMEM((tm + hb, d), F32), pltpu.VMEM((tm + hb, d), F32)],
        compiler_params=_cparams(1),
    )(p, p, p, dy, dy, w)


def _b_u2(p_ref, ph_ref, w_ref, bias_ref, ext, i, tm, d, kw, hb):
    a = p_ref[:, 0:d].astype(F32)
    sg = _sigmoid(p_ref[:, d:].astype(F32))
    halo = ph_ref[:, 0:d].astype(F32) * _sigmoid(ph_ref[:, d:].astype(F32))
    ext[0:hb, :] = jnp.where(i > 0, halo, 0.0)
    ext[hb:, :] = a * sg
    u2 = jnp.zeros((tm, d), F32) + bias_ref[...]
    for k in range(kw):
        u2 = u2 + w_ref[k:k + 1, :] * ext[pl.ds(hb - (kw - 1) + k, tm), :]
    return a, sg, u2


def _ln(u2):
    mu = jnp.mean(u2, axis=-1, keepdims=True)
    xc = u2 - mu
    rstd = lax.rsqrt(jnp.mean(xc * xc, axis=-1, keepdims=True) + EPS)
    return xc * rstd, rstd


def b_elem_fwd(p, w, bias, ln_g, ln_b):
    tp, d2 = p.shape
    d = d2 // 2
    kw = w.shape[0]
    tm = _pick(tp, CONV_TILE)
    hb = 32

    def body(p_ref, ph_ref, w_ref, bias_ref, g_ref, b_ref, y_ref, ext):
        i = pl.program_id(0)
        _, _, u2 = _b_u2(p_ref, ph_ref, w_ref, bias_ref, ext, i, tm, d, kw, hb)
        xhat, _ = _ln(u2)
        u3 = xhat * g_ref[...] + b_ref[...]
        y_ref[...] = (u3 * _sigmoid(u3)).astype(BF16)

    vec = pl.BlockSpec((1, d), lambda i: (0, 0))
    return pl.pallas_call(
        body, name="b_elem_fwd", grid=(tp // tm,), out_shape=jax.ShapeDtypeStruct((tp, d), BF16),
        in_specs=[pl.BlockSpec((tm, d2), lambda i: (i, 0)), pl.BlockSpec((hb, d2), _halo_before(tm, hb)),
                  pl.BlockSpec((kw, d), lambda i: (0, 0)), vec, vec, vec],
        out_specs=pl.BlockSpec((tm, d), lambda i: (i, 0)),
        scratch_shapes=[pltpu.VMEM((tm + hb, d), F32)],
        compiler_params=_cparams(1),
    )(p, p, w, bias, ln_g, ln_b)


def b_elem_bwd1(p, dy, w, bias, ln_g, ln_b):
    tp, d2 = p.shape
    d = d2 // 2
    kw = w.shape[0]
    tm = _pick(tp, CONV_TILE)
    hb = 32

    def body(p_ref, ph_ref, dy_ref, w_ref, bias_ref, g_ref, b_ref, du2_ref, st_ref, ext):
        i = pl.program_id(0)

        @pl.when(i == 0)
        def _():
            st_ref[...] = jnp.zeros_like(st_ref)

        _, _, u2 = _b_u2(p_ref, ph_ref, w_ref, bias_ref, ext, i, tm, d, kw, hb)
        xhat, rstd = _ln(u2)
        u3 = xhat * g_ref[...] + b_ref[...]
        s3 = _sigmoid(u3)
        du3 = dy_ref[...].astype(F32) * (s3 * (1.0 + u3 * (1.0 - s3)))
        dxh = du3 * g_ref[...]
        du2 = rstd * (dxh - jnp.mean(dxh, axis=-1, keepdims=True)
                      - xhat * jnp.mean(dxh * xhat, axis=-1, keepdims=True))
        du2_ref[...] = du2
        st_ref[0:1, :] += jnp.sum(du3 * xhat, axis=0, keepdims=True)
        st_ref[1:2, :] += jnp.sum(du3, axis=0, keepdims=True)
        st_ref[2:3, :] += jnp.sum(du2, axis=0, keepdims=True)

    vec = pl.BlockSpec((1, d), lambda i: (0, 0))
    return pl.pallas_call(
        body, name="b_elem_bwd1", grid=(tp // tm,),
        out_shape=[jax.ShapeDtypeStruct((tp, d), F32), jax.ShapeDtypeStruct((8, d), F32)],
        in_specs=[pl.BlockSpec((tm, d2), lambda i: (i, 0)), pl.BlockSpec((hb, d2), _halo_before(tm, hb)),
                  pl.BlockSpec((tm, d), lambda i: (i, 0)), pl.BlockSpec((kw, d), lambda i: (0, 0)), vec, vec, vec],
        out_specs=[pl.BlockSpec((tm, d), lambda i: (i, 0)), pl.BlockSpec((8, d), lambda i: (0, 0))],
        scratch_shapes=[pltpu.VMEM((tm + hb, d), F32)],
        compiler_params=_cparams(1),
    )(p, p, dy, w, bias, ln_g, ln_b)


def b_elem_bwd2(p, du2, w):
    tp, d2 = p.shape
    d = d2 // 2
    kw = w.shape[0]
    tm = _pick(tp, CONV_TILE)
    hb = 32
    n = tp // tm

    def body(p_ref, ph_ref, du2_ref, du2a_ref, w_ref, dp_ref, dw_ref, ext, dext):
        i = pl.program_id(0)

        @pl.when(i == 0)
        def _():
            dw_ref[...] = jnp.zeros_like(dw_ref)

        a = p_ref[:, 0:d].astype(F32)
        sg = _sigmoid(p_ref[:, d:].astype(F32))
        halo = ph_ref[:, 0:d].astype(F32) * _sigmoid(ph_ref[:, d:].astype(F32))
        ext[0:hb, :] = jnp.where(i > 0, halo, 0.0)
        ext[hb:, :] = a * sg
        du2_ = du2_ref[...]
        dext[0:tm, :] = du2_
        dext[tm:, :] = jnp.where(i < n - 1, du2a_ref[...], 0.0)
        du1 = jnp.zeros((tm, d), F32)
        for k in range(kw):
            dw_ref[k:k + 1, :] += jnp.sum(du2_ * ext[pl.ds(hb - (kw - 1) + k, tm), :], axis=0, keepdims=True)
            du1 = du1 + w_ref[k:k + 1, :] * dext[pl.ds(kw - 1 - k, tm), :]
        dp_ref[:, 0:d] = (du1 * sg).astype(BF16)
        dp_ref[:, d:] = (du1 * a * sg * (1.0 - sg)).astype(BF16)

    nb = tp // hb
    return pl.pallas_call(
        body, name="b_elem_bwd2", grid=(n,),
        out_shape=[jax.ShapeDtypeStruct((tp, d2), BF16), jax.ShapeDtypeStruct((32, d), F32)],
        in_specs=[pl.BlockSpec((tm, d2), lambda i: (i, 0)), pl.BlockSpec((hb, d2), _halo_before(tm, hb)),
                  pl.BlockSpec((tm, d), lambda i: (i, 0)), pl.BlockSpec((hb, d), _halo_after(tm, hb, nb)),
                  pl.BlockSpec((kw, d), lambda i: (0, 0))],
        out_specs=[pl.BlockSpec((tm, d2), lambda i: (i, 0)), pl.BlockSpec((32, d), lambda i: (0, 0))],
        scratch_shapes=[pltpu.VMEM((tm + hb, d), F32), pltpu.VMEM((tm + hb, d), F32)],
        compiler_params=_cparams(1),
    )(p, p, du2, du2, w)


ATT_TILE = 256


def _head_masks(d):
    c = lax.broadcasted_iota(jnp.int32, (d, LANES), 0)
    h = lax.broadcasted_iota(jnp.int32, (d, LANES), 1)
    seg = (c // HEAD_DIM == h).astype(F32)
    fold = (c % HEAD_DIM == h).astype(F32)
    return seg, seg.T, fold


def c_prep(p, b_f, qg, kg):
    tp = p.shape[0]
    d = qg.shape[1]
    tm = ATT_TILE
    seg, seg_t, _ = _head_masks(d)

    def body(p_ref, pf_ref, bf_ref, qg_ref, kg_ref, seg_ref, segt_ref, q_ref, k_ref, v_ref, cum_ref, carry):
        @pl.when(pl.program_id(0) == 0)
        def _():
            carry[...] = jnp.zeros_like(carry)

        def norm(x, g):
            ms = jnp.dot(x * x, seg_ref[...], precision=HI, preferred_element_type=F32) * (1.0 / HEAD_DIM)
            r = jnp.dot(lax.rsqrt(ms + EPS), segt_ref[...], precision=HI, preferred_element_type=F32)
            return x * r * g

        q_ref[...] = (norm(p_ref[:, 0:d], qg_ref[...]) * (HEAD_DIM ** -0.5)).astype(BF16)
        k_ref[...] = norm(p_ref[:, d:2 * d], kg_ref[...]).astype(BF16)
        v_ref[...] = p_ref[:, 2 * d:].astype(BF16)
        xf = pf_ref[...] + bf_ref[...]
        logf = jnp.minimum(xf, 0.0) - jnp.log(1.0 + jnp.exp(-jnp.abs(xf)))
        r_ = lax.broadcasted_iota(jnp.int32, (tm, tm), 0)
        c_ = lax.broadcasted_iota(jnp.int32, (tm, tm), 1)
        tri = (c_ <= r_).astype(F32)
        cum_ref[...] = jnp.dot(tri, logf, precision=HI, preferred_element_type=F32) + carry[0:1, :]
        carry[0:1, :] += jnp.sum(logf, axis=0, keepdims=True)

    vec = pl.BlockSpec((1, d), lambda i: (0, 0))
    rowd = pl.BlockSpec((tm, d), lambda i: (i, 0))
    return pl.pallas_call(
        body, name="c_prep", grid=(tp // tm,),
        out_shape=[jax.ShapeDtypeStruct((tp, d), BF16)] * 3 + [jax.ShapeDtypeStruct((tp, LANES), F32)],
        in_specs=[pl.BlockSpec((tm, 3 * d), lambda i: (i, 0)), pl.BlockSpec((tm, LANES), lambda i: (i, 3 * d // LANES)),
                  pl.BlockSpec((1, LANES), lambda i: (0, 0)), vec, vec,
                  pl.BlockSpec((d, LANES), lambda i: (0, 0)), pl.BlockSpec((LANES, d), lambda i: (0, 0))],
        out_specs=[rowd, rowd, rowd, pl.BlockSpec((tm, LANES), lambda i: (i, 0))],
        scratch_shapes=[pltpu.VMEM((8, LANES), F32)],
        compiler_params=_cparams(1),
    )(p, p, b_f, qg, kg, seg, seg_t)


def _pair_split(x2, lo):
    return jnp.where(lo, x2, jnp.zeros_like(x2)), jnp.where(lo, jnp.zeros_like(x2), x2)


def _lo_mask():
    return lax.broadcasted_iota(jnp.int32, (1, LANES), 1) < HEAD_DIM


def _scores(qh, k2, cqh, ckh, causal):
    s = lax.dot_general(qh, k2, NT, preferred_element_type=F32)
    return jnp.where(causal, s + cqh - ckh, NEG)


def attn_fwd(q, k, v, cqp, ckr):
    tp, d = q.shape
    t = ATT_TILE
    nb = tp // t
    npair = d // LANES

    def body(q_ref, k_ref, v_ref, cq_ref, ck_ref, o_ref, lse_ref):
        qi = pl.program_id(1)
        lo = _lo_mask()
        qa, qb = _pair_split(q_ref[...], lo)
        cq = cq_ref[0]
        cqa, cqb = cq[:, 0:1], cq[:, HEAD_DIM:HEAD_DIM + 1]
        rows = qi * t + lax.broadcasted_iota(jnp.int32, (t, t), 0)
        cols0 = lax.broadcasted_iota(jnp.int32, (t, t), 1)

        def step(kj, carry):
            off = pl.multiple_of(kj * t, t)
            k2 = k_ref[pl.ds(off, t), :]
            v2 = v_ref[pl.ds(off, t), :]
            ck = ck_ref[0, kj]
            causal = cols0 + off <= rows
            out = []
            for hd, (qh, cqh) in enumerate(((qa, cqa), (qb, cqb))):
                m, l, acc = carry[3 * hd:3 * hd + 3]
                s = _scores(qh, k2, cqh, ck[hd:hd + 1, :], causal)
                m_new = jnp.maximum(m, jnp.max(s, axis=-1, keepdims=True))
                alpha = jnp.exp(m - m_new)
                pr = jnp.exp(s - m_new)
                l = alpha * l + jnp.sum(pr, axis=-1, keepdims=True)
                acc = alpha * acc + jnp.dot(pr.astype(BF16), v2, preferred_element_type=F32)
                out += [m_new, l, acc]
            return tuple(out)

        init = (jnp.full((t, 1), NEG, F32), jnp.zeros((t, 1), F32), jnp.zeros((t, LANES), F32)) * 2
        ma, la, acca, mb, lb, accb = lax.fori_loop(0, qi + 1, step, init)
        o_ref[...] = jnp.where(lo, acca / la, accb / lb).astype(BF16)
        lse_ref[0] = jnp.where(lo, ma + jnp.log(la), mb + jnp.log(lb))

    return pl.pallas_call(
        body, name="attn_fwd", grid=(npair, nb),
        out_shape=[jax.ShapeDtypeStruct((tp, d), BF16), jax.ShapeDtypeStruct((npair, tp, LANES), F32)],
        in_specs=[pl.BlockSpec((t, LANES), lambda h, i: (i, h)), pl.BlockSpec((tp, LANES), lambda h, i: (0, h)),
                  pl.BlockSpec((tp, LANES), lambda h, i: (0, h)),
                  pl.BlockSpec((1, t, LANES), lambda h, i: (h, i, 0)),
                  pl.BlockSpec((1, nb, 2, t), lambda h, i: (h, 0, 0, 0))],
        out_specs=[pl.BlockSpec((t, LANES), lambda h, i: (i, h)), pl.BlockSpec((1, t, LANES), lambda h, i: (h, i, 0))],
        compiler_params=_cparams(2),
    )(q, k, v, cqp, ckr)


def attn_bwd_dq(q, k, v, cqp, ckr, do, lse):
    tp, d = q.shape
    t = ATT_TILE
    nb = tp // t
    npair = d // LANES

    def body(q_ref, k_ref, v_ref, cq_ref, ck_ref, do_ref, lse_ref, dq_ref, delta_ref):
        qi = pl.program_id(1)
        lo = _lo_mask()
        qa, qb = _pair_split(q_ref[...], lo)
        doa, dob = _pair_split(do_ref[...], lo)
        cq = cq_ref[0]
        lse2 = lse_ref[0]
        heads = ((qa, doa, cq[:, 0:1], lse2[:, 0:1]),
                 (qb, dob, cq[:, HEAD_DIM:HEAD_DIM + 1], lse2[:, HEAD_DIM:HEAD_DIM + 1]))
        rows = qi * t + lax.broadcasted_iota(jnp.int32, (t, t), 0)
        cols0 = lax.broadcasted_iota(jnp.int32, (t, t), 1)

        def p_dp(kj, hd):
            qh, doh, cqh, lseh = heads[hd]
            off = pl.multiple_of(kj * t, t)
            k2 = k_ref[pl.ds(off, t), :]
            ck = ck_ref[0, kj]
            pr = jnp.exp(_scores(qh, k2, cqh, ck[hd:hd + 1, :], cols0 + off <= rows) - lseh)
            dpr = lax.dot_general(doh, v_ref[pl.ds(off, t), :], NT, preferred_element_type=F32)
            return pr, dpr, k2

        def delta_step(kj, carry):
            out = []
            for hd in range(2):
                pr, dpr, _ = p_dp(kj, hd)
                out.append(carry[hd] + jnp.sum(pr * dpr, axis=-1, keepdims=True))
            return tuple(out)

        dl = lax.fori_loop(0, qi + 1, delta_step, (jnp.zeros((t, 1), F32),) * 2)
        delta_ref[0] = jnp.where(lo, dl[0], dl[1])

        def step(kj, carry):
            out = []
            for hd in range(2):
                pr, dpr, k2 = p_dp(kj, hd)
                ds = (pr * (dpr - dl[hd])).astype(BF16)
                out.append(carry[hd] + jnp.dot(ds, k2, preferred_element_type=F32))
            return tuple(out)

        dqa, dqb = lax.fori_loop(0, qi + 1, step, (jnp.zeros((t, LANES), F32),) * 2)
        dq_ref[...] = jnp.where(lo, dqa, dqb)

    pair_rows = pl.BlockSpec((1, t, LANES), lambda h, i: (h, i, 0))
    blk = pl.BlockSpec((t, LANES), lambda h, i: (i, h))
    full = pl.BlockSpec((tp, LANES), lambda h, i: (0, h))
    return pl.pallas_call(
        body, name="attn_bwd_dq", grid=(npair, nb),
        out_shape=[jax.ShapeDtypeStruct((tp, d), F32), jax.ShapeDtypeStruct((npair, tp, LANES), F32)],
        in_specs=[blk, full, full, pair_rows, pl.BlockSpec((1, nb, 2, t), lambda h, i: (h, 0, 0, 0)), blk, pair_rows],
        out_specs=[blk, pair_rows],
        compiler_params=_cparams(2),
    )(q, k, v, cqp, ckr, do, lse)


def attn_bwd_dkv(q, k, v, cqp, ckr, do, lse, delta):
    tp, d = q.shape
    t = ATT_TILE
    nb = tp // t
    npair = d // LANES

    def body(q_ref, k_ref, v_ref, cq_ref, ck_ref, do_ref, lse_ref, delta_ref, dk_ref, dv_ref, dck_ref):
        kj = pl.program_id(1)
        lo = _lo_mask()
        k2 = k_ref[...]
        v2 = v_ref[...]
        ck = ck_ref[0, 0]
        cols = kj * t + lax.broadcasted_iota(jnp.int32, (t, t), 1)
        rows0 = lax.broadcasted_iota(jnp.int32, (t, t), 0)

        def step(qi, carry):
            dk, dv, dca, dcb = carry
            off = pl.multiple_of(qi * t, t)
            qa, qb = _pair_split(q_ref[pl.ds(off, t), :], lo)
            doa, dob = _pair_split(do_ref[pl.ds(off, t), :], lo)
            cq = cq_ref[0, pl.ds(off, t), :]
            lse2 = lse_ref[0, pl.ds(off, t), :]
            dl2 = delta_ref[0, pl.ds(off, t), :]
            causal = cols <= rows0 + off
            dcs = []
            for hd, (qh, doh) in enumerate(((qa, doa), (qb, dob))):
                c0 = hd * HEAD_DIM
                pr = jnp.exp(_scores(qh, k2, cq[:, c0:c0 + 1], ck[hd:hd + 1, :], causal) - lse2[:, c0:c0 + 1])
                dpr = lax.dot_general(doh, v2, NT, preferred_element_type=F32)
                ds = pr * (dpr - dl2[:, c0:c0 + 1])
                dv = dv + lax.dot_general(pr.astype(BF16), doh, TN, preferred_element_type=F32)
                dk = dk + lax.dot_general(ds.astype(BF16), qh, TN, preferred_element_type=F32)
                dcs.append(jnp.sum(ds, axis=0, keepdims=True))
            return dk, dv, dca - dcs[0], dcb - dcs[1]

        zero = jnp.zeros((t, LANES), F32)
        zrow = jnp.zeros((1, t), F32)
        dk, dv, dca, dcb = lax.fori_loop(kj, nb, step, (zero, zero, zrow, zrow))
        dk_ref[...] = dk
        dv_ref[...] = dv.astype(BF16)
        dck_ref[0, 0, 0:1, :] = dca
        dck_ref[0, 0, 1:2, :] = dcb

    full = pl.BlockSpec((tp, LANES), lambda h, j: (0, h))
    pair_full = pl.BlockSpec((1, tp, LANES), lambda h, j: (h, 0, 0))
    blk = pl.BlockSpec((t, LANES), lambda h, j: (j, h))
    ckblk = pl.BlockSpec((1, 1, 2, t), lambda h, j: (h, j, 0, 0))
    return pl.pallas_call(
        body, name="attn_bwd_dkv", grid=(npair, nb),
        out_shape=[jax.ShapeDtypeStruct((tp, d), F32), jax.ShapeDtypeStruct((tp, d), BF16),
                   jax.ShapeDtypeStruct((npair, nb, 2, t), F32)],
        in_specs=[full, blk, blk, pair_full, ckblk, full, pair_full, pair_full],
        out_specs=[blk, blk, ckblk],
        compiler_params=_cparams(2),
    )(q, k, v, cqp, ckr, do, lse, delta)


L2E = 1.4426950408889634
LN2 = 0.6931471805599453
AUG = HEAD_DIM


def _split3(r):
    r1 = r.astype(BF16).astype(F32)
    r2 = (r - r1).astype(BF16).astype(F32)
    r3 = (r - r1 - r2).astype(BF16).astype(F32)
    return r1, r2, r3


def c_prep_slots(p, b_f, qg, kg):
    tp = p.shape[0]
    d = qg.shape[1]
    n_heads = d // HEAD_DIM
    tm = ATT_TILE
    seg, seg_t, _ = _head_masks(d)

    def body(p_ref, pf_ref, bf_ref, qg_ref, kg_ref, seg_ref, segt_ref, q_ref, k_ref, v_ref, cum_ref, carry):
        @pl.when(pl.program_id(0) == 0)
        def _():
            carry[...] = jnp.zeros_like(carry)

        def norm(x, g):
            ms = jnp.dot(x * x, seg_ref[...], precision=HI, preferred_element_type=F32) * (1.0 / HEAD_DIM)
            r = jnp.dot(lax.rsqrt(ms + EPS), segt_ref[...], precision=HI, preferred_element_type=F32)
            return x * r * g

        qn = norm(p_ref[:, 0:d], qg_ref[...]) * (HEAD_DIM ** -0.5 * L2E)
        kn = norm(p_ref[:, d:2 * d], kg_ref[...])
        v_ref[...] = p_ref[:, 2 * d:].astype(BF16)
        xf = pf_ref[...] + bf_ref[...]
        logf = jnp.minimum(xf, 0.0) - jnp.log(1.0 + jnp.exp(-jnp.abs(xf)))
        r_ = lax.broadcasted_iota(jnp.int32, (tm, tm), 0)
        c_ = lax.broadcasted_iota(jnp.int32, (tm, tm), 1)
        cum = jnp.dot((c_ <= r_).astype(F32), logf, precision=HI, preferred_element_type=F32) + carry[0:1, :]
        cum_ref[...] = cum
        carry[0:1, :] += jnp.sum(logf, axis=0, keepdims=True)

        lane = lax.broadcasted_iota(jnp.int32, (1, LANES), 1)
        ones_q = jnp.where((lane >= AUG + 3) & (lane < AUG + 6), 1.0, 0.0)
        ones_k = jnp.where((lane >= AUG) & (lane < AUG + 3), 1.0, 0.0)
        for h in range(n_heads):
            c0 = LANES * (h // 2)
            src_q, src_k = qn[:, c0:c0 + LANES], kn[:, c0:c0 + LANES]
            if h % 2:
                src_q = pltpu.roll(src_q, HEAD_DIM, axis=1)
                src_k = pltpu.roll(src_k, HEAD_DIM, axis=1)
            c = cum[:, h:h + 1] * L2E
            a1, a2, a3 = _split3(c)
            b1, b2, b3 = _split3(-c)
            aug_q = jnp.where(lane == AUG, a1, jnp.where(lane == AUG + 1, a2, jnp.where(lane == AUG + 2, a3, ones_q)))
            aug_k = jnp.where(lane == AUG + 3, b1,
                              jnp.where(lane == AUG + 4, b2, jnp.where(lane == AUG + 5, b3, ones_k)))
            q_ref[:, LANES * h:LANES * (h + 1)] = jnp.where(lane < HEAD_DIM, src_q, aug_q).astype(BF16)
            k_ref[:, LANES * h:LANES * (h + 1)] = jnp.where(lane < HEAD_DIM, src_k, aug_k).astype(BF16)

    vec = pl.BlockSpec((1, d), lambda i: (0, 0))
    rowd = pl.BlockSpec((tm, d), lambda i: (i, 0))
    slots = pl.BlockSpec((tm, n_heads * LANES), lambda i: (i, 0))
    return pl.pallas_call(
        body, name="c_prep_slots", grid=(tp // tm,),
        out_shape=[jax.ShapeDtypeStruct((tp, n_heads * LANES), BF16)] * 2
        + [jax.ShapeDtypeStruct((tp, d), BF16), jax.ShapeDtypeStruct((tp, LANES), F32)],
        in_specs=[pl.BlockSpec((tm, 3 * d), lambda i: (i, 0)), pl.BlockSpec((tm, LANES), lambda i: (i, 3 * d // LANES)),
                  pl.BlockSpec((1, LANES), lambda i: (0, 0)), vec, vec,
                  pl.BlockSpec((d, LANES), lambda i: (0, 0)), pl.BlockSpec((LANES, d), lambda i: (0, 0))],
        out_specs=[slots, slots, rowd, pl.BlockSpec((tm, LANES), lambda i: (i, 0))],
        scratch_shapes=[pltpu.VMEM((8, LANES), F32)],
        compiler_params=_cparams(1),
    )(p, p, b_f, qg, kg, seg, seg_t)


ATT_GROUP = 4


def _grouped_loop(n_blocks, body, carry):
    n_groups = n_blocks // ATT_GROUP
    carry = lax.fori_loop(0, n_groups, lambda i, c: body(i * ATT_GROUP, ATT_GROUP, c), carry)
    return lax.fori_loop(n_groups * ATT_GROUP, n_blocks, lambda kj, c: body(kj, 1, c), carry)


def _diag_mask(t):
    return lax.broadcasted_iota(jnp.int32, (t, t), 0) <= lax.broadcasted_iota(jnp.int32, (t, t), 1)


def attn_fwd_t(qpt, kp, vt4):
    tp = kp.shape[0]
    t = ATT_TILE
    nb = tp // t
    npair = vt4.shape[0]

    def body(q_ref, k_ref, v_ref, o_ref, lse_ref):
        qi = pl.program_id(1)
        qts = (q_ref[0:LANES, :], q_ref[LANES:2 * LANES, :])
        mask = _diag_mask(t)

        def blocks(kj0, n, carry, masked=False):
            out = []
            for hd in range(2):
                m, l, acc = carry[3 * hd:3 * hd + 3]
                sts = []
                for g in range(n):
                    off = pl.multiple_of((kj0 + g) * t, t)
                    st = jnp.dot(k_ref[pl.ds(off, t), LANES * hd:LANES * (hd + 1)], qts[hd],
                                 preferred_element_type=F32)
                    sts.append(jnp.where(mask, st, NEG) if masked else st)
                m_new = m
                for st in sts:
                    m_new = jnp.maximum(m_new, jnp.max(st, axis=0, keepdims=True))
                alpha = jnp.exp2(m - m_new)
                l = alpha * l
                acc = alpha * acc
                for g, st in enumerate(sts):
                    pt = jnp.exp2(st - m_new)
                    l = l + jnp.sum(pt, axis=0, keepdims=True)
                    acc = acc + jnp.dot(v_ref[0, kj0 + g], pt.astype(BF16), preferred_element_type=F32)
                out += [m_new, l, acc]
            return tuple(out)

        init = (jnp.full((1, t), NEG, F32), jnp.zeros((1, t), F32), jnp.zeros((LANES, t), F32)) * 2
        carry = _grouped_loop(qi, blocks, init)
        ma, la, acca, mb, lb, accb = blocks(qi, 1, carry, masked=True)
        row = lax.broadcasted_iota(jnp.int32, (LANES, 1), 0)
        o_ref[...] = jnp.where(row < HEAD_DIM, acca / la, accb / lb).astype(BF16)
        lse_ref[0, 0:1, :] = ma + jnp.log(la) * L2E
        lse_ref[0, 1:2, :] = mb + jnp.log(lb) * L2E

    return pl.pallas_call(
        body, name="attn_fwd_t", grid=(npair, nb),
        out_shape=[jax.ShapeDtypeStruct((npair * LANES, tp), BF16), jax.ShapeDtypeStruct((npair, 2, tp), F32)],
        in_specs=[pl.BlockSpec((2 * LANES, t), lambda h, i: (h, i)), pl.BlockSpec((tp, 2 * LANES), lambda h, i: (0, h)),
                  pl.BlockSpec((1, nb, LANES, t), lambda h, i: (h, 0, 0, 0))],
        out_specs=[pl.BlockSpec((LANES, t), lambda h, i: (h, i)), pl.BlockSpec((1, 2, t), lambda h, i: (h, 0, i))],
        compiler_params=_cparams(2),
    )(qpt, kp, vt4)


def attn_bwd_t(qpt, qp, kp, kpt4, v, do, dot_, lse, cqt):
    tp = kp.shape[0]
    t = ATT_TILE
    nb = tp // t
    n_heads = kpt4.shape[0]
    d = v.shape[1]

    def body(qt_ref, q_ref, k_ref, kt_ref, v_ref, do_ref, dot_ref, lse_ref, cq_ref, dq_ref, dk_ref, dv_ref,
             p_sc, dp_sc, dc_sc):
        hd = pl.program_id(0) % 2
        qi = pl.program_id(1)

        @pl.when(qi == 0)
        def _():
            dk_ref[...] = jnp.zeros_like(dk_ref)
            dc_sc[...] = jnp.zeros_like(dc_sc)

        @pl.when((qi == 0) & (hd == 0))
        def _():
            dv_ref[...] = jnp.zeros_like(dv_ref)

        row = lax.broadcasted_iota(jnp.int32, (LANES, 1), 0)
        lane = lax.broadcasted_iota(jnp.int32, (1, LANES), 1)
        dot_h = jnp.where(row // HEAD_DIM == hd, dot_ref[...], jnp.zeros_like(dot_ref))
        do_h = jnp.where(lane // HEAD_DIM == hd, do_ref[...], jnp.zeros_like(do_ref))
        rr = cq_ref[0] - lse_ref[0]
        r1, r2, r3 = _split3(jnp.where(hd == 0, rr[0:1, :], rr[1:2, :]))
        qt = qt_ref[...].astype(F32)
        qt = jnp.where(row == AUG, r1, jnp.where(row == AUG + 1, r2, jnp.where(row == AUG + 2, r3, qt))).astype(BF16)
        mask = _diag_mask(t)

        def pass1(kj0, n, delta, masked=False):
            for g in range(n):
                off = pl.multiple_of((kj0 + g) * t, t)
                st = jnp.dot(k_ref[pl.ds(off, t), :], qt, preferred_element_type=F32)
                if masked:
                    st = jnp.where(mask, st, NEG)
                pt = jnp.exp2(st)
                dpt = jnp.dot(v_ref[pl.ds(off, t), :], dot_h, preferred_element_type=F32)
                p_sc[kj0 + g] = pt
                dp_sc[kj0 + g] = dpt
                delta = delta + jnp.sum(pt * dpt, axis=0, keepdims=True)
            return delta

        delta = _grouped_loop(qi, pass1, jnp.zeros((1, t), F32))
        delta = pass1(qi, 1, delta, masked=True)

        def pass2(kj0, n, dq):
            for g in range(n):
                off = pl.multiple_of((kj0 + g) * t, t)
                pt = p_sc[kj0 + g]
                ds32 = pt * (dp_sc[kj0 + g] - delta)
                ds = ds32.astype(BF16)
                dc_sc[pl.ds(off, t), :] += ds32[:, 0:LANES] + ds32[:, LANES:2 * LANES]
                dk_ref[pl.ds(off, t), :] += jnp.dot(ds, q_ref[...], preferred_element_type=F32)
                dv_ref[pl.ds(off, t), :] += jnp.dot(pt.astype(BF16), do_h, preferred_element_type=F32)
                dq = dq + jnp.dot(kt_ref[0, kj0 + g], ds, preferred_element_type=F32)
            return dq

        dq_ref[...] = _grouped_loop(qi + 1, pass2, jnp.zeros((LANES, t), F32))

        @pl.when(qi == nb - 1)
        def _():
            dk_ref[:, AUG + 3:AUG + 4] = jnp.sum(dc_sc[...], axis=1, keepdims=True)

    once = dict(pipeline_mode=pl.Buffered(1))
    pair_rows = pl.BlockSpec((1, 2, t), lambda h, i: (h // 2, 0, i))
    return pl.pallas_call(
        body, name="attn_bwd_t", grid=(n_heads, nb),
        out_shape=[jax.ShapeDtypeStruct((n_heads * LANES, tp), F32), jax.ShapeDtypeStruct((tp, n_heads * LANES), F32),
                   jax.ShapeDtypeStruct((tp, d), F32)],
        in_specs=[pl.BlockSpec((LANES, t), lambda h, i: (h, i)), pl.BlockSpec((t, LANES), lambda h, i: (i, h)),
                  pl.BlockSpec((tp, LANES), lambda h, i: (0, h), **once),
                  pl.BlockSpec((1, nb, LANES, t), lambda h, i: (h, 0, 0, 0), **once),
                  pl.BlockSpec((tp, LANES), lambda h, i: (0, h // 2), **once),
                  pl.BlockSpec((t, LANES), lambda h, i: (i, h // 2)), pl.BlockSpec((LANES, t), lambda h, i: (h // 2, i)),
                  pair_rows, pair_rows],
        out_specs=[pl.BlockSpec((LANES, t), lambda h, i: (h, i)), pl.BlockSpec((tp, LANES), lambda h, i: (0, h)),
                   pl.BlockSpec((tp, LANES), lambda h, i: (0, h // 2))],
        scratch_shapes=[pltpu.VMEM((nb, t, t), F32), pltpu.VMEM((nb, t, t), F32), pltpu.VMEM((tp, LANES), F32)],
        compiler_params=_cparams(2),
    )(qpt, qp, kp, kpt4, v, do, dot_, lse, cqt)


def rev_cumsum_rows(x):
    r, tp = x.shape
    t = ATT_TILE
    nb = tp // t

    def body(x_ref, o_ref, carry):
        @pl.when(pl.program_id(0) == 0)
        def _():
            carry[...] = jnp.zeros_like(carry)

        xv = x_ref[...]
        r_ = lax.broadcasted_iota(jnp.int32, (t, t), 0)
        c_ = lax.broadcasted_iota(jnp.int32, (t, t), 1)
        o_ref[...] = jnp.dot(xv, (r_ >= c_).astype(F32), precision=HI, preferred_element_type=F32) + carry[:, 0:1]
        carry[...] += jnp.sum(xv, axis=1, keepdims=True)

    return pl.pallas_call(
        body, name="rev_cumsum_rows", grid=(nb,), out_shape=jax.ShapeDtypeStruct((r, tp), F32),
        in_specs=[pl.BlockSpec((r, t), lambda i: (0, nb - 1 - i))],
        out_specs=pl.BlockSpec((r, t), lambda i: (0, nb - 1 - i)),
        scratch_shapes=[pltpu.VMEM((r, LANES), F32)],
        compiler_params=_cparams(1),
    )(x)


def c_elem_bwd(p, b_f, qg, kg, dq, dk, dv, dlogf):
    tp, n_out = p.shape
    d = qg.shape[1]
    tm = ATT_TILE
    n = tp // tm
    seg, seg_t, fold = _head_masks(d)

    def body(p_ref, pf_ref, bf_ref, qg_ref, kg_ref, dq_ref, dk_ref, dv_ref, dlf_ref, seg_ref, segt_ref, fold_ref,
             dp_ref, sm_ref, accq, acck, accf):
        i = pl.program_id(0)

        @pl.when(i == 0)
        def _():
            accq[...] = jnp.zeros_like(accq)
            acck[...] = jnp.zeros_like(acck)
            accf[...] = jnp.zeros_like(accf)

        def norm_bwd(x, g, dy, acc):
            ms = jnp.dot(x * x, seg_ref[...], precision=HI, preferred_element_type=F32) * (1.0 / HEAD_DIM)
            r = jnp.dot(lax.rsqrt(ms + EPS), segt_ref[...], precision=HI, preferred_element_type=F32)
            xhat = x * r
            acc[0:1, :] += jnp.sum(dy * xhat, axis=0, keepdims=True)
            gy = dy * g
            mean = jnp.dot(jnp.dot(gy * xhat, seg_ref[...], precision=HI, preferred_element_type=F32),
                           segt_ref[...], precision=HI, preferred_element_type=F32) * (1.0 / HEAD_DIM)
            return r * (gy - xhat * mean)

        dp_ref[:, 0:d] = norm_bwd(p_ref[:, 0:d], qg_ref[...], dq_ref[...] * (HEAD_DIM ** -0.5), accq).astype(BF16)
        dp_ref[:, d:2 * d] = norm_bwd(p_ref[:, d:2 * d], kg_ref[...], dk_ref[...] * LN2, acck).astype(BF16)
        dp_ref[:, 2 * d:3 * d] = dv_ref[...].astype(BF16)
        df = dlf_ref[...] * _sigmoid(-(pf_ref[...] + bf_ref[...]))
        accf[0:1, :] += jnp.sum(df, axis=0, keepdims=True)
        dp_ref[:, 3 * d:] = df.astype(BF16)

        @pl.when(i == n - 1)
        def _():
            sm_ref[...] = jnp.zeros_like(sm_ref)
            sm_ref[0:1, :] = jnp.dot(accq[0:1, :], fold_ref[...], precision=HI, preferred_element_type=F32)
            sm_ref[1:2, :] = jnp.dot(acck[0:1, :], fold_ref[...], precision=HI, preferred_element_type=F32)
            sm_ref[2:3, :] = accf[0:1, :]

    vec = pl.BlockSpec((1, d), lambda i: (0, 0))
    rowd = pl.BlockSpec((tm, d), lambda i: (i, 0))
    rowl = pl.BlockSpec((tm, LANES), lambda i: (i, 0))
    return pl.pallas_call(
        body, name="c_elem_bwd", grid=(n,),
        out_shape=[jax.ShapeDtypeStruct((tp, n_out), BF16), jax.ShapeDtypeStruct((8, LANES), F32)],
        in_specs=[pl.BlockSpec((tm, 2 * d), lambda i: (i, 0)), pl.BlockSpec((tm, LANES), lambda i: (i, 3 * d // LANES)),
                  pl.BlockSpec((1, LANES), lambda i: (0, 0)), vec, vec, rowd, rowd, rowd, rowl,
                  pl.BlockSpec((d, LANES), lambda i: (0, 0)), pl.BlockSpec((LANES, d), lambda i: (0, 0)),
                  pl.BlockSpec((d, LANES), lambda i: (0, 0))],
        out_specs=[pl.BlockSpec((tm, n_out), lambda i: (i, 0)), pl.BlockSpec((8, LANES), lambda i: (0, 0))],
        scratch_shapes=[pltpu.VMEM((8, d), F32), pltpu.VMEM((8, d), F32), pltpu.VMEM((8, LANES), F32)],
        compiler_params=_cparams(1),
    )(p, p, b_f, qg, kg, dq, dk, dv, dlogf, seg, seg_t, fold)


def loss_head(h, tgt, t_real):
    tp, d = h.shape
    tm = _pick(tp, (768, 512, 256))

    def body(h_ref, t_ref, dh_ref, l_ref):
        i = pl.program_id(0)

        @pl.when(i == 0)
        def _():
            l_ref[...] = jnp.zeros_like(l_ref)

        row = i * tm + lax.broadcasted_iota(jnp.int32, (tm, 1), 0)
        valid = (row >= N_META) & (row < t_real)
        e = jnp.where(valid, h_ref[...] - t_ref[...], 0.0)
        dh_ref[...] = e * (1.0 / d)
        per_row = jnp.sum(e * e, axis=-1, keepdims=True) * (1.0 / d)
        l_ref[...] += 0.5 * jnp.sum(per_row, axis=0, keepdims=True)

    return pl.pallas_call(
        body, name="loss_head", grid=(tp // tm,),
        out_shape=[jax.ShapeDtypeStruct((tp, d), F32), jax.ShapeDtypeStruct((8, LANES), F32)],
        in_specs=[pl.BlockSpec((tm, d), lambda i: (i, 0)), pl.BlockSpec((tm, d), lambda i: (i, 0))],
        out_specs=[pl.BlockSpec((tm, d), lambda i: (i, 0)), pl.BlockSpec((8, LANES), lambda i: (0, 0))],
        compiler_params=_cparams(1),
    )(h, tgt)


def sum_devices(x):
    _, r, c = x.shape

    def body(x_ref, o_ref):
        acc = x_ref[0]
        for dev in range(1, N_DEV):
            acc = acc + x_ref[dev]
        o_ref[...] = acc

    return pl.pallas_call(
        body, name="sum_devices", out_shape=jax.ShapeDtypeStruct((r, c), F32),
        in_specs=[pl.BlockSpec(memory_space=pltpu.VMEM)], out_specs=pl.BlockSpec(memory_space=pltpu.VMEM),
    )(x)


def _adamw_math(w, g, m, v):
    m = ADAM_B1 * m + (1.0 - ADAM_B1) * g
    v = ADAM_B2 * v + (1.0 - ADAM_B2) * (g * g)
    m_hat = m / (1.0 - ADAM_B1 ** ADAM_STEP)
    v_hat = v / (1.0 - ADAM_B2 ** ADAM_STEP)
    delta = -ADAM_LR * (m_hat / (jnp.sqrt(v_hat) + ADAM_EPS) + ADAM_WD * w)
    return delta, m, v


def adamw_small(w, g, m, v):
    def body(w_ref, g_ref, m_ref, v_ref, d_ref, nm_ref, nv_ref):
        d_ref[...], nm_ref[...], nv_ref[...] = _adamw_math(w_ref[...], g_ref[...], m_ref[...], v_ref[...])

    vm = pl.BlockSpec(memory_space=pltpu.VMEM)
    return pl.pallas_call(
        body, name="adamw_small", out_shape=[jax.ShapeDtypeStruct(w.shape, F32)] * 3,
        in_specs=[vm] * 4, out_specs=[vm] * 3,
    )(w, g, m, v)


def adamw_reduce(w, m, v, parts):
    r, c = w.shape
    tr = _pick(r, (512, 384, 352, 256, 128, 8))

    def body(w_ref, m_ref, v_ref, p_ref, g_ref, d_ref, nm_ref, nv_ref):
        g = p_ref[0].astype(F32)
        for dev in range(1, N_DEV):
            g = g + p_ref[dev].astype(F32)
        g_ref[...] = g
        d_ref[...], nm_ref[...], nv_ref[...] = _adamw_math(w_ref[...], g, m_ref[...], v_ref[...])

    blk = pl.BlockSpec((tr, c), lambda i: (i, 0))
    return pl.pallas_call(
        body, name="adamw_reduce", grid=(r // tr,), out_shape=[jax.ShapeDtypeStruct((r, c), F32)] * 4,
        in_specs=[blk, blk, blk, pl.BlockSpec((N_DEV, tr, c), lambda i: (0, i, 0))],
        out_specs=[blk] * 4, compiler_params=_cparams(1),
    )(w, m, v, parts)


def _unshard(g, axis):
    g = jnp.moveaxis(g, 0, axis)
    return g.reshape(g.shape[:axis] + (g.shape[axis] * g.shape[axis + 1],) + g.shape[axis + 2:])


def _shard(full, axis):
    s = full.shape
    g = full.reshape(s[:axis] + (N_DEV, s[axis] // N_DEV) + s[axis + 1:])
    return jnp.moveaxis(g, axis, 0)


def _pad_lanes(a, n=LANES):
    flat = a.reshape(-1)
    pad = (-flat.shape[0]) % n
    return jnp.pad(flat, (0, pad)).reshape(-1, n)


BIG = ("ffn_w_gate", "ffn_w_up", "ffn_w_down", "a_w_in", "a_w_out", "b_w_in", "b_w_out", "c_w_in", "c_w_out")
SHARD_AXIS = {"ffn_w_gate": 3, "ffn_w_up": 3, "ffn_w_down": 2, "a_w_in": 2, "a_w_out": 1, "b_w_in": 2,
              "b_w_out": 1, "c_w_in": 2, "c_w_out": 1, "meta": 1, "ffn_norm": 2, "a_conv": 2, "b_conv": 2}
SMALL_SHARDED = ("meta", "ffn_norm", "a_conv", "b_conv")
SMALL_REPL = ("mix_norm", "b_conv_bias", "b_ln_g", "b_ln_b", "c_b_f", "c_q_norm", "c_k_norm")
WEIGHTS = ("meta", "ffn_norm", "ffn_w_gate", "ffn_w_up", "ffn_w_down", "mix_norm", "a_w_in", "a_conv", "a_w_out",
           "b_w_in", "b_conv", "b_conv_bias", "b_ln_g", "b_ln_b", "b_w_out", "c_w_in", "c_b_f", "c_q_norm",
           "c_k_norm", "c_w_out")
N_MIXERS = 3


def kernel(x, meta, ffn_norm, ffn_w_gate, ffn_w_up, ffn_w_down, mix_norm, a_w_in, a_conv, a_w_out, b_w_in, b_conv, b_conv_bias, b_ln_g, b_ln_b, b_w_out, c_w_in, c_b_f, c_q_norm, c_k_norm, c_w_out, loss_target, m_meta, m_ffn_norm, m_ffn_w_gate, m_ffn_w_up, m_ffn_w_down, m_mix_norm, m_a_w_in, m_a_conv, m_a_w_out, m_b_w_in, m_b_conv, m_b_conv_bias, m_b_ln_g, m_b_ln_b, m_b_w_out, m_c_w_in, m_c_b_f, m_c_q_norm, m_c_k_norm, m_c_w_out, v_meta, v_ffn_norm, v_ffn_w_gate, v_ffn_w_up, v_ffn_w_down, v_mix_norm, v_a_w_in, v_a_conv, v_a_w_out, v_b_w_in, v_b_conv, v_b_conv_bias, v_b_ln_g, v_b_ln_b, v_b_w_out, v_c_w_in, v_c_b_f, v_c_q_norm, v_c_k_norm, v_c_w_out):
    local = dict(locals())
    w = {n: local[n] for n in WEIGHTS}
    mom = {n: local["m_" + n] for n in WEIGHTS}
    var = {n: local["v_" + n] for n in WEIGHTS}
    d = x.shape[-1]
    depth = ffn_norm.shape[0]
    n_heads = d // HEAD_DIM
    seq = x.shape[1]
    t_real = N_META + seq
    tp = -(-t_real // ROW_ALIGN) * ROW_ALIGN
    nb = tp // ATT_TILE
    npair = d // LANES

    names = BIG + SMALL_SHARDED
    gathered = all_gather([w[n].astype(BF16) for n in BIG] + [w[n] for n in SMALL_SHARDED])
    full = {n: _unshard(g, SHARD_AXIS[n]) for n, g in zip(names, gathered)}
    n_c = full["c_w_in"].shape[-1]
    n_cp = 3 * d + LANES
    c_w_qkv = jnp.pad(full["c_w_in"], ((0, 0), (0, 0), (0, n_cp - n_c)))
    b_f_pad = jnp.pad(c_b_f, ((0, 0), (0, LANES - n_heads)))
    qg_t = jnp.tile(c_q_norm, (1, n_heads))
    kg_t = jnp.tile(c_k_norm, (1, n_heads))

    h = jnp.concatenate([full["meta"], x[0], jnp.zeros((tp - t_real, d), F32)], axis=0)
    saved = []
    for i in range(depth):
        mixer, j = i % N_MIXERS, i // N_MIXERS
        s = {"h0": h}
        h, s["gate_a"], s["up_a"] = ffn_fwd(h, full["ffn_norm"][i, 0:1], full["ffn_w_gate"][i, 0],
                                            full["ffn_w_up"][i, 0], full["ffn_w_down"][i, 0])
        s["h1"] = h
        g_mix = mix_norm[i:i + 1]
        if mixer == 0:
            s["p"], s["u"] = norm_matmul(h, g_mix, full["a_w_in"][j], BF16, True)
            s["y"] = a_elem_fwd(s["p"], full["a_conv"][j])
            h = res_matmul(h, s["y"], full["a_w_out"][j])
        elif mixer == 1:
            s["p"], s["u"] = norm_matmul(h, g_mix, full["b_w_in"][j], BF16, True)
            s["y"] = b_elem_fwd(s["p"], full["b_conv"][j], b_conv_bias[j:j + 1], b_ln_g[j:j + 1], b_ln_b[j:j + 1])
            h = res_matmul(h, s["y"], full["b_w_out"][j])
        else:
            s["p"], s["u"] = norm_matmul(h, g_mix, c_w_qkv[j], F32, True)
            s["qp"], s["kp"], s["v"], cum = c_prep_slots(s["p"], b_f_pad[j:j + 1], qg_t[j:j + 1], kg_t[j:j + 1])
            s["cqt"] = (cum[:, :n_heads].T * L2E).reshape(npair, 2, tp)
            s["qpt"] = s["qp"].T
            s["kpt4"] = s["kp"].T.reshape(n_heads, LANES, nb, ATT_TILE).transpose(0, 2, 1, 3)
            vt4 = s["v"].T.reshape(npair, LANES, nb, ATT_TILE).transpose(0, 2, 1, 3)
            ot, s["lse"] = attn_fwd_t(s["qpt"], s["kp"], vt4)
            s["y"] = ot.T
            h = res_matmul(h, s["y"], full["c_w_out"][j])
        s["h2"] = h
        h, s["gate_b"], s["up_b"] = ffn_fwd(h, full["ffn_norm"][i, 1:2], full["ffn_w_gate"][i, 1],
                                            full["ffn_w_up"][i, 1], full["ffn_w_down"][i, 1])
        saved.append(s)

    tgt = jnp.concatenate([jnp.zeros((N_META, d), F32), loss_target[0], jnp.zeros((tp - t_real, d), F32)], axis=0)
    dh, loss_part = loss_head(h, tgt, t_real)

    gfull = {n: [None] * full[n].shape[0] for n in ("a_w_in", "a_w_out", "b_w_in", "b_w_out", "c_w_in", "c_w_out")}
    gffn = {n: [[None, None] for _ in range(depth)] for n in ("ffn_w_gate", "ffn_w_up", "ffn_w_down")}
    g_ffn_norm = [[None, None] for _ in range(depth)]
    g_mix_norm = [None] * depth
    g_a_conv = [None] * a_conv.shape[0]
    g_b_conv = [None] * b_conv.shape[0]
    g_b_stats = [None] * b_conv.shape[0]
    g_c_small = [None] * c_b_f.shape[0]

    def ffn_backward(dh_out, i, half, h_in, gate, up):
        dh_in, hn, dgate, dup, act, dg = ffn_bwd(dh_out, h_in, full["ffn_norm"][i, half:half + 1], gate, up,
                                                 full["ffn_w_gate"][i, half], full["ffn_w_up"][i, half],
                                                 full["ffn_w_down"][i, half])
        gffn["ffn_w_gate"][i][half] = atb(hn, dgate)
        gffn["ffn_w_up"][i][half] = atb(hn, dup)
        gffn["ffn_w_down"][i][half] = atb(act, dh_out, 0.5)
        g_ffn_norm[i][half] = dg
        return dh_in

    for i in reversed(range(depth)):
        mixer, j = i % N_MIXERS, i // N_MIXERS
        s = saved[i]
        g_mix = mix_norm[i:i + 1]
        dh = ffn_backward(dh, i, 1, s["h2"], s["gate_b"], s["up_b"])
        if mixer == 0:
            dy = matmul_nt(dh, full["a_w_out"][j])
            dp, dwc = a_elem_bwd(s["p"], dy, full["a_conv"][j])
            g_a_conv[j] = dwc[:a_conv.shape[1]]
            gfull["a_w_out"][j] = atb(s["y"], dh)
            gfull["a_w_in"][j] = atb(s["u"], dp)
            dh, g_mix_norm[i] = matmul_nt_rms_bwd(dp, full["a_w_in"][j], s["h1"], g_mix, dh)
        elif mixer == 1:
            dy = matmul_nt(dh, full["b_w_out"][j])
            du2, g_b_stats[j] = b_elem_bwd1(s["p"], dy, full["b_conv"][j], b_conv_bias[j:j + 1],
                                            b_ln_g[j:j + 1], b_ln_b[j:j + 1])
            dp, dwc = b_elem_bwd2(s["p"], du2, full["b_conv"][j])
            g_b_conv[j] = dwc[:b_conv.shape[1]]
            gfull["b_w_out"][j] = atb(s["y"], dh)
            gfull["b_w_in"][j] = atb(s["u"], dp)
            dh, g_mix_norm[i] = matmul_nt_rms_bwd(dp, full["b_w_in"][j], s["h1"], g_mix, dh)
        else:
            do = matmul_nt(dh, full["c_w_out"][j])
            dqt, dkp, dv = attn_bwd_t(s["qpt"], s["qp"], s["kp"], s["kpt4"], s["v"], do, do.T, s["lse"], s["cqt"])
            dq = dqt.reshape(n_heads, LANES, tp)[:, :HEAD_DIM].reshape(d, tp).T
            dkp = dkp.reshape(tp, n_heads, LANES)
            dk = dkp[:, :, :HEAD_DIM].reshape(tp, d)
            dlogf = rev_cumsum_rows(-dkp[:, :, AUG + 3].T)
            dlogf = jnp.pad(dlogf.T, ((0, 0), (0, LANES - n_heads)))
            dp, g_c_small[j] = c_elem_bwd(s["p"], b_f_pad[j:j + 1], qg_t[j:j + 1], kg_t[j:j + 1], dq, dk, dv, dlogf)
            gfull["c_w_out"][j] = atb(s["y"], dh)
            gfull["c_w_in"][j] = atb(s["u"], dp)[:, :n_c]
            dh, g_mix_norm[i] = matmul_nt_rms_bwd(dp, c_w_qkv[j], s["h1"], g_mix, dh)
        dh = ffn_backward(dh, i, 0, s["h0"], s["gate_a"], s["up_a"])

    grad_x = dh[N_META:t_real][None]

    gbig = {n: jnp.stack([jnp.stack(r) for r in gffn[n]]) for n in gffn}
    gbig.update({n: jnp.stack(gfull[n]) for n in gfull})
    parts = exchange([_shard(gbig[n], SHARD_AXIS[n]).astype(BF16) for n in BIG])
    out_g, out_d, out_m, out_v = {}, {}, {}, {}
    for n, prt in zip(BIG, parts):
        shp = w[n].shape
        flat = lambda a: a.reshape(-1, shp[-1])
        res = adamw_reduce(flat(w[n]), flat(mom[n]), flat(var[n]), prt.reshape(N_DEV, -1, shp[-1]))
        out_g[n], out_d[n], out_m[n], out_v[n] = [r.reshape(shp) for r in res]

    def rows(a):
        return a.reshape(-1, d)

    def lane_rows(a):
        return jnp.pad(a.reshape(1, -1), ((0, 0), (0, d - a.size)))

    c_small = jnp.stack(g_c_small)
    pieces = [("meta", dh[:N_META]),
              ("ffn_norm", rows(jnp.stack([jnp.stack(r) for r in g_ffn_norm]))),
              ("mix_norm", rows(jnp.stack(g_mix_norm))),
              ("a_conv", rows(jnp.stack(g_a_conv))),
              ("b_conv", rows(jnp.stack(g_b_conv))),
              ("b_ln_g", rows(jnp.stack([st[0] for st in g_b_stats]))),
              ("b_ln_b", rows(jnp.stack([st[1] for st in g_b_stats]))),
              ("b_conv_bias", rows(jnp.stack([st[2] for st in g_b_stats]))),
              ("c_q_norm", lane_rows(c_small[:, 0, :HEAD_DIM])),
              ("c_k_norm", lane_rows(c_small[:, 1, :HEAD_DIM])),
              ("c_b_f", lane_rows(c_small[:, 2, :n_heads])),
              ("loss", lane_rows(loss_part[0:1, 0:1]))]
    packed = jnp.concatenate([pc for _, pc in pieces], axis=0)
    n_rows = packed.shape[0]
    packed = jnp.pad(packed, ((0, (-n_rows) % 8), (0, 0)))
    total = sum_devices(all_gather([packed])[0])
    small_g, r0 = {}, 0
    for n, pc in pieces:
        small_g[n] = total[r0:r0 + pc.shape[0]]
        r0 += pc.shape[0]
    loss = small_g.pop("loss")[0, 0]
    me = 4 * lax.axis_index("x") + 2 * lax.axis_index("y") + lax.axis_index("c")
    for n in SMALL_SHARDED:
        cols = w[n].shape[-1]
        g = lax.dynamic_slice_in_dim(small_g[n], me * cols, cols, axis=1)
        out_g[n] = g.reshape(w[n].shape)
    for n in SMALL_REPL:
        out_g[n] = small_g[n].reshape(-1)[:w[n].size].reshape(w[n].shape)
    small = SMALL_SHARDED + SMALL_REPL
    pack = lambda dct: jnp.concatenate([_pad_lanes(dct[n]) for n in small], axis=0)
    res = adamw_small(pack(w), pack(out_g), pack(mom), pack(var))
    r0 = 0
    for n in small:
        nr = -(-w[n].size // LANES)
        for dct, arr in zip((out_d, out_m, out_v), res):
            dct[n] = arr[r0:r0 + nr].reshape(-1)[:w[n].size].reshape(w[n].shape)
        r0 += nr

    return (loss, grad_x, *[out_g[n] for n in WEIGHTS], *[out_d[n] for n in WEIGHTS],
            *[out_m[n] for n in WEIGHTS], *[out_v[n] for n in WEIGHTS])
```

```python
import functools

import jax
import jax.numpy as jnp
from jax import lax
from jax.experimental import pallas as pl
from jax.experimental.pallas import tpu as pltpu

F32 = jnp.float32
BF16 = jnp.bfloat16
HI = lax.Precision.HIGHEST
EPS = 1e-6
NEG = -1e30
N_META = 16
HEAD_DIM = 64
LANES = 128
N_DEV = 8
ROW_ALIGN = 256
VMEM_LIMIT = 56 * 1024 * 1024
ADAM_LR, ADAM_B1, ADAM_B2, ADAM_EPS, ADAM_WD, ADAM_STEP = 0.001, 0.9, 0.999, 1e-08, 0.01, 10
MESH = pl.DeviceIdType.MESH
NT = (((1,), (1,)), ((), ()))
TN = (((0,), (0,)), ((), ()))


def _cparams(n_axes):
    return pltpu.CompilerParams(dimension_semantics=("arbitrary",) * n_axes, vmem_limit_bytes=VMEM_LIMIT)


def _pick(n, cands):
    for c in cands:
        if n % c == 0:
            return c
    raise ValueError(f"no tile for {n} among {cands}")


def _sigmoid(x):
    return 1.0 / (1.0 + jnp.exp(-x))


def _rms(x):
    r = lax.rsqrt(jnp.mean(x * x, axis=-1, keepdims=True) + EPS)
    return x * r, r


def _rms_bwd(xhat, r, g, dy):
    gy = dy * g
    dx = r * (gy - xhat * jnp.mean(gy * xhat, axis=-1, keepdims=True))
    return dx, jnp.sum(dy * xhat, axis=0, keepdims=True)


def _mesh_pos():
    return lax.axis_index("x"), lax.axis_index("y"), lax.axis_index("c")


def all_gather(arrs):
    n = len(arrs)
    hbm = pl.BlockSpec(memory_space=pltpu.HBM)

    def body(*refs):
        ins, outs = refs[:n], refs[n:2 * n]
        send_sems, recv_sems, local_sems = refs[2 * n:]
        x, y, c = _mesh_pos()
        me, sibling = (x, y, c), (x, y, 1 - c)
        chips = [(1 - x, y), (x, 1 - y), (1 - x, 1 - y)]

        def slot(a, px, py, pc):
            return outs[a].at[4 * px + 2 * py + pc]

        def copy(a, k, block, to, src=None):
            return pltpu.make_async_remote_copy(
                src_ref=slot(a, *block) if src is None else src, dst_ref=slot(a, *block),
                send_sem=send_sems.at[7 * a + k], recv_sem=recv_sems.at[7 * a + k],
                device_id=to, device_id_type=MESH)

        own, first, passed = [], [], []
        for a in range(n):
            cp = pltpu.make_async_copy(ins[a], slot(a, *me), local_sems.at[a])
            cp.start()
            own.append(cp)
            first.append(copy(a, 0, me, sibling, src=ins[a]))
            first += [copy(a, 1 + j, me, (*chip, c), src=ins[a]) for j, chip in enumerate(chips)]
        for cp in first:
            cp.start()
        for j, chip in enumerate(chips):
            for a in range(n):
                copy(a, 1 + j, (*chip, c), me).wait_recv()
                cp = copy(a, 4 + j, (*chip, c), sibling)
                cp.start()
                passed.append(cp)
        for a in range(n):
            copy(a, 0, sibling, me).wait_recv()
            for j, chip in enumerate(chips):
                copy(a, 4 + j, (*chip, 1 - c), me).wait_recv()
        for cp in first + passed:
            cp.wait_send()
        for cp in own:
            cp.wait()

    return pl.pallas_call(
        body, name="all_gather",
        out_shape=[jax.ShapeDtypeStruct((N_DEV,) + a.shape, a.dtype) for a in arrs],
        in_specs=[hbm] * n, out_specs=[hbm] * n,
        scratch_shapes=[pltpu.SemaphoreType.DMA((7 * n,)), pltpu.SemaphoreType.DMA((7 * n,)),
                        pltpu.SemaphoreType.DMA((n,))],
    )(*arrs)


N_CHIP = 4


def swap_with_sibling(arrs):
    n = len(arrs)
    hbm = pl.BlockSpec(memory_space=pltpu.HBM)

    def body(*refs):
        ins, outs = refs[:n], refs[n:2 * n]
        send_sems, recv_sems = refs[2 * n:]
        x, y, c = _mesh_pos()
        copies = [pltpu.make_async_remote_copy(
            src_ref=ins[a].at[1 - c], dst_ref=outs[a], send_sem=send_sems.at[a], recv_sem=recv_sems.at[a],
            device_id=(x, y, 1 - c), device_id_type=MESH) for a in range(n)]
        for cp in copies:
            cp.start()
        for cp in copies:
            cp.wait()

    return pl.pallas_call(
        body, name="swap_with_sibling",
        out_shape=[jax.ShapeDtypeStruct(a.shape[1:], a.dtype) for a in arrs],
        in_specs=[hbm] * n, out_specs=[hbm] * n,
        scratch_shapes=[pltpu.SemaphoreType.DMA((n,)), pltpu.SemaphoreType.DMA((n,))],
    )(*arrs)


def exchange_chips(arrs):
    n = len(arrs)
    hbm = pl.BlockSpec(memory_space=pltpu.HBM)

    def body(*refs):
        ins, outs = refs[:n], refs[n:2 * n]
        send_sems, recv_sems, local_sems = refs[2 * n:]
        x, y, c = _mesh_pos()
        me = 2 * x + y
        own, sent = [], []
        for a in range(n):
            cp = pltpu.make_async_copy(ins[a].at[me], outs[a].at[me], local_sems.at[a])
            cp.start()
            own.append(cp)
        for k in range(1, N_CHIP):
            px = 1 - x if k & 2 else x
            py = 1 - y if k & 1 else y
            peer = 2 * px + py
            for a in range(n):
                cp = pltpu.make_async_remote_copy(
                    src_ref=ins[a].at[peer], dst_ref=outs[a].at[me],
                    send_sem=send_sems.at[3 * a + k - 1], recv_sem=recv_sems.at[3 * a + k - 1],
                    device_id=(px, py, c), device_id_type=MESH)
                cp.start()
                sent.append((cp, a, k, peer))
        for cp, a, k, peer in sent:
            pltpu.make_async_remote_copy(
                src_ref=ins[a].at[peer], dst_ref=outs[a].at[peer],
                send_sem=send_sems.at[3 * a + k - 1], recv_sem=recv_sems.at[3 * a + k - 1],
                device_id=(x, y, c), device_id_type=MESH).wait_recv()
        for cp, a, k, peer in sent:
            cp.wait_send()
        for cp in own:
            cp.wait()

    return pl.pallas_call(
        body, name="exchange_chips",
        out_shape=[jax.ShapeDtypeStruct(a.shape, a.dtype) for a in arrs],
        in_specs=[hbm] * n, out_specs=[hbm] * n,
        scratch_shapes=[pltpu.SemaphoreType.DMA((3 * n,)), pltpu.SemaphoreType.DMA((3 * n,)),
                        pltpu.SemaphoreType.DMA((n,))],
    )(*arrs)


def add_bf16(a, b):
    r, c = a.shape
    tr = _pick(r, (2048, 1408, 1024, 512, 256, 128, 8))

    def body(a_ref, b_ref, o_ref):
        o_ref[...] = (a_ref[...].astype(F32) + b_ref[...].astype(F32)).astype(BF16)

    blk = pl.BlockSpec((tr, c), lambda i: (i, 0))
    return pl.pallas_call(
        body, name="add_bf16", grid=(r // tr,), out_shape=jax.ShapeDtypeStruct((r, c), BF16),
        in_specs=[blk, blk], out_specs=blk, compiler_params=_cparams(1),
    )(a, b)


def exchange(arrs):
    n = len(arrs)
    hbm = pl.BlockSpec(memory_space=pltpu.HBM)

    def body(*refs):
        ins, outs = refs[:n], refs[n:2 * n]
        send_sems, recv_sems, local_sems = refs[2 * n:]
        x, y, c = _mesh_pos()
        me = 4 * x + 2 * y + c
        own, sent = [], []
        for a in range(n):
            cp = pltpu.make_async_copy(ins[a].at[me], outs[a].at[me], local_sems.at[a])
            cp.start()
            own.append(cp)
        for k in range(1, N_DEV):
            px = 1 - x if k & 4 else x
            py = 1 - y if k & 2 else y
            pc = 1 - c if k & 1 else c
            peer = 4 * px + 2 * py + pc
            for a in range(n):
                cp = pltpu.make_async_remote_copy(
                    src_ref=ins[a].at[peer], dst_ref=outs[a].at[me],
                    send_sem=send_sems.at[7 * a + k - 1], recv_sem=recv_sems.at[7 * a + k - 1],
                    device_id=(px, py, pc), device_id_type=MESH)
                cp.start()
                sent.append((cp, a, k, peer))
        for cp, a, k, peer in sent:
            pltpu.make_async_remote_copy(
                src_ref=ins[a].at[peer], dst_ref=outs[a].at[peer],
                send_sem=send_sems.at[7 * a + k - 1], recv_sem=recv_sems.at[7 * a + k - 1],
                device_id=(x, y, c), device_id_type=MESH).wait_recv()
        for cp, a, k, peer in sent:
            cp.wait_send()
        for cp in own:
            cp.wait()

    return pl.pallas_call(
        body, name="exchange",
        out_shape=[jax.ShapeDtypeStruct(a.shape, a.dtype) for a in arrs],
        in_specs=[hbm] * n, out_specs=[hbm] * n,
        scratch_shapes=[pltpu.SemaphoreType.DMA((7 * n,)), pltpu.SemaphoreType.DMA((7 * n,)),
                        pltpu.SemaphoreType.DMA((n,))],
    )(*arrs)


def _resident(shape):
    return pl.BlockSpec(shape, lambda i: (0,) * len(shape), pipeline_mode=pl.Buffered(1))


def ffn_fwd(h, g, wg, wu, wd):
    tp, d = h.shape
    f = wg.shape[1]
    tm = _pick(tp, (384, 256))
    tf = _pick(f, (1408, 1024, 512, 256, 128))

    def body(h_ref, g_ref, wg_ref, wu_ref, wd_ref, ho_ref, gate_ref, up_ref):
        x = h_ref[...]
        xhat, _ = _rms(x)
        hn = (xhat * g_ref[...]).astype(BF16)
        acc = jnp.zeros((tm, d), F32)
        for c0 in range(0, f, tf):
            gate = jnp.dot(hn, wg_ref[:, c0:c0 + tf], preferred_element_type=F32)
            up = jnp.dot(hn, wu_ref[:, c0:c0 + tf], preferred_element_type=F32)
            gate_ref[:, c0:c0 + tf] = gate.astype(BF16)
            up_ref[:, c0:c0 + tf] = up.astype(BF16)
            act = (gate * _sigmoid(gate) * up).astype(BF16)
            acc = acc + jnp.dot(act, wd_ref[c0:c0 + tf, :], preferred_element_type=F32)
        ho_ref[...] = x + 0.5 * acc

    row = lambda n: pl.BlockSpec((tm, n), lambda i: (i, 0))
    return pl.pallas_call(
        body, name="ffn_fwd", grid=(tp // tm,),
        out_shape=[jax.ShapeDtypeStruct((tp, d), F32), jax.ShapeDtypeStruct((tp, f), BF16),
                   jax.ShapeDtypeStruct((tp, f), BF16)],
        in_specs=[row(d), _resident((1, d)), _resident((d, f)), _resident((d, f)), _resident((f, d))],
        out_specs=[row(d), row(f), row(f)],
        compiler_params=_cparams(1),
    )(h, g, wg, wu, wd)


def ffn_bwd(dho, h, g, gate, up, wg, wu, wd):
    tp, d = h.shape
    f = wg.shape[1]
    tm = _pick(tp, (256,))
    tf = _pick(f, (1408, 1024, 512, 256, 128))

    def body(dho_ref, h_ref, g_ref, gate_ref, up_ref, wg_ref, wu_ref, wd_ref,
             dhi_ref, hn_ref, dgate_ref, dup_ref, act_ref, dg_ref):
        @pl.when(pl.program_id(0) == 0)
        def _():
            dg_ref[...] = jnp.zeros_like(dg_ref)

        dho_ = dho_ref[...]
        dout = (0.5 * dho_).astype(BF16)
        dhn = jnp.zeros((tm, d), F32)
        for c0 in range(0, f, tf):
            dact = lax.dot_general(dout, wd_ref[c0:c0 + tf, :], NT, preferred_element_type=F32)
            gt = gate_ref[:, c0:c0 + tf].astype(F32)
            u = up_ref[:, c0:c0 + tf].astype(F32)
            sig = _sigmoid(gt)
            silu = gt * sig
            act_ref[:, c0:c0 + tf] = (silu * u).astype(BF16)
            dup = (dact * silu).astype(BF16)
            dgate = (dact * u * (sig * (1.0 + gt * (1.0 - sig)))).astype(BF16)
            dup_ref[:, c0:c0 + tf] = dup
            dgate_ref[:, c0:c0 + tf] = dgate
            dhn = dhn + (lax.dot_general(dgate, wg_ref[:, c0:c0 + tf], NT, preferred_element_type=F32)
                         + lax.dot_general(dup, wu_ref[:, c0:c0 + tf], NT, preferred_element_type=F32))
        xhat, r = _rms(h_ref[...])
        gg = g_ref[...]
        dx, dgp = _rms_bwd(xhat, r, gg, dhn)
        dg_ref[...] += dgp
        dhi_ref[...] = dho_ + dx
        hn_ref[...] = (xhat * gg).astype(BF16)

    row = lambda n: pl.BlockSpec((tm, n), lambda i: (i, 0))
    return pl.pallas_call(
        body, name="ffn_bwd", grid=(tp // tm,),
        out_shape=[jax.ShapeDtypeStruct((tp, d), F32), jax.ShapeDtypeStruct((tp, d), BF16),
                   jax.ShapeDtypeStruct((tp, f), BF16), jax.ShapeDtypeStruct((tp, f), BF16),
                   jax.ShapeDtypeStruct((tp, f), BF16), jax.ShapeDtypeStruct((1, d), F32)],
        in_specs=[row(d), row(d), _resident((1, d)), row(f), row(f),
                  _resident((d, f)), _resident((d, f)), _resident((f, d))],
        out_specs=[row(d), row(d), row(f), row(f), row(f), pl.BlockSpec((1, d), lambda i: (0, 0))],
        compiler_params=_cparams(1),
    )(dho, h, g, gate, up, wg, wu, wd)


def atb(a, b, scale=1.0):
    tp, m = a.shape
    n = b.shape[1]
    tmm = _pick(m, (1024, 1408, 640, 512, 256, 128))
    tn = _pick(n, (1024, 1408, 640, 512, 256, 128))
    tk = _pick(tp, (1408, 768, 512, 256))
    nk = tp // tk

    def body(a_ref, b_ref, o_ref):
        k = pl.program_id(2)

        @pl.when(k == 0)
        def _():
            o_ref[...] = jnp.zeros_like(o_ref)

        o_ref[...] += lax.dot_general(a_ref[...].astype(BF16), b_ref[...].astype(BF16), TN,
                                      preferred_element_type=F32)

        if scale != 1.0:
            @pl.when(k == nk - 1)
            def _():
                o_ref[...] = o_ref[...] * scale

    return pl.pallas_call(
        body, name="atb", grid=(m // tmm, n // tn, nk),
        out_shape=jax.ShapeDtypeStruct((m, n), F32),
        in_specs=[pl.BlockSpec((tk, tmm), lambda i, j, k: (k, i)), pl.BlockSpec((tk, tn), lambda i, j, k: (k, j))],
        out_specs=pl.BlockSpec((tmm, tn), lambda i, j, k: (i, j)),
        compiler_params=_cparams(3),
    )(a, b)


def norm_matmul(h, g, w, out_dtype, emit_u):
    tp, d = h.shape
    n = w.shape[1]
    tm = _pick(tp, (768, 512, 256))
    tn = _pick(n, (1024, 768, 640, 512, 256, 128))

    def body(h_ref, g_ref, w_ref, o_ref, *rest):
        u_sc = rest[-1]

        @pl.when(pl.program_id(1) == 0)
        def _():
            xhat, _ = _rms(h_ref[...])
            u_sc[...] = (xhat * g_ref[...]).astype(BF16)
            if emit_u:
                rest[0][...] = u_sc[...]

        o_ref[...] = jnp.dot(u_sc[...], w_ref[...], preferred_element_type=F32).astype(out_dtype)

    out_shape = [jax.ShapeDtypeStruct((tp, n), out_dtype)]
    out_specs = [pl.BlockSpec((tm, tn), lambda i, j: (i, j))]
    if emit_u:
        out_shape.append(jax.ShapeDtypeStruct((tp, d), BF16))
        out_specs.append(pl.BlockSpec((tm, d), lambda i, j: (i, 0)))
    return pl.pallas_call(
        body, name="norm_matmul", grid=(tp // tm, n // tn), out_shape=out_shape,
        in_specs=[pl.BlockSpec((tm, d), lambda i, j: (i, 0)), pl.BlockSpec((1, d), lambda i, j: (0, 0)),
                  pl.BlockSpec((d, tn), lambda i, j: (0, j))],
        out_specs=out_specs, scratch_shapes=[pltpu.VMEM((tm, d), BF16)],
        compiler_params=_cparams(2),
    )(h, g, w)


def res_matmul(h, y, w):
    tp, d = h.shape
    k = y.shape[1]
    tm = _pick(tp, (768, 512, 256))

    def body(h_ref, y_ref, w_ref, o_ref):
        o_ref[...] = h_ref[...] + jnp.dot(y_ref[...], w_ref[...], preferred_element_type=F32)

    return pl.pallas_call(
        body, name="res_matmul", grid=(tp // tm,), out_shape=jax.ShapeDtypeStruct((tp, d), F32),
        in_specs=[pl.BlockSpec((tm, d), lambda i: (i, 0)), pl.BlockSpec((tm, k), lambda i: (i, 0)),
                  pl.BlockSpec((k, d), lambda i: (0, 0))],
        out_specs=pl.BlockSpec((tm, d), lambda i: (i, 0)),
        compiler_params=_cparams(1),
    )(h, y, w)


def matmul_nt(a, w):
    tp, d = a.shape
    k = w.shape[0]
    tm = _pick(tp, (768, 512, 256))

    def body(a_ref, w_ref, o_ref):
        o_ref[...] = lax.dot_general(a_ref[...].astype(BF16), w_ref[...], NT,
                                     preferred_element_type=F32).astype(BF16)

    return pl.pallas_call(
        body, name="matmul_nt", grid=(tp // tm,), out_shape=jax.ShapeDtypeStruct((tp, k), BF16),
        in_specs=[pl.BlockSpec((tm, d), lambda i: (i, 0)), pl.BlockSpec((k, d), lambda i: (0, 0))],
        out_specs=pl.BlockSpec((tm, k), lambda i: (i, 0)),
        compiler_params=_cparams(1),
    )(a, w)


def matmul_nt_rms_bwd(dp, w, h, g, dres):
    tp, d = h.shape
    n = w.shape[1]
    tm = _pick(tp, (384, 256))

    def body(dp_ref, w_ref, h_ref, g_ref, dres_ref, dh_ref, dg_ref):
        @pl.when(pl.program_id(0) == 0)
        def _():
            dg_ref[...] = jnp.zeros_like(dg_ref)

        du = lax.dot_general(dp_ref[...], w_ref[...], NT, preferred_element_type=F32)
        xhat, r = _rms(h_ref[...])
        dx, dgp = _rms_bwd(xhat, r, g_ref[...], du)
        dg_ref[...] += dgp
        dh_ref[...] = dres_ref[...] + dx

    return pl.pallas_call(
        body, name="matmul_nt_rms_bwd", grid=(tp // tm,),
        out_shape=[jax.ShapeDtypeStruct((tp, d), F32), jax.ShapeDtypeStruct((1, d), F32)],
        in_specs=[pl.BlockSpec((tm, n), lambda i: (i, 0)), pl.BlockSpec((d, n), lambda i: (0, 0)),
                  pl.BlockSpec((tm, d), lambda i: (i, 0)), pl.BlockSpec((1, d), lambda i: (0, 0)),
                  pl.BlockSpec((tm, d), lambda i: (i, 0))],
        out_specs=[pl.BlockSpec((tm, d), lambda i: (i, 0)), pl.BlockSpec((1, d), lambda i: (0, 0))],
        compiler_params=_cparams(1),
    )(dp, w, h, g, dres)


CONV_TILE = (256,)


def _halo_before(tm, hb, col=0):
    return lambda i: (jnp.maximum(i * (tm // hb) - 1, 0), col)


def _halo_after(tm, hb, nblk, col=0):
    return lambda i: (jnp.minimum((i + 1) * (tm // hb), nblk - 1), col)


def a_elem_fwd(p, w):
    tp, d3 = p.shape
    d = d3 // 3
    kw = w.shape[0]
    tm = _pick(tp, CONV_TILE)
    hb = 16

    def body(p_ref, ph_ref, w_ref, y_ref, ext):
        i = pl.program_id(0)
        halo = ph_ref[:, d:2 * d].astype(F32) * ph_ref[:, 2 * d:].astype(F32)
        ext[0:hb, :] = jnp.where(i > 0, halo, 0.0)
        ext[hb:, :] = p_ref[:, d:2 * d].astype(F32) * p_ref[:, 2 * d:].astype(F32)
        z = jnp.zeros((tm, d), F32)
        for k in range(kw):
            z = z + w_ref[k:k + 1, :] * ext[pl.ds(hb - (kw - 1) + k, tm), :]
        y_ref[...] = (p_ref[:, 0:d].astype(F32) * z).astype(BF16)

    return pl.pallas_call(
        body, name="a_elem_fwd", grid=(tp // tm,), out_shape=jax.ShapeDtypeStruct((tp, d), BF16),
        in_specs=[pl.BlockSpec((tm, d3), lambda i: (i, 0)), pl.BlockSpec((hb, d3), _halo_before(tm, hb)),
                  pl.BlockSpec((kw, d), lambda i: (0, 0))],
        out_specs=pl.BlockSpec((tm, d), lambda i: (i, 0)),
        scratch_shapes=[pltpu.VMEM((tm + hb, d), F32)],
        compiler_params=_cparams(1),
    )(p, p, w)


def a_elem_bwd(p, dy, w):
    tp, d3 = p.shape
    d = d3 // 3
    kw = w.shape[0]
    tm = _pick(tp, CONV_TILE)
    hb = 16
    n = tp // tm

    def body(p_ref, ph_ref, pa_ref, dy_ref, dya_ref, w_ref, dp_ref, dw_ref, ext, dzext):
        i = pl.program_id(0)

        @pl.when(i == 0)
        def _():
            dw_ref[...] = jnp.zeros_like(dw_ref)

        b = p_ref[:, 0:d].astype(F32)
        c = p_ref[:, d:2 * d].astype(F32)
        v = p_ref[:, 2 * d:].astype(F32)
        dy_ = dy_ref[...].astype(F32)
        halo = ph_ref[:, d:2 * d].astype(F32) * ph_ref[:, 2 * d:].astype(F32)
        ext[0:hb, :] = jnp.where(i > 0, halo, 0.0)
        ext[hb:, :] = c * v
        dz = dy_ * b
        dzext[0:tm, :] = dz
        dzext[tm:, :] = jnp.where(i < n - 1, dya_ref[...].astype(F32) * pa_ref[:, 0:d].astype(F32), 0.0)
        z = jnp.zeros((tm, d), F32)
        dcv = jnp.zeros((tm, d), F32)
        for k in range(kw):
            sh = ext[pl.ds(hb - (kw - 1) + k, tm), :]
            z = z + w_ref[k:k + 1, :] * sh
            dw_ref[k:k + 1, :] += jnp.sum(dz * sh, axis=0, keepdims=True)
            dcv = dcv + w_ref[k:k + 1, :] * dzext[pl.ds(kw - 1 - k, tm), :]
        dp_ref[:, 0:d] = (dy_ * z).astype(BF16)
        dp_ref[:, d:2 * d] = (dcv * v).astype(BF16)
        dp_ref[:, 2 * d:] = (dcv * c).astype(BF16)

    nb = tp // hb
    return pl.pallas_call(
        body, name="a_elem_bwd", grid=(n,),
        out_shape=[jax.ShapeDtypeStruct((tp, d3), BF16), jax.ShapeDtypeStruct((8, d), F32)],
        in_specs=[pl.BlockSpec((tm, d3), lambda i: (i, 0)), pl.BlockSpec((hb, d3), _halo_before(tm, hb)),
                  pl.BlockSpec((hb, d3), _halo_after(tm, hb, nb)),
                  pl.BlockSpec((tm, d), lambda i: (i, 0)), pl.BlockSpec((hb, d), _halo_after(tm, hb, nb)),
                  pl.BlockSpec((kw, d), lambda i: (0, 0))],
        out_specs=[pl.BlockSpec((tm, d3), lambda i: (i, 0)), pl.BlockSpec((8, d), lambda i: (0, 0))],
        scratch_shapes=[pltpu.VMEM((tm + hb, d), F32), pltpu.VMEM((tm + hb, d), F32)],
        compiler_params=_cparams(1),
    )(p, p, p, dy, dy, w)


CH_R, CH_C = 128, 128


def _chunks(tm, d):
    return [(r0, c0) for c0 in range(0, d, CH_C) for r0 in range(0, tm, CH_R)]


def _b_u2(p_ref, ph_ref, w_ref, bias_ref, ext, u2_sc, i, tm, d, kw, hb):
    a = p_ref[:, 0:d].astype(F32)
    sg = _sigmoid(p_ref[:, d:].astype(F32))
    halo = ph_ref[:, 0:d].astype(F32) * _sigmoid(ph_ref[:, d:].astype(F32))
    ext[0:hb, :] = jnp.where(i > 0, halo, 0.0)
    ext[hb:, :] = a * sg
    for r0, c0 in _chunks(tm, d):
        cs = slice(c0, c0 + CH_C)
        acc = jnp.zeros((CH_R, CH_C), F32) + bias_ref[:, cs]
        for k in range(kw):
            acc = acc + w_ref[k:k + 1, cs] * ext[pl.ds(r0 + hb - (kw - 1) + k, CH_R), cs]
        u2_sc[r0:r0 + CH_R, cs] = acc
    return a, sg, u2_sc[...]


def _ln(u2):
    mu = jnp.mean(u2, axis=-1, keepdims=True)
    xc = u2 - mu
    rstd = lax.rsqrt(jnp.mean(xc * xc, axis=-1, keepdims=True) + EPS)
    return xc * rstd, rstd


def b_elem_fwd(p, w, bias, ln_g, ln_b):
    tp, d2 = p.shape
    d = d2 // 2
    kw = w.shape[0]
    tm = _pick(tp, CONV_TILE)
    hb = 32

    def body(p_ref, ph_ref, w_ref, bias_ref, g_ref, b_ref, y_ref, ext, u2_sc):
        i = pl.program_id(0)
        _, _, u2 = _b_u2(p_ref, ph_ref, w_ref, bias_ref, ext, u2_sc, i, tm, d, kw, hb)
        xhat, _ = _ln(u2)
        u3 = xhat * g_ref[...] + b_ref[...]
        y_ref[...] = (u3 * _sigmoid(u3)).astype(BF16)

    vec = pl.BlockSpec((1, d), lambda i: (0, 0))
    return pl.pallas_call(
        body, name="b_elem_fwd", grid=(tp // tm,), out_shape=jax.ShapeDtypeStruct((tp, d), BF16),
        in_specs=[pl.BlockSpec((tm, d2), lambda i: (i, 0)), pl.BlockSpec((hb, d2), _halo_before(tm, hb)),
                  pl.BlockSpec((kw, d), lambda i: (0, 0)), vec, vec, vec],
        out_specs=pl.BlockSpec((tm, d), lambda i: (i, 0)),
        scratch_shapes=[pltpu.VMEM((tm + hb, d), F32), pltpu.VMEM((tm, d), F32)],
        compiler_params=_cparams(1),
    )(p, p, w, bias, ln_g, ln_b)


def b_elem_bwd1(p, dy, w, bias, ln_g, ln_b):
    tp, d2 = p.shape
    d = d2 // 2
    kw = w.shape[0]
    tm = _pick(tp, CONV_TILE)
    hb = 32

    def body(p_ref, ph_ref, dy_ref, w_ref, bias_ref, g_ref, b_ref, du2_ref, st_ref, ext, u2_sc):
        i = pl.program_id(0)

        @pl.when(i == 0)
        def _():
            st_ref[...] = jnp.zeros_like(st_ref)

        _, _, u2 = _b_u2(p_ref, ph_ref, w_ref, bias_ref, ext, u2_sc, i, tm, d, kw, hb)
        xhat, rstd = _ln(u2)
        u3 = xhat * g_ref[...] + b_ref[...]
        s3 = _sigmoid(u3)
        du3 = dy_ref[...].astype(F32) * (s3 * (1.0 + u3 * (1.0 - s3)))
        dxh = du3 * g_ref[...]
        du2 = rstd * (dxh - jnp.mean(dxh, axis=-1, keepdims=True)
                      - xhat * jnp.mean(dxh * xhat, axis=-1, keepdims=True))
        du2_ref[...] = du2
        st_ref[0:1, :] += jnp.sum(du3 * xhat, axis=0, keepdims=True)
        st_ref[1:2, :] += jnp.sum(du3, axis=0, keepdims=True)
        st_ref[2:3, :] += jnp.sum(du2, axis=0, keepdims=True)

    vec = pl.BlockSpec((1, d), lambda i: (0, 0))
    return pl.pallas_call(
        body, name="b_elem_bwd1", grid=(tp // tm,),
        out_shape=[jax.ShapeDtypeStruct((tp, d), F32), jax.ShapeDtypeStruct((8, d), F32)],
        in_specs=[pl.BlockSpec((tm, d2), lambda i: (i, 0)), pl.BlockSpec((hb, d2), _halo_before(tm, hb)),
                  pl.BlockSpec((tm, d), lambda i: (i, 0)), pl.BlockSpec((kw, d), lambda i: (0, 0)), vec, vec, vec],
        out_specs=[pl.BlockSpec((tm, d), lambda i: (i, 0)), pl.BlockSpec((8, d), lambda i: (0, 0))],
        scratch_shapes=[pltpu.VMEM((tm + hb, d), F32), pltpu.VMEM((tm, d), F32)],
        compiler_params=_cparams(1),
    )(p, p, dy, w, bias, ln_g, ln_b)


def b_elem_bwd2(p, du2, w):
    tp, d2 = p.shape
    d = d2 // 2
    kw = w.shape[0]
    tm = _pick(tp, CONV_TILE)
    hb = 32
    n = tp // tm

    def body(p_ref, ph_ref, du2_ref, du2a_ref, w_ref, dp_ref, dw_ref, ext, dext, dwacc):
        i = pl.program_id(0)

        @pl.when(i == 0)
        def _():
            dwacc[...] = jnp.zeros_like(dwacc)

        halo = ph_ref[:, 0:d].astype(F32) * _sigmoid(ph_ref[:, d:].astype(F32))
        ext[0:hb, :] = jnp.where(i > 0, halo, 0.0)
        ext[hb:, :] = p_ref[:, 0:d].astype(F32) * _sigmoid(p_ref[:, d:].astype(F32))
        dext[0:tm, :] = du2_ref[...]
        dext[tm:, :] = jnp.where(i < n - 1, du2a_ref[...], 0.0)
        for r0, c0 in _chunks(tm, d):
            cs = slice(c0, c0 + CH_C)
            du2c = du2_ref[r0:r0 + CH_R, cs]
            du1 = jnp.zeros((CH_R, CH_C), F32)
            for k in range(kw):
                prod = du2c * ext[pl.ds(r0 + hb - (kw - 1) + k, CH_R), cs]
                part = prod[0:8]
                for r in range(8, CH_R, 8):
                    part = part + prod[r:r + 8]
                dwacc[8 * k:8 * k + 8, cs] += part
                du1 = du1 + w_ref[k:k + 1, cs] * dext[pl.ds(r0 + kw - 1 - k, CH_R), cs]
            a = p_ref[r0:r0 + CH_R, cs].astype(F32)
            sg = _sigmoid(p_ref[r0:r0 + CH_R, d + c0:d + c0 + CH_C].astype(F32))
            dp_ref[r0:r0 + CH_R, cs] = (du1 * sg).astype(BF16)
            dp_ref[r0:r0 + CH_R, d + c0:d + c0 + CH_C] = (du1 * a * sg * (1.0 - sg)).astype(BF16)

        @pl.when(i == n - 1)
        def _():
            dw_ref[...] = jnp.zeros_like(dw_ref)
            for k in range(kw):
                dw_ref[k:k + 1, :] = jnp.sum(dwacc[8 * k:8 * k + 8, :], axis=0, keepdims=True)

    nb = tp // hb
    return pl.pallas_call(
        body, name="b_elem_bwd2", grid=(n,),
        out_shape=[jax.ShapeDtypeStruct((tp, d2), BF16), jax.ShapeDtypeStruct((32, d), F32)],
        in_specs=[pl.BlockSpec((tm, d2), lambda i: (i, 0)), pl.BlockSpec((hb, d2), _halo_before(tm, hb)),
                  pl.BlockSpec((tm, d), lambda i: (i, 0)), pl.BlockSpec((hb, d), _halo_after(tm, hb, nb)),
                  pl.BlockSpec((kw, d), lambda i: (0, 0))],
        out_specs=[pl.BlockSpec((tm, d2), lambda i: (i, 0)), pl.BlockSpec((32, d), lambda i: (0, 0))],
        scratch_shapes=[pltpu.VMEM((tm + hb, d), F32), pltpu.VMEM((tm + hb, d), F32), pltpu.VMEM((8 * kw, d), F32)],
        compiler_params=_cparams(1),
    )(p, p, du2, du2, w)


ATT_TILE = 256


def _head_masks(d):
    c = lax.broadcasted_iota(jnp.int32, (d, LANES), 0)
    h = lax.broadcasted_iota(jnp.int32, (d, LANES), 1)
    seg = (c // HEAD_DIM == h).astype(F32)
    fold = (c % HEAD_DIM == h).astype(F32)
    return seg, seg.T, fold


def c_prep(p, b_f, qg, kg):
    tp = p.shape[0]
    d = qg.shape[1]
    tm = ATT_TILE
    seg, seg_t, _ = _head_masks(d)

    def body(p_ref, pf_ref, bf_ref, qg_ref, kg_ref, seg_ref, segt_ref, q_ref, k_ref, v_ref, cum_ref, carry):
        @pl.when(pl.program_id(0) == 0)
        def _():
            carry[...] = jnp.zeros_like(carry)

        def norm(x, g):
            ms = jnp.dot(x * x, seg_ref[...], precision=HI, preferred_element_type=F32) * (1.0 / HEAD_DIM)
            r = jnp.dot(lax.rsqrt(ms + EPS), segt_ref[...], precision=HI, preferred_element_type=F32)
            return x * r * g

        q_ref[...] = (norm(p_ref[:, 0:d], qg_ref[...]) * (HEAD_DIM ** -0.5)).astype(BF16)
        k_ref[...] = norm(p_ref[:, d:2 * d], kg_ref[...]).astype(BF16)
        v_ref[...] = p_ref[:, 2 * d:].astype(BF16)
        xf = pf_ref[...] + bf_ref[...]
        logf = jnp.minimum(xf, 0.0) - jnp.log(1.0 + jnp.exp(-jnp.abs(xf)))
        r_ = lax.broadcasted_iota(jnp.int32, (tm, tm), 0)
        c_ = lax.broadcasted_iota(jnp.int32, (tm, tm), 1)
        tri = (c_ <= r_).astype(F32)
        cum_ref[...] = jnp.dot(tri, logf, precision=HI, preferred_element_type=F32) + carry[0:1, :]
        carry[0:1, :] += jnp.sum(logf, axis=0, keepdims=True)

    vec = pl.BlockSpec((1, d), lambda i: (0, 0))
    rowd = pl.BlockSpec((tm, d), lambda i: (i, 0))
    return pl.pallas_call(
        body, name="c_prep", grid=(tp // tm,),
        out_shape=[jax.ShapeDtypeStruct((tp, d), BF16)] * 3 + [jax.ShapeDtypeStruct((tp, LANES), F32)],
        in_specs=[pl.BlockSpec((tm, 3 * d), lambda i: (i, 0)), pl.BlockSpec((tm, LANES), lambda i: (i, 3 * d // LANES)),
                  pl.BlockSpec((1, LANES), lambda i: (0, 0)), vec, vec,
                  pl.BlockSpec((d, LANES), lambda i: (0, 0)), pl.BlockSpec((LANES, d), lambda i: (0, 0))],
        out_specs=[rowd, rowd, rowd, pl.BlockSpec((tm, LANES), lambda i: (i, 0))],
        scratch_shapes=[pltpu.VMEM((8, LANES), F32)],
        compiler_params=_cparams(1),
    )(p, p, b_f, qg, kg, seg, seg_t)


def _pair_split(x2, lo):
    return jnp.where(lo, x2, jnp.zeros_like(x2)), jnp.where(lo, jnp.zeros_like(x2), x2)


def _lo_mask():
    return lax.broadcasted_iota(jnp.int32, (1, LANES), 1) < HEAD_DIM


def _scores(qh, k2, cqh, ckh, causal):
    s = lax.dot_general(qh, k2, NT, preferred_element_type=F32)
    return jnp.where(causal, s + cqh - ckh, NEG)


def attn_fwd(q, k, v, cqp, ckr):
    tp, d = q.shape
    t = ATT_TILE
    nb = tp // t
    npair = d // LANES

    def body(q_ref, k_ref, v_ref, cq_ref, ck_ref, o_ref, lse_ref):
        qi = pl.program_id(1)
        lo = _lo_mask()
        qa, qb = _pair_split(q_ref[...], lo)
        cq = cq_ref[0]
        cqa, cqb = cq[:, 0:1], cq[:, HEAD_DIM:HEAD_DIM + 1]
        rows = qi * t + lax.broadcasted_iota(jnp.int32, (t, t), 0)
        cols0 = lax.broadcasted_iota(jnp.int32, (t, t), 1)

        def step(kj, carry):
            off = pl.multiple_of(kj * t, t)
            k2 = k_ref[pl.ds(off, t), :]
            v2 = v_ref[pl.ds(off, t), :]
            ck = ck_ref[0, kj]
            causal = cols0 + off <= rows
            out = []
            for hd, (qh, cqh) in enumerate(((qa, cqa), (qb, cqb))):
                m, l, acc = carry[3 * hd:3 * hd + 3]
                s = _scores(qh, k2, cqh, ck[hd:hd + 1, :], causal)
                m_new = jnp.maximum(m, jnp.max(s, axis=-1, keepdims=True))
                alpha = jnp.exp(m - m_new)
                pr = jnp.exp(s - m_new)
                l = alpha * l + jnp.sum(pr, axis=-1, keepdims=True)
                acc = alpha * acc + jnp.dot(pr.astype(BF16), v2, preferred_element_type=F32)
                out += [m_new, l, acc]
            return tuple(out)

        init = (jnp.full((t, 1), NEG, F32), jnp.zeros((t, 1), F32), jnp.zeros((t, LANES), F32)) * 2
        ma, la, acca, mb, lb, accb = lax.fori_loop(0, qi + 1, step, init)
        o_ref[...] = jnp.where(lo, acca / la, accb / lb).astype(BF16)
        lse_ref[0] = jnp.where(lo, ma + jnp.log(la), mb + jnp.log(lb))

    return pl.pallas_call(
        body, name="attn_fwd", grid=(npair, nb),
        out_shape=[jax.ShapeDtypeStruct((tp, d), BF16), jax.ShapeDtypeStruct((npair, tp, LANES), F32)],
        in_specs=[pl.BlockSpec((t, LANES), lambda h, i: (i, h)), pl.BlockSpec((tp, LANES), lambda h, i: (0, h)),
                  pl.BlockSpec((tp, LANES), lambda h, i: (0, h)),
                  pl.BlockSpec((1, t, LANES), lambda h, i: (h, i, 0)),
                  pl.BlockSpec((1, nb, 2, t), lambda h, i: (h, 0, 0, 0))],
        out_specs=[pl.BlockSpec((t, LANES), lambda h, i: (i, h)), pl.BlockSpec((1, t, LANES), lambda h, i: (h, i, 0))],
        compiler_params=_cparams(2),
    )(q, k, v, cqp, ckr)


def attn_bwd_dq(q, k, v, cqp, ckr, do, lse):
    tp, d = q.shape
    t = ATT_TILE
    nb = tp // t
    npair = d // LANES

    def body(q_ref, k_ref, v_ref, cq_ref, ck_ref, do_ref, lse_ref, dq_ref, delta_ref):
        qi = pl.program_id(1)
        lo = _lo_mask()
        qa, qb = _pair_split(q_ref[...], lo)
        doa, dob = _pair_split(do_ref[...], lo)
        cq = cq_ref[0]
        lse2 = lse_ref[0]
        heads = ((qa, doa, cq[:, 0:1], lse2[:, 0:1]),
                 (qb, dob, cq[:, HEAD_DIM:HEAD_DIM + 1], lse2[:, HEAD_DIM:HEAD_DIM + 1]))
        rows = qi * t + lax.broadcasted_iota(jnp.int32, (t, t), 0)
        cols0 = lax.broadcasted_iota(jnp.int32, (t, t), 1)

        def p_dp(kj, hd):
            qh, doh, cqh, lseh = heads[hd]
            off = pl.multiple_of(kj * t, t)
            k2 = k_ref[pl.ds(off, t), :]
            ck = ck_ref[0, kj]
            pr = jnp.exp(_scores(qh, k2, cqh, ck[hd:hd + 1, :], cols0 + off <= rows) - lseh)
            dpr = lax.dot_general(doh, v_ref[pl.ds(off, t), :], NT, preferred_element_type=F32)
            return pr, dpr, k2

        def delta_step(kj, carry):
            out = []
            for hd in range(2):
                pr, dpr, _ = p_dp(kj, hd)
                out.append(carry[hd] + jnp.sum(pr * dpr, axis=-1, keepdims=True))
            return tuple(out)

        dl = lax.fori_loop(0, qi + 1, delta_step, (jnp.zeros((t, 1), F32),) * 2)
        delta_ref[0] = jnp.where(lo, dl[0], dl[1])

        def step(kj, carry):
            out = []
            for hd in range(2):
                pr, dpr, k2 = p_dp(kj, hd)
                ds = (pr * (dpr - dl[hd])).astype(BF16)
                out.append(carry[hd] + jnp.dot(ds, k2, preferred_element_type=F32))
            return tuple(out)

        dqa, dqb = lax.fori_loop(0, qi + 1, step, (jnp.zeros((t, LANES), F32),) * 2)
        dq_ref[...] = jnp.where(lo, dqa, dqb)

    pair_rows = pl.BlockSpec((1, t, LANES), lambda h, i: (h, i, 0))
    blk = pl.BlockSpec((t, LANES), lambda h, i: (i, h))
    full = pl.BlockSpec((tp, LANES), lambda h, i: (0, h))
    return pl.pallas_call(
        body, name="attn_bwd_dq", grid=(npair, nb),
        out_shape=[jax.ShapeDtypeStruct((tp, d), F32), jax.ShapeDtypeStruct((npair, tp, LANES), F32)],
        in_specs=[blk, full, full, pair_rows, pl.BlockSpec((1, nb, 2, t), lambda h, i: (h, 0, 0, 0)), blk, pair_rows],
        out_specs=[blk, pair_rows],
        compiler_params=_cparams(2),
    )(q, k, v, cqp, ckr, do, lse)


def attn_bwd_dkv(q, k, v, cqp, ckr, do, lse, delta):
    tp, d = q.shape
    t = ATT_TILE
    nb = tp // t
    npair = d // LANES

    def body(q_ref, k_ref, v_ref, cq_ref, ck_ref, do_ref, lse_ref, delta_ref, dk_ref, dv_ref, dck_ref):
        kj = pl.program_id(1)
        lo = _lo_mask()
        k2 = k_ref[...]
        v2 = v_ref[...]
        ck = ck_ref[0, 0]
        cols = kj * t + lax.broadcasted_iota(jnp.int32, (t, t), 1)
        rows0 = lax.broadcasted_iota(jnp.int32, (t, t), 0)

        def step(qi, carry):
            dk, dv, dca, dcb = carry
            off = pl.multiple_of(qi * t, t)
            qa, qb = _pair_split(q_ref[pl.ds(off, t), :], lo)
            doa, dob = _pair_split(do_ref[pl.ds(off, t), :], lo)
            cq = cq_ref[0, pl.ds(off, t), :]
            lse2 = lse_ref[0, pl.ds(off, t), :]
            dl2 = delta_ref[0, pl.ds(off, t), :]
            causal = cols <= rows0 + off
            dcs = []
            for hd, (qh, doh) in enumerate(((qa, doa), (qb, dob))):
                c0 = hd * HEAD_DIM
                pr = jnp.exp(_scores(qh, k2, cq[:, c0:c0 + 1], ck[hd:hd + 1, :], causal) - lse2[:, c0:c0 + 1])
                dpr = lax.dot_general(doh, v2, NT, preferred_element_type=F32)
                ds = pr * (dpr - dl2[:, c0:c0 + 1])
                dv = dv + lax.dot_general(pr.astype(BF16), doh, TN, preferred_element_type=F32)
                dk = dk + lax.dot_general(ds.astype(BF16), qh, TN, preferred_element_type=F32)
                dcs.append(jnp.sum(ds, axis=0, keepdims=True))
            return dk, dv, dca - dcs[0], dcb - dcs[1]

        zero = jnp.zeros((t, LANES), F32)
        zrow = jnp.zeros((1, t), F32)
        dk, dv, dca, dcb = lax.fori_loop(kj, nb, step, (zero, zero, zrow, zrow))
        dk_ref[...] = dk
        dv_ref[...] = dv.astype(BF16)
        dck_ref[0, 0, 0:1, :] = dca
        dck_ref[0, 0, 1:2, :] = dcb

    full = pl.BlockSpec((tp, LANES), lambda h, j: (0, h))
    pair_full = pl.BlockSpec((1, tp, LANES), lambda h, j: (h, 0, 0))
    blk = pl.BlockSpec((t, LANES), lambda h, j: (j, h))
    ckblk = pl.BlockSpec((1, 1, 2, t), lambda h, j: (h, j, 0, 0))
    return pl.pallas_call(
        body, name="attn_bwd_dkv", grid=(npair, nb),
        out_shape=[jax.ShapeDtypeStruct((tp, d), F32), jax.ShapeDtypeStruct((tp, d), BF16),
                   jax.ShapeDtypeStruct((npair, nb, 2, t), F32)],
        in_specs=[full, blk, blk, pair_full, ckblk, full, pair_full, pair_full],
        out_specs=[blk, blk, ckblk],
        compiler_params=_cparams(2),
    )(q, k, v, cqp, ckr, do, lse, delta)


L2E = 1.4426950408889634
LN2 = 0.6931471805599453
AUG = HEAD_DIM


def _split3(r):
    r1 = r.astype(BF16).astype(F32)
    r2 = (r - r1).astype(BF16).astype(F32)
    r3 = (r - r1 - r2).astype(BF16).astype(F32)
    return r1, r2, r3


def c_prep_slots(p, b_f, qg, kg):
    tp = p.shape[0]
    d = qg.shape[1]
    n_heads = d // HEAD_DIM
    tm = ATT_TILE
    seg, seg_t, _ = _head_masks(d)

    def body(p_ref, pf_ref, bf_ref, qg_ref, kg_ref, seg_ref, segt_ref, q_ref, k_ref, v_ref, cum_ref, carry):
        @pl.when(pl.program_id(0) == 0)
        def _():
            carry[...] = jnp.zeros_like(carry)

        def norm(x, g):
            ms = jnp.dot(x * x, seg_ref[...], precision=HI, preferred_element_type=F32) * (1.0 / HEAD_DIM)
            r = jnp.dot(lax.rsqrt(ms + EPS), segt_ref[...], precision=HI, preferred_element_type=F32)
            return x * r * g

        qn = norm(p_ref[:, 0:d], qg_ref[...]) * (HEAD_DIM ** -0.5 * L2E)
        kn = norm(p_ref[:, d:2 * d], kg_ref[...])
        v_ref[...] = p_ref[:, 2 * d:].astype(BF16)
        xf = pf_ref[...] + bf_ref[...]
        logf = jnp.minimum(xf, 0.0) - jnp.log(1.0 + jnp.exp(-jnp.abs(xf)))
        r_ = lax.broadcasted_iota(jnp.int32, (tm, tm), 0)
        c_ = lax.broadcasted_iota(jnp.int32, (tm, tm), 1)
        cum = jnp.dot((c_ <= r_).astype(F32), logf, precision=HI, preferred_element_type=F32) + carry[0:1, :]
        cum_ref[...] = cum
        carry[0:1, :] += jnp.sum(logf, axis=0, keepdims=True)

        lane = lax.broadcasted_iota(jnp.int32, (1, LANES), 1)
        ones_q = jnp.where((lane >= AUG + 3) & (lane < AUG + 6), 1.0, 0.0)
        ones_k = jnp.where((lane >= AUG) & (lane < AUG + 3), 1.0, 0.0)
        for h in range(n_heads):
            c0 = LANES * (h // 2)
            src_q, src_k = qn[:, c0:c0 + LANES], kn[:, c0:c0 + LANES]
            if h % 2:
                src_q = pltpu.roll(src_q, HEAD_DIM, axis=1)
                src_k = pltpu.roll(src_k, HEAD_DIM, axis=1)
            c = cum[:, h:h + 1] * L2E
            a1, a2, a3 = _split3(c)
            b1, b2, b3 = _split3(-c)
            aug_q = jnp.where(lane == AUG, a1, jnp.where(lane == AUG + 1, a2, jnp.where(lane == AUG + 2, a3, ones_q)))
            aug_k = jnp.where(lane == AUG + 3, b1,
                              jnp.where(lane == AUG + 4, b2, jnp.where(lane == AUG + 5, b3, ones_k)))
            q_ref[:, LANES * h:LANES * (h + 1)] = jnp.where(lane < HEAD_DIM, src_q, aug_q).astype(BF16)
            k_ref[:, LANES * h:LANES * (h + 1)] = jnp.where(lane < HEAD_DIM, src_k, aug_k).astype(BF16)

    vec = pl.BlockSpec((1, d), lambda i: (0, 0))
    rowd = pl.BlockSpec((tm, d), lambda i: (i, 0))
    slots = pl.BlockSpec((tm, n_heads * LANES), lambda i: (i, 0))
    return pl.pallas_call(
        body, name="c_prep_slots", grid=(tp // tm,),
        out_shape=[jax.ShapeDtypeStruct((tp, n_heads * LANES), BF16)] * 2
        + [jax.ShapeDtypeStruct((tp, d), BF16), jax.ShapeDtypeStruct((tp, LANES), F32)],
        in_specs=[pl.BlockSpec((tm, 3 * d), lambda i: (i, 0)), pl.BlockSpec((tm, LANES), lambda i: (i, 3 * d // LANES)),
                  pl.BlockSpec((1, LANES), lambda i: (0, 0)), vec, vec,
                  pl.BlockSpec((d, LANES), lambda i: (0, 0)), pl.BlockSpec((LANES, d), lambda i: (0, 0))],
        out_specs=[slots, slots, rowd, pl.BlockSpec((tm, LANES), lambda i: (i, 0))],
        scratch_shapes=[pltpu.VMEM((8, LANES), F32)],
        compiler_params=_cparams(1),
    )(p, p, b_f, qg, kg, seg, seg_t)


ATT_GROUP = 4


def _grouped_loop(n_blocks, body, carry):
    n_groups = n_blocks // ATT_GROUP
    carry = lax.fori_loop(0, n_groups, lambda i, c: body(i * ATT_GROUP, ATT_GROUP, c), carry)
    return lax.fori_loop(n_groups * ATT_GROUP, n_blocks, lambda kj, c: body(kj, 1, c), carry)


def _diag_mask(t):
    return lax.broadcasted_iota(jnp.int32, (t, t), 0) <= lax.broadcasted_iota(jnp.int32, (t, t), 1)


def attn_fwd_t(qpt, kp, vt4):
    tp = kp.shape[0]
    t = ATT_TILE
    nb = tp // t
    npair = vt4.shape[0]

    def body(q_ref, k_ref, v_ref, o_ref, lse_ref):
        qi = pl.program_id(1)
        qts = (q_ref[0:LANES, :], q_ref[LANES:2 * LANES, :])
        mask = _diag_mask(t)

        def blocks(kj0, n, carry, masked=False):
            out = []
            for hd in range(2):
                m, l, acc = carry[3 * hd:3 * hd + 3]
                sts = []
                for g in range(n):
                    off = pl.multiple_of((kj0 + g) * t, t)
                    st = jnp.dot(k_ref[pl.ds(off, t), LANES * hd:LANES * (hd + 1)], qts[hd],
                                 preferred_element_type=F32)
                    sts.append(jnp.where(mask, st, NEG) if masked else st)
                m_new = m
                for st in sts:
                    m_new = jnp.maximum(m_new, jnp.max(st, axis=0, keepdims=True))
                alpha = jnp.exp2(m - m_new)
                l = alpha * l
                acc = alpha * acc
                for g, st in enumerate(sts):
                    pt = jnp.exp2(st - m_new)
                    l = l + jnp.sum(pt, axis=0, keepdims=True)
                    acc = acc + jnp.dot(v_ref[0, kj0 + g], pt.astype(BF16), preferred_element_type=F32)
                out += [m_new, l, acc]
            return tuple(out)

        init = (jnp.full((1, t), NEG, F32), jnp.zeros((1, t), F32), jnp.zeros((LANES, t), F32)) * 2
        carry = _grouped_loop(qi, blocks, init)
        ma, la, acca, mb, lb, accb = blocks(qi, 1, carry, masked=True)
        row = lax.broadcasted_iota(jnp.int32, (LANES, 1), 0)
        o_ref[...] = jnp.where(row < HEAD_DIM, acca / la, accb / lb).astype(BF16)
        lse_ref[0, 0:1, :] = ma + jnp.log(la) * L2E
        lse_ref[0, 1:2, :] = mb + jnp.log(lb) * L2E

    return pl.pallas_call(
        body, name="attn_fwd_t", grid=(npair, nb),
        out_shape=[jax.ShapeDtypeStruct((npair * LANES, tp), BF16), jax.ShapeDtypeStruct((npair, 2, tp), F32)],
        in_specs=[pl.BlockSpec((2 * LANES, t), lambda h, i: (h, i)), pl.BlockSpec((tp, 2 * LANES), lambda h, i: (0, h)),
                  pl.BlockSpec((1, nb, LANES, t), lambda h, i: (h, 0, 0, 0))],
        out_specs=[pl.BlockSpec((LANES, t), lambda h, i: (h, i)), pl.BlockSpec((1, 2, t), lambda h, i: (h, 0, i))],
        compiler_params=_cparams(2),
    )(qpt, kp, vt4)


def attn_bwd_t(qpt, qp, kp, kpt4, v, do, dot_, lse, cqt):
    tp = kp.shape[0]
    t = ATT_TILE
    nb = tp // t
    n_heads = kpt4.shape[0]
    d = v.shape[1]

    def body(qt_ref, q_ref, k_ref, kt_ref, v_ref, do_ref, dot_ref, lse_ref, cq_ref, dq_ref, dk_ref, dv_ref,
             p_sc, dp_sc, dc_sc):
        hd = pl.program_id(0) % 2
        qi = pl.program_id(1)

        @pl.when(qi == 0)
        def _():
            dk_ref[...] = jnp.zeros_like(dk_ref)
            dc_sc[...] = jnp.zeros_like(dc_sc)

        @pl.when((qi == 0) & (hd == 0))
        def _():
            dv_ref[...] = jnp.zeros_like(dv_ref)

        row = lax.broadcasted_iota(jnp.int32, (LANES, 1), 0)
        lane = lax.broadcasted_iota(jnp.int32, (1, LANES), 1)
        dot_h = jnp.where(row // HEAD_DIM == hd, dot_ref[...], jnp.zeros_like(dot_ref))
        do_h = jnp.where(lane // HEAD_DIM == hd, do_ref[...], jnp.zeros_like(do_ref))
        rr = cq_ref[0] - lse_ref[0]
        r1, r2, r3 = _split3(jnp.where(hd == 0, rr[0:1, :], rr[1:2, :]))
        qt = qt_ref[...].astype(F32)
        qt = jnp.where(row == AUG, r1, jnp.where(row == AUG + 1, r2, jnp.where(row == AUG + 2, r3, qt))).astype(BF16)
        mask = _diag_mask(t)

        def pass1(kj0, n, delta, masked=False):
            for g in range(n):
                off = pl.multiple_of((kj0 + g) * t, t)
                st = jnp.dot(k_ref[pl.ds(off, t), :], qt, preferred_element_type=F32)
                if masked:
                    st = jnp.where(mask, st, NEG)
                pt = jnp.exp2(st)
                dpt = jnp.dot(v_ref[pl.ds(off, t), :], dot_h, preferred_element_type=F32)
                p_sc[kj0 + g] = pt
                dp_sc[kj0 + g] = dpt
                delta = delta + jnp.sum(pt * dpt, axis=0, keepdims=True)
            return delta

        delta = _grouped_loop(qi, pass1, jnp.zeros((1, t), F32))
        delta = pass1(qi, 1, delta, masked=True)

        def pass2(kj0, n, dq):
            for g in range(n):
                off = pl.multiple_of((kj0 + g) * t, t)
                pt = p_sc[kj0 + g]
                ds32 = pt * (dp_sc[kj0 + g] - delta)
                ds = ds32.astype(BF16)
                dc_sc[pl.ds(off, t), :] += ds32[:, 0:LANES] + ds32[:, LANES:2 * LANES]
                dk_ref[pl.ds(off, t), :] += jnp.dot(ds, q_ref[...], preferred_element_type=F32)
                dv_ref[pl.ds(off, t), :] += jnp.dot(pt.astype(BF16), do_h, preferred_element_type=F32)
                dq = dq + jnp.dot(kt_ref[0, kj0 + g], ds, preferred_element_type=F32)
            return dq

        dq_ref[...] = _grouped_loop(qi + 1, pass2, jnp.zeros((LANES, t), F32))

        @pl.when(qi == nb - 1)
        def _():
            dk_ref[:, AUG + 3:AUG + 4] = jnp.sum(dc_sc[...], axis=1, keepdims=True)

    once = dict(pipeline_mode=pl.Buffered(1))
    pair_rows = pl.BlockSpec((1, 2, t), lambda h, i: (h // 2, 0, i))
    return pl.pallas_call(
        body, name="attn_bwd_t", grid=(n_heads, nb),
        out_shape=[jax.ShapeDtypeStruct((n_heads * LANES, tp), F32), jax.ShapeDtypeStruct((tp, n_heads * LANES), F32),
                   jax.ShapeDtypeStruct((tp, d), F32)],
        in_specs=[pl.BlockSpec((LANES, t), lambda h, i: (h, i)), pl.BlockSpec((t, LANES), lambda h, i: (i, h)),
                  pl.BlockSpec((tp, LANES), lambda h, i: (0, h), **once),
                  pl.BlockSpec((1, nb, LANES, t), lambda h, i: (h, 0, 0, 0), **once),
                  pl.BlockSpec((tp, LANES), lambda h, i: (0, h // 2), **once),
                  pl.BlockSpec((t, LANES), lambda h, i: (i, h // 2)), pl.BlockSpec((LANES, t), lambda h, i: (h // 2, i)),
                  pair_rows, pair_rows],
        out_specs=[pl.BlockSpec((LANES, t), lambda h, i: (h, i)), pl.BlockSpec((tp, LANES), lambda h, i: (0, h)),
                   pl.BlockSpec((tp, LANES), lambda h, i: (0, h // 2))],
        scratch_shapes=[pltpu.VMEM((nb, t, t), F32), pltpu.VMEM((nb, t, t), F32), pltpu.VMEM((tp, LANES), F32)],
        compiler_params=_cparams(2),
    )(qpt, qp, kp, kpt4, v, do, dot_, lse, cqt)


def rev_cumsum_rows(x):
    r, tp = x.shape
    t = ATT_TILE
    nb = tp // t

    def body(x_ref, o_ref, carry):
        @pl.when(pl.program_id(0) == 0)
        def _():
            carry[...] = jnp.zeros_like(carry)

        xv = x_ref[...]
        r_ = lax.broadcasted_iota(jnp.int32, (t, t), 0)
        c_ = lax.broadcasted_iota(jnp.int32, (t, t), 1)
        o_ref[...] = jnp.dot(xv, (r_ >= c_).astype(F32), precision=HI, preferred_element_type=F32) + carry[:, 0:1]
        carry[...] += jnp.sum(xv, axis=1, keepdims=True)

    return pl.pallas_call(
        body, name="rev_cumsum_rows", grid=(nb,), out_shape=jax.ShapeDtypeStruct((r, tp), F32),
        in_specs=[pl.BlockSpec((r, t), lambda i: (0, nb - 1 - i))],
        out_specs=pl.BlockSpec((r, t), lambda i: (0, nb - 1 - i)),
        scratch_shapes=[pltpu.VMEM((r, LANES), F32)],
        compiler_params=_cparams(1),
    )(x)


def c_elem_bwd(p, b_f, qg, kg, dq, dk, dv, dlogf):
    tp, n_out = p.shape
    d = qg.shape[1]
    tm = ATT_TILE
    n = tp // tm
    seg, seg_t, fold = _head_masks(d)

    def body(p_ref, pf_ref, bf_ref, qg_ref, kg_ref, dq_ref, dk_ref, dv_ref, dlf_ref, seg_ref, segt_ref, fold_ref,
             dp_ref, sm_ref, accq, acck, accf):
        i = pl.program_id(0)

        @pl.when(i == 0)
        def _():
            accq[...] = jnp.zeros_like(accq)
            acck[...] = jnp.zeros_like(acck)
            accf[...] = jnp.zeros_like(accf)

        def norm_bwd(x, g, dy, acc):
            ms = jnp.dot(x * x, seg_ref[...], precision=HI, preferred_element_type=F32) * (1.0 / HEAD_DIM)
            r = jnp.dot(lax.rsqrt(ms + EPS), segt_ref[...], precision=HI, preferred_element_type=F32)
            xhat = x * r
            acc[0:1, :] += jnp.sum(dy * xhat, axis=0, keepdims=True)
            gy = dy * g
            mean = jnp.dot(jnp.dot(gy * xhat, seg_ref[...], precision=HI, preferred_element_type=F32),
                           segt_ref[...], precision=HI, preferred_element_type=F32) * (1.0 / HEAD_DIM)
            return r * (gy - xhat * mean)

        dp_ref[:, 0:d] = norm_bwd(p_ref[:, 0:d], qg_ref[...], dq_ref[...] * (HEAD_DIM ** -0.5), accq).astype(BF16)
        dp_ref[:, d:2 * d] = norm_bwd(p_ref[:, d:2 * d], kg_ref[...], dk_ref[...] * LN2, acck).astype(BF16)
        dp_ref[:, 2 * d:3 * d] = dv_ref[...].astype(BF16)
        df = dlf_ref[...] * _sigmoid(-(pf_ref[...] + bf_ref[...]))
        accf[0:1, :] += jnp.sum(df, axis=0, keepdims=True)
        dp_ref[:, 3 * d:] = df.astype(BF16)

        @pl.when(i == n - 1)
        def _():
            sm_ref[...] = jnp.zeros_like(sm_ref)
            sm_ref[0:1, :] = jnp.dot(accq[0:1, :], fold_ref[...], precision=HI, preferred_element_type=F32)
            sm_ref[1:2, :] = jnp.dot(acck[0:1, :], fold_ref[...], precision=HI, preferred_element_type=F32)
            sm_ref[2:3, :] = accf[0:1, :]

    vec = pl.BlockSpec((1, d), lambda i: (0, 0))
    rowd = pl.BlockSpec((tm, d), lambda i: (i, 0))
    rowl = pl.BlockSpec((tm, LANES), lambda i: (i, 0))
    return pl.pallas_call(
        body, name="c_elem_bwd", grid=(n,),
        out_shape=[jax.ShapeDtypeStruct((tp, n_out), BF16), jax.ShapeDtypeStruct((8, LANES), F32)],
        in_specs=[pl.BlockSpec((tm, 2 * d), lambda i: (i, 0)), pl.BlockSpec((tm, LANES), lambda i: (i, 3 * d // LANES)),
                  pl.BlockSpec((1, LANES), lambda i: (0, 0)), vec, vec, rowd, rowd, rowd, rowl,
                  pl.BlockSpec((d, LANES), lambda i: (0, 0)), pl.BlockSpec((LANES, d), lambda i: (0, 0)),
                  pl.BlockSpec((d, LANES), lambda i: (0, 0))],
        out_specs=[pl.BlockSpec((tm, n_out), lambda i: (i, 0)), pl.BlockSpec((8, LANES), lambda i: (0, 0))],
        scratch_shapes=[pltpu.VMEM((8, d), F32), pltpu.VMEM((8, d), F32), pltpu.VMEM((8, LANES), F32)],
        compiler_params=_cparams(1),
    )(p, p, b_f, qg, kg, dq, dk, dv, dlogf, seg, seg_t, fold)


def loss_head(h, tgt, t_real):
    tp, d = h.shape
    tm = _pick(tp, (768, 512, 256))

    def body(h_ref, t_ref, dh_ref, l_ref):
        i = pl.program_id(0)

        @pl.when(i == 0)
        def _():
            l_ref[...] = jnp.zeros_like(l_ref)

        row = i * tm + lax.broadcasted_iota(jnp.int32, (tm, 1), 0)
        valid = (row >= N_META) & (row < t_real)
        e = jnp.where(valid, h_ref[...] - t_ref[...], 0.0)
        dh_ref[...] = e * (1.0 / d)
        per_row = jnp.sum(e * e, axis=-1, keepdims=True) * (1.0 / d)
        l_ref[...] += 0.5 * jnp.sum(per_row, axis=0, keepdims=True)

    return pl.pallas_call(
        body, name="loss_head", grid=(tp // tm,),
        out_shape=[jax.ShapeDtypeStruct((tp, d), F32), jax.ShapeDtypeStruct((8, LANES), F32)],
        in_specs=[pl.BlockSpec((tm, d), lambda i: (i, 0)), pl.BlockSpec((tm, d), lambda i: (i, 0))],
        out_specs=[pl.BlockSpec((tm, d), lambda i: (i, 0)), pl.BlockSpec((8, LANES), lambda i: (0, 0))],
        compiler_params=_cparams(1),
    )(h, tgt)


def sum_devices(x):
    _, r, c = x.shape

    def body(x_ref, o_ref):
        acc = x_ref[0]
        for dev in range(1, N_DEV):
            acc = acc + x_ref[dev]
        o_ref[...] = acc

    return pl.pallas_call(
        body, name="sum_devices", out_shape=jax.ShapeDtypeStruct((r, c), F32),
        in_specs=[pl.BlockSpec(memory_space=pltpu.VMEM)], out_specs=pl.BlockSpec(memory_space=pltpu.VMEM),
    )(x)


def _adamw_math(w, g, m, v):
    m = ADAM_B1 * m + (1.0 - ADAM_B1) * g
    v = ADAM_B2 * v + (1.0 - ADAM_B2) * (g * g)
    m_hat = m / (1.0 - ADAM_B1 ** ADAM_STEP)
    v_hat = v / (1.0 - ADAM_B2 ** ADAM_STEP)
    delta = -ADAM_LR * (m_hat / (jnp.sqrt(v_hat) + ADAM_EPS) + ADAM_WD * w)
    return delta, m, v


def adamw_small(w, g, m, v):
    def body(w_ref, g_ref, m_ref, v_ref, d_ref, nm_ref, nv_ref):
        d_ref[...], nm_ref[...], nv_ref[...] = _adamw_math(w_ref[...], g_ref[...], m_ref[...], v_ref[...])

    vm = pl.BlockSpec(memory_space=pltpu.VMEM)
    return pl.pallas_call(
        body, name="adamw_small", out_shape=[jax.ShapeDtypeStruct(w.shape, F32)] * 3,
        in_specs=[vm] * 4, out_specs=[vm] * 3,
    )(w, g, m, v)


def adamw_reduce(w, m, v, parts):
    r, c = w.shape
    n_parts = parts.shape[0]
    tr = _pick(r, (512, 384, 352, 256, 128, 8))

    def body(w_ref, m_ref, v_ref, p_ref, g_ref, d_ref, nm_ref, nv_ref):
        g = p_ref[0].astype(F32)
        for j in range(1, n_parts):
            g = g + p_ref[j].astype(F32)
        g_ref[...] = g
        d_ref[...], nm_ref[...], nv_ref[...] = _adamw_math(w_ref[...], g, m_ref[...], v_ref[...])

    blk = pl.BlockSpec((tr, c), lambda i: (i, 0))
    return pl.pallas_call(
        body, name="adamw_reduce", grid=(r // tr,), out_shape=[jax.ShapeDtypeStruct((r, c), F32)] * 4,
        in_specs=[blk, blk, blk, pl.BlockSpec((n_parts, tr, c), lambda i: (0, i, 0))],
        out_specs=[blk] * 4, compiler_params=_cparams(1),
    )(w, m, v, parts)


def _unshard(g, axis):
    g = jnp.moveaxis(g, 0, axis)
    return g.reshape(g.shape[:axis] + (g.shape[axis] * g.shape[axis + 1],) + g.shape[axis + 2:])


def _shard(full, axis):
    s = full.shape
    g = full.reshape(s[:axis] + (N_DEV, s[axis] // N_DEV) + s[axis + 1:])
    return jnp.moveaxis(g, axis, 0)


def _pad_lanes(a, n=LANES):
    flat = a.reshape(-1)
    pad = (-flat.shape[0]) % n
    return jnp.pad(flat, (0, pad)).reshape(-1, n)


BIG = ("ffn_w_gate", "ffn_w_up", "ffn_w_down", "a_w_in", "a_w_out", "b_w_in", "b_w_out", "c_w_in", "c_w_out")
SHARD_AXIS = {"ffn_w_gate": 3, "ffn_w_up": 3, "ffn_w_down": 2, "a_w_in": 2, "a_w_out": 1, "b_w_in": 2,
              "b_w_out": 1, "c_w_in": 2, "c_w_out": 1, "meta": 1, "ffn_norm": 2, "a_conv": 2, "b_conv": 2}
SMALL_SHARDED = ("meta", "ffn_norm", "a_conv", "b_conv")
SMALL_REPL = ("mix_norm", "b_conv_bias", "b_ln_g", "b_ln_b", "c_b_f", "c_q_norm", "c_k_norm")
WEIGHTS = ("meta", "ffn_norm", "ffn_w_gate", "ffn_w_up", "ffn_w_down", "mix_norm", "a_w_in", "a_conv", "a_w_out",
           "b_w_in", "b_conv", "b_conv_bias", "b_ln_g", "b_ln_b", "b_w_out", "c_w_in", "c_b_f", "c_q_norm",
           "c_k_norm", "c_w_out")
N_MIXERS = 3


def kernel(x, meta, ffn_norm, ffn_w_gate, ffn_w_up, ffn_w_down, mix_norm, a_w_in, a_conv, a_w_out, b_w_in, b_conv, b_conv_bias, b_ln_g, b_ln_b, b_w_out, c_w_in, c_b_f, c_q_norm, c_k_norm, c_w_out, loss_target, m_meta, m_ffn_norm, m_ffn_w_gate, m_ffn_w_up, m_ffn_w_down, m_mix_norm, m_a_w_in, m_a_conv, m_a_w_out, m_b_w_in, m_b_conv, m_b_conv_bias, m_b_ln_g, m_b_ln_b, m_b_w_out, m_c_w_in, m_c_b_f, m_c_q_norm, m_c_k_norm, m_c_w_out, v_meta, v_ffn_norm, v_ffn_w_gate, v_ffn_w_up, v_ffn_w_down, v_mix_norm, v_a_w_in, v_a_conv, v_a_w_out, v_b_w_in, v_b_conv, v_b_conv_bias, v_b_ln_g, v_b_ln_b, v_b_w_out, v_c_w_in, v_c_b_f, v_c_q_norm, v_c_k_norm, v_c_w_out):
    local = dict(locals())
    w = {n: local[n] for n in WEIGHTS}
    mom = {n: local["m_" + n] for n in WEIGHTS}
    var = {n: local["v_" + n] for n in WEIGHTS}
    d = x.shape[-1]
    depth = ffn_norm.shape[0]
    n_heads = d // HEAD_DIM
    seq = x.shape[1]
    t_real = N_META + seq
    tp = -(-t_real // ROW_ALIGN) * ROW_ALIGN
    nb = tp // ATT_TILE
    npair = d // LANES

    names = BIG + SMALL_SHARDED
    gathered = all_gather([w[n].astype(BF16) for n in BIG] + [w[n] for n in SMALL_SHARDED])
    full = {n: _unshard(g, SHARD_AXIS[n]) for n, g in zip(names, gathered)}
    n_c = full["c_w_in"].shape[-1]
    n_cp = 3 * d + LANES
    c_w_qkv = jnp.pad(full["c_w_in"], ((0, 0), (0, 0), (0, n_cp - n_c)))
    b_f_pad = jnp.pad(c_b_f, ((0, 0), (0, LANES - n_heads)))
    qg_t = jnp.tile(c_q_norm, (1, n_heads))
    kg_t = jnp.tile(c_k_norm, (1, n_heads))

    h = jnp.concatenate([full["meta"], x[0], jnp.zeros((tp - t_real, d), F32)], axis=0)
    saved = []
    for i in range(depth):
        mixer, j = i % N_MIXERS, i // N_MIXERS
        s = {"h0": h}
        h, s["gate_a"], s["up_a"] = ffn_fwd(h, full["ffn_norm"][i, 0:1], full["ffn_w_gate"][i, 0],
                                            full["ffn_w_up"][i, 0], full["ffn_w_down"][i, 0])
        s["h1"] = h
        g_mix = mix_norm[i:i + 1]
        if mixer == 0:
            s["p"], s["u"] = norm_matmul(h, g_mix, full["a_w_in"][j], BF16, True)
            s["y"] = a_elem_fwd(s["p"], full["a_conv"][j])
            h = res_matmul(h, s["y"], full["a_w_out"][j])
        elif mixer == 1:
            s["p"], s["u"] = norm_matmul(h, g_mix, full["b_w_in"][j], BF16, True)
            s["y"] = b_elem_fwd(s["p"], full["b_conv"][j], b_conv_bias[j:j + 1], b_ln_g[j:j + 1], b_ln_b[j:j + 1])
            h = res_matmul(h, s["y"], full["b_w_out"][j])
        else:
            s["p"], s["u"] = norm_matmul(h, g_mix, c_w_qkv[j], F32, True)
            s["qp"], s["kp"], s["v"], cum = c_prep_slots(s["p"], b_f_pad[j:j + 1], qg_t[j:j + 1], kg_t[j:j + 1])
            s["cqt"] = (cum[:, :n_heads].T * L2E).reshape(npair, 2, tp)
            s["qpt"] = s["qp"].T
            s["kpt4"] = s["kp"].T.reshape(n_heads, LANES, nb, ATT_TILE).transpose(0, 2, 1, 3)
            vt4 = s["v"].T.reshape(npair, LANES, nb, ATT_TILE).transpose(0, 2, 1, 3)
            ot, s["lse"] = attn_fwd_t(s["qpt"], s["kp"], vt4)
            s["y"] = ot.T
            h = res_matmul(h, s["y"], full["c_w_out"][j])
        s["h2"] = h
        h, s["gate_b"], s["up_b"] = ffn_fwd(h, full["ffn_norm"][i, 1:2], full["ffn_w_gate"][i, 1],
                                            full["ffn_w_up"][i, 1], full["ffn_w_down"][i, 1])
        saved.append(s)

    tgt = jnp.concatenate([jnp.zeros((N_META, d), F32), loss_target[0], jnp.zeros((tp - t_real, d), F32)], axis=0)
    dh, loss_part = loss_head(h, tgt, t_real)

    gfull = {n: [None] * full[n].shape[0] for n in ("a_w_in", "a_w_out", "b_w_in", "b_w_out", "c_w_in", "c_w_out")}
    gffn = {n: [[None, None] for _ in range(depth)] for n in ("ffn_w_gate", "ffn_w_up", "ffn_w_down")}
    g_ffn_norm = [[None, None] for _ in range(depth)]
    g_mix_norm = [None] * depth
    g_a_conv = [None] * a_conv.shape[0]
    g_b_conv = [None] * b_conv.shape[0]
    g_b_stats = [None] * b_conv.shape[0]
    g_c_small = [None] * c_b_f.shape[0]

    def ffn_backward(dh_out, i, half, h_in, gate, up):
        dh_in, hn, dgate, dup, act, dg = ffn_bwd(dh_out, h_in, full["ffn_norm"][i, half:half + 1], gate, up,
                                                 full["ffn_w_gate"][i, half], full["ffn_w_up"][i, half],
                                                 full["ffn_w_down"][i, half])
        gffn["ffn_w_gate"][i][half] = atb(hn, dgate)
        gffn["ffn_w_up"][i][half] = atb(hn, dup)
        gffn["ffn_w_down"][i][half] = atb(act, dh_out, 0.5)
        g_ffn_norm[i][half] = dg
        return dh_in

    for i in reversed(range(depth)):
        mixer, j = i % N_MIXERS, i // N_MIXERS
        s = saved[i]
        g_mix = mix_norm[i:i + 1]
        dh = ffn_backward(dh, i, 1, s["h2"], s["gate_b"], s["up_b"])
        if mixer == 0:
            dy = matmul_nt(dh, full["a_w_out"][j])
            dp, dwc = a_elem_bwd(s["p"], dy, full["a_conv"][j])
            g_a_conv[j] = dwc[:a_conv.shape[1]]
            gfull["a_w_out"][j] = atb(s["y"], dh)
            gfull["a_w_in"][j] = atb(s["u"], dp)
            dh, g_mix_norm[i] = matmul_nt_rms_bwd(dp, full["a_w_in"][j], s["h1"], g_mix, dh)
        elif mixer == 1:
            dy = matmul_nt(dh, full["b_w_out"][j])
            du2, g_b_stats[j] = b_elem_bwd1(s["p"], dy, full["b_conv"][j], b_conv_bias[j:j + 1],
                                            b_ln_g[j:j + 1], b_ln_b[j:j + 1])
            dp, dwc = b_elem_bwd2(s["p"], du2, full["b_conv"][j])
            g_b_conv[j] = dwc[:b_conv.shape[1]]
            gfull["b_w_out"][j] = atb(s["y"], dh)
            gfull["b_w_in"][j] = atb(s["u"], dp)
            dh, g_mix_norm[i] = matmul_nt_rms_bwd(dp, full["b_w_in"][j], s["h1"], g_mix, dh)
        else:
            do = matmul_nt(dh, full["c_w_out"][j])
            dqt, dkp, dv = attn_bwd_t(s["qpt"], s["qp"], s["kp"], s["kpt4"], s["v"], do, do.T, s["lse"], s["cqt"])
            dq = dqt.reshape(n_heads, LANES, tp)[:, :HEAD_DIM].reshape(d, tp).T
            dkp = dkp.reshape(tp, n_heads, LANES)
            dk = dkp[:, :, :HEAD_DIM].reshape(tp, d)
            dlogf = rev_cumsum_rows(-dkp[:, :, AUG + 3].T)
            dlogf = jnp.pad(dlogf.T, ((0, 0), (0, LANES - n_heads)))
            dp, g_c_small[j] = c_elem_bwd(s["p"], b_f_pad[j:j + 1], qg_t[j:j + 1], kg_t[j:j + 1], dq, dk, dv, dlogf)
            gfull["c_w_out"][j] = atb(s["y"], dh)
            gfull["c_w_in"][j] = atb(s["u"], dp)[:, :n_c]
            dh, g_mix_norm[i] = matmul_nt_rms_bwd(dp, c_w_qkv[j], s["h1"], g_mix, dh)
        dh = ffn_backward(dh, i, 0, s["h0"], s["gate_a"], s["up_a"])

    grad_x = dh[N_META:t_real][None]

    gbig = {n: jnp.stack([jnp.stack(r) for r in gffn[n]]) for n in gffn}
    gbig.update({n: jnp.stack(gfull[n]) for n in gfull})
    my_c = lax.axis_index("c")

    def by_core(n):
        g = _shard(gbig[n], SHARD_AXIS[n]).astype(BF16)
        return jnp.moveaxis(g.reshape((2, 2, 2) + g.shape[1:]), 2, 0).reshape((2, N_CHIP) + g.shape[1:])

    halves = [by_core(n) for n in BIG]
    theirs = swap_with_sibling(halves)
    chip_sums = []
    for hv, th in zip(halves, theirs):
        mine = lax.dynamic_index_in_dim(hv, my_c, 0, keepdims=False)
        chip_sums.append(add_bf16(mine.reshape(-1, hv.shape[-1]), th.reshape(-1, hv.shape[-1])).reshape(th.shape))
    parts = exchange_chips(chip_sums)
    out_g, out_d, out_m, out_v = {}, {}, {}, {}
    for n, prt in zip(BIG, parts):
        shp = w[n].shape
        flat = lambda a: a.reshape(-1, shp[-1])
        res = adamw_reduce(flat(w[n]), flat(mom[n]), flat(var[n]), prt.reshape(N_CHIP, -1, shp[-1]))
        out_g[n], out_d[n], out_m[n], out_v[n] = [r.reshape(shp) for r in res]

    def rows(a):
        return a.reshape(-1, d)

    def lane_rows(a):
        return jnp.pad(a.reshape(1, -1), ((0, 0), (0, d - a.size)))

    c_small = jnp.stack(g_c_small)
    pieces = [("meta", dh[:N_META]),
              ("ffn_norm", rows(jnp.stack([jnp.stack(r) for r in g_ffn_norm]))),
              ("mix_norm", rows(jnp.stack(g_mix_norm))),
              ("a_conv", rows(jnp.stack(g_a_conv))),
              ("b_conv", rows(jnp.stack(g_b_conv))),
              ("b_ln_g", rows(jnp.stack([st[0] for st in g_b_stats]))),
              ("b_ln_b", rows(jnp.stack([st[1] for st in g_b_stats]))),
              ("b_conv_bias", rows(jnp.stack([st[2] for st in g_b_stats]))),
              ("c_q_norm", lane_rows(c_small[:, 0, :HEAD_DIM])),
              ("c_k_norm", lane_rows(c_small[:, 1, :HEAD_DIM])),
              ("c_b_f", lane_rows(c_small[:, 2, :n_heads])),
              ("loss", lane_rows(loss_part[0:1, 0:1]))]
    packed = jnp.concatenate([pc for _, pc in pieces], axis=0)
    n_rows = packed.shape[0]
    packed = jnp.pad(packed, ((0, (-n_rows) % 8), (0, 0)))
    total = sum_devices(all_gather([packed])[0])
    small_g, r0 = {}, 0
    for n, pc in pieces:
        small_g[n] = total[r0:r0 + pc.shape[0]]
        r0 += pc.shape[0]
    loss = small_g.pop("loss")[0, 0]
    me = 4 * lax.axis_index("x") + 2 * lax.axis_index("y") + lax.axis_index("c")
    for n in SMALL_SHARDED:
        cols = w[n].shape[-1]
        g = lax.dynamic_slice_in_dim(small_g[n], me * cols, cols, axis=1)
        out_g[n] = g.reshape(w[n].shape)
    for n in SMALL_REPL:
        out_g[n] = small_g[n].reshape(-1)[:w[n].size].reshape(w[n].shape)
    small = SMALL_SHARDED + SMALL_REPL
    pack = lambda dct: jnp.concatenate([_pad_lanes(dct[n]) for n in small], axis=0)
    res = adamw_small(pack(w), pack(out_g), pack(mom), pack(var))
    r0 = 0
    for n in small:
        nr = -(-w[n].size // LANES)
        for dct, arr in zip((out_d, out_m, out_v), res):
            dct[n] = arr[r0:r0 + nr].reshape(-1)[:w[n].size].reshape(w[n].shape)
        r0 += nr

    return (loss, grad_x, *[out_g[n] for n in WEIGHTS], *[out_d[n] for n in WEIGHTS],
            *[out_m[n] for n in WEIGHTS], *[out_v[n] for n in WEIGHTS])
```

```python
import functools

import jax
import jax.numpy as jnp
from jax import lax
from jax.experimental import pallas as pl
from jax.experimental.pallas import tpu as pltpu

F32 = jnp.float32
BF16 = jnp.bfloat16
HI = lax.Precision.HIGHEST
EPS = 1e-6
NEG = -1e30
N_META = 16
HEAD_DIM = 64
LANES = 128
N_DEV = 8
ROW_ALIGN = 256
VMEM_LIMIT = 56 * 1024 * 1024
ADAM_LR, ADAM_B1, ADAM_B2, ADAM_EPS, ADAM_WD, ADAM_STEP = 0.001, 0.9, 0.999, 1e-08, 0.01, 10
MESH = pl.DeviceIdType.MESH
NT = (((1,), (1,)), ((), ()))
TN = (((0,), (0,)), ((), ()))


def _cparams(n_axes):
    return pltpu.CompilerParams(dimension_semantics=("arbitrary",) * n_axes, vmem_limit_bytes=VMEM_LIMIT)


def _pick(n, cands):
    for c in cands:
        if n % c == 0:
            return c
    raise ValueError(f"no tile for {n} among {cands}")


def _sigmoid(x):
    return 1.0 / (1.0 + jnp.exp(-x))


def _rms(x):
    r = lax.rsqrt(jnp.mean(x * x, axis=-1, keepdims=True) + EPS)
    return x * r, r


def _rms_bwd(xhat, r, g, dy):
    gy = dy * g
    dx = r * (gy - xhat * jnp.mean(gy * xhat, axis=-1, keepdims=True))
    return dx, jnp.sum(dy * xhat, axis=0, keepdims=True)


def _mesh_pos():
    return lax.axis_index("x"), lax.axis_index("y"), lax.axis_index("c")


def all_gather(arrs):
    n = len(arrs)
    hbm = pl.BlockSpec(memory_space=pltpu.HBM)

    def body(*refs):
        ins, outs = refs[:n], refs[n:2 * n]
        send_sems, recv_sems, local_sems = refs[2 * n:]
        x, y, c = _mesh_pos()
        me, sibling = (x, y, c), (x, y, 1 - c)
        chips = [(1 - x, y), (x, 1 - y), (1 - x, 1 - y)]

        def slot(a, px, py, pc):
            return outs[a].at[4 * px + 2 * py + pc]

        def copy(a, k, block, to, src=None):
            return pltpu.make_async_remote_copy(
                src_ref=slot(a, *block) if src is None else src, dst_ref=slot(a, *block),
                send_sem=send_sems.at[7 * a + k], recv_sem=recv_sems.at[7 * a + k],
                device_id=to, device_id_type=MESH)

        own, first, passed = [], [], []
        for a in range(n):
            cp = pltpu.make_async_copy(ins[a], slot(a, *me), local_sems.at[a])
            cp.start()
            own.append(cp)
            first.append(copy(a, 0, me, sibling, src=ins[a]))
            first += [copy(a, 1 + j, me, (*chip, c), src=ins[a]) for j, chip in enumerate(chips)]
        for cp in first:
            cp.start()
        for j, chip in enumerate(chips):
            for a in range(n):
                copy(a, 1 + j, (*chip, c), me).wait_recv()
                cp = copy(a, 4 + j, (*chip, c), sibling)
                cp.start()
                passed.append(cp)
        for a in range(n):
            copy(a, 0, sibling, me).wait_recv()
            for j, chip in enumerate(chips):
                copy(a, 4 + j, (*chip, 1 - c), me).wait_recv()
        for cp in first + passed:
            cp.wait_send()
        for cp in own:
            cp.wait()

    return pl.pallas_call(
        body, name="all_gather",
        out_shape=[jax.ShapeDtypeStruct((N_DEV,) + a.shape, a.dtype) for a in arrs],
        in_specs=[hbm] * n, out_specs=[hbm] * n,
        scratch_shapes=[pltpu.SemaphoreType.DMA((7 * n,)), pltpu.SemaphoreType.DMA((7 * n,)),
                        pltpu.SemaphoreType.DMA((n,))],
    )(*arrs)


N_CHIP = 4


def swap_with_sibling(arrs):
    n = len(arrs)
    hbm = pl.BlockSpec(memory_space=pltpu.HBM)

    def body(*refs):
        ins, mine, theirs = refs[:n], refs[n:2 * n], refs[2 * n:3 * n]
        send_sems, recv_sems, local_sems = refs[3 * n:]
        x, y, c = _mesh_pos()
        copies = [pltpu.make_async_remote_copy(
            src_ref=ins[a].at[1 - c], dst_ref=theirs[a], send_sem=send_sems.at[a], recv_sem=recv_sems.at[a],
            device_id=(x, y, 1 - c), device_id_type=MESH) for a in range(n)]
        keeps = [pltpu.make_async_copy(ins[a].at[c], mine[a], local_sems.at[a]) for a in range(n)]
        for cp in copies + keeps:
            cp.start()
        for cp in copies + keeps:
            cp.wait()

    return pl.pallas_call(
        body, name="swap_with_sibling",
        out_shape=[jax.ShapeDtypeStruct(a.shape[1:], a.dtype) for a in arrs] * 2,
        in_specs=[hbm] * n, out_specs=[hbm] * (2 * n),
        scratch_shapes=[pltpu.SemaphoreType.DMA((n,)), pltpu.SemaphoreType.DMA((n,)), pltpu.SemaphoreType.DMA((n,))],
    )(*arrs)


def exchange_chips(arrs):
    n = len(arrs)
    hbm = pl.BlockSpec(memory_space=pltpu.HBM)

    def body(*refs):
        ins, outs = refs[:n], refs[n:2 * n]
        send_sems, recv_sems, local_sems = refs[2 * n:]
        x, y, c = _mesh_pos()
        me = 2 * x + y
        own, sent = [], []
        for a in range(n):
            cp = pltpu.make_async_copy(ins[a].at[me], outs[a].at[me], local_sems.at[a])
            cp.start()
            own.append(cp)
        for k in range(1, N_CHIP):
            px = 1 - x if k & 2 else x
            py = 1 - y if k & 1 else y
            peer = 2 * px + py
            for a in range(n):
                cp = pltpu.make_async_remote_copy(
                    src_ref=ins[a].at[peer], dst_ref=outs[a].at[me],
                    send_sem=send_sems.at[3 * a + k - 1], recv_sem=recv_sems.at[3 * a + k - 1],
                    device_id=(px, py, c), device_id_type=MESH)
                cp.start()
                sent.append((cp, a, k, peer))
        for cp, a, k, peer in sent:
            pltpu.make_async_remote_copy(
                src_ref=ins[a].at[peer], dst_ref=outs[a].at[peer],
                send_sem=send_sems.at[3 * a + k - 1], recv_sem=recv_sems.at[3 * a + k - 1],
                device_id=(x, y, c), device_id_type=MESH).wait_recv()
        for cp, a, k, peer in sent:
            cp.wait_send()
        for cp in own:
            cp.wait()

    return pl.pallas_call(
        body, name="exchange_chips",
        out_shape=[jax.ShapeDtypeStruct(a.shape, a.dtype) for a in arrs],
        in_specs=[hbm] * n, out_specs=[hbm] * n,
        scratch_shapes=[pltpu.SemaphoreType.DMA((3 * n,)), pltpu.SemaphoreType.DMA((3 * n,)),
                        pltpu.SemaphoreType.DMA((n,))],
    )(*arrs)


def add_bf16(a, b):
    r, c = a.shape
    tr = _pick(r, (2048, 1408, 1024, 512, 256, 128, 8))

    def body(a_ref, b_ref, o_ref):
        o_ref[...] = (a_ref[...].astype(F32) + b_ref[...].astype(F32)).astype(BF16)

    blk = pl.BlockSpec((tr, c), lambda i: (i, 0))
    return pl.pallas_call(
        body, name="add_bf16", grid=(r // tr,), out_shape=jax.ShapeDtypeStruct((r, c), BF16),
        in_specs=[blk, blk], out_specs=blk, compiler_params=_cparams(1),
    )(a, b)


def _resident(shape):
    return pl.BlockSpec(shape, lambda i: (0,) * len(shape), pipeline_mode=pl.Buffered(1))


def ffn_fwd(h, g, wg, wu, wd):
    tp, d = h.shape
    f = wg.shape[1]
    tm = _pick(tp, (384, 256))
    tf = _pick(f, (1408, 1024, 512, 256, 128))

    def body(h_ref, g_ref, wg_ref, wu_ref, wd_ref, ho_ref, gate_ref, up_ref):
        x = h_ref[...]
        xhat, _ = _rms(x)
        hn = (xhat * g_ref[...]).astype(BF16)
        acc = jnp.zeros((tm, d), F32)
        for c0 in range(0, f, tf):
            gate = jnp.dot(hn, wg_ref[:, c0:c0 + tf], preferred_element_type=F32)
            up = jnp.dot(hn, wu_ref[:, c0:c0 + tf], preferred_element_type=F32)
            gate_ref[:, c0:c0 + tf] = gate.astype(BF16)
            up_ref[:, c0:c0 + tf] = up.astype(BF16)
            act = (gate * _sigmoid(gate) * up).astype(BF16)
            acc = acc + jnp.dot(act, wd_ref[c0:c0 + tf, :], preferred_element_type=F32)
        ho_ref[...] = x + 0.5 * acc

    row = lambda n: pl.BlockSpec((tm, n), lambda i: (i, 0))
    return pl.pallas_call(
        body, name="ffn_fwd", grid=(tp // tm,),
        out_shape=[jax.ShapeDtypeStruct((tp, d), F32), jax.ShapeDtypeStruct((tp, f), BF16),
                   jax.ShapeDtypeStruct((tp, f), BF16)],
        in_specs=[row(d), _resident((1, d)), _resident((d, f)), _resident((d, f)), _resident((f, d))],
        out_specs=[row(d), row(f), row(f)],
        compiler_params=_cparams(1),
    )(h, g, wg, wu, wd)


def ffn_bwd(dho, h, g, gate, up, wg, wu, wd):
    tp, d = h.shape
    f = wg.shape[1]
    tm = _pick(tp, (256,))
    tf = _pick(f, (1408, 1024, 512, 256, 128))

    def body(dho_ref, h_ref, g_ref, gate_ref, up_ref, wg_ref, wu_ref, wd_ref,
             dhi_ref, hn_ref, dgate_ref, dup_ref, act_ref, dg_ref):
        @pl.when(pl.program_id(0) == 0)
        def _():
            dg_ref[...] = jnp.zeros_like(dg_ref)

        dho_ = dho_ref[...]
        dout = (0.5 * dho_).astype(BF16)
        dhn = jnp.zeros((tm, d), F32)
        for c0 in range(0, f, tf):
            dact = lax.dot_general(dout, wd_ref[c0:c0 + tf, :], NT, preferred_element_type=F32)
            gt = gate_ref[:, c0:c0 + tf].astype(F32)
            u = up_ref[:, c0:c0 + tf].astype(F32)
            sig = _sigmoid(gt)
            silu = gt * sig
            act_ref[:, c0:c0 + tf] = (silu * u).astype(BF16)
            dup = (dact * silu).astype(BF16)
            dgate = (dact * u * (sig * (1.0 + gt * (1.0 - sig)))).astype(BF16)
            dup_ref[:, c0:c0 + tf] = dup
            dgate_ref[:, c0:c0 + tf] = dgate
            dhn = dhn + (lax.dot_general(dgate, wg_ref[:, c0:c0 + tf], NT, preferred_element_type=F32)
                         + lax.dot_general(dup, wu_ref[:, c0:c0 + tf], NT, preferred_element_type=F32))
        xhat, r = _rms(h_ref[...])
        gg = g_ref[...]
        dx, dgp = _rms_bwd(xhat, r, gg, dhn)
        dg_ref[...] += dgp
        dhi_ref[...] = dho_ + dx
        hn_ref[...] = (xhat * gg).astype(BF16)

    row = lambda n: pl.BlockSpec((tm, n), lambda i: (i, 0))
    return pl.pallas_call(
        body, name="ffn_bwd", grid=(tp // tm,),
        out_shape=[jax.ShapeDtypeStruct((tp, d), F32), jax.ShapeDtypeStruct((tp, d), BF16),
                   jax.ShapeDtypeStruct((tp, f), BF16), jax.ShapeDtypeStruct((tp, f), BF16),
                   jax.ShapeDtypeStruct((tp, f), BF16), jax.ShapeDtypeStruct((1, d), F32)],
        in_specs=[row(d), row(d), _resident((1, d)), row(f), row(f),
                  _resident((d, f)), _resident((d, f)), _resident((f, d))],
        out_specs=[row(d), row(d), row(f), row(f), row(f), pl.BlockSpec((1, d), lambda i: (0, 0))],
        compiler_params=_cparams(1),
    )(dho, h, g, gate, up, wg, wu, wd)


def atb(a, b, scale=1.0):
    tp, m = a.shape
    n = b.shape[1]
    tmm = _pick(m, (1024, 1408, 640, 512, 256, 128))
    tn = _pick(n, (1024, 1408, 640, 512, 256, 128))
    tk = _pick(tp, (1408, 768, 512, 256))
    nk = tp // tk

    def body(a_ref, b_ref, o_ref, acc):
        k = pl.program_id(2)

        @pl.when(k == 0)
        def _():
            acc[...] = jnp.zeros_like(acc)

        acc[...] += lax.dot_general(a_ref[...].astype(BF16), b_ref[...].astype(BF16), TN,
                                    preferred_element_type=F32)

        @pl.when(k == nk - 1)
        def _():
            o_ref[...] = (acc[...] * scale).astype(BF16)

    return pl.pallas_call(
        body, name="atb", grid=(m // tmm, n // tn, nk),
        out_shape=jax.ShapeDtypeStruct((m, n), BF16),
        in_specs=[pl.BlockSpec((tk, tmm), lambda i, j, k: (k, i)), pl.BlockSpec((tk, tn), lambda i, j, k: (k, j))],
        out_specs=pl.BlockSpec((tmm, tn), lambda i, j, k: (i, j)),
        scratch_shapes=[pltpu.VMEM((tmm, tn), F32)],
        compiler_params=_cparams(3),
    )(a, b)


def norm_matmul(h, g, w, out_dtype, emit_u):
    tp, d = h.shape
    n = w.shape[1]
    tm = _pick(tp, (768, 512, 256))
    tn = _pick(n, (1024, 768, 640, 512, 256, 128))

    def body(h_ref, g_ref, w_ref, o_ref, *rest):
        u_sc = rest[-1]

        @pl.when(pl.program_id(1) == 0)
        def _():
            xhat, _ = _rms(h_ref[...])
            u_sc[...] = (xhat * g_ref[...]).astype(BF16)
            if emit_u:
                rest[0][...] = u_sc[...]

        o_ref[...] = jnp.dot(u_sc[...], w_ref[...], preferred_element_type=F32).astype(out_dtype)

    out_shape = [jax.ShapeDtypeStruct((tp, n), out_dtype)]
    out_specs = [pl.BlockSpec((tm, tn), lambda i, j: (i, j))]
    if emit_u:
        out_shape.append(jax.ShapeDtypeStruct((tp, d), BF16))
        out_specs.append(pl.BlockSpec((tm, d), lambda i, j: (i, 0)))
    return pl.pallas_call(
        body, name="norm_matmul", grid=(tp // tm, n // tn), out_shape=out_shape,
        in_specs=[pl.BlockSpec((tm, d), lambda i, j: (i, 0)), pl.BlockSpec((1, d), lambda i, j: (0, 0)),
                  pl.BlockSpec((d, tn), lambda i, j: (0, j))],
        out_specs=out_specs, scratch_shapes=[pltpu.VMEM((tm, d), BF16)],
        compiler_params=_cparams(2),
    )(h, g, w)


def res_matmul(h, y, w):
    tp, d = h.shape
    k = y.shape[1]
    tm = _pick(tp, (768, 512, 256))

    def body(h_ref, y_ref, w_ref, o_ref):
        o_ref[...] = h_ref[...] + jnp.dot(y_ref[...], w_ref[...], preferred_element_type=F32)

    return pl.pallas_call(
        body, name="res_matmul", grid=(tp // tm,), out_shape=jax.ShapeDtypeStruct((tp, d), F32),
        in_specs=[pl.BlockSpec((tm, d), lambda i: (i, 0)), pl.BlockSpec((tm, k), lambda i: (i, 0)),
                  pl.BlockSpec((k, d), lambda i: (0, 0))],
        out_specs=pl.BlockSpec((tm, d), lambda i: (i, 0)),
        compiler_params=_cparams(1),
    )(h, y, w)


def matmul_nt(a, w):
    tp, d = a.shape
    k = w.shape[0]
    tm = _pick(tp, (768, 512, 256))

    def body(a_ref, w_ref, o_ref):
        o_ref[...] = lax.dot_general(a_ref[...].astype(BF16), w_ref[...], NT,
                                     preferred_element_type=F32).astype(BF16)

    return pl.pallas_call(
        body, name="matmul_nt", grid=(tp // tm,), out_shape=jax.ShapeDtypeStruct((tp, k), BF16),
        in_specs=[pl.BlockSpec((tm, d), lambda i: (i, 0)), pl.BlockSpec((k, d), lambda i: (0, 0))],
        out_specs=pl.BlockSpec((tm, k), lambda i: (i, 0)),
        compiler_params=_cparams(1),
    )(a, w)


def matmul_nt_rms_bwd(dp, w, h, g, dres):
    tp, d = h.shape
    n = w.shape[1]
    tm = _pick(tp, (384, 256))

    def body(dp_ref, w_ref, h_ref, g_ref, dres_ref, dh_ref, dg_ref):
        @pl.when(pl.program_id(0) == 0)
        def _():
            dg_ref[...] = jnp.zeros_like(dg_ref)

        du = lax.dot_general(dp_ref[...], w_ref[...], NT, preferred_element_type=F32)
        xhat, r = _rms(h_ref[...])
        dx, dgp = _rms_bwd(xhat, r, g_ref[...], du)
        dg_ref[...] += dgp
        dh_ref[...] = dres_ref[...] + dx

    return pl.pallas_call(
        body, name="matmul_nt_rms_bwd", grid=(tp // tm,),
        out_shape=[jax.ShapeDtypeStruct((tp, d), F32), jax.ShapeDtypeStruct((1, d), F32)],
        in_specs=[pl.BlockSpec((tm, n), lambda i: (i, 0)), pl.BlockSpec((d, n), lambda i: (0, 0)),
                  pl.BlockSpec((tm, d), lambda i: (i, 0)), pl.BlockSpec((1, d), lambda i: (0, 0)),
                  pl.BlockSpec((tm, d), lambda i: (i, 0))],
        out_specs=[pl.BlockSpec((tm, d), lambda i: (i, 0)), pl.BlockSpec((1, d), lambda i: (0, 0))],
        compiler_params=_cparams(1),
    )(dp, w, h, g, dres)


CONV_TILE = (256,)


def _halo_before(tm, hb, col=0):
    return lambda i: (jnp.maximum(i * (tm // hb) - 1, 0), col)


def _halo_after(tm, hb, nblk, col=0):
    return lambda i: (jnp.minimum((i + 1) * (tm // hb), nblk - 1), col)


def a_elem_fwd(p, w):
    tp, d3 = p.shape
    d = d3 // 3
    kw = w.shape[0]
    tm = _pick(tp, CONV_TILE)
    hb = 16

    def body(p_ref, ph_ref, w_ref, y_ref, ext):
        i = pl.program_id(0)
        halo = ph_ref[:, d:2 * d].astype(F32) * ph_ref[:, 2 * d:].astype(F32)
        ext[0:hb, :] = jnp.where(i > 0, halo, 0.0)
        ext[hb:, :] = p_ref[:, d:2 * d].astype(F32) * p_ref[:, 2 * d:].astype(F32)
        z = jnp.zeros((tm, d), F32)
        for k in range(kw):
            z = z + w_ref[k:k + 1, :] * ext[pl.ds(hb - (kw - 1) + k, tm), :]
        y_ref[...] = (p_ref[:, 0:d].astype(F32) * z).astype(BF16)

    return pl.pallas_call(
        body, name="a_elem_fwd", grid=(tp // tm,), out_shape=jax.ShapeDtypeStruct((tp, d), BF16),
        in_specs=[pl.BlockSpec((tm, d3), lambda i: (i, 0)), pl.BlockSpec((hb, d3), _halo_before(tm, hb)),
                  pl.BlockSpec((kw, d), lambda i: (0, 0))],
        out_specs=pl.BlockSpec((tm, d), lambda i: (i, 0)),
        scratch_shapes=[pltpu.VMEM((tm + hb, d), F32)],
        compiler_params=_cparams(1),
    )(p, p, w)


def a_elem_bwd(p, dy, w):
    tp, d3 = p.shape
    d = d3 // 3
    kw = w.shape[0]
    tm = _pick(tp, CONV_TILE)
    hb = 16
    n = tp // tm

    def body(p_ref, ph_ref, pa_ref, dy_ref, dya_ref, w_ref, dp_ref, dw_ref, ext, dzext):
        i = pl.program_id(0)

        @pl.when(i == 0)
        def _():
            dw_ref[...] = jnp.zeros_like(dw_ref)

        b = p_ref[:, 0:d].astype(F32)
        c = p_ref[:, d:2 * d].astype(F32)
        v = p_ref[:, 2 * d:].astype(F32)
        dy_ = dy_ref[...].astype(F32)
        halo = ph_ref[:, d:2 * d].astype(F32) * ph_ref[:, 2 * d:].astype(F32)
        ext[0:hb, :] = jnp.where(i > 0, halo, 0.0)
        ext[hb:, :] = c * v
        dz = dy_ * b
        dzext[0:tm, :] = dz
        dzext[tm:, :] = jnp.where(i < n - 1, dya_ref[...].astype(F32) * pa_ref[:, 0:d].astype(F32), 0.0)
        z = jnp.zeros((tm, d), F32)
        dcv = jnp.zeros((tm, d), F32)
        for k in range(kw):
            sh = ext[pl.ds(hb - (kw - 1) + k, tm), :]
            z = z + w_ref[k:k + 1, :] * sh
            dw_ref[k:k + 1, :] += jnp.sum(dz * sh, axis=0, keepdims=True)
            dcv = dcv + w_ref[k:k + 1, :] * dzext[pl.ds(kw - 1 - k, tm), :]
        dp_ref[:, 0:d] = (dy_ * z).astype(BF16)
        dp_ref[:, d:2 * d] = (dcv * v).astype(BF16)
        dp_ref[:, 2 * d:] = (dcv * c).astype(BF16)

    nb = tp // hb
    return pl.pallas_call(
        body, name="a_elem_bwd", grid=(n,),
        out_shape=[jax.ShapeDtypeStruct((tp, d3), BF16), jax.ShapeDtypeStruct((8, d), F32)],
        in_specs=[pl.BlockSpec((tm, d3), lambda i: (i, 0)), pl.BlockSpec((hb, d3), _halo_before(tm, hb)),
                  pl.BlockSpec((hb, d3), _halo_after(tm, hb, nb)),
                  pl.BlockSpec((tm, d), lambda i: (i, 0)), pl.BlockSpec((hb, d), _halo_after(tm, hb, nb)),
                  pl.BlockSpec((kw, d), lambda i: (0, 0))],
        out_specs=[pl.BlockSpec((tm, d3), lambda i: (i, 0)), pl.BlockSpec((8, d), lambda i: (0, 0))],
        scratch_shapes=[pltpu.VMEM((tm + hb, d), F32), pltpu.VMEM((tm + hb, d), F32)],
        compiler_params=_cparams(1),
    )(p, p, p, dy, dy, w)


CH_R, CH_C = 128, 128


def _chunks(tm, d):
    return [(r0, c0) for c0 in range(0, d, CH_C) for r0 in range(0, tm, CH_R)]


def _b_u2(p_ref, ph_ref, w_ref, bias_ref, ext, u2_sc, i, tm, d, kw, hb):
    a = p_ref[:, 0:d].astype(F32)
    sg = _sigmoid(p_ref[:, d:].astype(F32))
    halo = ph_ref[:, 0:d].astype(F32) * _sigmoid(ph_ref[:, d:].astype(F32))
    ext[0:hb, :] = jnp.where(i > 0, halo, 0.0)
    ext[hb:, :] = a * sg
    for r0, c0 in _chunks(tm, d):
        cs = slice(c0, c0 + CH_C)
        acc = jnp.zeros((CH_R, CH_C), F32) + bias_ref[:, cs]
        for k in range(kw):
            acc = acc + w_ref[k:k + 1, cs] * ext[pl.ds(r0 + hb - (kw - 1) + k, CH_R), cs]
        u2_sc[r0:r0 + CH_R, cs] = acc
    return a, sg, u2_sc[...]


def _ln(u2):
    mu = jnp.mean(u2, axis=-1, keepdims=True)
    xc = u2 - mu
    rstd = lax.rsqrt(jnp.mean(xc * xc, axis=-1, keepdims=True) + EPS)
    return xc * rstd, rstd


def b_elem_fwd(p, w, bias, ln_g, ln_b):
    tp, d2 = p.shape
    d = d2 // 2
    kw = w.shape[0]
    tm = _pick(tp, CONV_TILE)
    hb = 32

    def body(p_ref, ph_ref, w_ref, bias_ref, g_ref, b_ref, y_ref, ext, u2_sc):
        i = pl.program_id(0)
        _, _, u2 = _b_u2(p_ref, ph_ref, w_ref, bias_ref, ext, u2_sc, i, tm, d, kw, hb)
        xhat, _ = _ln(u2)
        u3 = xhat * g_ref[...] + b_ref[...]
        y_ref[...] = (u3 * _sigmoid(u3)).astype(BF16)

    vec = pl.BlockSpec((1, d), lambda i: (0, 0))
    return pl.pallas_call(
        body, name="b_elem_fwd", grid=(tp // tm,), out_shape=jax.ShapeDtypeStruct((tp, d), BF16),
        in_specs=[pl.BlockSpec((tm, d2), lambda i: (i, 0)), pl.BlockSpec((hb, d2), _halo_before(tm, hb)),
                  pl.BlockSpec((kw, d), lambda i: (0, 0)), vec, vec, vec],
        out_specs=pl.BlockSpec((tm, d), lambda i: (i, 0)),
        scratch_shapes=[pltpu.VMEM((tm + hb, d), F32), pltpu.VMEM((tm, d), F32)],
        compiler_params=_cparams(1),
    )(p, p, w, bias, ln_g, ln_b)


def b_elem_bwd1(p, dy, w, bias, ln_g, ln_b):
    tp, d2 = p.shape
    d = d2 // 2
    kw = w.shape[0]
    tm = _pick(tp, CONV_TILE)
    hb = 32

    def body(p_ref, ph_ref, dy_ref, w_ref, bias_ref, g_ref, b_ref, du2_ref, st_ref, ext, u2_sc):
        i = pl.program_id(0)

        @pl.when(i == 0)
        def _():
            st_ref[...] = jnp.zeros_like(st_ref)

        _, _, u2 = _b_u2(p_ref, ph_ref, w_ref, bias_ref, ext, u2_sc, i, tm, d, kw, hb)
        xhat, rstd = _ln(u2)
        u3 = xhat * g_ref[...] + b_ref[...]
        s3 = _sigmoid(u3)
        du3 = dy_ref[...].astype(F32) * (s3 * (1.0 + u3 * (1.0 - s3)))
        dxh = du3 * g_ref[...]
        du2 = rstd * (dxh - jnp.mean(dxh, axis=-1, keepdims=True)
                      - xhat * jnp.mean(dxh * xhat, axis=-1, keepdims=True))
        du2_ref[...] = du2
        st_ref[0:1, :] += jnp.sum(du3 * xhat, axis=0, keepdims=True)
        st_ref[1:2, :] += jnp.sum(du3, axis=0, keepdims=True)
        st_ref[2:3, :] += jnp.sum(du2, axis=0, keepdims=True)

    vec = pl.BlockSpec((1, d), lambda i: (0, 0))
    return pl.pallas_call(
        body, name="b_elem_bwd1", grid=(tp // tm,),
        out_shape=[jax.ShapeDtypeStruct((tp, d), F32), jax.ShapeDtypeStruct((8, d), F32)],
        in_specs=[pl.BlockSpec((tm, d2), lambda i: (i, 0)), pl.BlockSpec((hb, d2), _halo_before(tm, hb)),
                  pl.BlockSpec((tm, d), lambda i: (i, 0)), pl.BlockSpec((kw, d), lambda i: (0, 0)), vec, vec, vec],
        out_specs=[pl.BlockSpec((tm, d), lambda i: (i, 0)), pl.BlockSpec((8, d), lambda i: (0, 0))],
        scratch_shapes=[pltpu.VMEM((tm + hb, d), F32), pltpu.VMEM((tm, d), F32)],
        compiler_params=_cparams(1),
    )(p, p, dy, w, bias, ln_g, ln_b)


def b_elem_bwd2(p, du2, w):
    tp, d2 = p.shape
    d = d2 // 2
    kw = w.shape[0]
    tm = _pick(tp, CONV_TILE)
    hb = 32
    n = tp // tm

    def body(p_ref, ph_ref, du2_ref, du2a_ref, w_ref, dp_ref, dw_ref, ext, dext, dwacc):
        i = pl.program_id(0)

        @pl.when(i == 0)
        def _():
            dwacc[...] = jnp.zeros_like(dwacc)

        halo = ph_ref[:, 0:d].astype(F32) * _sigmoid(ph_ref[:, d:].astype(F32))
        ext[0:hb, :] = jnp.where(i > 0, halo, 0.0)
        ext[hb:, :] = p_ref[:, 0:d].astype(F32) * _sigmoid(p_ref[:, d:].astype(F32))
        dext[0:tm, :] = du2_ref[...]
        dext[tm:, :] = jnp.where(i < n - 1, du2a_ref[...], 0.0)
        for r0, c0 in _chunks(tm, d):
            cs = slice(c0, c0 + CH_C)
            du2c = du2_ref[r0:r0 + CH_R, cs]
            du1 = jnp.zeros((CH_R, CH_C), F32)
            for k in range(kw):
                prod = du2c * ext[pl.ds(r0 + hb - (kw - 1) + k, CH_R), cs]
                part = prod[0:8]
                for r in range(8, CH_R, 8):
                    part = part + prod[r:r + 8]
                dwacc[8 * k:8 * k + 8, cs] += part
                du1 = du1 + w_ref[k:k + 1, cs] * dext[pl.ds(r0 + kw - 1 - k, CH_R), cs]
            a = p_ref[r0:r0 + CH_R, cs].astype(F32)
            sg = _sigmoid(p_ref[r0:r0 + CH_R, d + c0:d + c0 + CH_C].astype(F32))
            dp_ref[r0:r0 + CH_R, cs] = (du1 * sg).astype(BF16)
            dp_ref[r0:r0 + CH_R, d + c0:d + c0 + CH_C] = (du1 * a * sg * (1.0 - sg)).astype(BF16)

        @pl.when(i == n - 1)
        def _():
            dw_ref[...] = jnp.zeros_like(dw_ref)
            for k in range(kw):
                dw_ref[k:k + 1, :] = jnp.sum(dwacc[8 * k:8 * k + 8, :], axis=0, keepdims=True)

    nb = tp // hb
    return pl.pallas_call(
        body, name="b_elem_bwd2", grid=(n,),
        out_shape=[jax.ShapeDtypeStruct((tp, d2), BF16), jax.ShapeDtypeStruct((32, d), F32)],
        in_specs=[pl.BlockSpec((tm, d2), lambda i: (i, 0)), pl.BlockSpec((hb, d2), _halo_before(tm, hb)),
                  pl.BlockSpec((tm, d), lambda i: (i, 0)), pl.BlockSpec((hb, d), _halo_after(tm, hb, nb)),
                  pl.BlockSpec((kw, d), lambda i: (0, 0))],
        out_specs=[pl.BlockSpec((tm, d2), lambda i: (i, 0)), pl.BlockSpec((32, d), lambda i: (0, 0))],
        scratch_shapes=[pltpu.VMEM((tm + hb, d), F32), pltpu.VMEM((tm + hb, d), F32), pltpu.VMEM((8 * kw, d), F32)],
        compiler_params=_cparams(1),
    )(p, p, du2, du2, w)


ATT_TILE = 256


def _head_masks(d):
    c = lax.broadcasted_iota(jnp.int32, (d, LANES), 0)
    h = lax.broadcasted_iota(jnp.int32, (d, LANES), 1)
    seg = (c // HEAD_DIM == h).astype(F32)
    fold = (c % HEAD_DIM == h).astype(F32)
    return seg, seg.T, fold


L2E = 1.4426950408889634
LN2 = 0.6931471805599453
AUG = HEAD_DIM


def _split3(r):
    r1 = r.astype(BF16).astype(F32)
    r2 = (r - r1).astype(BF16).astype(F32)
    r3 = (r - r1 - r2).astype(BF16).astype(F32)
    return r1, r2, r3


def c_prep_slots(p, b_f, qg, kg):
    tp = p.shape[0]
    d = qg.shape[1]
    n_heads = d // HEAD_DIM
    tm = ATT_TILE
    seg, seg_t, _ = _head_masks(d)

    def body(p_ref, pf_ref, bf_ref, qg_ref, kg_ref, seg_ref, segt_ref, q_ref, k_ref, v_ref, cum_ref, carry):
        @pl.when(pl.program_id(0) == 0)
        def _():
            carry[...] = jnp.zeros_like(carry)

        def norm(x, g):
            ms = jnp.dot(x * x, seg_ref[...], precision=HI, preferred_element_type=F32) * (1.0 / HEAD_DIM)
            r = jnp.dot(lax.rsqrt(ms + EPS), segt_ref[...], precision=HI, preferred_element_type=F32)
            return x * r * g

        qn = norm(p_ref[:, 0:d], qg_ref[...]) * (HEAD_DIM ** -0.5 * L2E)
        kn = norm(p_ref[:, d:2 * d], kg_ref[...])
        v_ref[...] = p_ref[:, 2 * d:].astype(BF16)
        xf = pf_ref[...] + bf_ref[...]
        logf = jnp.minimum(xf, 0.0) - jnp.log(1.0 + jnp.exp(-jnp.abs(xf)))
        r_ = lax.broadcasted_iota(jnp.int32, (tm, tm), 0)
        c_ = lax.broadcasted_iota(jnp.int32, (tm, tm), 1)
        cum = jnp.dot((c_ <= r_).astype(F32), logf, precision=HI, preferred_element_type=F32) + carry[0:1, :]
        cum_ref[...] = cum
        carry[0:1, :] += jnp.sum(logf, axis=0, keepdims=True)

        lane = lax.broadcasted_iota(jnp.int32, (1, LANES), 1)
        ones_q = jnp.where((lane >= AUG + 3) & (lane < AUG + 6), 1.0, 0.0)
        ones_k = jnp.where((lane >= AUG) & (lane < AUG + 3), 1.0, 0.0)
        for h in range(n_heads):
            c0 = LANES * (h // 2)
            src_q, src_k = qn[:, c0:c0 + LANES], kn[:, c0:c0 + LANES]
            if h % 2:
                src_q = pltpu.roll(src_q, HEAD_DIM, axis=1)
                src_k = pltpu.roll(src_k, HEAD_DIM, axis=1)
            c = cum[:, h:h + 1] * L2E
            a1, a2, a3 = _split3(c)
            b1, b2, b3 = _split3(-c)
            aug_q = jnp.where(lane == AUG, a1, jnp.where(lane == AUG + 1, a2, jnp.where(lane == AUG + 2, a3, ones_q)))
            aug_k = jnp.where(lane == AUG + 3, b1,
                              jnp.where(lane == AUG + 4, b2, jnp.where(lane == AUG + 5, b3, ones_k)))
            q_ref[:, LANES * h:LANES * (h + 1)] = jnp.where(lane < HEAD_DIM, src_q, aug_q).astype(BF16)
            k_ref[:, LANES * h:LANES * (h + 1)] = jnp.where(lane < HEAD_DIM, src_k, aug_k).astype(BF16)

    vec = pl.BlockSpec((1, d), lambda i: (0, 0))
    rowd = pl.BlockSpec((tm, d), lambda i: (i, 0))
    slots = pl.BlockSpec((tm, n_heads * LANES), lambda i: (i, 0))
    return pl.pallas_call(
        body, name="c_prep_slots", grid=(tp // tm,),
        out_shape=[jax.ShapeDtypeStruct((tp, n_heads * LANES), BF16)] * 2
        + [jax.ShapeDtypeStruct((tp, d), BF16), jax.ShapeDtypeStruct((tp, LANES), F32)],
        in_specs=[pl.BlockSpec((tm, 3 * d), lambda i: (i, 0)), pl.BlockSpec((tm, LANES), lambda i: (i, 3 * d // LANES)),
                  pl.BlockSpec((1, LANES), lambda i: (0, 0)), vec, vec,
                  pl.BlockSpec((d, LANES), lambda i: (0, 0)), pl.BlockSpec((LANES, d), lambda i: (0, 0))],
        out_specs=[slots, slots, rowd, pl.BlockSpec((tm, LANES), lambda i: (i, 0))],
        scratch_shapes=[pltpu.VMEM((8, LANES), F32)],
        compiler_params=_cparams(1),
    )(p, p, b_f, qg, kg, seg, seg_t)


ATT_GROUPS = (8, 2)


def _grouped_loop(n_blocks, body, carry):
    start = 0
    for g in ATT_GROUPS:
        count = (n_blocks - start) // g
        carry = lax.fori_loop(0, count, lambda i, c, g=g, start=start: body(start + i * g, g, c), carry)
        start = start + count * g
    return lax.fori_loop(start, n_blocks, lambda kj, c: body(kj, 1, c), carry)


def _diag_mask(t):
    return lax.broadcasted_iota(jnp.int32, (t, t), 0) <= lax.broadcasted_iota(jnp.int32, (t, t), 1)


def attn_fwd_t(qpt, kp, vt4):
    tp = kp.shape[0]
    t = ATT_TILE
    nb = tp // t
    npair = vt4.shape[0]

    def body(q_ref, k_ref, v_ref, o_ref, lse_ref):
        qi = pl.program_id(1)
        qts = (q_ref[0:LANES, :], q_ref[LANES:2 * LANES, :])
        mask = _diag_mask(t)

        def blocks(kj0, n, carry, masked=False):
            out = []
            for hd in range(2):
                m, l, acc = carry[3 * hd:3 * hd + 3]
                sts = []
                for g in range(n):
                    off = pl.multiple_of((kj0 + g) * t, t)
                    st = jnp.dot(k_ref[pl.ds(off, t), LANES * hd:LANES * (hd + 1)], qts[hd],
                                 preferred_element_type=F32)
                    sts.append(jnp.where(mask, st, NEG) if masked else st)
                m_new = m
                for st in sts:
                    m_new = jnp.maximum(m_new, jnp.max(st, axis=0, keepdims=True))
                alpha = jnp.exp2(m - m_new)
                l = alpha * l
                acc = alpha * acc
                for g, st in enumerate(sts):
                    pt = jnp.exp2(st - m_new)
                    l = l + jnp.sum(pt, axis=0, keepdims=True)
                    acc = acc + jnp.dot(v_ref[0, kj0 + g], pt.astype(BF16), preferred_element_type=F32)
                out += [m_new, l, acc]
            return tuple(out)

        init = (jnp.full((1, t), NEG, F32), jnp.zeros((1, t), F32), jnp.zeros((LANES, t), F32)) * 2
        carry = _grouped_loop(qi, blocks, init)
        ma, la, acca, mb, lb, accb = blocks(qi, 1, carry, masked=True)
        row = lax.broadcasted_iota(jnp.int32, (LANES, 1), 0)
        o_ref[...] = jnp.where(row < HEAD_DIM, acca / la, accb / lb).astype(BF16)
        lse_ref[0, 0:1, :] = ma + jnp.log(la) * L2E
        lse_ref[0, 1:2, :] = mb + jnp.log(lb) * L2E

    return pl.pallas_call(
        body, name="attn_fwd_t", grid=(npair, nb),
        out_shape=[jax.ShapeDtypeStruct((npair * LANES, tp), BF16), jax.ShapeDtypeStruct((npair, 2, tp), F32)],
        in_specs=[pl.BlockSpec((2 * LANES, t), lambda h, i: (h, i)), pl.BlockSpec((tp, 2 * LANES), lambda h, i: (0, h)),
                  pl.BlockSpec((1, nb, LANES, t), lambda h, i: (h, 0, 0, 0))],
        out_specs=[pl.BlockSpec((LANES, t), lambda h, i: (h, i)), pl.BlockSpec((1, 2, t), lambda h, i: (h, 0, i))],
        compiler_params=_cparams(2),
    )(qpt, kp, vt4)


def attn_bwd_t(qpt, qp, kp, kpt4, v, do, dot_, lse, cqt):
    tp = kp.shape[0]
    t = ATT_TILE
    nb = tp // t
    n_heads = kpt4.shape[0]
    d = v.shape[1]

    def body(qt_ref, q_ref, k_ref, kt_ref, v_ref, do_ref, dot_ref, lse_ref, cq_ref, dq_ref, dk_ref, dv_ref,
             p_sc, dp_sc, dc_sc):
        hd = pl.program_id(0) % 2
        qi = pl.program_id(1)

        @pl.when(qi == 0)
        def _():
            dk_ref[...] = jnp.zeros_like(dk_ref)
            dc_sc[...] = jnp.zeros_like(dc_sc)

        @pl.when((qi == 0) & (hd == 0))
        def _():
            dv_ref[...] = jnp.zeros_like(dv_ref)

        row = lax.broadcasted_iota(jnp.int32, (LANES, 1), 0)
        lane = lax.broadcasted_iota(jnp.int32, (1, LANES), 1)
        dot_h = jnp.where(row // HEAD_DIM == hd, dot_ref[...], jnp.zeros_like(dot_ref))
        do_h = jnp.where(lane // HEAD_DIM == hd, do_ref[...], jnp.zeros_like(do_ref))
        rr = cq_ref[0] - lse_ref[0]
        r1, r2, r3 = _split3(jnp.where(hd == 0, rr[0:1, :], rr[1:2, :]))
        qt = qt_ref[...].astype(F32)
        qt = jnp.where(row == AUG, r1, jnp.where(row == AUG + 1, r2, jnp.where(row == AUG + 2, r3, qt))).astype(BF16)
        mask = _diag_mask(t)

        def pass1(kj0, n, delta, masked=False):
            for g in range(n):
                off = pl.multiple_of((kj0 + g) * t, t)
                st = jnp.dot(k_ref[pl.ds(off, t), :], qt, preferred_element_type=F32)
                if masked:
                    st = jnp.where(mask, st, NEG)
                pt = jnp.exp2(st)
                dpt = jnp.dot(v_ref[pl.ds(off, t), :], dot_h, preferred_element_type=F32)
                p_sc[kj0 + g] = pt
                dp_sc[kj0 + g] = dpt
                delta = delta + jnp.sum(pt * dpt, axis=0, keepdims=True)
            return delta

        delta = _grouped_loop(qi, pass1, jnp.zeros((1, t), F32))
        delta = pass1(qi, 1, delta, masked=True)

        def pass2(kj0, n, dq):
            for g in range(n):
                off = pl.multiple_of((kj0 + g) * t, t)
                pt = p_sc[kj0 + g]
                ds32 = pt * (dp_sc[kj0 + g] - delta)
                ds = ds32.astype(BF16)
                dc_sc[pl.ds(off, t), :] += ds32[:, 0:LANES] + ds32[:, LANES:2 * LANES]
                dk_ref[pl.ds(off, t), :] += jnp.dot(ds, q_ref[...], preferred_element_type=F32)
                dv_ref[pl.ds(off, t), :] += jnp.dot(pt.astype(BF16), do_h, preferred_element_type=F32)
                dq = dq + jnp.dot(kt_ref[0, kj0 + g], ds, preferred_element_type=F32)
            return dq

        dq_ref[...] = _grouped_loop(qi + 1, pass2, jnp.zeros((LANES, t), F32))

        @pl.when(qi == nb - 1)
        def _():
            dk_ref[:, AUG + 3:AUG + 4] = jnp.sum(dc_sc[...], axis=1, keepdims=True)

    once = dict(pipeline_mode=pl.Buffered(1))
    pair_rows = pl.BlockSpec((1, 2, t), lambda h, i: (h // 2, 0, i))
    return pl.pallas_call(
        body, name="attn_bwd_t", grid=(n_heads, nb),
        out_shape=[jax.ShapeDtypeStruct((n_heads * LANES, tp), F32), jax.ShapeDtypeStruct((tp, n_heads * LANES), F32),
                   jax.ShapeDtypeStruct((tp, d), F32)],
        in_specs=[pl.BlockSpec((LANES, t), lambda h, i: (h, i)), pl.BlockSpec((t, LANES), lambda h, i: (i, h)),
                  pl.BlockSpec((tp, LANES), lambda h, i: (0, h), **once),
                  pl.BlockSpec((1, nb, LANES, t), lambda h, i: (h, 0, 0, 0), **once),
                  pl.BlockSpec((tp, LANES), lambda h, i: (0, h // 2), **once),
                  pl.BlockSpec((t, LANES), lambda h, i: (i, h // 2)), pl.BlockSpec((LANES, t), lambda h, i: (h // 2, i)),
                  pair_rows, pair_rows],
        out_specs=[pl.BlockSpec((LANES, t), lambda h, i: (h, i)), pl.BlockSpec((tp, LANES), lambda h, i: (0, h)),
                   pl.BlockSpec((tp, LANES), lambda h, i: (0, h // 2))],
        scratch_shapes=[pltpu.VMEM((nb, t, t), F32), pltpu.VMEM((nb, t, t), F32), pltpu.VMEM((tp, LANES), F32)],
        compiler_params=_cparams(2),
    )(qpt, qp, kp, kpt4, v, do, dot_, lse, cqt)


def rev_cumsum_rows(x):
    r, tp = x.shape
    t = ATT_TILE
    nb = tp // t

    def body(x_ref, o_ref, carry):
        @pl.when(pl.program_id(0) == 0)
        def _():
            carry[...] = jnp.zeros_like(carry)

        xv = x_ref[...]
        r_ = lax.broadcasted_iota(jnp.int32, (t, t), 0)
        c_ = lax.broadcasted_iota(jnp.int32, (t, t), 1)
        o_ref[...] = jnp.dot(xv, (r_ >= c_).astype(F32), precision=HI, preferred_element_type=F32) + carry[:, 0:1]
        carry[...] += jnp.sum(xv, axis=1, keepdims=True)

    return pl.pallas_call(
        body, name="rev_cumsum_rows", grid=(nb,), out_shape=jax.ShapeDtypeStruct((r, tp), F32),
        in_specs=[pl.BlockSpec((r, t), lambda i: (0, nb - 1 - i))],
        out_specs=pl.BlockSpec((r, t), lambda i: (0, nb - 1 - i)),
        scratch_shapes=[pltpu.VMEM((r, LANES), F32)],
        compiler_params=_cparams(1),
    )(x)


def c_elem_bwd(p, b_f, qg, kg, dq, dk, dv, dlogf):
    tp, n_out = p.shape
    d = qg.shape[1]
    tm = ATT_TILE
    n = tp // tm
    seg, seg_t, fold = _head_masks(d)

    def body(p_ref, pf_ref, bf_ref, qg_ref, kg_ref, dq_ref, dk_ref, dv_ref, dlf_ref, seg_ref, segt_ref, fold_ref,
             dp_ref, sm_ref, accq, acck, accf):
        i = pl.program_id(0)

        @pl.when(i == 0)
        def _():
            accq[...] = jnp.zeros_like(accq)
            acck[...] = jnp.zeros_like(acck)
            accf[...] = jnp.zeros_like(accf)

        def norm_bwd(x, g, dy, acc):
            ms = jnp.dot(x * x, seg_ref[...], precision=HI, preferred_element_type=F32) * (1.0 / HEAD_DIM)
            r = jnp.dot(lax.rsqrt(ms + EPS), segt_ref[...], precision=HI, preferred_element_type=F32)
            xhat = x * r
            acc[0:1, :] += jnp.sum(dy * xhat, axis=0, keepdims=True)
            gy = dy * g
            mean = jnp.dot(jnp.dot(gy * xhat, seg_ref[...], precision=HI, preferred_element_type=F32),
                           segt_ref[...], precision=HI, preferred_element_type=F32) * (1.0 / HEAD_DIM)
            return r * (gy - xhat * mean)

        dp_ref[:, 0:d] = norm_bwd(p_ref[:, 0:d], qg_ref[...], dq_ref[...] * (HEAD_DIM ** -0.5), accq).astype(BF16)
        dp_ref[:, d:2 * d] = norm_bwd(p_ref[:, d:2 * d], kg_ref[...], dk_ref[...] * LN2, acck).astype(BF16)
        dp_ref[:, 2 * d:3 * d] = dv_ref[...].astype(BF16)
        df = dlf_ref[...] * _sigmoid(-(pf_ref[...] + bf_ref[...]))
        accf[0:1, :] += jnp.sum(df, axis=0, keepdims=True)
        dp_ref[:, 3 * d:] = df.astype(BF16)

        @pl.when(i == n - 1)
        def _():
            sm_ref[...] = jnp.zeros_like(sm_ref)
            sm_ref[0:1, :] = jnp.dot(accq[0:1, :], fold_ref[...], precision=HI, preferred_element_type=F32)
            sm_ref[1:2, :] = jnp.dot(acck[0:1, :], fold_ref[...], precision=HI, preferred_element_type=F32)
            sm_ref[2:3, :] = accf[0:1, :]

    vec = pl.BlockSpec((1, d), lambda i: (0, 0))
    rowd = pl.BlockSpec((tm, d), lambda i: (i, 0))
    rowl = pl.BlockSpec((tm, LANES), lambda i: (i, 0))
    return pl.pallas_call(
        body, name="c_elem_bwd", grid=(n,),
        out_shape=[jax.ShapeDtypeStruct((tp, n_out), BF16), jax.ShapeDtypeStruct((8, LANES), F32)],
        in_specs=[pl.BlockSpec((tm, 2 * d), lambda i: (i, 0)), pl.BlockSpec((tm, LANES), lambda i: (i, 3 * d // LANES)),
                  pl.BlockSpec((1, LANES), lambda i: (0, 0)), vec, vec, rowd, rowd, rowd, rowl,
                  pl.BlockSpec((d, LANES), lambda i: (0, 0)), pl.BlockSpec((LANES, d), lambda i: (0, 0)),
                  pl.BlockSpec((d, LANES), lambda i: (0, 0))],
        out_specs=[pl.BlockSpec((tm, n_out), lambda i: (i, 0)), pl.BlockSpec((8, LANES), lambda i: (0, 0))],
        scratch_shapes=[pltpu.VMEM((8, d), F32), pltpu.VMEM((8, d), F32), pltpu.VMEM((8, LANES), F32)],
        compiler_params=_cparams(1),
    )(p, p, b_f, qg, kg, dq, dk, dv, dlogf, seg, seg_t, fold)


def loss_head(h, tgt, t_real):
    tp, d = h.shape
    tm = _pick(tp, (768, 512, 256))

    def body(h_ref, t_ref, dh_ref, l_ref):
        i = pl.program_id(0)

        @pl.when(i == 0)
        def _():
            l_ref[...] = jnp.zeros_like(l_ref)

        row = i * tm + lax.broadcasted_iota(jnp.int32, (tm, 1), 0)
        valid = (row >= N_META) & (row < t_real)
        e = jnp.where(valid, h_ref[...] - t_ref[...], 0.0)
        dh_ref[...] = e * (1.0 / d)
        per_row = jnp.sum(e * e, axis=-1, keepdims=True) * (1.0 / d)
        l_ref[...] += 0.5 * jnp.sum(per_row, axis=0, keepdims=True)

    return pl.pallas_call(
        body, name="loss_head", grid=(tp // tm,),
        out_shape=[jax.ShapeDtypeStruct((tp, d), F32), jax.ShapeDtypeStruct((8, LANES), F32)],
        in_specs=[pl.BlockSpec((tm, d), lambda i: (i, 0)), pl.BlockSpec((tm, d), lambda i: (i, 0))],
        out_specs=[pl.BlockSpec((tm, d), lambda i: (i, 0)), pl.BlockSpec((8, LANES), lambda i: (0, 0))],
        compiler_params=_cparams(1),
    )(h, tgt)


def sum_devices(x):
    _, r, c = x.shape

    def body(x_ref, o_ref):
        acc = x_ref[0]
        for dev in range(1, N_DEV):
            acc = acc + x_ref[dev]
        o_ref[...] = acc

    return pl.pallas_call(
        body, name="sum_devices", out_shape=jax.ShapeDtypeStruct((r, c), F32),
        in_specs=[pl.BlockSpec(memory_space=pltpu.VMEM)], out_specs=pl.BlockSpec(memory_space=pltpu.VMEM),
    )(x)


def _adamw_math(w, g, m, v):
    m = ADAM_B1 * m + (1.0 - ADAM_B1) * g
    v = ADAM_B2 * v + (1.0 - ADAM_B2) * (g * g)
    m_hat = m / (1.0 - ADAM_B1 ** ADAM_STEP)
    v_hat = v / (1.0 - ADAM_B2 ** ADAM_STEP)
    delta = -ADAM_LR * (m_hat / (jnp.sqrt(v_hat) + ADAM_EPS) + ADAM_WD * w)
    return delta, m, v


def adamw_small(w, g, m, v):
    def body(w_ref, g_ref, m_ref, v_ref, d_ref, nm_ref, nv_ref):
        d_ref[...], nm_ref[...], nv_ref[...] = _adamw_math(w_ref[...], g_ref[...], m_ref[...], v_ref[...])

    vm = pl.BlockSpec(memory_space=pltpu.VMEM)
    return pl.pallas_call(
        body, name="adamw_small", out_shape=[jax.ShapeDtypeStruct(w.shape, F32)] * 3,
        in_specs=[vm] * 4, out_specs=[vm] * 3,
    )(w, g, m, v)


def adamw_reduce(w, m, v, parts):
    r, c = w.shape
    n_parts = parts.shape[0]
    tr = _pick(r, (512, 384, 352, 256, 128, 8))

    def body(w_ref, m_ref, v_ref, p_ref, g_ref, d_ref, nm_ref, nv_ref):
        g = p_ref[0].astype(F32)
        for j in range(1, n_parts):
            g = g + p_ref[j].astype(F32)
        g_ref[...] = g
        d_ref[...], nm_ref[...], nv_ref[...] = _adamw_math(w_ref[...], g, m_ref[...], v_ref[...])

    blk = pl.BlockSpec((tr, c), lambda i: (i, 0))
    return pl.pallas_call(
        body, name="adamw_reduce", grid=(r // tr,), out_shape=[jax.ShapeDtypeStruct((r, c), F32)] * 4,
        in_specs=[blk, blk, blk, pl.BlockSpec((n_parts, tr, c), lambda i: (0, i, 0))],
        out_specs=[blk] * 4, compiler_params=_cparams(1),
    )(w, m, v, parts)


def _unshard(g, axis):
    g = jnp.moveaxis(g, 0, axis)
    return g.reshape(g.shape[:axis] + (g.shape[axis] * g.shape[axis + 1],) + g.shape[axis + 2:])


def _shard(full, axis):
    s = full.shape
    g = full.reshape(s[:axis] + (N_DEV, s[axis] // N_DEV) + s[axis + 1:])
    return jnp.moveaxis(g, axis, 0)


def _pad_lanes(a, n=LANES):
    flat = a.reshape(-1)
    pad = (-flat.shape[0]) % n
    return jnp.pad(flat, (0, pad)).reshape(-1, n)


BIG = ("ffn_w_gate", "ffn_w_up", "ffn_w_down", "a_w_in", "a_w_out", "b_w_in", "b_w_out", "c_w_in", "c_w_out")
SHARD_AXIS = {"ffn_w_gate": 3, "ffn_w_up": 3, "ffn_w_down": 2, "a_w_in": 2, "a_w_out": 1, "b_w_in": 2,
              "b_w_out": 1, "c_w_in": 2, "c_w_out": 1, "meta": 1, "ffn_norm": 2, "a_conv": 2, "b_conv": 2}
SMALL_SHARDED = ("meta", "ffn_norm", "a_conv", "b_conv")
SMALL_REPL = ("mix_norm", "b_conv_bias", "b_ln_g", "b_ln_b", "c_b_f", "c_q_norm", "c_k_norm")
WEIGHTS = ("meta", "ffn_norm", "ffn_w_gate", "ffn_w_up", "ffn_w_down", "mix_norm", "a_w_in", "a_conv", "a_w_out",
           "b_w_in", "b_conv", "b_conv_bias", "b_ln_g", "b_ln_b", "b_w_out", "c_w_in", "c_b_f", "c_q_norm",
           "c_k_norm", "c_w_out")
N_MIXERS = 3


def kernel(x, meta, ffn_norm, ffn_w_gate, ffn_w_up, ffn_w_down, mix_norm, a_w_in, a_conv, a_w_out, b_w_in, b_conv, b_conv_bias, b_ln_g, b_ln_b, b_w_out, c_w_in, c_b_f, c_q_norm, c_k_norm, c_w_out, loss_target, m_meta, m_ffn_norm, m_ffn_w_gate, m_ffn_w_up, m_ffn_w_down, m_mix_norm, m_a_w_in, m_a_conv, m_a_w_out, m_b_w_in, m_b_conv, m_b_conv_bias, m_b_ln_g, m_b_ln_b, m_b_w_out, m_c_w_in, m_c_b_f, m_c_q_norm, m_c_k_norm, m_c_w_out, v_meta, v_ffn_norm, v_ffn_w_gate, v_ffn_w_up, v_ffn_w_down, v_mix_norm, v_a_w_in, v_a_conv, v_a_w_out, v_b_w_in, v_b_conv, v_b_conv_bias, v_b_ln_g, v_b_ln_b, v_b_w_out, v_c_w_in, v_c_b_f, v_c_q_norm, v_c_k_norm, v_c_w_out):
    local = dict(locals())
    w = {n: local[n] for n in WEIGHTS}
    mom = {n: local["m_" + n] for n in WEIGHTS}
    var = {n: local["v_" + n] for n in WEIGHTS}
    d = x.shape[-1]
    depth = ffn_norm.shape[0]
    n_heads = d // HEAD_DIM
    seq = x.shape[1]
    t_real = N_META + seq
    tp = -(-t_real // ROW_ALIGN) * ROW_ALIGN
    nb = tp // ATT_TILE
    npair = d // LANES

    names = BIG + SMALL_SHARDED
    gathered = all_gather([w[n].astype(BF16) for n in BIG] + [w[n] for n in SMALL_SHARDED])
    full = {n: _unshard(g, SHARD_AXIS[n]) for n, g in zip(names, gathered)}
    n_c = full["c_w_in"].shape[-1]
    n_cp = 3 * d + LANES
    c_w_qkv = jnp.pad(full["c_w_in"], ((0, 0), (0, 0), (0, n_cp - n_c)))
    b_f_pad = jnp.pad(c_b_f, ((0, 0), (0, LANES - n_heads)))
    qg_t = jnp.tile(c_q_norm, (1, n_heads))
    kg_t = jnp.tile(c_k_norm, (1, n_heads))

    h = jnp.concatenate([full["meta"], x[0], jnp.zeros((tp - t_real, d), F32)], axis=0)
    saved = []
    for i in range(depth):
        mixer, j = i % N_MIXERS, i // N_MIXERS
        s = {"h0": h}
        h, s["gate_a"], s["up_a"] = ffn_fwd(h, full["ffn_norm"][i, 0:1], full["ffn_w_gate"][i, 0],
                                            full["ffn_w_up"][i, 0], full["ffn_w_down"][i, 0])
        s["h1"] = h
        g_mix = mix_norm[i:i + 1]
        if mixer == 0:
            s["p"], s["u"] = norm_matmul(h, g_mix, full["a_w_in"][j], BF16, True)
            s["y"] = a_elem_fwd(s["p"], full["a_conv"][j])
            h = res_matmul(h, s["y"], full["a_w_out"][j])
        elif mixer == 1:
            s["p"], s["u"] = norm_matmul(h, g_mix, full["b_w_in"][j], BF16, True)
            s["y"] = b_elem_fwd(s["p"], full["b_conv"][j], b_conv_bias[j:j + 1], b_ln_g[j:j + 1], b_ln_b[j:j + 1])
            h = res_matmul(h, s["y"], full["b_w_out"][j])
        else:
            s["p"], s["u"] = norm_matmul(h, g_mix, c_w_qkv[j], F32, True)
            s["qp"], s["kp"], s["v"], cum = c_prep_slots(s["p"], b_f_pad[j:j + 1], qg_t[j:j + 1], kg_t[j:j + 1])
            s["cqt"] = (cum[:, :n_heads].T * L2E).reshape(npair, 2, tp)
            s["qpt"] = s["qp"].T
            s["kpt4"] = s["kp"].T.reshape(n_heads, LANES, nb, ATT_TILE).transpose(0, 2, 1, 3)
            vt4 = s["v"].T.reshape(npair, LANES, nb, ATT_TILE).transpose(0, 2, 1, 3)
            ot, s["lse"] = attn_fwd_t(s["qpt"], s["kp"], vt4)
            s["y"] = ot.T
            h = res_matmul(h, s["y"], full["c_w_out"][j])
        s["h2"] = h
        h, s["gate_b"], s["up_b"] = ffn_fwd(h, full["ffn_norm"][i, 1:2], full["ffn_w_gate"][i, 1],
                                            full["ffn_w_up"][i, 1], full["ffn_w_down"][i, 1])
        saved.append(s)

    tgt = jnp.concatenate([jnp.zeros((N_META, d), F32), loss_target[0], jnp.zeros((tp - t_real, d), F32)], axis=0)
    dh, loss_part = loss_head(h, tgt, t_real)

    gfull = {n: [None] * full[n].shape[0] for n in ("a_w_in", "a_w_out", "b_w_in", "b_w_out", "c_w_in", "c_w_out")}
    gffn = {n: [[None, None] for _ in range(depth)] for n in ("ffn_w_gate", "ffn_w_up", "ffn_w_down")}
    g_ffn_norm = [[None, None] for _ in range(depth)]
    g_mix_norm = [None] * depth
    g_a_conv = [None] * a_conv.shape[0]
    g_b_conv = [None] * b_conv.shape[0]
    g_b_stats = [None] * b_conv.shape[0]
    g_c_small = [None] * c_b_f.shape[0]

    def ffn_backward(dh_out, i, half, h_in, gate, up):
        dh_in, hn, dgate, dup, act, dg = ffn_bwd(dh_out, h_in, full["ffn_norm"][i, half:half + 1], gate, up,
                                                 full["ffn_w_gate"][i, half], full["ffn_w_up"][i, half],
                                                 full["ffn_w_down"][i, half])
        gffn["ffn_w_gate"][i][half] = atb(hn, dgate)
        gffn["ffn_w_up"][i][half] = atb(hn, dup)
        gffn["ffn_w_down"][i][half] = atb(act, dh_out, 0.5)
        g_ffn_norm[i][half] = dg
        return dh_in

    for i in reversed(range(depth)):
        mixer, j = i % N_MIXERS, i // N_MIXERS
        s = saved[i]
        g_mix = mix_norm[i:i + 1]
        dh = ffn_backward(dh, i, 1, s["h2"], s["gate_b"], s["up_b"])
        if mixer == 0:
            dy = matmul_nt(dh, full["a_w_out"][j])
            dp, dwc = a_elem_bwd(s["p"], dy, full["a_conv"][j])
            g_a_conv[j] = dwc[:a_conv.shape[1]]
            gfull["a_w_out"][j] = atb(s["y"], dh)
            gfull["a_w_in"][j] = atb(s["u"], dp)
            dh, g_mix_norm[i] = matmul_nt_rms_bwd(dp, full["a_w_in"][j], s["h1"], g_mix, dh)
        elif mixer == 1:
            dy = matmul_nt(dh, full["b_w_out"][j])
            du2, g_b_stats[j] = b_elem_bwd1(s["p"], dy, full["b_conv"][j], b_conv_bias[j:j + 1],
                                            b_ln_g[j:j + 1], b_ln_b[j:j + 1])
            dp, dwc = b_elem_bwd2(s["p"], du2, full["b_conv"][j])
            g_b_conv[j] = dwc[:b_conv.shape[1]]
            gfull["b_w_out"][j] = atb(s["y"], dh)
            gfull["b_w_in"][j] = atb(s["u"], dp)
            dh, g_mix_norm[i] = matmul_nt_rms_bwd(dp, full["b_w_in"][j], s["h1"], g_mix, dh)
        else:
            do = matmul_nt(dh, full["c_w_out"][j])
            dqt, dkp, dv = attn_bwd_t(s["qpt"], s["qp"], s["kp"], s["kpt4"], s["v"], do, do.T, s["lse"], s["cqt"])
            dq = dqt.reshape(n_heads, LANES, tp)[:, :HEAD_DIM].reshape(d, tp).T
            dkp = dkp.reshape(tp, n_heads, LANES)
            dk = dkp[:, :, :HEAD_DIM].reshape(tp, d)
            dlogf = rev_cumsum_rows(-dkp[:, :, AUG + 3].T)
            dlogf = jnp.pad(dlogf.T, ((0, 0), (0, LANES - n_heads)))
            dp, g_c_small[j] = c_elem_bwd(s["p"], b_f_pad[j:j + 1], qg_t[j:j + 1], kg_t[j:j + 1], dq, dk, dv, dlogf)
            gfull["c_w_out"][j] = atb(s["y"], dh)
            gfull["c_w_in"][j] = atb(s["u"], dp)[:, :n_c]
            dh, g_mix_norm[i] = matmul_nt_rms_bwd(dp, c_w_qkv[j], s["h1"], g_mix, dh)
        dh = ffn_backward(dh, i, 0, s["h0"], s["gate_a"], s["up_a"])

    grad_x = dh[N_META:t_real][None]

    gbig = {n: jnp.stack([jnp.stack(r) for r in gffn[n]]) for n in gffn}
    gbig.update({n: jnp.stack(gfull[n]) for n in gfull})
    def by_core(n):
        g = _shard(gbig[n], SHARD_AXIS[n])
        return jnp.moveaxis(g.reshape((2, 2, 2) + g.shape[1:]), 2, 0).reshape((2, N_CHIP) + g.shape[1:])

    swapped = swap_with_sibling([by_core(n) for n in BIG])
    chip_sums = []
    for mine, th in zip(swapped[:len(BIG)], swapped[len(BIG):]):
        chip_sums.append(add_bf16(mine.reshape(-1, th.shape[-1]), th.reshape(-1, th.shape[-1])).reshape(th.shape))
    parts = exchange_chips(chip_sums)
    out_g, out_d, out_m, out_v = {}, {}, {}, {}
    for n, prt in zip(BIG, parts):
        shp = w[n].shape
        flat = lambda a: a.reshape(-1, shp[-1])
        res = adamw_reduce(flat(w[n]), flat(mom[n]), flat(var[n]), prt.reshape(N_CHIP, -1, shp[-1]))
        out_g[n], out_d[n], out_m[n], out_v[n] = [r.reshape(shp) for r in res]

    def rows(a):
        return a.reshape(-1, d)

    def lane_rows(a):
        return jnp.pad(a.reshape(1, -1), ((0, 0), (0, d - a.size)))

    c_small = jnp.stack(g_c_small)
    pieces = [("meta", dh[:N_META]),
              ("ffn_norm", rows(jnp.stack([jnp.stack(r) for r in g_ffn_norm]))),
              ("mix_norm", rows(jnp.stack(g_mix_norm))),
              ("a_conv", rows(jnp.stack(g_a_conv))),
              ("b_conv", rows(jnp.stack(g_b_conv))),
              ("b_ln_g", rows(jnp.stack([st[0] for st in g_b_stats]))),
              ("b_ln_b", rows(jnp.stack([st[1] for st in g_b_stats]))),
              ("b_conv_bias", rows(jnp.stack([st[2] for st in g_b_stats]))),
              ("c_q_norm", lane_rows(c_small[:, 0, :HEAD_DIM])),
              ("c_k_norm", lane_rows(c_small[:, 1, :HEAD_DIM])),
              ("c_b_f", lane_rows(c_small[:, 2, :n_heads])),
              ("loss", lane_rows(loss_part[0:1, 0:1]))]
    packed = jnp.concatenate([pc for _, pc in pieces], axis=0)
    n_rows = packed.shape[0]
    packed = jnp.pad(packed, ((0, (-n_rows) % 8), (0, 0)))
    total = sum_devices(all_gather([packed])[0])
    small_g, r0 = {}, 0
    for n, pc in pieces:
        small_g[n] = total[r0:r0 + pc.shape[0]]
        r0 += pc.shape[0]
    loss = small_g.pop("loss")[0, 0]
    me = 4 * lax.axis_index("x") + 2 * lax.axis_index("y") + lax.axis_index("c")
    for n in SMALL_SHARDED:
        cols = w[n].shape[-1]
        g = lax.dynamic_slice_in_dim(small_g[n], me * cols, cols, axis=1)
        out_g[n] = g.reshape(w[n].shape)
    for n in SMALL_REPL:
        out_g[n] = small_g[n].reshape(-1)[:w[n].size].reshape(w[n].shape)
    small = SMALL_SHARDED + SMALL_REPL
    pack = lambda dct: jnp.concatenate([_pad_lanes(dct[n]) for n in small], axis=0)
    res = adamw_small(pack(w), pack(out_g), pack(mom), pack(var))
    r0 = 0
    for n in small:
        nr = -(-w[n].size // LANES)
        for dct, arr in zip((out_d, out_m, out_v), res):
            dct[n] = arr[r0:r0 + nr].reshape(-1)[:w[n].size].reshape(w[n].shape)
        r0 += nr

    return (loss, grad_x, *[out_g[n] for n in WEIGHTS], *[out_d[n] for n in WEIGHTS],
            *[out_m[n] for n in WEIGHTS], *[out_v[n] for n in WEIGHTS])
```

```python
import functools

import jax
import jax.numpy as jnp
from jax import lax
from jax.experimental import pallas as pl
from jax.experimental.pallas import tpu as pltpu

F32 = jnp.float32
BF16 = jnp.bfloat16
HI = lax.Precision.HIGHEST
EPS = 1e-6
NEG = -1e30
N_META = 16
HEAD_DIM = 64
LANES = 128
N_DEV = 8
ROW_ALIGN = 256
VMEM_LIMIT = 56 * 1024 * 1024
ADAM_LR, ADAM_B1, ADAM_B2, ADAM_EPS, ADAM_WD, ADAM_STEP = 0.001, 0.9, 0.999, 1e-08, 0.01, 10
MESH = pl.DeviceIdType.MESH
NT = (((1,), (1,)), ((), ()))
TN = (((0,), (0,)), ((), ()))


def _cparams(n_axes):
    return pltpu.CompilerParams(dimension_semantics=("arbitrary",) * n_axes, vmem_limit_bytes=VMEM_LIMIT)


def _pick(n, cands):
    for c in cands:
        if n % c == 0:
            return c
    raise ValueError(f"no tile for {n} among {cands}")


def _sigmoid(x):
    return 1.0 / (1.0 + jnp.exp(-x))


def _rms(x):
    r = lax.rsqrt(jnp.mean(x * x, axis=-1, keepdims=True) + EPS)
    return x * r, r


def _rms_bwd(xhat, r, g, dy):
    gy = dy * g
    dx = r * (gy - xhat * jnp.mean(gy * xhat, axis=-1, keepdims=True))
    return dx, jnp.sum(dy * xhat, axis=0, keepdims=True)


def _mesh_pos():
    return lax.axis_index("x"), lax.axis_index("y"), lax.axis_index("c")


def all_gather(arrs):
    n = len(arrs)
    hbm = pl.BlockSpec(memory_space=pltpu.HBM)

    def body(*refs):
        ins, outs = refs[:n], refs[n:2 * n]
        send_sems, recv_sems, local_sems = refs[2 * n:]
        x, y, c = _mesh_pos()
        me, sibling = (x, y, c), (x, y, 1 - c)
        chips = [(1 - x, y), (x, 1 - y), (1 - x, 1 - y)]

        def slot(a, px, py, pc):
            return outs[a].at[4 * px + 2 * py + pc]

        def copy(a, k, block, to, src=None):
            return pltpu.make_async_remote_copy(
                src_ref=slot(a, *block) if src is None else src, dst_ref=slot(a, *block),
                send_sem=send_sems.at[7 * a + k], recv_sem=recv_sems.at[7 * a + k],
                device_id=to, device_id_type=MESH)

        own, first, passed = [], [], []
        for a in range(n):
            cp = pltpu.make_async_copy(ins[a], slot(a, *me), local_sems.at[a])
            cp.start()
            own.append(cp)
            first.append(copy(a, 0, me, sibling, src=ins[a]))
            first += [copy(a, 1 + j, me, (*chip, c), src=ins[a]) for j, chip in enumerate(chips)]
        for cp in first:
            cp.start()
        for j, chip in enumerate(chips):
            for a in range(n):
                copy(a, 1 + j, (*chip, c), me).wait_recv()
                cp = copy(a, 4 + j, (*chip, c), sibling)
                cp.start()
                passed.append(cp)
        for a in range(n):
            copy(a, 0, sibling, me).wait_recv()
            for j, chip in enumerate(chips):
                copy(a, 4 + j, (*chip, 1 - c), me).wait_recv()
        for cp in first + passed:
            cp.wait_send()
        for cp in own:
            cp.wait()

    return pl.pallas_call(
        body, name="all_gather",
        out_shape=[jax.ShapeDtypeStruct((N_DEV,) + a.shape, a.dtype) for a in arrs],
        in_specs=[hbm] * n, out_specs=[hbm] * n,
        scratch_shapes=[pltpu.SemaphoreType.DMA((7 * n,)), pltpu.SemaphoreType.DMA((7 * n,)),
                        pltpu.SemaphoreType.DMA((n,))],
    )(*arrs)


N_CHIP = 4


def swap_with_sibling(arrs):
    n = len(arrs)
    hbm = pl.BlockSpec(memory_space=pltpu.HBM)

    def body(*refs):
        ins, outs = refs[:n], refs[n:2 * n]
        send_sems, recv_sems = refs[2 * n:]
        x, y, c = _mesh_pos()
        copies = [pltpu.make_async_remote_copy(
            src_ref=ins[a].at[1 - c], dst_ref=outs[a], send_sem=send_sems.at[a], recv_sem=recv_sems.at[a],
            device_id=(x, y, 1 - c), device_id_type=MESH) for a in range(n)]
        for cp in copies:
            cp.start()
        for cp in copies:
            cp.wait()

    return pl.pallas_call(
        body, name="swap_with_sibling",
        out_shape=[jax.ShapeDtypeStruct(a.shape[1:], a.dtype) for a in arrs],
        in_specs=[hbm] * n, out_specs=[hbm] * n,
        scratch_shapes=[pltpu.SemaphoreType.DMA((n,)), pltpu.SemaphoreType.DMA((n,))],
    )(*arrs)


def exchange_chips(arrs):
    n = len(arrs)
    hbm = pl.BlockSpec(memory_space=pltpu.HBM)

    def body(*refs):
        ins, outs = refs[:n], refs[n:2 * n]
        send_sems, recv_sems, local_sems = refs[2 * n:]
        x, y, c = _mesh_pos()
        me = 2 * x + y
        own, sent = [], []
        for a in range(n):
            cp = pltpu.make_async_copy(ins[a].at[me], outs[a].at[me], local_sems.at[a])
            cp.start()
            own.append(cp)
        for k in range(1, N_CHIP):
            px = 1 - x if k & 2 else x
            py = 1 - y if k & 1 else y
            peer = 2 * px + py
            for a in range(n):
                cp = pltpu.make_async_remote_copy(
                    src_ref=ins[a].at[peer], dst_ref=outs[a].at[me],
                    send_sem=send_sems.at[3 * a + k - 1], recv_sem=recv_sems.at[3 * a + k - 1],
                    device_id=(px, py, c), device_id_type=MESH)
                cp.start()
                sent.append((cp, a, k, peer))
        for cp, a, k, peer in sent:
            pltpu.make_async_remote_copy(
                src_ref=ins[a].at[peer], dst_ref=outs[a].at[peer],
                send_sem=send_sems.at[3 * a + k - 1], recv_sem=recv_sems.at[3 * a + k - 1],
                device_id=(x, y, c), device_id_type=MESH).wait_recv()
        for cp, a, k, peer in sent:
            cp.wait_send()
        for cp in own:
            cp.wait()

    return pl.pallas_call(
        body, name="exchange_chips",
        out_shape=[jax.ShapeDtypeStruct(a.shape, a.dtype) for a in arrs],
        in_specs=[hbm] * n, out_specs=[hbm] * n,
        scratch_shapes=[pltpu.SemaphoreType.DMA((3 * n,)), pltpu.SemaphoreType.DMA((3 * n,)),
                        pltpu.SemaphoreType.DMA((n,))],
    )(*arrs)


def add_own_half(halves, theirs):
    _, r, c = halves.shape
    tr = _pick(r, (2048, 1408, 1024, 512, 256, 128, 8))

    def body(a_ref, b_ref, o_ref):
        mine = a_ref[lax.axis_index("c")]
        o_ref[...] = (mine.astype(F32) + b_ref[...].astype(F32)).astype(BF16)

    blk = pl.BlockSpec((tr, c), lambda i: (i, 0))
    return pl.pallas_call(
        body, name="add_own_half", grid=(r // tr,), out_shape=jax.ShapeDtypeStruct((r, c), BF16),
        in_specs=[pl.BlockSpec((2, tr, c), lambda i: (0, i, 0)), blk], out_specs=blk, compiler_params=_cparams(1),
    )(halves, theirs)


def _resident(shape):
    return pl.BlockSpec(shape, lambda i: (0,) * len(shape), pipeline_mode=pl.Buffered(1))


def ffn_fwd(h, g, wg, wu, wd):
    tp, d = h.shape
    f = wg.shape[1]
    tm = _pick(tp, (384, 256))
    tf = _pick(f, (1408, 1024, 512, 256, 128))

    def body(h_ref, g_ref, wg_ref, wu_ref, wd_ref, ho_ref, gate_ref, up_ref):
        x = h_ref[...]
        xhat, _ = _rms(x)
        hn = (xhat * g_ref[...]).astype(BF16)
        acc = jnp.zeros((tm, d), F32)
        for c0 in range(0, f, tf):
            gate = jnp.dot(hn, wg_ref[:, c0:c0 + tf], preferred_element_type=F32)
            up = jnp.dot(hn, wu_ref[:, c0:c0 + tf], preferred_element_type=F32)
            gate_ref[:, c0:c0 + tf] = gate.astype(BF16)
            up_ref[:, c0:c0 + tf] = up.astype(BF16)
            act = (gate * _sigmoid(gate) * up).astype(BF16)
            acc = acc + jnp.dot(act, wd_ref[c0:c0 + tf, :], preferred_element_type=F32)
        ho_ref[...] = x + 0.5 * acc

    row = lambda n: pl.BlockSpec((tm, n), lambda i: (i, 0))
    return pl.pallas_call(
        body, name="ffn_fwd", grid=(tp // tm,),
        out_shape=[jax.ShapeDtypeStruct((tp, d), F32), jax.ShapeDtypeStruct((tp, f), BF16),
                   jax.ShapeDtypeStruct((tp, f), BF16)],
        in_specs=[row(d), _resident((1, d)), _resident((d, f)), _resident((d, f)), _resident((f, d))],
        out_specs=[row(d), row(f), row(f)],
        compiler_params=_cparams(1),
    )(h, g, wg, wu, wd)


def ffn_bwd(dho, h, g, gate, up, wg, wu, wd):
    tp, d = h.shape
    f = wg.shape[1]
    tm = _pick(tp, (256,))
    tf = _pick(f, (1408, 1024, 512, 256, 128))

    def body(dho_ref, h_ref, g_ref, gate_ref, up_ref, wg_ref, wu_ref, wd_ref,
             dhi_ref, hn_ref, dgate_ref, dup_ref, act_ref, dg_ref):
        @pl.when(pl.program_id(0) == 0)
        def _():
            dg_ref[...] = jnp.zeros_like(dg_ref)

        dho_ = dho_ref[...]
        dout = (0.5 * dho_).astype(BF16)
        dhn = jnp.zeros((tm, d), F32)
        for c0 in range(0, f, tf):
            dact = lax.dot_general(dout, wd_ref[c0:c0 + tf, :], NT, preferred_element_type=F32)
            gt = gate_ref[:, c0:c0 + tf].astype(F32)
            u = up_ref[:, c0:c0 + tf].astype(F32)
            sig = _sigmoid(gt)
            silu = gt * sig
            act_ref[:, c0:c0 + tf] = (silu * u).astype(BF16)
            dup = (dact * silu).astype(BF16)
            dgate = (dact * u * (sig * (1.0 + gt * (1.0 - sig)))).astype(BF16)
            dup_ref[:, c0:c0 + tf] = dup
            dgate_ref[:, c0:c0 + tf] = dgate
            dhn = dhn + (lax.dot_general(dgate, wg_ref[:, c0:c0 + tf], NT, preferred_element_type=F32)
                         + lax.dot_general(dup, wu_ref[:, c0:c0 + tf], NT, preferred_element_type=F32))
        xhat, r = _rms(h_ref[...])
        gg = g_ref[...]
        dx, dgp = _rms_bwd(xhat, r, gg, dhn)
        dg_ref[...] += dgp
        dhi_ref[...] = dho_ + dx
        hn_ref[...] = (xhat * gg).astype(BF16)

    row = lambda n: pl.BlockSpec((tm, n), lambda i: (i, 0))
    return pl.pallas_call(
        body, name="ffn_bwd", grid=(tp // tm,),
        out_shape=[jax.ShapeDtypeStruct((tp, d), F32), jax.ShapeDtypeStruct((tp, d), BF16),
                   jax.ShapeDtypeStruct((tp, f), BF16), jax.ShapeDtypeStruct((tp, f), BF16),
                   jax.ShapeDtypeStruct((tp, f), BF16), jax.ShapeDtypeStruct((1, d), F32)],
        in_specs=[row(d), row(d), _resident((1, d)), row(f), row(f),
                  _resident((d, f)), _resident((d, f)), _resident((f, d))],
        out_specs=[row(d), row(d), row(f), row(f), row(f), pl.BlockSpec((1, d), lambda i: (0, 0))],
        compiler_params=_cparams(1),
    )(dho, h, g, gate, up, wg, wu, wd)


def atb(a, b, scale=1.0):
    tp, m = a.shape
    n = b.shape[1]
    tmm = _pick(m, (1024, 1408, 640, 512, 256, 128))
    tn = _pick(n, (1024, 1408, 640, 512, 256, 128))
    tk = _pick(tp, (1408, 768, 512, 256))
    nk = tp // tk

    def body(a_ref, b_ref, o_ref, acc):
        k = pl.program_id(2)

        @pl.when(k == 0)
        def _():
            acc[...] = jnp.zeros_like(acc)

        acc[...] += lax.dot_general(a_ref[...].astype(BF16), b_ref[...].astype(BF16), TN,
                                    preferred_element_type=F32)

        @pl.when(k == nk - 1)
        def _():
            o_ref[...] = (acc[...] * scale).astype(BF16)

    return pl.pallas_call(
        body, name="atb", grid=(m // tmm, n // tn, nk),
        out_shape=jax.ShapeDtypeStruct((m, n), BF16),
        in_specs=[pl.BlockSpec((tk, tmm), lambda i, j, k: (k, i)), pl.BlockSpec((tk, tn), lambda i, j, k: (k, j))],
        out_specs=pl.BlockSpec((tmm, tn), lambda i, j, k: (i, j)),
        scratch_shapes=[pltpu.VMEM((tmm, tn), F32)],
        compiler_params=_cparams(3),
    )(a, b)


def norm_matmul(h, g, w, out_dtype, emit_u):
    tp, d = h.shape
    n = w.shape[1]
    tm = _pick(tp, (768, 512, 256))
    tn = _pick(n, (1024, 768, 640, 512, 256, 128))

    def body(h_ref, g_ref, w_ref, o_ref, *rest):
        u_sc = rest[-1]

        @pl.when(pl.program_id(1) == 0)
        def _():
            xhat, _ = _rms(h_ref[...])
            u_sc[...] = (xhat * g_ref[...]).astype(BF16)
            if emit_u:
                rest[0][...] = u_sc[...]

        o_ref[...] = jnp.dot(u_sc[...], w_ref[...], preferred_element_type=F32).astype(out_dtype)

    out_shape = [jax.ShapeDtypeStruct((tp, n), out_dtype)]
    out_specs = [pl.BlockSpec((tm, tn), lambda i, j: (i, j))]
    if emit_u:
        out_shape.append(jax.ShapeDtypeStruct((tp, d), BF16))
        out_specs.append(pl.BlockSpec((tm, d), lambda i, j: (i, 0)))
    return pl.pallas_call(
        body, name="norm_matmul", grid=(tp // tm, n // tn), out_shape=out_shape,
        in_specs=[pl.BlockSpec((tm, d), lambda i, j: (i, 0)), pl.BlockSpec((1, d), lambda i, j: (0, 0)),
                  pl.BlockSpec((d, tn), lambda i, j: (0, j))],
        out_specs=out_specs, scratch_shapes=[pltpu.VMEM((tm, d), BF16)],
        compiler_params=_cparams(2),
    )(h, g, w)


def res_matmul(h, y, w):
    tp, d = h.shape
    k = y.shape[1]
    tm = _pick(tp, (768, 512, 256))

    def body(h_ref, y_ref, w_ref, o_ref):
        o_ref[...] = h_ref[...] + jnp.dot(y_ref[...], w_ref[...], preferred_element_type=F32)

    return pl.pallas_call(
        body, name="res_matmul", grid=(tp // tm,), out_shape=jax.ShapeDtypeStruct((tp, d), F32),
        in_specs=[pl.BlockSpec((tm, d), lambda i: (i, 0)), pl.BlockSpec((tm, k), lambda i: (i, 0)),
                  pl.BlockSpec((k, d), lambda i: (0, 0))],
        out_specs=pl.BlockSpec((tm, d), lambda i: (i, 0)),
        compiler_params=_cparams(1),
    )(h, y, w)


def matmul_nt(a, w):
    tp, d = a.shape
    k = w.shape[0]
    tm = _pick(tp, (768, 512, 256))

    def body(a_ref, w_ref, o_ref):
        o_ref[...] = lax.dot_general(a_ref[...].astype(BF16), w_ref[...], NT,
                                     preferred_element_type=F32).astype(BF16)

    return pl.pallas_call(
        body, name="matmul_nt", grid=(tp // tm,), out_shape=jax.ShapeDtypeStruct((tp, k), BF16),
        in_specs=[pl.BlockSpec((tm, d), lambda i: (i, 0)), pl.BlockSpec((k, d), lambda i: (0, 0))],
        out_specs=pl.BlockSpec((tm, k), lambda i: (i, 0)),
        compiler_params=_cparams(1),
    )(a, w)


def matmul_nt_rms_bwd(dp, w, h, g, dres):
    tp, d = h.shape
    n = w.shape[1]
    tm = _pick(tp, (384, 256))

    def body(dp_ref, w_ref, h_ref, g_ref, dres_ref, dh_ref, dg_ref):
        @pl.when(pl.program_id(0) == 0)
        def _():
            dg_ref[...] = jnp.zeros_like(dg_ref)

        du = lax.dot_general(dp_ref[...], w_ref[...], NT, preferred_element_type=F32)
        xhat, r = _rms(h_ref[...])
        dx, dgp = _rms_bwd(xhat, r, g_ref[...], du)
        dg_ref[...] += dgp
        dh_ref[...] = dres_ref[...] + dx

    return pl.pallas_call(
        body, name="matmul_nt_rms_bwd", grid=(tp // tm,),
        out_shape=[jax.ShapeDtypeStruct((tp, d), F32), jax.ShapeDtypeStruct((1, d), F32)],
        in_specs=[pl.BlockSpec((tm, n), lambda i: (i, 0)), pl.BlockSpec((d, n), lambda i: (0, 0)),
                  pl.BlockSpec((tm, d), lambda i: (i, 0)), pl.BlockSpec((1, d), lambda i: (0, 0)),
                  pl.BlockSpec((tm, d), lambda i: (i, 0))],
        out_specs=[pl.BlockSpec((tm, d), lambda i: (i, 0)), pl.BlockSpec((1, d), lambda i: (0, 0))],
        compiler_params=_cparams(1),
    )(dp, w, h, g, dres)


CONV_TILE = (256,)


def _halo_before(tm, hb, col=0):
    return lambda i: (jnp.maximum(i * (tm // hb) - 1, 0), col)


def _halo_after(tm, hb, nblk, col=0):
    return lambda i: (jnp.minimum((i + 1) * (tm // hb), nblk - 1), col)


def a_elem_fwd(p, w):
    tp, d3 = p.shape
    d = d3 // 3
    kw = w.shape[0]
    tm = _pick(tp, CONV_TILE)
    hb = 16

    def body(p_ref, ph_ref, w_ref, y_ref, ext):
        i = pl.program_id(0)
        halo = ph_ref[:, d:2 * d].astype(F32) * ph_ref[:, 2 * d:].astype(F32)
        ext[0:hb, :] = jnp.where(i > 0, halo, 0.0)
        ext[hb:, :] = p_ref[:, d:2 * d].astype(F32) * p_ref[:, 2 * d:].astype(F32)
        z = jnp.zeros((tm, d), F32)
        for k in range(kw):
            z = z + w_ref[k:k + 1, :] * ext[pl.ds(hb - (kw - 1) + k, tm), :]
        y_ref[...] = (p_ref[:, 0:d].astype(F32) * z).astype(BF16)

    return pl.pallas_call(
        body, name="a_elem_fwd", grid=(tp // tm,), out_shape=jax.ShapeDtypeStruct((tp, d), BF16),
        in_specs=[pl.BlockSpec((tm, d3), lambda i: (i, 0)), pl.BlockSpec((hb, d3), _halo_before(tm, hb)),
                  pl.BlockSpec((kw, d), lambda i: (0, 0))],
        out_specs=pl.BlockSpec((tm, d), lambda i: (i, 0)),
        scratch_shapes=[pltpu.VMEM((tm + hb, d), F32)],
        compiler_params=_cparams(1),
    )(p, p, w)


def a_elem_bwd(p, dy, w):
    tp, d3 = p.shape
    d = d3 // 3
    kw = w.shape[0]
    tm = _pick(tp, CONV_TILE)
    hb = 16
    n = tp // tm

    def body(p_ref, ph_ref, pa_ref, dy_ref, dya_ref, w_ref, dp_ref, dw_ref, ext, dzext):
        i = pl.program_id(0)

        @pl.when(i == 0)
        def _():
            dw_ref[...] = jnp.zeros_like(dw_ref)

        b = p_ref[:, 0:d].astype(F32)
        c = p_ref[:, d:2 * d].astype(F32)
        v = p_ref[:, 2 * d:].astype(F32)
        dy_ = dy_ref[...].astype(F32)
        halo = ph_ref[:, d:2 * d].astype(F32) * ph_ref[:, 2 * d:].astype(F32)
        ext[0:hb, :] = jnp.where(i > 0, halo, 0.0)
        ext[hb:, :] = c * v
        dz = dy_ * b
        dzext[0:tm, :] = dz
        dzext[tm:, :] = jnp.where(i < n - 1, dya_ref[...].astype(F32) * pa_ref[:, 0:d].astype(F32), 0.0)
        z = jnp.zeros((tm, d), F32)
        dcv = jnp.zeros((tm, d), F32)
        for k in range(kw):
            sh = ext[pl.ds(hb - (kw - 1) + k, tm), :]
            z = z + w_ref[k:k + 1, :] * sh
            dw_ref[k:k + 1, :] += jnp.sum(dz * sh, axis=0, keepdims=True)
            dcv = dcv + w_ref[k:k + 1, :] * dzext[pl.ds(kw - 1 - k, tm), :]
        dp_ref[:, 0:d] = (dy_ * z).astype(BF16)
        dp_ref[:, d:2 * d] = (dcv * v).astype(BF16)
        dp_ref[:, 2 * d:] = (dcv * c).astype(BF16)

    nb = tp // hb
    return pl.pallas_call(
        body, name="a_elem_bwd", grid=(n,),
        out_shape=[jax.ShapeDtypeStruct((tp, d3), BF16), jax.ShapeDtypeStruct((8, d), F32)],
        in_specs=[pl.BlockSpec((tm, d3), lambda i: (i, 0)), pl.BlockSpec((hb, d3), _halo_before(tm, hb)),
                  pl.BlockSpec((hb, d3), _halo_after(tm, hb, nb)),
                  pl.BlockSpec((tm, d), lambda i: (i, 0)), pl.BlockSpec((hb, d), _halo_after(tm, hb, nb)),
                  pl.BlockSpec((kw, d), lambda i: (0, 0))],
        out_specs=[pl.BlockSpec((tm, d3), lambda i: (i, 0)), pl.BlockSpec((8, d), lambda i: (0, 0))],
        scratch_shapes=[pltpu.VMEM((tm + hb, d), F32), pltpu.VMEM((tm + hb, d), F32)],
        compiler_params=_cparams(1),
    )(p, p, p, dy, dy, w)


CH_R, CH_C = 128, 128


def _chunks(tm, d):
    return [(r0, c0) for c0 in range(0, d, CH_C) for r0 in range(0, tm, CH_R)]


def _b_u2(p_ref, ph_ref, w_ref, bias_ref, ext, u2_sc, i, tm, d, kw, hb):
    a = p_ref[:, 0:d].astype(F32)
    sg = _sigmoid(p_ref[:, d:].astype(F32))
    halo = ph_ref[:, 0:d].astype(F32) * _sigmoid(ph_ref[:, d:].astype(F32))
    ext[0:hb, :] = jnp.where(i > 0, halo, 0.0)
    ext[hb:, :] = a * sg
    for r0, c0 in _chunks(tm, d):
        cs = slice(c0, c0 + CH_C)
        acc = jnp.zeros((CH_R, CH_C), F32) + bias_ref[:, cs]
        for k in range(kw):
            acc = acc + w_ref[k:k + 1, cs] * ext[pl.ds(r0 + hb - (kw - 1) + k, CH_R), cs]
        u2_sc[r0:r0 + CH_R, cs] = acc
    return a, sg, u2_sc[...]


def _ln(u2):
    mu = jnp.mean(u2, axis=-1, keepdims=True)
    xc = u2 - mu
    rstd = lax.rsqrt(jnp.mean(xc * xc, axis=-1, keepdims=True) + EPS)
    return xc * rstd, rstd


def b_elem_fwd(p, w, bias, ln_g, ln_b):
    tp, d2 = p.shape
    d = d2 // 2
    kw = w.shape[0]
    tm = _pick(tp, CONV_TILE)
    hb = 32

    def body(p_ref, ph_ref, w_ref, bias_ref, g_ref, b_ref, y_ref, ext, u2_sc):
        i = pl.program_id(0)
        _, _, u2 = _b_u2(p_ref, ph_ref, w_ref, bias_ref, ext, u2_sc, i, tm, d, kw, hb)
        xhat, _ = _ln(u2)
        u3 = xhat * g_ref[...] + b_ref[...]
        y_ref[...] = (u3 * _sigmoid(u3)).astype(BF16)

    vec = pl.BlockSpec((1, d), lambda i: (0, 0))
    return pl.pallas_call(
        body, name="b_elem_fwd", grid=(tp // tm,), out_shape=jax.ShapeDtypeStruct((tp, d), BF16),
        in_specs=[pl.BlockSpec((tm, d2), lambda i: (i, 0)), pl.BlockSpec((hb, d2), _halo_before(tm, hb)),
                  pl.BlockSpec((kw, d), lambda i: (0, 0)), vec, vec, vec],
        out_specs=pl.BlockSpec((tm, d), lambda i: (i, 0)),
        scratch_shapes=[pltpu.VMEM((tm + hb, d), F32), pltpu.VMEM((tm, d), F32)],
        compiler_params=_cparams(1),
    )(p, p, w, bias, ln_g, ln_b)


def b_elem_bwd1(p, dy, w, bias, ln_g, ln_b):
    tp, d2 = p.shape
    d = d2 // 2
    kw = w.shape[0]
    tm = _pick(tp, CONV_TILE)
    hb = 32

    def body(p_ref, ph_ref, dy_ref, w_ref, bias_ref, g_ref, b_ref, du2_ref, st_ref, ext, u2_sc):
        i = pl.program_id(0)

        @pl.when(i == 0)
        def _():
            st_ref[...] = jnp.zeros_like(st_ref)

        _, _, u2 = _b_u2(p_ref, ph_ref, w_ref, bias_ref, ext, u2_sc, i, tm, d, kw, hb)
        xhat, rstd = _ln(u2)
        u3 = xhat * g_ref[...] + b_ref[...]
        s3 = _sigmoid(u3)
        du3 = dy_ref[...].astype(F32) * (s3 * (1.0 + u3 * (1.0 - s3)))
        dxh = du3 * g_ref[...]
        du2 = rstd * (dxh - jnp.mean(dxh, axis=-1, keepdims=True)
                      - xhat * jnp.mean(dxh * xhat, axis=-1, keepdims=True))
        du2_ref[...] = du2
        st_ref[0:1, :] += jnp.sum(du3 * xhat, axis=0, keepdims=True)
        st_ref[1:2, :] += jnp.sum(du3, axis=0, keepdims=True)
        st_ref[2:3, :] += jnp.sum(du2, axis=0, keepdims=True)

    vec = pl.BlockSpec((1, d), lambda i: (0, 0))
    return pl.pallas_call(
        body, name="b_elem_bwd1", grid=(tp // tm,),
        out_shape=[jax.ShapeDtypeStruct((tp, d), F32), jax.ShapeDtypeStruct((8, d), F32)],
        in_specs=[pl.BlockSpec((tm, d2), lambda i: (i, 0)), pl.BlockSpec((hb, d2), _halo_before(tm, hb)),
                  pl.BlockSpec((tm, d), lambda i: (i, 0)), pl.BlockSpec((kw, d), lambda i: (0, 0)), vec, vec, vec],
        out_specs=[pl.BlockSpec((tm, d), lambda i: (i, 0)), pl.BlockSpec((8, d), lambda i: (0, 0))],
        scratch_shapes=[pltpu.VMEM((tm + hb, d), F32), pltpu.VMEM((tm, d), F32)],
        compiler_params=_cparams(1),
    )(p, p, dy, w, bias, ln_g, ln_b)


def b_elem_bwd2(p, du2, w):
    tp, d2 = p.shape
    d = d2 // 2
    kw = w.shape[0]
    tm = _pick(tp, CONV_TILE)
    hb = 32
    n = tp // tm

    def body(p_ref, ph_ref, du2_ref, du2a_ref, w_ref, dp_ref, dw_ref, ext, dext, dwacc):
        i = pl.program_id(0)

        @pl.when(i == 0)
        def _():
            dwacc[...] = jnp.zeros_like(dwacc)

        halo = ph_ref[:, 0:d].astype(F32) * _sigmoid(ph_ref[:, d:].astype(F32))
        ext[0:hb, :] = jnp.where(i > 0, halo, 0.0)
        ext[hb:, :] = p_ref[:, 0:d].astype(F32) * _sigmoid(p_ref[:, d:].astype(F32))
        dext[0:tm, :] = du2_ref[...]
        dext[tm:, :] = jnp.where(i < n - 1, du2a_ref[...], 0.0)
        for r0, c0 in _chunks(tm, d):
            cs = slice(c0, c0 + CH_C)
            du2c = du2_ref[r0:r0 + CH_R, cs]
            du1 = jnp.zeros((CH_R, CH_C), F32)
            for k in range(kw):
                prod = du2c * ext[pl.ds(r0 + hb - (kw - 1) + k, CH_R), cs]
                part = prod[0:8]
                for r in range(8, CH_R, 8):
                    part = part + prod[r:r + 8]
                dwacc[8 * k:8 * k + 8, cs] += part
                du1 = du1 + w_ref[k:k + 1, cs] * dext[pl.ds(r0 + kw - 1 - k, CH_R), cs]
            a = p_ref[r0:r0 + CH_R, cs].astype(F32)
            sg = _sigmoid(p_ref[r0:r0 + CH_R, d + c0:d + c0 + CH_C].astype(F32))
            dp_ref[r0:r0 + CH_R, cs] = (du1 * sg).astype(BF16)
            dp_ref[r0:r0 + CH_R, d + c0:d + c0 + CH_C] = (du1 * a * sg * (1.0 - sg)).astype(BF16)

        @pl.when(i == n - 1)
        def _():
            dw_ref[...] = jnp.zeros_like(dw_ref)
            for k in range(kw):
                dw_ref[k:k + 1, :] = jnp.sum(dwacc[8 * k:8 * k + 8, :], axis=0, keepdims=True)

    nb = tp // hb
    return pl.pallas_call(
        body, name="b_elem_bwd2", grid=(n,),
        out_shape=[jax.ShapeDtypeStruct((tp, d2), BF16), jax.ShapeDtypeStruct((32, d), F32)],
        in_specs=[pl.BlockSpec((tm, d2), lambda i: (i, 0)), pl.BlockSpec((hb, d2), _halo_before(tm, hb)),
                  pl.BlockSpec((tm, d), lambda i: (i, 0)), pl.BlockSpec((hb, d), _halo_after(tm, hb, nb)),
                  pl.BlockSpec((kw, d), lambda i: (0, 0))],
        out_specs=[pl.BlockSpec((tm, d2), lambda i: (i, 0)), pl.BlockSpec((32, d), lambda i: (0, 0))],
        scratch_shapes=[pltpu.VMEM((tm + hb, d), F32), pltpu.VMEM((tm + hb, d), F32), pltpu.VMEM((8 * kw, d), F32)],
        compiler_params=_cparams(1),
    )(p, p, du2, du2, w)


ATT_TILE = 256


def _head_masks(d):
    c = lax.broadcasted_iota(jnp.int32, (d, LANES), 0)
    h = lax.broadcasted_iota(jnp.int32, (d, LANES), 1)
    seg = (c // HEAD_DIM == h).astype(F32)
    fold = (c % HEAD_DIM == h).astype(F32)
    return seg, seg.T, fold


L2E = 1.4426950408889634
LN2 = 0.6931471805599453
AUG = HEAD_DIM


def _split3(r):
    r1 = r.astype(BF16).astype(F32)
    r2 = (r - r1).astype(BF16).astype(F32)
    r3 = (r - r1 - r2).astype(BF16).astype(F32)
    return r1, r2, r3


def c_prep_slots(p, b_f, qg, kg):
    tp = p.shape[0]
    d = qg.shape[1]
    n_heads = d // HEAD_DIM
    tm = ATT_TILE
    seg, seg_t, _ = _head_masks(d)

    def body(p_ref, pf_ref, bf_ref, qg_ref, kg_ref, seg_ref, segt_ref, q_ref, k_ref, v_ref, cum_ref, carry):
        @pl.when(pl.program_id(0) == 0)
        def _():
            carry[...] = jnp.zeros_like(carry)

        def norm(x, g):
            ms = jnp.dot(x * x, seg_ref[...], precision=HI, preferred_element_type=F32) * (1.0 / HEAD_DIM)
            r = jnp.dot(lax.rsqrt(ms + EPS), segt_ref[...], precision=HI, preferred_element_type=F32)
            return x * r * g

        qn = norm(p_ref[:, 0:d], qg_ref[...]) * (HEAD_DIM ** -0.5 * L2E)
        kn = norm(p_ref[:, d:2 * d], kg_ref[...])
        v_ref[...] = p_ref[:, 2 * d:].astype(BF16)
        xf = pf_ref[...] + bf_ref[...]
        logf = jnp.minimum(xf, 0.0) - jnp.log(1.0 + jnp.exp(-jnp.abs(xf)))
        r_ = lax.broadcasted_iota(jnp.int32, (tm, tm), 0)
        c_ = lax.broadcasted_iota(jnp.int32, (tm, tm), 1)
        cum = jnp.dot((c_ <= r_).astype(F32), logf, precision=HI, preferred_element_type=F32) + carry[0:1, :]
        cum_ref[...] = cum
        carry[0:1, :] += jnp.sum(logf, axis=0, keepdims=True)

        lane = lax.broadcasted_iota(jnp.int32, (1, LANES), 1)
        ones_q = jnp.where((lane >= AUG + 3) & (lane < AUG + 6), 1.0, 0.0)
        ones_k = jnp.where((lane >= AUG) & (lane < AUG + 3), 1.0, 0.0)
        for h in range(n_heads):
            c0 = LANES * (h // 2)
            src_q, src_k = qn[:, c0:c0 + LANES], kn[:, c0:c0 + LANES]
            if h % 2:
                src_q = pltpu.roll(src_q, HEAD_DIM, axis=1)
                src_k = pltpu.roll(src_k, HEAD_DIM, axis=1)
            c = cum[:, h:h + 1] * L2E
            a1, a2, a3 = _split3(c)
            b1, b2, b3 = _split3(-c)
            aug_q = jnp.where(lane == AUG, a1, jnp.where(lane == AUG + 1, a2, jnp.where(lane == AUG + 2, a3, ones_q)))
            aug_k = jnp.where(lane == AUG + 3, b1,
                              jnp.where(lane == AUG + 4, b2, jnp.where(lane == AUG + 5, b3, ones_k)))
            q_ref[:, LANES * h:LANES * (h + 1)] = jnp.where(lane < HEAD_DIM, src_q, aug_q).astype(BF16)
            k_ref[:, LANES * h:LANES * (h + 1)] = jnp.where(lane < HEAD_DIM, src_k, aug_k).astype(BF16)

    vec = pl.BlockSpec((1, d), lambda i: (0, 0))
    rowd = pl.BlockSpec((tm, d), lambda i: (i, 0))
    slots = pl.BlockSpec((tm, n_heads * LANES), lambda i: (i, 0))
    return pl.pallas_call(
        body, name="c_prep_slots", grid=(tp // tm,),
        out_shape=[jax.ShapeDtypeStruct((tp, n_heads * LANES), BF16)] * 2
        + [jax.ShapeDtypeStruct((tp, d), BF16), jax.ShapeDtypeStruct((tp, LANES), F32)],
        in_specs=[pl.BlockSpec((tm, 3 * d), lambda i: (i, 0)), pl.BlockSpec((tm, LANES), lambda i: (i, 3 * d // LANES)),
                  pl.BlockSpec((1, LANES), lambda i: (0, 0)), vec, vec,
                  pl.BlockSpec((d, LANES), lambda i: (0, 0)), pl.BlockSpec((LANES, d), lambda i: (0, 0))],
        out_specs=[slots, slots, rowd, pl.BlockSpec((tm, LANES), lambda i: (i, 0))],
        scratch_shapes=[pltpu.VMEM((8, LANES), F32)],
        compiler_params=_cparams(1),
    )(p, p, b_f, qg, kg, seg, seg_t)


ATT_GROUPS = (8, 2)


def _grouped_loop(n_blocks, body, carry):
    start = 0
    for g in ATT_GROUPS:
        count = (n_blocks - start) // g
        carry = lax.fori_loop(0, count, lambda i, c, g=g, start=start: body(start + i * g, g, c), carry)
        start = start + count * g
    return lax.fori_loop(start, n_blocks, lambda kj, c: body(kj, 1, c), carry)


def _diag_mask(t):
    return lax.broadcasted_iota(jnp.int32, (t, t), 0) <= lax.broadcasted_iota(jnp.int32, (t, t), 1)


def attn_fwd_t(qpt, kp, vt4):
    tp = kp.shape[0]
    t = ATT_TILE
    nb = tp // t
    npair = vt4.shape[0]

    def body(q_ref, k_ref, v_ref, o_ref, lse_ref):
        qi = pl.program_id(1)
        qts = (q_ref[0:LANES, :], q_ref[LANES:2 * LANES, :])
        mask = _diag_mask(t)

        def blocks(kj0, n, carry, masked=False):
            out = []
            for hd in range(2):
                m, l, acc = carry[3 * hd:3 * hd + 3]
                sts = []
                for g in range(n):
                    off = pl.multiple_of((kj0 + g) * t, t)
                    st = jnp.dot(k_ref[pl.ds(off, t), LANES * hd:LANES * (hd + 1)], qts[hd],
                                 preferred_element_type=F32)
                    sts.append(jnp.where(mask, st, NEG) if masked else st)
                m_new = m
                for st in sts:
                    m_new = jnp.maximum(m_new, jnp.max(st, axis=0, keepdims=True))
                alpha = jnp.exp2(m - m_new)
                l = alpha * l
                acc = alpha * acc
                for g, st in enumerate(sts):
                    pt = jnp.exp2(st - m_new)
                    l = l + jnp.sum(pt, axis=0, keepdims=True)
                    acc = acc + jnp.dot(v_ref[0, kj0 + g], pt.astype(BF16), preferred_element_type=F32)
                out += [m_new, l, acc]
            return tuple(out)

        init = (jnp.full((1, t), NEG, F32), jnp.zeros((1, t), F32), jnp.zeros((LANES, t), F32)) * 2
        carry = _grouped_loop(qi, blocks, init)
        ma, la, acca, mb, lb, accb = blocks(qi, 1, carry, masked=True)
        row = lax.broadcasted_iota(jnp.int32, (LANES, 1), 0)
        o_ref[...] = jnp.where(row < HEAD_DIM, acca / la, accb / lb).astype(BF16)
        lse_ref[0, 0:1, :] = ma + jnp.log(la) * L2E
        lse_ref[0, 1:2, :] = mb + jnp.log(lb) * L2E

    return pl.pallas_call(
        body, name="attn_fwd_t", grid=(npair, nb),
        out_shape=[jax.ShapeDtypeStruct((npair * LANES, tp), BF16), jax.ShapeDtypeStruct((npair, 2, tp), F32)],
        in_specs=[pl.BlockSpec((2 * LANES, t), lambda h, i: (h, i)), pl.BlockSpec((tp, 2 * LANES), lambda h, i: (0, h)),
                  pl.BlockSpec((1, nb, LANES, t), lambda h, i: (h, 0, 0, 0))],
        out_specs=[pl.BlockSpec((LANES, t), lambda h, i: (h, i)), pl.BlockSpec((1, 2, t), lambda h, i: (h, 0, i))],
        compiler_params=_cparams(2),
    )(qpt, kp, vt4)


def attn_bwd_t(qpt, qp, kp, kpt4, v, do, dot_, lse, cqt):
    tp = kp.shape[0]
    t = ATT_TILE
    nb = tp // t
    n_heads = kpt4.shape[0]
    d = v.shape[1]

    def body(qt_ref, q_ref, k_ref, kt_ref, v_ref, do_ref, dot_ref, lse_ref, cq_ref, dq_ref, dk_ref, dv_ref,
             p_sc, dp_sc, dc_sc):
        hd = pl.program_id(0) % 2
        qi = pl.program_id(1)

        @pl.when(qi == 0)
        def _():
            dk_ref[...] = jnp.zeros_like(dk_ref)
            dc_sc[...] = jnp.zeros_like(dc_sc)

        @pl.when((qi == 0) & (hd == 0))
        def _():
            dv_ref[...] = jnp.zeros_like(dv_ref)

        row = lax.broadcasted_iota(jnp.int32, (LANES, 1), 0)
        lane = lax.broadcasted_iota(jnp.int32, (1, LANES), 1)
        dot_h = jnp.where(row // HEAD_DIM == hd, dot_ref[...], jnp.zeros_like(dot_ref))
        do_h = jnp.where(lane // HEAD_DIM == hd, do_ref[...], jnp.zeros_like(do_ref))
        rr = cq_ref[0] - lse_ref[0]
        r1, r2, r3 = _split3(jnp.where(hd == 0, rr[0:1, :], rr[1:2, :]))
        qt = qt_ref[...].astype(F32)
        qt = jnp.where(row == AUG, r1, jnp.where(row == AUG + 1, r2, jnp.where(row == AUG + 2, r3, qt))).astype(BF16)
        mask = _diag_mask(t)

        def pass1(kj0, n, delta, masked=False):
            for g in range(n):
                off = pl.multiple_of((kj0 + g) * t, t)
                st = jnp.dot(k_ref[pl.ds(off, t), :], qt, preferred_element_type=F32)
                if masked:
                    st = jnp.where(mask, st, NEG)
                pt = jnp.exp2(st)
                dpt = jnp.dot(v_ref[pl.ds(off, t), :], dot_h, preferred_element_type=F32)
                p_sc[kj0 + g] = pt
                dp_sc[kj0 + g] = dpt
                delta = delta + jnp.sum(pt * dpt, axis=0, keepdims=True)
            return delta

        delta = _grouped_loop(qi, pass1, jnp.zeros((1, t), F32))
        delta = pass1(qi, 1, delta, masked=True)

        def pass2(kj0, n, dq):
            for g in range(n):
                off = pl.multiple_of((kj0 + g) * t, t)
                pt = p_sc[kj0 + g]
                ds32 = pt * (dp_sc[kj0 + g] - delta)
                ds = ds32.astype(BF16)
                dc_sc[pl.ds(off, t), :] += ds32[:, 0:LANES] + ds32[:, LANES:2 * LANES]
                dk_ref[pl.ds(off, t), :] += jnp.dot(ds, q_ref[...], preferred_element_type=F32)
                dv_ref[pl.ds(off, t), :] += jnp.dot(pt.astype(BF16), do_h, preferred_element_type=F32)
                dq = dq + jnp.dot(kt_ref[0, kj0 + g], ds, preferred_element_type=F32)
            return dq

        dq_ref[...] = _grouped_loop(qi + 1, pass2, jnp.zeros((LANES, t), F32))

        @pl.when(qi == nb - 1)
        def _():
            dk_ref[:, AUG + 3:AUG + 4] = jnp.sum(dc_sc[...], axis=1, keepdims=True)

    once = dict(pipeline_mode=pl.Buffered(1))
    pair_rows = pl.BlockSpec((1, 2, t), lambda h, i: (h // 2, 0, i))
    return pl.pallas_call(
        body, name="attn_bwd_t", grid=(n_heads, nb),
        out_shape=[jax.ShapeDtypeStruct((n_heads * LANES, tp), F32), jax.ShapeDtypeStruct((tp, n_heads * LANES), F32),
                   jax.ShapeDtypeStruct((tp, d), F32)],
        in_specs=[pl.BlockSpec((LANES, t), lambda h, i: (h, i)), pl.BlockSpec((t, LANES), lambda h, i: (i, h)),
                  pl.BlockSpec((tp, LANES), lambda h, i: (0, h), **once),
                  pl.BlockSpec((1, nb, LANES, t), lambda h, i: (h, 0, 0, 0), **once),
                  pl.BlockSpec((tp, LANES), lambda h, i: (0, h // 2), **once),
                  pl.BlockSpec((t, LANES), lambda h, i: (i, h // 2)), pl.BlockSpec((LANES, t), lambda h, i: (h // 2, i)),
                  pair_rows, pair_rows],
        out_specs=[pl.BlockSpec((LANES, t), lambda h, i: (h, i)), pl.BlockSpec((tp, LANES), lambda h, i: (0, h)),
                   pl.BlockSpec((tp, LANES), lambda h, i: (0, h // 2))],
        scratch_shapes=[pltpu.VMEM((nb, t, t), F32), pltpu.VMEM((nb, t, t), F32), pltpu.VMEM((tp, LANES), F32)],
        compiler_params=_cparams(2),
    )(qpt, qp, kp, kpt4, v, do, dot_, lse, cqt)


def rev_cumsum_rows(x):
    r, tp = x.shape
    t = ATT_TILE
    nb = tp // t

    def body(x_ref, o_ref, carry):
        @pl.when(pl.program_id(0) == 0)
        def _():
            carry[...] = jnp.zeros_like(carry)

        xv = x_ref[...]
        r_ = lax.broadcasted_iota(jnp.int32, (t, t), 0)
        c_ = lax.broadcasted_iota(jnp.int32, (t, t), 1)
        o_ref[...] = jnp.dot(xv, (r_ >= c_).astype(F32), precision=HI, preferred_element_type=F32) + carry[:, 0:1]
        carry[...] += jnp.sum(xv, axis=1, keepdims=True)

    return pl.pallas_call(
        body, name="rev_cumsum_rows", grid=(nb,), out_shape=jax.ShapeDtypeStruct((r, tp), F32),
        in_specs=[pl.BlockSpec((r, t), lambda i: (0, nb - 1 - i))],
        out_specs=pl.BlockSpec((r, t), lambda i: (0, nb - 1 - i)),
        scratch_shapes=[pltpu.VMEM((r, LANES), F32)],
        compiler_params=_cparams(1),
    )(x)


def c_elem_bwd(p, b_f, qg, kg, dq, dk, dv, dlogf):
    tp, n_out = p.shape
    d = qg.shape[1]
    tm = ATT_TILE
    n = tp // tm
    seg, seg_t, fold = _head_masks(d)

    def body(p_ref, pf_ref, bf_ref, qg_ref, kg_ref, dq_ref, dk_ref, dv_ref, dlf_ref, seg_ref, segt_ref, fold_ref,
             dp_ref, sm_ref, accq, acck, accf):
        i = pl.program_id(0)

        @pl.when(i == 0)
        def _():
            accq[...] = jnp.zeros_like(accq)
            acck[...] = jnp.zeros_like(acck)
            accf[...] = jnp.zeros_like(accf)

        def norm_bwd(x, g, dy, acc):
            ms = jnp.dot(x * x, seg_ref[...], precision=HI, preferred_element_type=F32) * (1.0 / HEAD_DIM)
            r = jnp.dot(lax.rsqrt(ms + EPS), segt_ref[...], precision=HI, preferred_element_type=F32)
            xhat = x * r
            acc[0:1, :] += jnp.sum(dy * xhat, axis=0, keepdims=True)
            gy = dy * g
            mean = jnp.dot(jnp.dot(gy * xhat, seg_ref[...], precision=HI, preferred_element_type=F32),
                           segt_ref[...], precision=HI, preferred_element_type=F32) * (1.0 / HEAD_DIM)
            return r * (gy - xhat * mean)

        dp_ref[:, 0:d] = norm_bwd(p_ref[:, 0:d], qg_ref[...], dq_ref[...] * (HEAD_DIM ** -0.5), accq).astype(BF16)
        dp_ref[:, d:2 * d] = norm_bwd(p_ref[:, d:2 * d], kg_ref[...], dk_ref[...] * LN2, acck).astype(BF16)
        dp_ref[:, 2 * d:3 * d] = dv_ref[...].astype(BF16)
        df = dlf_ref[...] * _sigmoid(-(pf_ref[...] + bf_ref[...]))
        accf[0:1, :] += jnp.sum(df, axis=0, keepdims=True)
        dp_ref[:, 3 * d:] = df.astype(BF16)

        @pl.when(i == n - 1)
        def _():
            sm_ref[...] = jnp.zeros_like(sm_ref)
            sm_ref[0:1, :] = jnp.dot(accq[0:1, :], fold_ref[...], precision=HI, preferred_element_type=F32)
            sm_ref[1:2, :] = jnp.dot(acck[0:1, :], fold_ref[...], precision=HI, preferred_element_type=F32)
            sm_ref[2:3, :] = accf[0:1, :]

    vec = pl.BlockSpec((1, d), lambda i: (0, 0))
    rowd = pl.BlockSpec((tm, d), lambda i: (i, 0))
    rowl = pl.BlockSpec((tm, LANES), lambda i: (i, 0))
    return pl.pallas_call(
        body, name="c_elem_bwd", grid=(n,),
        out_shape=[jax.ShapeDtypeStruct((tp, n_out), BF16), jax.ShapeDtypeStruct((8, LANES), F32)],
        in_specs=[pl.BlockSpec((tm, 2 * d), lambda i: (i, 0)), pl.BlockSpec((tm, LANES), lambda i: (i, 3 * d // LANES)),
                  pl.BlockSpec((1, LANES), lambda i: (0, 0)), vec, vec, rowd, rowd, rowd, rowl,
                  pl.BlockSpec((d, LANES), lambda i: (0, 0)), pl.BlockSpec((LANES, d), lambda i: (0, 0)),
                  pl.BlockSpec((d, LANES), lambda i: (0, 0))],
        out_specs=[pl.BlockSpec((tm, n_out), lambda i: (i, 0)), pl.BlockSpec((8, LANES), lambda i: (0, 0))],
        scratch_shapes=[pltpu.VMEM((8, d), F32), pltpu.VMEM((8, d), F32), pltpu.VMEM((8, LANES), F32)],
        compiler_params=_cparams(1),
    )(p, p, b_f, qg, kg, dq, dk, dv, dlogf, seg, seg_t, fold)


def loss_head(h, tgt, t_real):
    tp, d = h.shape
    tm = _pick(tp, (768, 512, 256))

    def body(h_ref, t_ref, dh_ref, l_ref):
        i = pl.program_id(0)

        @pl.when(i == 0)
        def _():
            l_ref[...] = jnp.zeros_like(l_ref)

        row = i * tm + lax.broadcasted_iota(jnp.int32, (tm, 1), 0)
        valid = (row >= N_META) & (row < t_real)
        e = jnp.where(valid, h_ref[...] - t_ref[...], 0.0)
        dh_ref[...] = e * (1.0 / d)
        per_row = jnp.sum(e * e, axis=-1, keepdims=True) * (1.0 / d)
        l_ref[...] += 0.5 * jnp.sum(per_row, axis=0, keepdims=True)

    return pl.pallas_call(
        body, name="loss_head", grid=(tp // tm,),
        out_shape=[jax.ShapeDtypeStruct((tp, d), F32), jax.ShapeDtypeStruct((8, LANES), F32)],
        in_specs=[pl.BlockSpec((tm, d), lambda i: (i, 0)), pl.BlockSpec((tm, d), lambda i: (i, 0))],
        out_specs=[pl.BlockSpec((tm, d), lambda i: (i, 0)), pl.BlockSpec((8, LANES), lambda i: (0, 0))],
        compiler_params=_cparams(1),
    )(h, tgt)


def sum_devices(x):
    _, r, c = x.shape

    def body(x_ref, o_ref):
        acc = x_ref[0]
        for dev in range(1, N_DEV):
            acc = acc + x_ref[dev]
        o_ref[...] = acc

    return pl.pallas_call(
        body, name="sum_devices", out_shape=jax.ShapeDtypeStruct((r, c), F32),
        in_specs=[pl.BlockSpec(memory_space=pltpu.VMEM)], out_specs=pl.BlockSpec(memory_space=pltpu.VMEM),
    )(x)


def _adamw_math(w, g, m, v):
    m = ADAM_B1 * m + (1.0 - ADAM_B1) * g
    v = ADAM_B2 * v + (1.0 - ADAM_B2) * (g * g)
    m_hat = m / (1.0 - ADAM_B1 ** ADAM_STEP)
    v_hat = v / (1.0 - ADAM_B2 ** ADAM_STEP)
    delta = -ADAM_LR * (m_hat / (jnp.sqrt(v_hat) + ADAM_EPS) + ADAM_WD * w)
    return delta, m, v


def adamw_small(w, g, m, v):
    def body(w_ref, g_ref, m_ref, v_ref, d_ref, nm_ref, nv_ref):
        d_ref[...], nm_ref[...], nv_ref[...] = _adamw_math(w_ref[...], g_ref[...], m_ref[...], v_ref[...])

    vm = pl.BlockSpec(memory_space=pltpu.VMEM)
    return pl.pallas_call(
        body, name="adamw_small", out_shape=[jax.ShapeDtypeStruct(w.shape, F32)] * 3,
        in_specs=[vm] * 4, out_specs=[vm] * 3,
    )(w, g, m, v)


def adamw_reduce(w, m, v, parts):
    r, c = w.shape
    n_parts = parts.shape[0]
    tr = _pick(r, (512, 384, 352, 256, 128, 8))

    def body(w_ref, m_ref, v_ref, p_ref, g_ref, d_ref, nm_ref, nv_ref):
        g = p_ref[0].astype(F32)
        for j in range(1, n_parts):
            g = g + p_ref[j].astype(F32)
        g_ref[...] = g
        d_ref[...], nm_ref[...], nv_ref[...] = _adamw_math(w_ref[...], g, m_ref[...], v_ref[...])

    blk = pl.BlockSpec((tr, c), lambda i: (i, 0))
    return pl.pallas_call(
        body, name="adamw_reduce", grid=(r // tr,), out_shape=[jax.ShapeDtypeStruct((r, c), F32)] * 4,
        in_specs=[blk, blk, blk, pl.BlockSpec((n_parts, tr, c), lambda i: (0, i, 0))],
        out_specs=[blk] * 4, compiler_params=_cparams(1),
    )(w, m, v, parts)


def _unshard(g, axis):
    g = jnp.moveaxis(g, 0, axis)
    return g.reshape(g.shape[:axis] + (g.shape[axis] * g.shape[axis + 1],) + g.shape[axis + 2:])


def _shard(full, axis):
    s = full.shape
    g = full.reshape(s[:axis] + (N_DEV, s[axis] // N_DEV) + s[axis + 1:])
    return jnp.moveaxis(g, axis, 0)


def _pad_lanes(a, n=LANES):
    flat = a.reshape(-1)
    pad = (-flat.shape[0]) % n
    return jnp.pad(flat, (0, pad)).reshape(-1, n)


BIG = ("ffn_w_gate", "ffn_w_up", "ffn_w_down", "a_w_in", "a_w_out", "b_w_in", "b_w_out", "c_w_in", "c_w_out")
SHARD_AXIS = {"ffn_w_gate": 3, "ffn_w_up": 3, "ffn_w_down": 2, "a_w_in": 2, "a_w_out": 1, "b_w_in": 2,
              "b_w_out": 1, "c_w_in": 2, "c_w_out": 1, "meta": 1, "ffn_norm": 2, "a_conv": 2, "b_conv": 2}
SMALL_SHARDED = ("meta", "ffn_norm", "a_conv", "b_conv")
SMALL_REPL = ("mix_norm", "b_conv_bias", "b_ln_g", "b_ln_b", "c_b_f", "c_q_norm", "c_k_norm")
WEIGHTS = ("meta", "ffn_norm", "ffn_w_gate", "ffn_w_up", "ffn_w_down", "mix_norm", "a_w_in", "a_conv", "a_w_out",
           "b_w_in", "b_conv", "b_conv_bias", "b_ln_g", "b_ln_b", "b_w_out", "c_w_in", "c_b_f", "c_q_norm",
           "c_k_norm", "c_w_out")
N_MIXERS = 3


def kernel(x, meta, ffn_norm, ffn_w_gate, ffn_w_up, ffn_w_down, mix_norm, a_w_in, a_conv, a_w_out, b_w_in, b_conv, b_conv_bias, b_ln_g, b_ln_b, b_w_out, c_w_in, c_b_f, c_q_norm, c_k_norm, c_w_out, loss_target, m_meta, m_ffn_norm, m_ffn_w_gate, m_ffn_w_up, m_ffn_w_down, m_mix_norm, m_a_w_in, m_a_conv, m_a_w_out, m_b_w_in, m_b_conv, m_b_conv_bias, m_b_ln_g, m_b_ln_b, m_b_w_out, m_c_w_in, m_c_b_f, m_c_q_norm, m_c_k_norm, m_c_w_out, v_meta, v_ffn_norm, v_ffn_w_gate, v_ffn_w_up, v_ffn_w_down, v_mix_norm, v_a_w_in, v_a_conv, v_a_w_out, v_b_w_in, v_b_conv, v_b_conv_bias, v_b_ln_g, v_b_ln_b, v_b_w_out, v_c_w_in, v_c_b_f, v_c_q_norm, v_c_k_norm, v_c_w_out):
    local = dict(locals())
    w = {n: local[n] for n in WEIGHTS}
    mom = {n: local["m_" + n] for n in WEIGHTS}
    var = {n: local["v_" + n] for n in WEIGHTS}
    d = x.shape[-1]
    depth = ffn_norm.shape[0]
    n_heads = d // HEAD_DIM
    seq = x.shape[1]
    t_real = N_META + seq
    tp = -(-t_real // ROW_ALIGN) * ROW_ALIGN
    nb = tp // ATT_TILE
    npair = d // LANES

    names = BIG + SMALL_SHARDED
    gathered = all_gather([w[n].astype(BF16) for n in BIG] + [w[n] for n in SMALL_SHARDED])
    full = {n: _unshard(g, SHARD_AXIS[n]) for n, g in zip(names, gathered)}
    n_c = full["c_w_in"].shape[-1]
    n_cp = 3 * d + LANES
    c_w_qkv = jnp.pad(full["c_w_in"], ((0, 0), (0, 0), (0, n_cp - n_c)))
    b_f_pad = jnp.pad(c_b_f, ((0, 0), (0, LANES - n_heads)))
    qg_t = jnp.tile(c_q_norm, (1, n_heads))
    kg_t = jnp.tile(c_k_norm, (1, n_heads))

    h = jnp.concatenate([full["meta"], x[0], jnp.zeros((tp - t_real, d), F32)], axis=0)
    saved = []
    for i in range(depth):
        mixer, j = i % N_MIXERS, i // N_MIXERS
        s = {"h0": h}
        h, s["gate_a"], s["up_a"] = ffn_fwd(h, full["ffn_norm"][i, 0:1], full["ffn_w_gate"][i, 0],
                                            full["ffn_w_up"][i, 0], full["ffn_w_down"][i, 0])
        s["h1"] = h
        g_mix = mix_norm[i:i + 1]
        if mixer == 0:
            s["p"], s["u"] = norm_matmul(h, g_mix, full["a_w_in"][j], BF16, True)
            s["y"] = a_elem_fwd(s["p"], full["a_conv"][j])
            h = res_matmul(h, s["y"], full["a_w_out"][j])
        elif mixer == 1:
            s["p"], s["u"] = norm_matmul(h, g_mix, full["b_w_in"][j], BF16, True)
            s["y"] = b_elem_fwd(s["p"], full["b_conv"][j], b_conv_bias[j:j + 1], b_ln_g[j:j + 1], b_ln_b[j:j + 1])
            h = res_matmul(h, s["y"], full["b_w_out"][j])
        else:
            s["p"], s["u"] = norm_matmul(h, g_mix, c_w_qkv[j], F32, True)
            s["qp"], s["kp"], s["v"], cum = c_prep_slots(s["p"], b_f_pad[j:j + 1], qg_t[j:j + 1], kg_t[j:j + 1])
            s["cqt"] = (cum[:, :n_heads].T * L2E).reshape(npair, 2, tp)
            s["qpt"] = s["qp"].T
            s["kpt4"] = s["kp"].T.reshape(n_heads, LANES, nb, ATT_TILE).transpose(0, 2, 1, 3)
            vt4 = s["v"].T.reshape(npair, LANES, nb, ATT_TILE).transpose(0, 2, 1, 3)
            ot, s["lse"] = attn_fwd_t(s["qpt"], s["kp"], vt4)
            s["y"] = ot.T
            h = res_matmul(h, s["y"], full["c_w_out"][j])
        s["h2"] = h
        h, s["gate_b"], s["up_b"] = ffn_fwd(h, full["ffn_norm"][i, 1:2], full["ffn_w_gate"][i, 1],
                                            full["ffn_w_up"][i, 1], full["ffn_w_down"][i, 1])
        saved.append(s)

    tgt = jnp.concatenate([jnp.zeros((N_META, d), F32), loss_target[0], jnp.zeros((tp - t_real, d), F32)], axis=0)
    dh, loss_part = loss_head(h, tgt, t_real)

    gfull = {n: [None] * full[n].shape[0] for n in ("a_w_in", "a_w_out", "b_w_in", "b_w_out", "c_w_in", "c_w_out")}
    gffn = {n: [[None, None] for _ in range(depth)] for n in ("ffn_w_gate", "ffn_w_up", "ffn_w_down")}
    g_ffn_norm = [[None, None] for _ in range(depth)]
    g_mix_norm = [None] * depth
    g_a_conv = [None] * a_conv.shape[0]
    g_b_conv = [None] * b_conv.shape[0]
    g_b_stats = [None] * b_conv.shape[0]
    g_c_small = [None] * c_b_f.shape[0]

    def ffn_backward(dh_out, i, half, h_in, gate, up):
        dh_in, hn, dgate, dup, act, dg = ffn_bwd(dh_out, h_in, full["ffn_norm"][i, half:half + 1], gate, up,
                                                 full["ffn_w_gate"][i, half], full["ffn_w_up"][i, half],
                                                 full["ffn_w_down"][i, half])
        gffn["ffn_w_gate"][i][half] = atb(hn, dgate)
        gffn["ffn_w_up"][i][half] = atb(hn, dup)
        gffn["ffn_w_down"][i][half] = atb(act, dh_out, 0.5)
        g_ffn_norm[i][half] = dg
        return dh_in

    for i in reversed(range(depth)):
        mixer, j = i % N_MIXERS, i // N_MIXERS
        s = saved[i]
        g_mix = mix_norm[i:i + 1]
        dh = ffn_backward(dh, i, 1, s["h2"], s["gate_b"], s["up_b"])
        if mixer == 0:
            dy = matmul_nt(dh, full["a_w_out"][j])
            dp, dwc = a_elem_bwd(s["p"], dy, full["a_conv"][j])
            g_a_conv[j] = dwc[:a_conv.shape[1]]
            gfull["a_w_out"][j] = atb(s["y"], dh)
            gfull["a_w_in"][j] = atb(s["u"], dp)
            dh, g_mix_norm[i] = matmul_nt_rms_bwd(dp, full["a_w_in"][j], s["h1"], g_mix, dh)
        elif mixer == 1:
            dy = matmul_nt(dh, full["b_w_out"][j])
            du2, g_b_stats[j] = b_elem_bwd1(s["p"], dy, full["b_conv"][j], b_conv_bias[j:j + 1],
                                            b_ln_g[j:j + 1], b_ln_b[j:j + 1])
            dp, dwc = b_elem_bwd2(s["p"], du2, full["b_conv"][j])
            g_b_conv[j] = dwc[:b_conv.shape[1]]
            gfull["b_w_out"][j] = atb(s["y"], dh)
            gfull["b_w_in"][j] = atb(s["u"], dp)
            dh, g_mix_norm[i] = matmul_nt_rms_bwd(dp, full["b_w_in"][j], s["h1"], g_mix, dh)
        else:
            do = matmul_nt(dh, full["c_w_out"][j])
            dqt, dkp, dv = attn_bwd_t(s["qpt"], s["qp"], s["kp"], s["kpt4"], s["v"], do, do.T, s["lse"], s["cqt"])
            dq = dqt.reshape(n_heads, LANES, tp)[:, :HEAD_DIM].reshape(d, tp).T
            dkp = dkp.reshape(tp, n_heads, LANES)
            dk = dkp[:, :, :HEAD_DIM].reshape(tp, d)
            dlogf = rev_cumsum_rows(-dkp[:, :, AUG + 3].T)
            dlogf = jnp.pad(dlogf.T, ((0, 0), (0, LANES - n_heads)))
            dp, g_c_small[j] = c_elem_bwd(s["p"], b_f_pad[j:j + 1], qg_t[j:j + 1], kg_t[j:j + 1], dq, dk, dv, dlogf)
            gfull["c_w_out"][j] = atb(s["y"], dh)
            gfull["c_w_in"][j] = atb(s["u"], dp)[:, :n_c]
            dh, g_mix_norm[i] = matmul_nt_rms_bwd(dp, c_w_qkv[j], s["h1"], g_mix, dh)
        dh = ffn_backward(dh, i, 0, s["h0"], s["gate_a"], s["up_a"])

    grad_x = dh[N_META:t_real][None]

    gbig = {n: jnp.stack([jnp.stack(r) for r in gffn[n]]) for n in gffn}
    gbig.update({n: jnp.stack(gfull[n]) for n in gfull})
    def by_core(n):
        g = _shard(gbig[n], SHARD_AXIS[n])
        return jnp.moveaxis(g.reshape((2, 2, 2) + g.shape[1:]), 2, 0).reshape((2, N_CHIP) + g.shape[1:])

    halves = [by_core(n) for n in BIG]
    chip_sums = []
    for hv, th in zip(halves, swap_with_sibling(halves)):
        cols = th.shape[-1]
        chip_sums.append(add_own_half(hv.reshape(2, -1, cols), th.reshape(-1, cols)).reshape(th.shape))
    parts = exchange_chips(chip_sums)
    out_g, out_d, out_m, out_v = {}, {}, {}, {}
    for n, prt in zip(BIG, parts):
        shp = w[n].shape
        flat = lambda a: a.reshape(-1, shp[-1])
        res = adamw_reduce(flat(w[n]), flat(mom[n]), flat(var[n]), prt.reshape(N_CHIP, -1, shp[-1]))
        out_g[n], out_d[n], out_m[n], out_v[n] = [r.reshape(shp) for r in res]

    def rows(a):
        return a.reshape(-1, d)

    def lane_rows(a):
        return jnp.pad(a.reshape(1, -1), ((0, 0), (0, d - a.size)))

    c_small = jnp.stack(g_c_small)
    pieces = [("meta", dh[:N_META]),
              ("ffn_norm", rows(jnp.stack([jnp.stack(r) for r in g_ffn_norm]))),
              ("mix_norm", rows(jnp.stack(g_mix_norm))),
              ("a_conv", rows(jnp.stack(g_a_conv))),
              ("b_conv", rows(jnp.stack(g_b_conv))),
              ("b_ln_g", rows(jnp.stack([st[0] for st in g_b_stats]))),
              ("b_ln_b", rows(jnp.stack([st[1] for st in g_b_stats]))),
              ("b_conv_bias", rows(jnp.stack([st[2] for st in g_b_stats]))),
              ("c_q_norm", lane_rows(c_small[:, 0, :HEAD_DIM])),
              ("c_k_norm", lane_rows(c_small[:, 1, :HEAD_DIM])),
              ("c_b_f", lane_rows(c_small[:, 2, :n_heads])),
              ("loss", lane_rows(loss_part[0:1, 0:1]))]
    packed = jnp.concatenate([pc for _, pc in pieces], axis=0)
    n_rows = packed.shape[0]
    packed = jnp.pad(packed, ((0, (-n_rows) % 8), (0, 0)))
    total = sum_devices(all_gather([packed])[0])
    small_g, r0 = {}, 0
    for n, pc in pieces:
        small_g[n] = total[r0:r0 + pc.shape[0]]
        r0 += pc.shape[0]
    loss = small_g.pop("loss")[0, 0]
    me = 4 * lax.axis_index("x") + 2 * lax.axis_index("y") + lax.axis_index("c")
    for n in SMALL_SHARDED:
        cols = w[n].shape[-1]
        g = lax.dynamic_slice_in_dim(small_g[n], me * cols, cols, axis=1)
        out_g[n] = g.reshape(w[n].shape)
    for n in SMALL_REPL:
        out_g[n] = small_g[n].reshape(-1)[:w[n].size].reshape(w[n].shape)
    small = SMALL_SHARDED + SMALL_REPL
    pack = lambda dct: jnp.concatenate([_pad_lanes(dct[n]) for n in small], axis=0)
    res = adamw_small(pack(w), pack(out_g), pack(mom), pack(var))
    r0 = 0
    for n in small:
        nr = -(-w[n].size // LANES)
        for dct, arr in zip((out_d, out_m, out_v), res):
            dct[n] = arr[r0:r0 + nr].reshape(-1)[:w[n].size].reshape(w[n].shape)
        r0 += nr

    return (loss, grad_x, *[out_g[n] for n in WEIGHTS], *[out_d[n] for n in WEIGHTS],
            *[out_m[n] for n in WEIGHTS], *[out_v[n] for n in WEIGHTS])
```

```python
import functools

import jax
import jax.numpy as jnp
from jax import lax
from jax.experimental import pallas as pl
from jax.experimental.pallas import tpu as pltpu

F32 = jnp.float32
BF16 = jnp.bfloat16
HI = lax.Precision.HIGHEST
EPS = 1e-6
NEG = -1e30
N_META = 16
HEAD_DIM = 64
LANES = 128
N_DEV = 8
ROW_ALIGN = 256
VMEM_LIMIT = 56 * 1024 * 1024
ADAM_LR, ADAM_B1, ADAM_B2, ADAM_EPS, ADAM_WD, ADAM_STEP = 0.001, 0.9, 0.999, 1e-08, 0.01, 10
MESH = pl.DeviceIdType.MESH
NT = (((1,), (1,)), ((), ()))
TN = (((0,), (0,)), ((), ()))


def _cparams(n_axes):
    return pltpu.CompilerParams(dimension_semantics=("arbitrary",) * n_axes, vmem_limit_bytes=VMEM_LIMIT)


def _pick(n, cands):
    for c in cands:
        if n % c == 0:
            return c
    raise ValueError(f"no tile for {n} among {cands}")


def _sigmoid(x):
    return 1.0 / (1.0 + jnp.exp(-x))


def _rms(x):
    r = lax.rsqrt(jnp.mean(x * x, axis=-1, keepdims=True) + EPS)
    return x * r, r


def _rms_bwd(xhat, r, g, dy):
    gy = dy * g
    dx = r * (gy - xhat * jnp.mean(gy * xhat, axis=-1, keepdims=True))
    return dx, jnp.sum(dy * xhat, axis=0, keepdims=True)


def _mesh_pos():
    return lax.axis_index("x"), lax.axis_index("y"), lax.axis_index("c")


def all_gather(arrs):
    n = len(arrs)
    hbm = pl.BlockSpec(memory_space=pltpu.HBM)

    def body(*refs):
        ins, outs = refs[:n], refs[n:2 * n]
        send_sems, recv_sems, local_sems = refs[2 * n:]
        x, y, c = _mesh_pos()
        me, sibling = (x, y, c), (x, y, 1 - c)
        xn, yn, dg = (1 - x, y), (x, 1 - y), (1 - x, 1 - y)

        def slot(a, px, py, pc):
            return outs[a].at[4 * px + 2 * py + pc]

        def copy(a, k, block, to, src=None):
            return pltpu.make_async_remote_copy(
                src_ref=slot(a, *block) if src is None else src, dst_ref=slot(a, *block),
                send_sem=send_sems.at[7 * a + k], recv_sem=recv_sems.at[7 * a + k],
                device_id=to, device_id_type=MESH)

        own, sent = [], []

        def start(cp):
            cp.start()
            sent.append(cp)

        for a in range(n):
            cp = pltpu.make_async_copy(ins[a], slot(a, *me), local_sems.at[a])
            cp.start()
            own.append(cp)
        for a in range(n):
            start(copy(a, 0, me, sibling, src=ins[a]))
            start(copy(a, 1, me, (*xn, c), src=ins[a]))
            start(copy(a, 2, me, (*yn, c), src=ins[a]))
        for a in range(n):
            via_x = a % 2 == 0
            copy(a, 1, (*xn, c), me).wait_recv()
            if via_x:
                start(copy(a, 3, (*xn, c), (*yn, c)))
            start(copy(a, 4, (*xn, c), sibling))
            copy(a, 2, (*yn, c), me).wait_recv()
            if not via_x:
                start(copy(a, 3, (*yn, c), (*xn, c)))
            start(copy(a, 5, (*yn, c), sibling))
        for a in range(n):
            copy(a, 3, (*dg, c), me).wait_recv()
            start(copy(a, 6, (*dg, c), sibling))
        for a in range(n):
            copy(a, 0, sibling, me).wait_recv()
            for k, chip in ((4, xn), (5, yn), (6, dg)):
                copy(a, k, (*chip, 1 - c), me).wait_recv()
        for cp in sent:
            cp.wait_send()
        for cp in own:
            cp.wait()

    return pl.pallas_call(
        body, name="all_gather",
        out_shape=[jax.ShapeDtypeStruct((N_DEV,) + a.shape, a.dtype) for a in arrs],
        in_specs=[hbm] * n, out_specs=[hbm] * n,
        scratch_shapes=[pltpu.SemaphoreType.DMA((7 * n,)), pltpu.SemaphoreType.DMA((7 * n,)),
                        pltpu.SemaphoreType.DMA((n,))],
    )(*arrs)


N_CHIP = 4


def swap_with_sibling(arrs):
    n = len(arrs)
    hbm = pl.BlockSpec(memory_space=pltpu.HBM)

    def body(*refs):
        ins, outs = refs[:n], refs[n:2 * n]
        send_sems, recv_sems = refs[2 * n:]
        x, y, c = _mesh_pos()
        copies = [pltpu.make_async_remote_copy(
            src_ref=ins[a].at[1 - c], dst_ref=outs[a], send_sem=send_sems.at[a], recv_sem=recv_sems.at[a],
            device_id=(x, y, 1 - c), device_id_type=MESH) for a in range(n)]
        for cp in copies:
            cp.start()
        for cp in copies:
            cp.wait()

    return pl.pallas_call(
        body, name="swap_with_sibling",
        out_shape=[jax.ShapeDtypeStruct(a.shape[1:], a.dtype) for a in arrs],
        in_specs=[hbm] * n, out_specs=[hbm] * n,
        scratch_shapes=[pltpu.SemaphoreType.DMA((n,)), pltpu.SemaphoreType.DMA((n,))],
    )(*arrs)


def exchange_chips(arrs):
    n = len(arrs)
    hbm = pl.BlockSpec(memory_space=pltpu.HBM)

    def body(*refs):
        ins, outs = refs[:n], refs[n:2 * n]
        send_sems, recv_sems, local_sems = refs[2 * n:]
        x, y, c = _mesh_pos()
        me = 2 * x + y
        own, sent = [], []
        for a in range(n):
            cp = pltpu.make_async_copy(ins[a].at[me], outs[a].at[me], local_sems.at[a])
            cp.start()
            own.append(cp)
        for k in range(1, N_CHIP):
            px = 1 - x if k & 2 else x
            py = 1 - y if k & 1 else y
            peer = 2 * px + py
            for a in range(n):
                cp = pltpu.make_async_remote_copy(
                    src_ref=ins[a].at[peer], dst_ref=outs[a].at[me],
                    send_sem=send_sems.at[3 * a + k - 1], recv_sem=recv_sems.at[3 * a + k - 1],
                    device_id=(px, py, c), device_id_type=MESH)
                cp.start()
                sent.append((cp, a, k, peer))
        for cp, a, k, peer in sent:
            pltpu.make_async_remote_copy(
                src_ref=ins[a].at[peer], dst_ref=outs[a].at[peer],
                send_sem=send_sems.at[3 * a + k - 1], recv_sem=recv_sems.at[3 * a + k - 1],
                device_id=(x, y, c), device_id_type=MESH).wait_recv()
        for cp, a, k, peer in sent:
            cp.wait_send()
        for cp in own:
            cp.wait()

    return pl.pallas_call(
        body, name="exchange_chips",
        out_shape=[jax.ShapeDtypeStruct(a.shape, a.dtype) for a in arrs],
        in_specs=[hbm] * n, out_specs=[hbm] * n,
        scratch_shapes=[pltpu.SemaphoreType.DMA((3 * n,)), pltpu.SemaphoreType.DMA((3 * n,)),
                        pltpu.SemaphoreType.DMA((n,))],
    )(*arrs)


def add_own_half(halves, theirs):
    _, r, c = halves.shape
    tr = _pick(r, (2048, 1408, 1024, 512, 256, 128, 8))

    def body(a_ref, b_ref, o_ref):
        mine = a_ref[lax.axis_index("c")]
        o_ref[...] = (mine.astype(F32) + b_ref[...].astype(F32)).astype(BF16)

    blk = pl.BlockSpec((tr, c), lambda i: (i, 0))
    return pl.pallas_call(
        body, name="add_own_half", grid=(r // tr,), out_shape=jax.ShapeDtypeStruct((r, c), BF16),
        in_specs=[pl.BlockSpec((2, tr, c), lambda i: (0, i, 0)), blk], out_specs=blk, compiler_params=_cparams(1),
    )(halves, theirs)


FFN_CHUNK = (1408, 1024, 512, 256, 128)


def _resident(shape):
    return pl.BlockSpec(shape, lambda i: (0,) * len(shape), pipeline_mode=pl.Buffered(1))


def ffn_fwd(h, g, wg, wu, wd):
    tp, d = h.shape
    f = wg.shape[1]
    tm = _pick(tp, (384, 256))
    tf = _pick(f, FFN_CHUNK)

    def body(h_ref, g_ref, wg_ref, wu_ref, wd_ref, ho_ref, gate_ref, up_ref):
        x = h_ref[...]
        xhat, _ = _rms(x)
        hn = (xhat * g_ref[...]).astype(BF16)
        acc = jnp.zeros((tm, d), F32)
        for c0 in range(0, f, tf):
            gate = jnp.dot(hn, wg_ref[:, c0:c0 + tf], preferred_element_type=F32)
            up = jnp.dot(hn, wu_ref[:, c0:c0 + tf], preferred_element_type=F32)
            gate_ref[:, c0:c0 + tf] = gate.astype(BF16)
            up_ref[:, c0:c0 + tf] = up.astype(BF16)
            act = (gate * _sigmoid(gate) * up).astype(BF16)
            acc = acc + jnp.dot(act, wd_ref[c0:c0 + tf, :], preferred_element_type=F32)
        ho_ref[...] = x + 0.5 * acc

    row = lambda n: pl.BlockSpec((tm, n), lambda i: (i, 0))
    return pl.pallas_call(
        body, name="ffn_fwd", grid=(tp // tm,),
        out_shape=[jax.ShapeDtypeStruct((tp, d), F32), jax.ShapeDtypeStruct((tp, f), BF16),
                   jax.ShapeDtypeStruct((tp, f), BF16)],
        in_specs=[row(d), _resident((1, d)), _resident((d, f)), _resident((d, f)), _resident((f, d))],
        out_specs=[row(d), row(f), row(f)],
        compiler_params=_cparams(1),
    )(h, g, wg, wu, wd)


def ffn_bwd(dho, h, g, gate, up, wg, wu, wd):
    tp, d = h.shape
    f = wg.shape[1]
    tm = _pick(tp, (256,))
    tf = _pick(f, FFN_CHUNK)

    def body(dho_ref, h_ref, g_ref, gate_ref, up_ref, wg_ref, wu_ref, wd_ref,
             dhi_ref, hn_ref, dgate_ref, dup_ref, act_ref, dg_ref):
        @pl.when(pl.program_id(0) == 0)
        def _():
            dg_ref[...] = jnp.zeros_like(dg_ref)

        dho_ = dho_ref[...]
        dout = (0.5 * dho_).astype(BF16)
        dhn = jnp.zeros((tm, d), F32)
        for c0 in range(0, f, tf):
            dact = lax.dot_general(dout, wd_ref[c0:c0 + tf, :], NT, preferred_element_type=F32)
            gt = gate_ref[:, c0:c0 + tf].astype(F32)
            u = up_ref[:, c0:c0 + tf].astype(F32)
            sig = _sigmoid(gt)
            silu = gt * sig
            act_ref[:, c0:c0 + tf] = (silu * u).astype(BF16)
            dup = (dact * silu).astype(BF16)
            dgate = (dact * u * (sig * (1.0 + gt * (1.0 - sig)))).astype(BF16)
            dup_ref[:, c0:c0 + tf] = dup
            dgate_ref[:, c0:c0 + tf] = dgate
            dhn = dhn + (lax.dot_general(dgate, wg_ref[:, c0:c0 + tf], NT, preferred_element_type=F32)
                         + lax.dot_general(dup, wu_ref[:, c0:c0 + tf], NT, preferred_element_type=F32))
        xhat, r = _rms(h_ref[...])
        gg = g_ref[...]
        dx, dgp = _rms_bwd(xhat, r, gg, dhn)
        dg_ref[...] += dgp
        dhi_ref[...] = dho_ + dx
        hn_ref[...] = (xhat * gg).astype(BF16)

    row = lambda n: pl.BlockSpec((tm, n), lambda i: (i, 0))
    return pl.pallas_call(
        body, name="ffn_bwd", grid=(tp // tm,),
        out_shape=[jax.ShapeDtypeStruct((tp, d), F32), jax.ShapeDtypeStruct((tp, d), BF16),
                   jax.ShapeDtypeStruct((tp, f), BF16), jax.ShapeDtypeStruct((tp, f), BF16),
                   jax.ShapeDtypeStruct((tp, f), BF16), jax.ShapeDtypeStruct((1, d), F32)],
        in_specs=[row(d), row(d), _resident((1, d)), row(f), row(f),
                  _resident((d, f)), _resident((d, f)), _resident((f, d))],
        out_specs=[row(d), row(d), row(f), row(f), row(f), pl.BlockSpec((1, d), lambda i: (0, 0))],
        compiler_params=_cparams(1),
    )(dho, h, g, gate, up, wg, wu, wd)


def atb(a, b, scale=1.0):
    tp, m = a.shape
    n = b.shape[1]
    tmm = _pick(m, (1024, 1408, 640, 512, 256, 128))
    tn = _pick(n, (1024, 1408, 640, 512, 256, 128))
    tk = _pick(tp, (1408, 768, 512, 256))
    nk = tp // tk

    def body(a_ref, b_ref, o_ref, acc):
        k = pl.program_id(2)

        @pl.when(k == 0)
        def _():
            acc[...] = jnp.zeros_like(acc)

        acc[...] += lax.dot_general(a_ref[...].astype(BF16), b_ref[...].astype(BF16), TN,
                                    preferred_element_type=F32)

        @pl.when(k == nk - 1)
        def _():
            o_ref[...] = (acc[...] * scale).astype(BF16)

    return pl.pallas_call(
        body, name="atb", grid=(m // tmm, n // tn, nk),
        out_shape=jax.ShapeDtypeStruct((m, n), BF16),
        in_specs=[pl.BlockSpec((tk, tmm), lambda i, j, k: (k, i)), pl.BlockSpec((tk, tn), lambda i, j, k: (k, j))],
        out_specs=pl.BlockSpec((tmm, tn), lambda i, j, k: (i, j)),
        scratch_shapes=[pltpu.VMEM((tmm, tn), F32)],
        compiler_params=_cparams(3),
    )(a, b)


def norm_matmul(h, g, w, out_dtype, emit_u):
    tp, d = h.shape
    n = w.shape[1]
    tm = _pick(tp, (768, 512, 256))
    tn = _pick(n, (1024, 768, 640, 512, 256, 128))

    def body(h_ref, g_ref, w_ref, o_ref, *rest):
        u_sc = rest[-1]

        @pl.when(pl.program_id(1) == 0)
        def _():
            xhat, _ = _rms(h_ref[...])
            u_sc[...] = (xhat * g_ref[...]).astype(BF16)
            if emit_u:
                rest[0][...] = u_sc[...]

        o_ref[...] = jnp.dot(u_sc[...], w_ref[...], preferred_element_type=F32).astype(out_dtype)

    out_shape = [jax.ShapeDtypeStruct((tp, n), out_dtype)]
    out_specs = [pl.BlockSpec((tm, tn), lambda i, j: (i, j))]
    if emit_u:
        out_shape.append(jax.ShapeDtypeStruct((tp, d), BF16))
        out_specs.append(pl.BlockSpec((tm, d), lambda i, j: (i, 0)))
    return pl.pallas_call(
        body, name="norm_matmul", grid=(tp // tm, n // tn), out_shape=out_shape,
        in_specs=[pl.BlockSpec((tm, d), lambda i, j: (i, 0)), pl.BlockSpec((1, d), lambda i, j: (0, 0)),
                  pl.BlockSpec((d, tn), lambda i, j: (0, j))],
        out_specs=out_specs, scratch_shapes=[pltpu.VMEM((tm, d), BF16)],
        compiler_params=_cparams(2),
    )(h, g, w)


def res_matmul(h, y, w):
    tp, d = h.shape
    k = y.shape[1]
    tm = _pick(tp, (768, 512, 256))

    def body(h_ref, y_ref, w_ref, o_ref):
        o_ref[...] = h_ref[...] + jnp.dot(y_ref[...], w_ref[...], preferred_element_type=F32)

    return pl.pallas_call(
        body, name="res_matmul", grid=(tp // tm,), out_shape=jax.ShapeDtypeStruct((tp, d), F32),
        in_specs=[pl.BlockSpec((tm, d), lambda i: (i, 0)), pl.BlockSpec((tm, k), lambda i: (i, 0)),
                  pl.BlockSpec((k, d), lambda i: (0, 0))],
        out_specs=pl.BlockSpec((tm, d), lambda i: (i, 0)),
        compiler_params=_cparams(1),
    )(h, y, w)


def matmul_nt(a, w):
    tp, d = a.shape
    k = w.shape[0]
    tm = _pick(tp, (768, 512, 256))

    def body(a_ref, w_ref, o_ref):
        o_ref[...] = lax.dot_general(a_ref[...].astype(BF16), w_ref[...], NT,
                                     preferred_element_type=F32).astype(BF16)

    return pl.pallas_call(
        body, name="matmul_nt", grid=(tp // tm,), out_shape=jax.ShapeDtypeStruct((tp, k), BF16),
        in_specs=[pl.BlockSpec((tm, d), lambda i: (i, 0)), pl.BlockSpec((k, d), lambda i: (0, 0))],
        out_specs=pl.BlockSpec((tm, k), lambda i: (i, 0)),
        compiler_params=_cparams(1),
    )(a, w)


def matmul_nt_rms_bwd(dp, w, h, g, dres):
    tp, d = h.shape
    n = w.shape[1]
    tm = _pick(tp, (384, 256))

    def body(dp_ref, w_ref, h_ref, g_ref, dres_ref, dh_ref, dg_ref):
        @pl.when(pl.program_id(0) == 0)
        def _():
            dg_ref[...] = jnp.zeros_like(dg_ref)

        du = lax.dot_general(dp_ref[...], w_ref[...], NT, preferred_element_type=F32)
        xhat, r = _rms(h_ref[...])
        dx, dgp = _rms_bwd(xhat, r, g_ref[...], du)
        dg_ref[...] += dgp
        dh_ref[...] = dres_ref[...] + dx

    return pl.pallas_call(
        body, name="matmul_nt_rms_bwd", grid=(tp // tm,),
        out_shape=[jax.ShapeDtypeStruct((tp, d), F32), jax.ShapeDtypeStruct((1, d), F32)],
        in_specs=[pl.BlockSpec((tm, n), lambda i: (i, 0)), pl.BlockSpec((d, n), lambda i: (0, 0)),
                  pl.BlockSpec((tm, d), lambda i: (i, 0)), pl.BlockSpec((1, d), lambda i: (0, 0)),
                  pl.BlockSpec((tm, d), lambda i: (i, 0))],
        out_specs=[pl.BlockSpec((tm, d), lambda i: (i, 0)), pl.BlockSpec((1, d), lambda i: (0, 0))],
        compiler_params=_cparams(1),
    )(dp, w, h, g, dres)


CONV_TILE = (256,)


def _halo_before(tm, hb, col=0):
    return lambda i: (jnp.maximum(i * (tm // hb) - 1, 0), col)


def _halo_after(tm, hb, nblk, col=0):
    return lambda i: (jnp.minimum((i + 1) * (tm // hb), nblk - 1), col)


def a_elem_fwd(p, w):
    tp, d3 = p.shape
    d = d3 // 3
    kw = w.shape[0]
    tm = _pick(tp, CONV_TILE)
    hb = 16

    def body(p_ref, ph_ref, w_ref, y_ref, ext):
        i = pl.program_id(0)
        halo = ph_ref[:, d:2 * d].astype(F32) * ph_ref[:, 2 * d:].astype(F32)
        ext[0:hb, :] = jnp.where(i > 0, halo, 0.0)
        ext[hb:, :] = p_ref[:, d:2 * d].astype(F32) * p_ref[:, 2 * d:].astype(F32)
        z = jnp.zeros((tm, d), F32)
        for k in range(kw):
            z = z + w_ref[k:k + 1, :] * ext[pl.ds(hb - (kw - 1) + k, tm), :]
        y_ref[...] = (p_ref[:, 0:d].astype(F32) * z).astype(BF16)

    return pl.pallas_call(
        body, name="a_elem_fwd", grid=(tp // tm,), out_shape=jax.ShapeDtypeStruct((tp, d), BF16),
        in_specs=[pl.BlockSpec((tm, d3), lambda i: (i, 0)), pl.BlockSpec((hb, d3), _halo_before(tm, hb)),
                  pl.BlockSpec((kw, d), lambda i: (0, 0))],
        out_specs=pl.BlockSpec((tm, d), lambda i: (i, 0)),
        scratch_shapes=[pltpu.VMEM((tm + hb, d), F32)],
        compiler_params=_cparams(1),
    )(p, p, w)


def a_elem_bwd(p, dy, w):
    tp, d3 = p.shape
    d = d3 // 3
    kw = w.shape[0]
    tm = _pick(tp, CONV_TILE)
    hb = 16
    n = tp // tm

    def body(p_ref, ph_ref, pa_ref, dy_ref, dya_ref, w_ref, dp_ref, dw_ref, ext, dzext):
        i = pl.program_id(0)

        @pl.when(i == 0)
        def _():
            dw_ref[...] = jnp.zeros_like(dw_ref)

        b = p_ref[:, 0:d].astype(F32)
        c = p_ref[:, d:2 * d].astype(F32)
        v = p_ref[:, 2 * d:].astype(F32)
        dy_ = dy_ref[...].astype(F32)
        halo = ph_ref[:, d:2 * d].astype(F32) * ph_ref[:, 2 * d:].astype(F32)
        ext[0:hb, :] = jnp.where(i > 0, halo, 0.0)
        ext[hb:, :] = c * v
        dz = dy_ * b
        dzext[0:tm, :] = dz
        dzext[tm:, :] = jnp.where(i < n - 1, dya_ref[...].astype(F32) * pa_ref[:, 0:d].astype(F32), 0.0)
        z = jnp.zeros((tm, d), F32)
        dcv = jnp.zeros((tm, d), F32)
        for k in range(kw):
            sh = ext[pl.ds(hb - (kw - 1) + k, tm), :]
            z = z + w_ref[k:k + 1, :] * sh
            dw_ref[k:k + 1, :] += jnp.sum(dz * sh, axis=0, keepdims=True)
            dcv = dcv + w_ref[k:k + 1, :] * dzext[pl.ds(kw - 1 - k, tm), :]
        dp_ref[:, 0:d] = (dy_ * z).astype(BF16)
        dp_ref[:, d:2 * d] = (dcv * v).astype(BF16)
        dp_ref[:, 2 * d:] = (dcv * c).astype(BF16)

    nb = tp // hb
    return pl.pallas_call(
        body, name="a_elem_bwd", grid=(n,),
        out_shape=[jax.ShapeDtypeStruct((tp, d3), BF16), jax.ShapeDtypeStruct((8, d), F32)],
        in_specs=[pl.BlockSpec((tm, d3), lambda i: (i, 0)), pl.BlockSpec((hb, d3), _halo_before(tm, hb)),
                  pl.BlockSpec((hb, d3), _halo_after(tm, hb, nb)),
                  pl.BlockSpec((tm, d), lambda i: (i, 0)), pl.BlockSpec((hb, d), _halo_after(tm, hb, nb)),
                  pl.BlockSpec((kw, d), lambda i: (0, 0))],
        out_specs=[pl.BlockSpec((tm, d3), lambda i: (i, 0)), pl.BlockSpec((8, d), lambda i: (0, 0))],
        scratch_shapes=[pltpu.VMEM((tm + hb, d), F32), pltpu.VMEM((tm + hb, d), F32)],
        compiler_params=_cparams(1),
    )(p, p, p, dy, dy, w)


CH_R, CH_C = 128, 128


def _chunks(tm, d):
    return [(r0, c0) for c0 in range(0, d, CH_C) for r0 in range(0, tm, CH_R)]


def _b_u2(p_ref, ph_ref, w_ref, bias_ref, ext, u2_sc, i, tm, d, kw, hb):
    a = p_ref[:, 0:d].astype(F32)
    sg = _sigmoid(p_ref[:, d:].astype(F32))
    halo = ph_ref[:, 0:d].astype(F32) * _sigmoid(ph_ref[:, d:].astype(F32))
    ext[0:hb, :] = jnp.where(i > 0, halo, 0.0)
    ext[hb:, :] = a * sg
    for r0, c0 in _chunks(tm, d):
        cs = slice(c0, c0 + CH_C)
        acc = jnp.zeros((CH_R, CH_C), F32) + bias_ref[:, cs]
        for k in range(kw):
            acc = acc + w_ref[k:k + 1, cs] * ext[pl.ds(r0 + hb - (kw - 1) + k, CH_R), cs]
        u2_sc[r0:r0 + CH_R, cs] = acc
    return a, sg, u2_sc[...]


def _ln(u2):
    mu = jnp.mean(u2, axis=-1, keepdims=True)
    xc = u2 - mu
    rstd = lax.rsqrt(jnp.mean(xc * xc, axis=-1, keepdims=True) + EPS)
    return xc * rstd, rstd


def b_elem_fwd(p, w, bias, ln_g, ln_b):
    tp, d2 = p.shape
    d = d2 // 2
    kw = w.shape[0]
    tm = _pick(tp, CONV_TILE)
    hb = 32

    def body(p_ref, ph_ref, w_ref, bias_ref, g_ref, b_ref, y_ref, ext, u2_sc):
        i = pl.program_id(0)
        _, _, u2 = _b_u2(p_ref, ph_ref, w_ref, bias_ref, ext, u2_sc, i, tm, d, kw, hb)
        xhat, _ = _ln(u2)
        u3 = xhat * g_ref[...] + b_ref[...]
        y_ref[...] = (u3 * _sigmoid(u3)).astype(BF16)

    vec = pl.BlockSpec((1, d), lambda i: (0, 0))
    return pl.pallas_call(
        body, name="b_elem_fwd", grid=(tp // tm,), out_shape=jax.ShapeDtypeStruct((tp, d), BF16),
        in_specs=[pl.BlockSpec((tm, d2), lambda i: (i, 0)), pl.BlockSpec((hb, d2), _halo_before(tm, hb)),
                  pl.BlockSpec((kw, d), lambda i: (0, 0)), vec, vec, vec],
        out_specs=pl.BlockSpec((tm, d), lambda i: (i, 0)),
        scratch_shapes=[pltpu.VMEM((tm + hb, d), F32), pltpu.VMEM((tm, d), F32)],
        compiler_params=_cparams(1),
    )(p, p, w, bias, ln_g, ln_b)


def b_elem_bwd1(p, dy, w, bias, ln_g, ln_b):
    tp, d2 = p.shape
    d = d2 // 2
    kw = w.shape[0]
    tm = _pick(tp, CONV_TILE)
    hb = 32

    def body(p_ref, ph_ref, dy_ref, w_ref, bias_ref, g_ref, b_ref, du2_ref, st_ref, ext, u2_sc):
        i = pl.program_id(0)

        @pl.when(i == 0)
        def _():
            st_ref[...] = jnp.zeros_like(st_ref)

        _, _, u2 = _b_u2(p_ref, ph_ref, w_ref, bias_ref, ext, u2_sc, i, tm, d, kw, hb)
        xhat, rstd = _ln(u2)
        u3 = xhat * g_ref[...] + b_ref[...]
        s3 = _sigmoid(u3)
        du3 = dy_ref[...].astype(F32) * (s3 * (1.0 + u3 * (1.0 - s3)))
        dxh = du3 * g_ref[...]
        du2 = rstd * (dxh - jnp.mean(dxh, axis=-1, keepdims=True)
                      - xhat * jnp.mean(dxh * xhat, axis=-1, keepdims=True))
        du2_ref[...] = du2
        st_ref[0:1, :] += jnp.sum(du3 * xhat, axis=0, keepdims=True)
        st_ref[1:2, :] += jnp.sum(du3, axis=0, keepdims=True)
        st_ref[2:3, :] += jnp.sum(du2, axis=0, keepdims=True)

    vec = pl.BlockSpec((1, d), lambda i: (0, 0))
    return pl.pallas_call(
        body, name="b_elem_bwd1", grid=(tp // tm,),
        out_shape=[jax.ShapeDtypeStruct((tp, d), F32), jax.ShapeDtypeStruct((8, d), F32)],
        in_specs=[pl.BlockSpec((tm, d2), lambda i: (i, 0)), pl.BlockSpec((hb, d2), _halo_before(tm, hb)),
                  pl.BlockSpec((tm, d), lambda i: (i, 0)), pl.BlockSpec((kw, d), lambda i: (0, 0)), vec, vec, vec],
        out_specs=[pl.BlockSpec((tm, d), lambda i: (i, 0)), pl.BlockSpec((8, d), lambda i: (0, 0))],
        scratch_shapes=[pltpu.VMEM((tm + hb, d), F32), pltpu.VMEM((tm, d), F32)],
        compiler_params=_cparams(1),
    )(p, p, dy, w, bias, ln_g, ln_b)


def b_elem_bwd2(p, du2, w):
    tp, d2 = p.shape
    d = d2 // 2
    kw = w.shape[0]
    tm = _pick(tp, CONV_TILE)
    hb = 32
    n = tp // tm

    def body(p_ref, ph_ref, du2_ref, du2a_ref, w_ref, dp_ref, dw_ref, ext, dext, dwacc):
        i = pl.program_id(0)

        @pl.when(i == 0)
        def _():
            dwacc[...] = jnp.zeros_like(dwacc)

        halo = ph_ref[:, 0:d].astype(F32) * _sigmoid(ph_ref[:, d:].astype(F32))
        ext[0:hb, :] = jnp.where(i > 0, halo, 0.0)
        ext[hb:, :] = p_ref[:, 0:d].astype(F32) * _sigmoid(p_ref[:, d:].astype(F32))
        dext[0:tm, :] = du2_ref[...]
        dext[tm:, :] = jnp.where(i < n - 1, du2a_ref[...], 0.0)
        for r0, c0 in _chunks(tm, d):
            cs = slice(c0, c0 + CH_C)
            du2c = du2_ref[r0:r0 + CH_R, cs]
            du1 = jnp.zeros((CH_R, CH_C), F32)
            for k in range(kw):
                prod = du2c * ext[pl.ds(r0 + hb - (kw - 1) + k, CH_R), cs]
                part = prod[0:8]
                for r in range(8, CH_R, 8):
                    part = part + prod[r:r + 8]
                dwacc[8 * k:8 * k + 8, cs] += part
                du1 = du1 + w_ref[k:k + 1, cs] * dext[pl.ds(r0 + kw - 1 - k, CH_R), cs]
            a = p_ref[r0:r0 + CH_R, cs].astype(F32)
            sg = _sigmoid(p_ref[r0:r0 + CH_R, d + c0:d + c0 + CH_C].astype(F32))
            dp_ref[r0:r0 + CH_R, cs] = (du1 * sg).astype(BF16)
            dp_ref[r0:r0 + CH_R, d + c0:d + c0 + CH_C] = (du1 * a * sg * (1.0 - sg)).astype(BF16)

        @pl.when(i == n - 1)
        def _():
            dw_ref[...] = jnp.zeros_like(dw_ref)
            for k in range(kw):
                dw_ref[k:k + 1, :] = jnp.sum(dwacc[8 * k:8 * k + 8, :], axis=0, keepdims=True)

    nb = tp // hb
    return pl.pallas_call(
        body, name="b_elem_bwd2", grid=(n,),
        out_shape=[jax.ShapeDtypeStruct((tp, d2), BF16), jax.ShapeDtypeStruct((32, d), F32)],
        in_specs=[pl.BlockSpec((tm, d2), lambda i: (i, 0)), pl.BlockSpec((hb, d2), _halo_before(tm, hb)),
                  pl.BlockSpec((tm, d), lambda i: (i, 0)), pl.BlockSpec((hb, d), _halo_after(tm, hb, nb)),
                  pl.BlockSpec((kw, d), lambda i: (0, 0))],
        out_specs=[pl.BlockSpec((tm, d2), lambda i: (i, 0)), pl.BlockSpec((32, d), lambda i: (0, 0))],
        scratch_shapes=[pltpu.VMEM((tm + hb, d), F32), pltpu.VMEM((tm + hb, d), F32), pltpu.VMEM((8 * kw, d), F32)],
        compiler_params=_cparams(1),
    )(p, p, du2, du2, w)


ATT_TILE = 256


def _head_masks(d):
    c = lax.broadcasted_iota(jnp.int32, (d, LANES), 0)
    h = lax.broadcasted_iota(jnp.int32, (d, LANES), 1)
    seg = (c // HEAD_DIM == h).astype(F32)
    fold = (c % HEAD_DIM == h).astype(F32)
    return seg, seg.T, fold


L2E = 1.4426950408889634
LN2 = 0.6931471805599453
AUG = HEAD_DIM


def _split3(r):
    r1 = r.astype(BF16).astype(F32)
    r2 = (r - r1).astype(BF16).astype(F32)
    r3 = (r - r1 - r2).astype(BF16).astype(F32)
    return r1, r2, r3


def c_prep_slots(p, b_f, qg, kg):
    tp = p.shape[0]
    d = qg.shape[1]
    n_heads = d // HEAD_DIM
    tm = ATT_TILE
    seg, seg_t, _ = _head_masks(d)

    def body(p_ref, pf_ref, bf_ref, qg_ref, kg_ref, seg_ref, segt_ref, q_ref, k_ref, v_ref, cum_ref, carry):
        @pl.when(pl.program_id(0) == 0)
        def _():
            carry[...] = jnp.zeros_like(carry)

        def norm(x, g):
            ms = jnp.dot(x * x, seg_ref[...], precision=HI, preferred_element_type=F32) * (1.0 / HEAD_DIM)
            r = jnp.dot(lax.rsqrt(ms + EPS), segt_ref[...], precision=HI, preferred_element_type=F32)
            return x * r * g

        qn = norm(p_ref[:, 0:d], qg_ref[...]) * (HEAD_DIM ** -0.5 * L2E)
        kn = norm(p_ref[:, d:2 * d], kg_ref[...])
        v_ref[...] = p_ref[:, 2 * d:].astype(BF16)
        xf = pf_ref[...] + bf_ref[...]
        logf = jnp.minimum(xf, 0.0) - jnp.log(1.0 + jnp.exp(-jnp.abs(xf)))
        r_ = lax.broadcasted_iota(jnp.int32, (tm, tm), 0)
        c_ = lax.broadcasted_iota(jnp.int32, (tm, tm), 1)
        cum = jnp.dot((c_ <= r_).astype(F32), logf, precision=HI, preferred_element_type=F32) + carry[0:1, :]
        cum_ref[...] = cum
        carry[0:1, :] += jnp.sum(logf, axis=0, keepdims=True)

        lane = lax.broadcasted_iota(jnp.int32, (1, LANES), 1)
        ones_q = jnp.where((lane >= AUG + 3) & (lane < AUG + 6), 1.0, 0.0)
        ones_k = jnp.where((lane >= AUG) & (lane < AUG + 3), 1.0, 0.0)
        for h in range(n_heads):
            c0 = LANES * (h // 2)
            src_q, src_k = qn[:, c0:c0 + LANES], kn[:, c0:c0 + LANES]
            if h % 2:
                src_q = pltpu.roll(src_q, HEAD_DIM, axis=1)
                src_k = pltpu.roll(src_k, HEAD_DIM, axis=1)
            c = cum[:, h:h + 1] * L2E
            a1, a2, a3 = _split3(c)
            b1, b2, b3 = _split3(-c)
            aug_q = jnp.where(lane == AUG, a1, jnp.where(lane == AUG + 1, a2, jnp.where(lane == AUG + 2, a3, ones_q)))
            aug_k = jnp.where(lane == AUG + 3, b1,
                              jnp.where(lane == AUG + 4, b2, jnp.where(lane == AUG + 5, b3, ones_k)))
            q_ref[:, LANES * h:LANES * (h + 1)] = jnp.where(lane < HEAD_DIM, src_q, aug_q).astype(BF16)
            k_ref[:, LANES * h:LANES * (h + 1)] = jnp.where(lane < HEAD_DIM, src_k, aug_k).astype(BF16)

    vec = pl.BlockSpec((1, d), lambda i: (0, 0))
    rowd = pl.BlockSpec((tm, d), lambda i: (i, 0))
    slots = pl.BlockSpec((tm, n_heads * LANES), lambda i: (i, 0))
    return pl.pallas_call(
        body, name="c_prep_slots", grid=(tp // tm,),
        out_shape=[jax.ShapeDtypeStruct((tp, n_heads * LANES), BF16)] * 2
        + [jax.ShapeDtypeStruct((tp, d), BF16), jax.ShapeDtypeStruct((tp, LANES), F32)],
        in_specs=[pl.BlockSpec((tm, 3 * d), lambda i: (i, 0)), pl.BlockSpec((tm, LANES), lambda i: (i, 3 * d // LANES)),
                  pl.BlockSpec((1, LANES), lambda i: (0, 0)), vec, vec,
                  pl.BlockSpec((d, LANES), lambda i: (0, 0)), pl.BlockSpec((LANES, d), lambda i: (0, 0))],
        out_specs=[slots, slots, rowd, pl.BlockSpec((tm, LANES), lambda i: (i, 0))],
        scratch_shapes=[pltpu.VMEM((8, LANES), F32)],
        compiler_params=_cparams(1),
    )(p, p, b_f, qg, kg, seg, seg_t)


ATT_GROUPS = (8, 2)


def _grouped_loop(n_blocks, body, carry):
    start = 0
    for g in ATT_GROUPS:
        count = (n_blocks - start) // g
        carry = lax.fori_loop(0, count, lambda i, c, g=g, start=start: body(start + i * g, g, c), carry)
        start = start + count * g
    return lax.fori_loop(start, n_blocks, lambda kj, c: body(kj, 1, c), carry)


def _diag_mask(t):
    return lax.broadcasted_iota(jnp.int32, (t, t), 0) <= lax.broadcasted_iota(jnp.int32, (t, t), 1)


def attn_fwd_t(qpt, kp, vt4):
    tp = kp.shape[0]
    t = ATT_TILE
    nb = tp // t
    npair = vt4.shape[0]

    def body(q_ref, k_ref, v_ref, o_ref, lse_ref):
        qi = pl.program_id(1)
        qts = (q_ref[0:LANES, :], q_ref[LANES:2 * LANES, :])
        mask = _diag_mask(t)

        def blocks(kj0, n, carry, masked=False):
            out = []
            for hd in range(2):
                m, l, acc = carry[3 * hd:3 * hd + 3]
                sts = []
                for g in range(n):
                    off = pl.multiple_of((kj0 + g) * t, t)
                    st = jnp.dot(k_ref[pl.ds(off, t), LANES * hd:LANES * (hd + 1)], qts[hd],
                                 preferred_element_type=F32)
                    sts.append(jnp.where(mask, st, NEG) if masked else st)
                m_new = m
                for st in sts:
                    m_new = jnp.maximum(m_new, jnp.max(st, axis=0, keepdims=True))
                alpha = jnp.exp2(m - m_new)
                l = alpha * l
                acc = alpha * acc
                for g, st in enumerate(sts):
                    pt = jnp.exp2(st - m_new)
                    l = l + jnp.sum(pt, axis=0, keepdims=True)
                    acc = acc + jnp.dot(v_ref[0, kj0 + g], pt.astype(BF16), preferred_element_type=F32)
                out += [m_new, l, acc]
            return tuple(out)

        init = (jnp.full((1, t), NEG, F32), jnp.zeros((1, t), F32), jnp.zeros((LANES, t), F32)) * 2
        carry = _grouped_loop(qi, blocks, init)
        ma, la, acca, mb, lb, accb = blocks(qi, 1, carry, masked=True)
        row = lax.broadcasted_iota(jnp.int32, (LANES, 1), 0)
        o_ref[...] = jnp.where(row < HEAD_DIM, acca / la, accb / lb).astype(BF16)
        lse_ref[0, 0:1, :] = ma + jnp.log(la) * L2E
        lse_ref[0, 1:2, :] = mb + jnp.log(lb) * L2E

    return pl.pallas_call(
        body, name="attn_fwd_t", grid=(npair, nb),
        out_shape=[jax.ShapeDtypeStruct((npair * LANES, tp), BF16), jax.ShapeDtypeStruct((npair, 2, tp), F32)],
        in_specs=[pl.BlockSpec((2 * LANES, t), lambda h, i: (h, i)), pl.BlockSpec((tp, 2 * LANES), lambda h, i: (0, h)),
                  pl.BlockSpec((1, nb, LANES, t), lambda h, i: (h, 0, 0, 0))],
        out_specs=[pl.BlockSpec((LANES, t), lambda h, i: (h, i)), pl.BlockSpec((1, 2, t), lambda h, i: (h, 0, i))],
        compiler_params=_cparams(2),
    )(qpt, kp, vt4)


def attn_bwd_t(qpt, qp, kp, kpt4, v, do, dot_, lse, cqt):
    tp = kp.shape[0]
    t = ATT_TILE
    nb = tp // t
    n_heads = kpt4.shape[0]
    d = v.shape[1]

    def body(qt_ref, q_ref, k_ref, kt_ref, v_ref, do_ref, dot_ref, lse_ref, cq_ref, dq_ref, dk_ref, dv_ref,
             p_sc, dp_sc, dc_sc):
        hd = pl.program_id(0) % 2
        qi = pl.program_id(1)

        @pl.when(qi == 0)
        def _():
            dk_ref[...] = jnp.zeros_like(dk_ref)
            dc_sc[...] = jnp.zeros_like(dc_sc)

        @pl.when((qi == 0) & (hd == 0))
        def _():
            dv_ref[...] = jnp.zeros_like(dv_ref)

        row = lax.broadcasted_iota(jnp.int32, (LANES, 1), 0)
        lane = lax.broadcasted_iota(jnp.int32, (1, LANES), 1)
        dot_h = jnp.where(row // HEAD_DIM == hd, dot_ref[...], jnp.zeros_like(dot_ref))
        do_h = jnp.where(lane // HEAD_DIM == hd, do_ref[...], jnp.zeros_like(do_ref))
        rr = cq_ref[0] - lse_ref[0]
        r1, r2, r3 = _split3(jnp.where(hd == 0, rr[0:1, :], rr[1:2, :]))
        qt = qt_ref[...].astype(F32)
        qt = jnp.where(row == AUG, r1, jnp.where(row == AUG + 1, r2, jnp.where(row == AUG + 2, r3, qt))).astype(BF16)
        mask = _diag_mask(t)

        def pass1(kj0, n, delta, masked=False):
            for g in range(n):
                off = pl.multiple_of((kj0 + g) * t, t)
                st = jnp.dot(k_ref[pl.ds(off, t), :], qt, preferred_element_type=F32)
                if masked:
                    st = jnp.where(mask, st, NEG)
                pt = jnp.exp2(st)
                dpt = jnp.dot(v_ref[pl.ds(off, t), :], dot_h, preferred_element_type=F32)
                p_sc[kj0 + g] = pt
                dp_sc[kj0 + g] = dpt
                delta = delta + jnp.sum(pt * dpt, axis=0, keepdims=True)
            return delta

        delta = _grouped_loop(qi, pass1, jnp.zeros((1, t), F32))
        delta = pass1(qi, 1, delta, masked=True)

        def pass2(kj0, n, dq):
            for g in range(n):
                off = pl.multiple_of((kj0 + g) * t, t)
                pt = p_sc[kj0 + g]
                ds32 = pt * (dp_sc[kj0 + g] - delta)
                ds = ds32.astype(BF16)
                dc_sc[pl.ds(off, t), :] += ds32[:, 0:LANES] + ds32[:, LANES:2 * LANES]
                dk_ref[pl.ds(off, t), :] += jnp.dot(ds, q_ref[...], preferred_element_type=F32)
                dv_ref[pl.ds(off, t), :] += jnp.dot(pt.astype(BF16), do_h, preferred_element_type=F32)
                dq = dq + jnp.dot(kt_ref[0, kj0 + g], ds, preferred_element_type=F32)
            return dq

        dq_ref[...] = _grouped_loop(qi + 1, pass2, jnp.zeros((LANES, t), F32))

        @pl.when(qi == nb - 1)
        def _():
            dk_ref[:, AUG + 3:AUG + 4] = jnp.sum(dc_sc[...], axis=1, keepdims=True)

    once = dict(pipeline_mode=pl.Buffered(1))
    pair_rows = pl.BlockSpec((1, 2, t), lambda h, i: (h // 2, 0, i))
    return pl.pallas_call(
        body, name="attn_bwd_t", grid=(n_heads, nb),
        out_shape=[jax.ShapeDtypeStruct((n_heads * LANES, tp), F32), jax.ShapeDtypeStruct((tp, n_heads * LANES), F32),
                   jax.ShapeDtypeStruct((tp, d), F32)],
        in_specs=[pl.BlockSpec((LANES, t), lambda h, i: (h, i)), pl.BlockSpec((t, LANES), lambda h, i: (i, h)),
                  pl.BlockSpec((tp, LANES), lambda h, i: (0, h), **once),
                  pl.BlockSpec((1, nb, LANES, t), lambda h, i: (h, 0, 0, 0), **once),
                  pl.BlockSpec((tp, LANES), lambda h, i: (0, h // 2), **once),
                  pl.BlockSpec((t, LANES), lambda h, i: (i, h // 2)), pl.BlockSpec((LANES, t), lambda h, i: (h // 2, i)),
                  pair_rows, pair_rows],
        out_specs=[pl.BlockSpec((LANES, t), lambda h, i: (h, i)), pl.BlockSpec((tp, LANES), lambda h, i: (0, h)),
                   pl.BlockSpec((tp, LANES), lambda h, i: (0, h // 2))],
        scratch_shapes=[pltpu.VMEM((nb, t, t), F32), pltpu.VMEM((nb, t, t), F32), pltpu.VMEM((tp, LANES), F32)],
        compiler_params=_cparams(2),
    )(qpt, qp, kp, kpt4, v, do, dot_, lse, cqt)


def rev_cumsum_rows(x):
    r, tp = x.shape
    t = ATT_TILE
    nb = tp // t

    def body(x_ref, o_ref, carry):
        @pl.when(pl.program_id(0) == 0)
        def _():
            carry[...] = jnp.zeros_like(carry)

        xv = x_ref[...]
        r_ = lax.broadcasted_iota(jnp.int32, (t, t), 0)
        c_ = lax.broadcasted_iota(jnp.int32, (t, t), 1)
        o_ref[...] = jnp.dot(xv, (r_ >= c_).astype(F32), precision=HI, preferred_element_type=F32) + carry[:, 0:1]
        carry[...] += jnp.sum(xv, axis=1, keepdims=True)

    return pl.pallas_call(
        body, name="rev_cumsum_rows", grid=(nb,), out_shape=jax.ShapeDtypeStruct((r, tp), F32),
        in_specs=[pl.BlockSpec((r, t), lambda i: (0, nb - 1 - i))],
        out_specs=pl.BlockSpec((r, t), lambda i: (0, nb - 1 - i)),
        scratch_shapes=[pltpu.VMEM((r, LANES), F32)],
        compiler_params=_cparams(1),
    )(x)


def c_elem_bwd(p, b_f, qg, kg, dq, dk, dv, dlogf):
    tp, n_out = p.shape
    d = qg.shape[1]
    tm = ATT_TILE
    n = tp // tm
    seg, seg_t, fold = _head_masks(d)

    def body(p_ref, pf_ref, bf_ref, qg_ref, kg_ref, dq_ref, dk_ref, dv_ref, dlf_ref, seg_ref, segt_ref, fold_ref,
             dp_ref, sm_ref, accq, acck, accf):
        i = pl.program_id(0)

        @pl.when(i == 0)
        def _():
            accq[...] = jnp.zeros_like(accq)
            acck[...] = jnp.zeros_like(acck)
            accf[...] = jnp.zeros_like(accf)

        def norm_bwd(x, g, dy, acc):
            ms = jnp.dot(x * x, seg_ref[...], precision=HI, preferred_element_type=F32) * (1.0 / HEAD_DIM)
            r = jnp.dot(lax.rsqrt(ms + EPS), segt_ref[...], precision=HI, preferred_element_type=F32)
            xhat = x * r
            acc[0:1, :] += jnp.sum(dy * xhat, axis=0, keepdims=True)
            gy = dy * g
            mean = jnp.dot(jnp.dot(gy * xhat, seg_ref[...], precision=HI, preferred_element_type=F32),
                           segt_ref[...], precision=HI, preferred_element_type=F32) * (1.0 / HEAD_DIM)
            return r * (gy - xhat * mean)

        dp_ref[:, 0:d] = norm_bwd(p_ref[:, 0:d], qg_ref[...], dq_ref[...] * (HEAD_DIM ** -0.5), accq).astype(BF16)
        dp_ref[:, d:2 * d] = norm_bwd(p_ref[:, d:2 * d], kg_ref[...], dk_ref[...] * LN2, acck).astype(BF16)
        dp_ref[:, 2 * d:3 * d] = dv_ref[...].astype(BF16)
        df = dlf_ref[...] * _sigmoid(-(pf_ref[...] + bf_ref[...]))
        accf[0:1, :] += jnp.sum(df, axis=0, keepdims=True)
        dp_ref[:, 3 * d:] = df.astype(BF16)

        @pl.when(i == n - 1)
        def _():
            sm_ref[...] = jnp.zeros_like(sm_ref)
            sm_ref[0:1, :] = jnp.dot(accq[0:1, :], fold_ref[...], precision=HI, preferred_element_type=F32)
            sm_ref[1:2, :] = jnp.dot(acck[0:1, :], fold_ref[...], precision=HI, preferred_element_type=F32)
            sm_ref[2:3, :] = accf[0:1, :]

    vec = pl.BlockSpec((1, d), lambda i: (0, 0))
    rowd = pl.BlockSpec((tm, d), lambda i: (i, 0))
    rowl = pl.BlockSpec((tm, LANES), lambda i: (i, 0))
    return pl.pallas_call(
        body, name="c_elem_bwd", grid=(n,),
        out_shape=[jax.ShapeDtypeStruct((tp, n_out), BF16), jax.ShapeDtypeStruct((8, LANES), F32)],
        in_specs=[pl.BlockSpec((tm, 2 * d), lambda i: (i, 0)), pl.BlockSpec((tm, LANES), lambda i: (i, 3 * d // LANES)),
                  pl.BlockSpec((1, LANES), lambda i: (0, 0)), vec, vec, rowd, rowd, rowd, rowl,
                  pl.BlockSpec((d, LANES), lambda i: (0, 0)), pl.BlockSpec((LANES, d), lambda i: (0, 0)),
                  pl.BlockSpec((d, LANES), lambda i: (0, 0))],
        out_specs=[pl.BlockSpec((tm, n_out), lambda i: (i, 0)), pl.BlockSpec((8, LANES), lambda i: (0, 0))],
        scratch_shapes=[pltpu.VMEM((8, d), F32), pltpu.VMEM((8, d), F32), pltpu.VMEM((8, LANES), F32)],
        compiler_params=_cparams(1),
    )(p, p, b_f, qg, kg, dq, dk, dv, dlogf, seg, seg_t, fold)


def loss_head(h, tgt, t_real):
    tp, d = h.shape
    tm = _pick(tp, (768, 512, 256))

    def body(h_ref, t_ref, dh_ref, l_ref):
        i = pl.program_id(0)

        @pl.when(i == 0)
        def _():
            l_ref[...] = jnp.zeros_like(l_ref)

        row = i * tm + lax.broadcasted_iota(jnp.int32, (tm, 1), 0)
        valid = (row >= N_META) & (row < t_real)
        e = jnp.where(valid, h_ref[...] - t_ref[...], 0.0)
        dh_ref[...] = e * (1.0 / d)
        per_row = jnp.sum(e * e, axis=-1, keepdims=True) * (1.0 / d)
        l_ref[...] += 0.5 * jnp.sum(per_row, axis=0, keepdims=True)

    return pl.pallas_call(
        body, name="loss_head", grid=(tp // tm,),
        out_shape=[jax.ShapeDtypeStruct((tp, d), F32), jax.ShapeDtypeStruct((8, LANES), F32)],
        in_specs=[pl.BlockSpec((tm, d), lambda i: (i, 0)), pl.BlockSpec((tm, d), lambda i: (i, 0))],
        out_specs=[pl.BlockSpec((tm, d), lambda i: (i, 0)), pl.BlockSpec((8, LANES), lambda i: (0, 0))],
        compiler_params=_cparams(1),
    )(h, tgt)


def sum_devices(x):
    _, r, c = x.shape

    def body(x_ref, o_ref):
        acc = x_ref[0]
        for dev in range(1, N_DEV):
            acc = acc + x_ref[dev]
        o_ref[...] = acc

    return pl.pallas_call(
        body, name="sum_devices", out_shape=jax.ShapeDtypeStruct((r, c), F32),
        in_specs=[pl.BlockSpec(memory_space=pltpu.VMEM)], out_specs=pl.BlockSpec(memory_space=pltpu.VMEM),
    )(x)


def _adamw_math(w, g, m, v):
    m = ADAM_B1 * m + (1.0 - ADAM_B1) * g
    v = ADAM_B2 * v + (1.0 - ADAM_B2) * (g * g)
    m_hat = m / (1.0 - ADAM_B1 ** ADAM_STEP)
    v_hat = v / (1.0 - ADAM_B2 ** ADAM_STEP)
    delta = -ADAM_LR * (m_hat / (jnp.sqrt(v_hat) + ADAM_EPS) + ADAM_WD * w)
    return delta, m, v


def adamw_small(w, g, m, v):
    def body(w_ref, g_ref, m_ref, v_ref, d_ref, nm_ref, nv_ref):
        d_ref[...], nm_ref[...], nv_ref[...] = _adamw_math(w_ref[...], g_ref[...], m_ref[...], v_ref[...])

    vm = pl.BlockSpec(memory_space=pltpu.VMEM)
    return pl.pallas_call(
        body, name="adamw_small", out_shape=[jax.ShapeDtypeStruct(w.shape, F32)] * 3,
        in_specs=[vm] * 4, out_specs=[vm] * 3,
    )(w, g, m, v)


def adamw_reduce(w, m, v, parts):
    r, c = w.shape
    n_parts = parts.shape[0]
    tr = _pick(r, (512, 384, 352, 256, 128, 8))

    def body(w_ref, m_ref, v_ref, p_ref, g_ref, d_ref, nm_ref, nv_ref):
        g = p_ref[0].astype(F32)
        for j in range(1, n_parts):
            g = g + p_ref[j].astype(F32)
        g_ref[...] = g
        d_ref[...], nm_ref[...], nv_ref[...] = _adamw_math(w_ref[...], g, m_ref[...], v_ref[...])

    blk = pl.BlockSpec((tr, c), lambda i: (i, 0))
    return pl.pallas_call(
        body, name="adamw_reduce", grid=(r // tr,), out_shape=[jax.ShapeDtypeStruct((r, c), F32)] * 4,
        in_specs=[blk, blk, blk, pl.BlockSpec((n_parts, tr, c), lambda i: (0, i, 0))],
        out_specs=[blk] * 4, compiler_params=_cparams(1),
    )(w, m, v, parts)


def _unshard(g, axis):
    g = jnp.moveaxis(g, 0, axis)
    return g.reshape(g.shape[:axis] + (g.shape[axis] * g.shape[axis + 1],) + g.shape[axis + 2:])


def _shard(full, axis):
    s = full.shape
    g = full.reshape(s[:axis] + (N_DEV, s[axis] // N_DEV) + s[axis + 1:])
    return jnp.moveaxis(g, axis, 0)


def _pad_lanes(a, n=LANES):
    flat = a.reshape(-1)
    pad = (-flat.shape[0]) % n
    return jnp.pad(flat, (0, pad)).reshape(-1, n)


BIG = ("ffn_w_gate", "ffn_w_up", "ffn_w_down", "a_w_in", "a_w_out", "b_w_in", "b_w_out", "c_w_in", "c_w_out")
SHARD_AXIS = {"ffn_w_gate": 3, "ffn_w_up": 3, "ffn_w_down": 2, "a_w_in": 2, "a_w_out": 1, "b_w_in": 2,
              "b_w_out": 1, "c_w_in": 2, "c_w_out": 1, "meta": 1, "ffn_norm": 2, "a_conv": 2, "b_conv": 2}
SMALL_SHARDED = ("meta", "ffn_norm", "a_conv", "b_conv")
SMALL_REPL = ("mix_norm", "b_conv_bias", "b_ln_g", "b_ln_b", "c_b_f", "c_q_norm", "c_k_norm")
WEIGHTS = ("meta", "ffn_norm", "ffn_w_gate", "ffn_w_up", "ffn_w_down", "mix_norm", "a_w_in", "a_conv", "a_w_out",
           "b_w_in", "b_conv", "b_conv_bias", "b_ln_g", "b_ln_b", "b_w_out", "c_w_in", "c_b_f", "c_q_norm",
           "c_k_norm", "c_w_out")
N_MIXERS = 3


def kernel(x, meta, ffn_norm, ffn_w_gate, ffn_w_up, ffn_w_down, mix_norm, a_w_in, a_conv, a_w_out, b_w_in, b_conv, b_conv_bias, b_ln_g, b_ln_b, b_w_out, c_w_in, c_b_f, c_q_norm, c_k_norm, c_w_out, loss_target, m_meta, m_ffn_norm, m_ffn_w_gate, m_ffn_w_up, m_ffn_w_down, m_mix_norm, m_a_w_in, m_a_conv, m_a_w_out, m_b_w_in, m_b_conv, m_b_conv_bias, m_b_ln_g, m_b_ln_b, m_b_w_out, m_c_w_in, m_c_b_f, m_c_q_norm, m_c_k_norm, m_c_w_out, v_meta, v_ffn_norm, v_ffn_w_gate, v_ffn_w_up, v_ffn_w_down, v_mix_norm, v_a_w_in, v_a_conv, v_a_w_out, v_b_w_in, v_b_conv, v_b_conv_bias, v_b_ln_g, v_b_ln_b, v_b_w_out, v_c_w_in, v_c_b_f, v_c_q_norm, v_c_k_norm, v_c_w_out):
    local = dict(locals())
    w = {n: local[n] for n in WEIGHTS}
    mom = {n: local["m_" + n] for n in WEIGHTS}
    var = {n: local["v_" + n] for n in WEIGHTS}
    d = x.shape[-1]
    depth = ffn_norm.shape[0]
    n_heads = d // HEAD_DIM
    seq = x.shape[1]
    t_real = N_META + seq
    tp = -(-t_real // ROW_ALIGN) * ROW_ALIGN
    nb = tp // ATT_TILE
    npair = d // LANES

    names = BIG + SMALL_SHARDED
    gathered = all_gather([w[n].astype(BF16) for n in BIG] + [w[n] for n in SMALL_SHARDED])
    full = {n: _unshard(g, SHARD_AXIS[n]) for n, g in zip(names, gathered)}
    n_c = full["c_w_in"].shape[-1]
    n_cp = 3 * d + LANES
    c_w_qkv = jnp.pad(full["c_w_in"], ((0, 0), (0, 0), (0, n_cp - n_c)))
    b_f_pad = jnp.pad(c_b_f, ((0, 0), (0, LANES - n_heads)))
    qg_t = jnp.tile(c_q_norm, (1, n_heads))
    kg_t = jnp.tile(c_k_norm, (1, n_heads))

    h = jnp.concatenate([full["meta"], x[0], jnp.zeros((tp - t_real, d), F32)], axis=0)
    saved = []
    for i in range(depth):
        mixer, j = i % N_MIXERS, i // N_MIXERS
        s = {"h0": h}
        h, s["gate_a"], s["up_a"] = ffn_fwd(h, full["ffn_norm"][i, 0:1], full["ffn_w_gate"][i, 0],
                                            full["ffn_w_up"][i, 0], full["ffn_w_down"][i, 0])
        s["h1"] = h
        g_mix = mix_norm[i:i + 1]
        if mixer == 0:
            s["p"], s["u"] = norm_matmul(h, g_mix, full["a_w_in"][j], BF16, True)
            s["y"] = a_elem_fwd(s["p"], full["a_conv"][j])
            h = res_matmul(h, s["y"], full["a_w_out"][j])
        elif mixer == 1:
            s["p"], s["u"] = norm_matmul(h, g_mix, full["b_w_in"][j], BF16, True)
            s["y"] = b_elem_fwd(s["p"], full["b_conv"][j], b_conv_bias[j:j + 1], b_ln_g[j:j + 1], b_ln_b[j:j + 1])
            h = res_matmul(h, s["y"], full["b_w_out"][j])
        else:
            s["p"], s["u"] = norm_matmul(h, g_mix, c_w_qkv[j], F32, True)
            s["qp"], s["kp"], s["v"], cum = c_prep_slots(s["p"], b_f_pad[j:j + 1], qg_t[j:j + 1], kg_t[j:j + 1])
            s["cqt"] = (cum[:, :n_heads].T * L2E).reshape(npair, 2, tp)
            s["qpt"] = s["qp"].T
            s["kpt4"] = s["kp"].T.reshape(n_heads, LANES, nb, ATT_TILE).transpose(0, 2, 1, 3)
            vt4 = s["v"].T.reshape(npair, LANES, nb, ATT_TILE).transpose(0, 2, 1, 3)
            ot, s["lse"] = attn_fwd_t(s["qpt"], s["kp"], vt4)
            s["y"] = ot.T
            h = res_matmul(h, s["y"], full["c_w_out"][j])
        s["h2"] = h
        h, s["gate_b"], s["up_b"] = ffn_fwd(h, full["ffn_norm"][i, 1:2], full["ffn_w_gate"][i, 1],
                                            full["ffn_w_up"][i, 1], full["ffn_w_down"][i, 1])
        saved.append(s)

    tgt = jnp.concatenate([jnp.zeros((N_META, d), F32), loss_target[0], jnp.zeros((tp - t_real, d), F32)], axis=0)
    dh, loss_part = loss_head(h, tgt, t_real)

    gfull = {n: [None] * full[n].shape[0] for n in ("a_w_in", "a_w_out", "b_w_in", "b_w_out", "c_w_in", "c_w_out")}
    gffn = {n: [[None, None] for _ in range(depth)] for n in ("ffn_w_gate", "ffn_w_up", "ffn_w_down")}
    g_ffn_norm = [[None, None] for _ in range(depth)]
    g_mix_norm = [None] * depth
    g_a_conv = [None] * a_conv.shape[0]
    g_b_conv = [None] * b_conv.shape[0]
    g_b_stats = [None] * b_conv.shape[0]
    g_c_small = [None] * c_b_f.shape[0]

    def ffn_backward(dh_out, i, half, h_in, gate, up):
        dh_in, hn, dgate, dup, act, dg = ffn_bwd(dh_out, h_in, full["ffn_norm"][i, half:half + 1], gate, up,
                                                 full["ffn_w_gate"][i, half], full["ffn_w_up"][i, half],
                                                 full["ffn_w_down"][i, half])
        gffn["ffn_w_gate"][i][half] = atb(hn, dgate)
        gffn["ffn_w_up"][i][half] = atb(hn, dup)
        gffn["ffn_w_down"][i][half] = atb(act, dh_out, 0.5)
        g_ffn_norm[i][half] = dg
        return dh_in

    for i in reversed(range(depth)):
        mixer, j = i % N_MIXERS, i // N_MIXERS
        s = saved[i]
        g_mix = mix_norm[i:i + 1]
        dh = ffn_backward(dh, i, 1, s["h2"], s["gate_b"], s["up_b"])
        if mixer == 0:
            dy = matmul_nt(dh, full["a_w_out"][j])
            dp, dwc = a_elem_bwd(s["p"], dy, full["a_conv"][j])
            g_a_conv[j] = dwc[:a_conv.shape[1]]
            gfull["a_w_out"][j] = atb(s["y"], dh)
            gfull["a_w_in"][j] = atb(s["u"], dp)
            dh, g_mix_norm[i] = matmul_nt_rms_bwd(dp, full["a_w_in"][j], s["h1"], g_mix, dh)
        elif mixer == 1:
            dy = matmul_nt(dh, full["b_w_out"][j])
            du2, g_b_stats[j] = b_elem_bwd1(s["p"], dy, full["b_conv"][j], b_conv_bias[j:j + 1],
                                            b_ln_g[j:j + 1], b_ln_b[j:j + 1])
            dp, dwc = b_elem_bwd2(s["p"], du2, full["b_conv"][j])
            g_b_conv[j] = dwc[:b_conv.shape[1]]
            gfull["b_w_out"][j] = atb(s["y"], dh)
            gfull["b_w_in"][j] = atb(s["u"], dp)
            dh, g_mix_norm[i] = matmul_nt_rms_bwd(dp, full["b_w_in"][j], s["h1"], g_mix, dh)
        else:
            do = matmul_nt(dh, full["c_w_out"][j])
            dqt, dkp, dv = attn_bwd_t(s["qpt"], s["qp"], s["kp"], s["kpt4"], s["v"], do, do.T, s["lse"], s["cqt"])
            dq = dqt.reshape(n_heads, LANES, tp)[:, :HEAD_DIM].reshape(d, tp).T
            dkp = dkp.reshape(tp, n_heads, LANES)
            dk = dkp[:, :, :HEAD_DIM].reshape(tp, d)
            dlogf = rev_cumsum_rows(-dkp[:, :, AUG + 3].T)
            dlogf = jnp.pad(dlogf.T, ((0, 0), (0, LANES - n_heads)))
            dp, g_c_small[j] = c_elem_bwd(s["p"], b_f_pad[j:j + 1], qg_t[j:j + 1], kg_t[j:j + 1], dq, dk, dv, dlogf)
            gfull["c_w_out"][j] = atb(s["y"], dh)
            gfull["c_w_in"][j] = atb(s["u"], dp)[:, :n_c]
            dh, g_mix_norm[i] = matmul_nt_rms_bwd(dp, c_w_qkv[j], s["h1"], g_mix, dh)
        dh = ffn_backward(dh, i, 0, s["h0"], s["gate_a"], s["up_a"])

    grad_x = dh[N_META:t_real][None]

    gbig = {n: jnp.stack([jnp.stack(r) for r in gffn[n]]) for n in gffn}
    gbig.update({n: jnp.stack(gfull[n]) for n in gfull})
    def by_core(n):
        g = _shard(gbig[n], SHARD_AXIS[n])
        return jnp.moveaxis(g.reshape((2, 2, 2) + g.shape[1:]), 2, 0).reshape((2, N_CHIP) + g.shape[1:])

    halves = [by_core(n) for n in BIG]
    chip_sums = []
    for hv, th in zip(halves, swap_with_sibling(halves)):
        cols = th.shape[-1]
        chip_sums.append(add_own_half(hv.reshape(2, -1, cols), th.reshape(-1, cols)).reshape(th.shape))
    parts = exchange_chips(chip_sums)
    out_g, out_d, out_m, out_v = {}, {}, {}, {}
    for n, prt in zip(BIG, parts):
        shp = w[n].shape
        flat = lambda a: a.reshape(-1, shp[-1])
        res = adamw_reduce(flat(w[n]), flat(mom[n]), flat(var[n]), prt.reshape(N_CHIP, -1, shp[-1]))
        out_g[n], out_d[n], out_m[n], out_v[n] = [r.reshape(shp) for r in res]

    def rows(a):
        return a.reshape(-1, d)

    def lane_rows(a):
        return jnp.pad(a.reshape(1, -1), ((0, 0), (0, d - a.size)))

    c_small = jnp.stack(g_c_small)
    pieces = [("meta", dh[:N_META]),
              ("ffn_norm", rows(jnp.stack([jnp.stack(r) for r in g_ffn_norm]))),
              ("mix_norm", rows(jnp.stack(g_mix_norm))),
              ("a_conv", rows(jnp.stack(g_a_conv))),
              ("b_conv", rows(jnp.stack(g_b_conv))),
              ("b_ln_g", rows(jnp.stack([st[0] for st in g_b_stats]))),
              ("b_ln_b", rows(jnp.stack([st[1] for st in g_b_stats]))),
              ("b_conv_bias", rows(jnp.stack([st[2] for st in g_b_stats]))),
              ("c_q_norm", lane_rows(c_small[:, 0, :HEAD_DIM])),
              ("c_k_norm", lane_rows(c_small[:, 1, :HEAD_DIM])),
              ("c_b_f", lane_rows(c_small[:, 2, :n_heads])),
              ("loss", lane_rows(loss_part[0:1, 0:1]))]
    packed = jnp.concatenate([pc for _, pc in pieces], axis=0)
    n_rows = packed.shape[0]
    packed = jnp.pad(packed, ((0, (-n_rows) % 8), (0, 0)))
    total = sum_devices(all_gather([packed])[0])
    small_g, r0 = {}, 0
    for n, pc in pieces:
        small_g[n] = total[r0:r0 + pc.shape[0]]
        r0 += pc.shape[0]
    loss = small_g.pop("loss")[0, 0]
    me = 4 * lax.axis_index("x") + 2 * lax.axis_index("y") + lax.axis_index("c")
    for n in SMALL_SHARDED:
        cols = w[n].shape[-1]
        g = lax.dynamic_slice_in_dim(small_g[n], me * cols, cols, axis=1)
        out_g[n] = g.reshape(w[n].shape)
    for n in SMALL_REPL:
        out_g[n] = small_g[n].reshape(-1)[:w[n].size].reshape(w[n].shape)
    small = SMALL_SHARDED + SMALL_REPL
    pack = lambda dct: jnp.concatenate([_pad_lanes(dct[n]) for n in small], axis=0)
    res = adamw_small(pack(w), pack(out_g), pack(mom), pack(var))
    r0 = 0
    for n in small:
        nr = -(-w[n].size // LANES)
        for dct, arr in zip((out_d, out_m, out_v), res):
            dct[n] = arr[r0:r0 + nr].reshape(-1)[:w[n].size].reshape(w[n].shape)
        r0 += nr

    return (loss, grad_x, *[out_g[n] for n in WEIGHTS], *[out_d[n] for n in WEIGHTS],
            *[out_m[n] for n in WEIGHTS], *[out_v[n] for n in WEIGHTS])
```

```python
import functools

import jax
import jax.numpy as jnp
from jax import lax
from jax.experimental import pallas as pl
from jax.experimental.pallas import tpu as pltpu

F32 = jnp.float32
BF16 = jnp.bfloat16
HI = lax.Precision.HIGHEST
EPS = 1e-6
NEG = -1e30
N_META = 16
HEAD_DIM = 64
LANES = 128
N_DEV = 8
ROW_ALIGN = 256
VMEM_LIMIT = 56 * 1024 * 1024
ADAM_LR, ADAM_B1, ADAM_B2, ADAM_EPS, ADAM_WD, ADAM_STEP = 0.001, 0.9, 0.999, 1e-08, 0.01, 10
MESH = pl.DeviceIdType.MESH
NT = (((1,), (1,)), ((), ()))
TN = (((0,), (0,)), ((), ()))


def _cparams(n_axes):
    return pltpu.CompilerParams(dimension_semantics=("arbitrary",) * n_axes, vmem_limit_bytes=VMEM_LIMIT)


def _pick(n, cands):
    for c in cands:
        if n % c == 0:
            return c
    raise ValueError(f"no tile for {n} among {cands}")


def _sigmoid(x):
    return 1.0 / (1.0 + jnp.exp(-x))


def _rms(x):
    r = lax.rsqrt(jnp.mean(x * x, axis=-1, keepdims=True) + EPS)
    return x * r, r


def _rms_bwd(xhat, r, g, dy):
    gy = dy * g
    dx = r * (gy - xhat * jnp.mean(gy * xhat, axis=-1, keepdims=True))
    return dx, jnp.sum(dy * xhat, axis=0, keepdims=True)


def _mesh_pos():
    return lax.axis_index("x"), lax.axis_index("y"), lax.axis_index("c")


def all_gather(arrs):
    n = len(arrs)
    hbm = pl.BlockSpec(memory_space=pltpu.HBM)

    def body(*refs):
        ins, outs = refs[:n], refs[n:2 * n]
        send_sems, recv_sems, local_sems = refs[2 * n:]
        x, y, c = _mesh_pos()
        me, sibling = (x, y, c), (x, y, 1 - c)
        xn, yn, dg = (1 - x, y), (x, 1 - y), (1 - x, 1 - y)

        def slot(a, px, py, pc):
            return outs[a].at[4 * px + 2 * py + pc]

        def copy(a, k, block, to, src=None):
            return pltpu.make_async_remote_copy(
                src_ref=slot(a, *block) if src is None else src, dst_ref=slot(a, *block),
                send_sem=send_sems.at[7 * a + k], recv_sem=recv_sems.at[7 * a + k],
                device_id=to, device_id_type=MESH)

        own, sent = [], []

        def start(cp):
            cp.start()
            sent.append(cp)

        for a in range(n):
            cp = pltpu.make_async_copy(ins[a], slot(a, *me), local_sems.at[a])
            cp.start()
            own.append(cp)
        for a in range(n):
            start(copy(a, 0, me, sibling, src=ins[a]))
            start(copy(a, 1, me, (*xn, c), src=ins[a]))
            start(copy(a, 2, me, (*yn, c), src=ins[a]))
        for a in range(n):
            via_x = a % 2 == 0
            copy(a, 1, (*xn, c), me).wait_recv()
            if via_x:
                start(copy(a, 3, (*xn, c), (*yn, c)))
            start(copy(a, 4, (*xn, c), sibling))
            copy(a, 2, (*yn, c), me).wait_recv()
            if not via_x:
                start(copy(a, 3, (*yn, c), (*xn, c)))
            start(copy(a, 5, (*yn, c), sibling))
        for a in range(n):
            copy(a, 3, (*dg, c), me).wait_recv()
            start(copy(a, 6, (*dg, c), sibling))
        for a in range(n):
            copy(a, 0, sibling, me).wait_recv()
            for k, chip in ((4, xn), (5, yn), (6, dg)):
                copy(a, k, (*chip, 1 - c), me).wait_recv()
        for cp in sent:
            cp.wait_send()
        for cp in own:
            cp.wait()

    return pl.pallas_call(
        body, name="all_gather",
        out_shape=[jax.ShapeDtypeStruct((N_DEV,) + a.shape, a.dtype) for a in arrs],
        in_specs=[hbm] * n, out_specs=[hbm] * n,
        scratch_shapes=[pltpu.SemaphoreType.DMA((7 * n,)), pltpu.SemaphoreType.DMA((7 * n,)),
                        pltpu.SemaphoreType.DMA((n,))],
    )(*arrs)


N_CHIP = 4


def swap_with_sibling(arrs):
    n = len(arrs)
    hbm = pl.BlockSpec(memory_space=pltpu.HBM)

    def body(*refs):
        ins, outs = refs[:n], refs[n:2 * n]
        send_sems, recv_sems = refs[2 * n:]
        x, y, c = _mesh_pos()
        copies = [pltpu.make_async_remote_copy(
            src_ref=ins[a].at[1 - c], dst_ref=outs[a], send_sem=send_sems.at[a], recv_sem=recv_sems.at[a],
            device_id=(x, y, 1 - c), device_id_type=MESH) for a in range(n)]
        for cp in copies:
            cp.start()
        for cp in copies:
            cp.wait()

    return pl.pallas_call(
        body, name="swap_with_sibling",
        out_shape=[jax.ShapeDtypeStruct(a.shape[1:], a.dtype) for a in arrs],
        in_specs=[hbm] * n, out_specs=[hbm] * n,
        scratch_shapes=[pltpu.SemaphoreType.DMA((n,)), pltpu.SemaphoreType.DMA((n,))],
    )(*arrs)


def exchange_chips(arrs):
    n = len(arrs)
    hbm = pl.BlockSpec(memory_space=pltpu.HBM)

    def body(*refs):
        ins, outs = refs[:n], refs[n:2 * n]
        send_sems, recv_sems, local_sems = refs[2 * n:]
        x, y, c = _mesh_pos()
        me = 2 * x + y
        own, sent = [], []
        for a in range(n):
            cp = pltpu.make_async_copy(ins[a].at[me], outs[a].at[me], local_sems.at[a])
            cp.start()
            own.append(cp)
        for k in range(1, N_CHIP):
            px = 1 - x if k & 2 else x
            py = 1 - y if k & 1 else y
            peer = 2 * px + py
            for a in range(n):
                cp = pltpu.make_async_remote_copy(
                    src_ref=ins[a].at[peer], dst_ref=outs[a].at[me],
                    send_sem=send_sems.at[3 * a + k - 1], recv_sem=recv_sems.at[3 * a + k - 1],
                    device_id=(px, py, c), device_id_type=MESH)
                cp.start()
                sent.append((cp, a, k, peer))
        for cp, a, k, peer in sent:
            pltpu.make_async_remote_copy(
                src_ref=ins[a].at[peer], dst_ref=outs[a].at[peer],
                send_sem=send_sems.at[3 * a + k - 1], recv_sem=recv_sems.at[3 * a + k - 1],
                device_id=(x, y, c), device_id_type=MESH).wait_recv()
        for cp, a, k, peer in sent:
            cp.wait_send()
        for cp in own:
            cp.wait()

    return pl.pallas_call(
        body, name="exchange_chips",
        out_shape=[jax.ShapeDtypeStruct(a.shape, a.dtype) for a in arrs],
        in_specs=[hbm] * n, out_specs=[hbm] * n,
        scratch_shapes=[pltpu.SemaphoreType.DMA((3 * n,)), pltpu.SemaphoreType.DMA((3 * n,)),
                        pltpu.SemaphoreType.DMA((n,))],
    )(*arrs)


def add_own_half(halves, theirs):
    _, r, c = halves.shape
    tr = _pick(r, (2048, 1408, 1024, 512, 256, 128, 8))

    def body(a_ref, b_ref, o_ref):
        mine = a_ref[lax.axis_index("c")]
        o_ref[...] = (mine.astype(F32) + b_ref[...].astype(F32)).astype(BF16)

    blk = pl.BlockSpec((tr, c), lambda i: (i, 0))
    return pl.pallas_call(
        body, name="add_own_half", grid=(r // tr,), out_shape=jax.ShapeDtypeStruct((r, c), BF16),
        in_specs=[pl.BlockSpec((2, tr, c), lambda i: (0, i, 0)), blk], out_specs=blk, compiler_params=_cparams(1),
    )(halves, theirs)


FFN_CHUNK = (1408, 1024, 512, 256, 128)


def _resident(shape):
    return pl.BlockSpec(shape, lambda i: (0,) * len(shape), pipeline_mode=pl.Buffered(1))


def ffn_fwd(h, g, wg, wu, wd):
    tp, d = h.shape
    f = wg.shape[1]
    tm = _pick(tp, (384, 256))
    tf = _pick(f, FFN_CHUNK)

    def body(h_ref, g_ref, wg_ref, wu_ref, wd_ref, ho_ref, gate_ref, up_ref):
        x = h_ref[...]
        xhat, _ = _rms(x)
        hn = (xhat * g_ref[...]).astype(BF16)
        acc = jnp.zeros((tm, d), F32)
        for c0 in range(0, f, tf):
            gate = jnp.dot(hn, wg_ref[:, c0:c0 + tf], preferred_element_type=F32)
            up = jnp.dot(hn, wu_ref[:, c0:c0 + tf], preferred_element_type=F32)
            gate_ref[:, c0:c0 + tf] = gate.astype(BF16)
            up_ref[:, c0:c0 + tf] = up.astype(BF16)
            act = (gate * _sigmoid(gate) * up).astype(BF16)
            acc = acc + jnp.dot(act, wd_ref[c0:c0 + tf, :], preferred_element_type=F32)
        ho_ref[...] = x + 0.5 * acc

    row = lambda n: pl.BlockSpec((tm, n), lambda i: (i, 0))
    return pl.pallas_call(
        body, name="ffn_fwd", grid=(tp // tm,),
        out_shape=[jax.ShapeDtypeStruct((tp, d), F32), jax.ShapeDtypeStruct((tp, f), BF16),
                   jax.ShapeDtypeStruct((tp, f), BF16)],
        in_specs=[row(d), _resident((1, d)), _resident((d, f)), _resident((d, f)), _resident((f, d))],
        out_specs=[row(d), row(f), row(f)],
        compiler_params=_cparams(1),
    )(h, g, wg, wu, wd)


def ffn_bwd(dho, h, g, gate, up, wg, wu, wd):
    tp, d = h.shape
    f = wg.shape[1]
    tm = _pick(tp, (256,))
    tf = _pick(f, FFN_CHUNK)

    def body(dho_ref, h_ref, g_ref, gate_ref, up_ref, wg_ref, wu_ref, wd_ref,
             dhi_ref, hn_ref, dgate_ref, dup_ref, act_ref, dg_ref):
        @pl.when(pl.program_id(0) == 0)
        def _():
            dg_ref[...] = jnp.zeros_like(dg_ref)

        dho_ = dho_ref[...]
        dout = (0.5 * dho_).astype(BF16)
        dhn = jnp.zeros((tm, d), F32)
        for c0 in range(0, f, tf):
            dact = lax.dot_general(dout, wd_ref[c0:c0 + tf, :], NT, preferred_element_type=F32)
            gt = gate_ref[:, c0:c0 + tf].astype(F32)
            u = up_ref[:, c0:c0 + tf].astype(F32)
            sig = _sigmoid(gt)
            silu = gt * sig
            act_ref[:, c0:c0 + tf] = (silu * u).astype(BF16)
            dup = (dact * silu).astype(BF16)
            dgate = (dact * u * (sig * (1.0 + gt * (1.0 - sig)))).astype(BF16)
            dup_ref[:, c0:c0 + tf] = dup
            dgate_ref[:, c0:c0 + tf] = dgate
            dhn = dhn + (lax.dot_general(dgate, wg_ref[:, c0:c0 + tf], NT, preferred_element_type=F32)
                         + lax.dot_general(dup, wu_ref[:, c0:c0 + tf], NT, preferred_element_type=F32))
        xhat, r = _rms(h_ref[...])
        gg = g_ref[...]
        dx, dgp = _rms_bwd(xhat, r, gg, dhn)
        dg_ref[...] += dgp
        dhi_ref[...] = dho_ + dx
        hn_ref[...] = (xhat * gg).astype(BF16)

    row = lambda n: pl.BlockSpec((tm, n), lambda i: (i, 0))
    return pl.pallas_call(
        body, name="ffn_bwd", grid=(tp // tm,),
        out_shape=[jax.ShapeDtypeStruct((tp, d), F32), jax.ShapeDtypeStruct((tp, d), BF16),
                   jax.ShapeDtypeStruct((tp, f), BF16), jax.ShapeDtypeStruct((tp, f), BF16),
                   jax.ShapeDtypeStruct((tp, f), BF16), jax.ShapeDtypeStruct((1, d), F32)],
        in_specs=[row(d), row(d), _resident((1, d)), row(f), row(f),
                  _resident((d, f)), _resident((d, f)), _resident((f, d))],
        out_specs=[row(d), row(d), row(f), row(f), row(f), pl.BlockSpec((1, d), lambda i: (0, 0))],
        compiler_params=_cparams(1),
    )(dho, h, g, gate, up, wg, wu, wd)


def atb(a, b, scale=1.0):
    tp, m = a.shape
    n = b.shape[1]
    tmm = _pick(m, (1024, 1408, 640, 512, 256, 128))
    tn = _pick(n, (1024, 1408, 640, 512, 256, 128))
    tk = _pick(tp, (1408, 768, 512, 256))
    nk = tp // tk

    def body(a_ref, b_ref, o_ref, acc):
        k = pl.program_id(2)

        @pl.when(k == 0)
        def _():
            acc[...] = jnp.zeros_like(acc)

        acc[...] += lax.dot_general(a_ref[...].astype(BF16), b_ref[...].astype(BF16), TN,
                                    preferred_element_type=F32)

        @pl.when(k == nk - 1)
        def _():
            o_ref[...] = (acc[...] * scale).astype(BF16)

    return pl.pallas_call(
        body, name="atb", grid=(m // tmm, n // tn, nk),
        out_shape=jax.ShapeDtypeStruct((m, n), BF16),
        in_specs=[pl.BlockSpec((tk, tmm), lambda i, j, k: (k, i)), pl.BlockSpec((tk, tn), lambda i, j, k: (k, j))],
        out_specs=pl.BlockSpec((tmm, tn), lambda i, j, k: (i, j)),
        scratch_shapes=[pltpu.VMEM((tmm, tn), F32)],
        compiler_params=_cparams(3),
    )(a, b)


def norm_matmul(h, g, w, out_dtype, emit_u):
    tp, d = h.shape
    n = w.shape[1]
    tm = _pick(tp, (768, 512, 256))
    tn = _pick(n, (1024, 768, 640, 512, 256, 128))

    def body(h_ref, g_ref, w_ref, o_ref, *rest):
        u_sc = rest[-1]

        @pl.when(pl.program_id(1) == 0)
        def _():
            xhat, _ = _rms(h_ref[...])
            u_sc[...] = (xhat * g_ref[...]).astype(BF16)
            if emit_u:
                rest[0][...] = u_sc[...]

        o_ref[...] = jnp.dot(u_sc[...], w_ref[...], preferred_element_type=F32).astype(out_dtype)

    out_shape = [jax.ShapeDtypeStruct((tp, n), out_dtype)]
    out_specs = [pl.BlockSpec((tm, tn), lambda i, j: (i, j))]
    if emit_u:
        out_shape.append(jax.ShapeDtypeStruct((tp, d), BF16))
        out_specs.append(pl.BlockSpec((tm, d), lambda i, j: (i, 0)))
    return pl.pallas_call(
        body, name="norm_matmul", grid=(tp // tm, n // tn), out_shape=out_shape,
        in_specs=[pl.BlockSpec((tm, d), lambda i, j: (i, 0)), pl.BlockSpec((1, d), lambda i, j: (0, 0)),
                  pl.BlockSpec((d, tn), lambda i, j: (0, j))],
        out_specs=out_specs, scratch_shapes=[pltpu.VMEM((tm, d), BF16)],
        compiler_params=_cparams(2),
    )(h, g, w)


def res_matmul(h, y, w):
    tp, d = h.shape
    k = y.shape[1]
    tm = _pick(tp, (768, 512, 256))

    def body(h_ref, y_ref, w_ref, o_ref):
        o_ref[...] = h_ref[...] + jnp.dot(y_ref[...], w_ref[...], preferred_element_type=F32)

    return pl.pallas_call(
        body, name="res_matmul", grid=(tp // tm,), out_shape=jax.ShapeDtypeStruct((tp, d), F32),
        in_specs=[pl.BlockSpec((tm, d), lambda i: (i, 0)), pl.BlockSpec((tm, k), lambda i: (i, 0)),
                  pl.BlockSpec((k, d), lambda i: (0, 0))],
        out_specs=pl.BlockSpec((tm, d), lambda i: (i, 0)),
        compiler_params=_cparams(1),
    )(h, y, w)


def matmul_nt(a, w):
    tp, d = a.shape
    k = w.shape[0]
    tm = _pick(tp, (768, 512, 256))

    def body(a_ref, w_ref, o_ref):
        o_ref[...] = lax.dot_general(a_ref[...].astype(BF16), w_ref[...], NT,
                                     preferred_element_type=F32).astype(BF16)

    return pl.pallas_call(
        body, name="matmul_nt", grid=(tp // tm,), out_shape=jax.ShapeDtypeStruct((tp, k), BF16),
        in_specs=[pl.BlockSpec((tm, d), lambda i: (i, 0)), pl.BlockSpec((k, d), lambda i: (0, 0))],
        out_specs=pl.BlockSpec((tm, k), lambda i: (i, 0)),
        compiler_params=_cparams(1),
    )(a, w)


def matmul_nt_rms_bwd(dp, w, h, g, dres):
    tp, d = h.shape
    n = w.shape[1]
    tm = _pick(tp, (768, 512, 256))

    def body(dp_ref, w_ref, h_ref, g_ref, dres_ref, dh_ref, dg_ref):
        @pl.when(pl.program_id(0) == 0)
        def _():
            dg_ref[...] = jnp.zeros_like(dg_ref)

        du = lax.dot_general(dp_ref[...], w_ref[...], NT, preferred_element_type=F32)
        xhat, r = _rms(h_ref[...])
        dx, dgp = _rms_bwd(xhat, r, g_ref[...], du)
        dg_ref[...] += dgp
        dh_ref[...] = dres_ref[...] + dx

    return pl.pallas_call(
        body, name="matmul_nt_rms_bwd", grid=(tp // tm,),
        out_shape=[jax.ShapeDtypeStruct((tp, d), F32), jax.ShapeDtypeStruct((1, d), F32)],
        in_specs=[pl.BlockSpec((tm, n), lambda i: (i, 0)), _resident((d, n)),
                  pl.BlockSpec((tm, d), lambda i: (i, 0)), _resident((1, d)),
                  pl.BlockSpec((tm, d), lambda i: (i, 0))],
        out_specs=[pl.BlockSpec((tm, d), lambda i: (i, 0)), pl.BlockSpec((1, d), lambda i: (0, 0))],
        compiler_params=_cparams(1),
    )(dp, w, h, g, dres)


CONV_TILE = (256,)


def _halo_before(tm, hb, col=0):
    return lambda i: (jnp.maximum(i * (tm // hb) - 1, 0), col)


def _halo_after(tm, hb, nblk, col=0):
    return lambda i: (jnp.minimum((i + 1) * (tm // hb), nblk - 1), col)


def a_elem_fwd(p, w):
    tp, d3 = p.shape
    d = d3 // 3
    kw = w.shape[0]
    tm = _pick(tp, CONV_TILE)
    hb = 16

    def body(p_ref, ph_ref, w_ref, y_ref, ext):
        i = pl.program_id(0)
        halo = ph_ref[:, d:2 * d].astype(F32) * ph_ref[:, 2 * d:].astype(F32)
        ext[0:hb, :] = jnp.where(i > 0, halo, 0.0)
        ext[hb:, :] = p_ref[:, d:2 * d].astype(F32) * p_ref[:, 2 * d:].astype(F32)
        z = jnp.zeros((tm, d), F32)
        for k in range(kw):
            z = z + w_ref[k:k + 1, :] * ext[pl.ds(hb - (kw - 1) + k, tm), :]
        y_ref[...] = (p_ref[:, 0:d].astype(F32) * z).astype(BF16)

    return pl.pallas_call(
        body, name="a_elem_fwd", grid=(tp // tm,), out_shape=jax.ShapeDtypeStruct((tp, d), BF16),
        in_specs=[pl.BlockSpec((tm, d3), lambda i: (i, 0)), pl.BlockSpec((hb, d3), _halo_before(tm, hb)),
                  pl.BlockSpec((kw, d), lambda i: (0, 0))],
        out_specs=pl.BlockSpec((tm, d), lambda i: (i, 0)),
        scratch_shapes=[pltpu.VMEM((tm + hb, d), F32)],
        compiler_params=_cparams(1),
    )(p, p, w)


def a_elem_bwd(p, dy, w):
    tp, d3 = p.shape
    d = d3 // 3
    kw = w.shape[0]
    tm = _pick(tp, CONV_TILE)
    hb = 16
    n = tp // tm

    def body(p_ref, ph_ref, pa_ref, dy_ref, dya_ref, w_ref, dp_ref, dw_ref, ext, dzext):
        i = pl.program_id(0)

        @pl.when(i == 0)
        def _():
            dw_ref[...] = jnp.zeros_like(dw_ref)

        b = p_ref[:, 0:d].astype(F32)
        c = p_ref[:, d:2 * d].astype(F32)
        v = p_ref[:, 2 * d:].astype(F32)
        dy_ = dy_ref[...].astype(F32)
        halo = ph_ref[:, d:2 * d].astype(F32) * ph_ref[:, 2 * d:].astype(F32)
        ext[0:hb, :] = jnp.where(i > 0, halo, 0.0)
        ext[hb:, :] = c * v
        dz = dy_ * b
        dzext[0:tm, :] = dz
        dzext[tm:, :] = jnp.where(i < n - 1, dya_ref[...].astype(F32) * pa_ref[:, 0:d].astype(F32), 0.0)
        z = jnp.zeros((tm, d), F32)
        dcv = jnp.zeros((tm, d), F32)
        for k in range(kw):
            sh = ext[pl.ds(hb - (kw - 1) + k, tm), :]
            z = z + w_ref[k:k + 1, :] * sh
            dw_ref[k:k + 1, :] += jnp.sum(dz * sh, axis=0, keepdims=True)
            dcv = dcv + w_ref[k:k + 1, :] * dzext[pl.ds(kw - 1 - k, tm), :]
        dp_ref[:, 0:d] = (dy_ * z).astype(BF16)
        dp_ref[:, d:2 * d] = (dcv * v).astype(BF16)
        dp_ref[:, 2 * d:] = (dcv * c).astype(BF16)

    nb = tp // hb
    return pl.pallas_call(
        body, name="a_elem_bwd", grid=(n,),
        out_shape=[jax.ShapeDtypeStruct((tp, d3), BF16), jax.ShapeDtypeStruct((8, d), F32)],
        in_specs=[pl.BlockSpec((tm, d3), lambda i: (i, 0)), pl.BlockSpec((hb, d3), _halo_before(tm, hb)),
                  pl.BlockSpec((hb, d3), _halo_after(tm, hb, nb)),
                  pl.BlockSpec((tm, d), lambda i: (i, 0)), pl.BlockSpec((hb, d), _halo_after(tm, hb, nb)),
                  pl.BlockSpec((kw, d), lambda i: (0, 0))],
        out_specs=[pl.BlockSpec((tm, d3), lambda i: (i, 0)), pl.BlockSpec((8, d), lambda i: (0, 0))],
        scratch_shapes=[pltpu.VMEM((tm + hb, d), F32), pltpu.VMEM((tm + hb, d), F32)],
        compiler_params=_cparams(1),
    )(p, p, p, dy, dy, w)


CH_R, CH_C = 128, 128


def _chunks(tm, d):
    return [(r0, c0) for c0 in range(0, d, CH_C) for r0 in range(0, tm, CH_R)]


def _b_u2(p_ref, ph_ref, w_ref, bias_ref, ext, u2_sc, i, tm, d, kw, hb):
    a = p_ref[:, 0:d].astype(F32)
    sg = _sigmoid(p_ref[:, d:].astype(F32))
    halo = ph_ref[:, 0:d].astype(F32) * _sigmoid(ph_ref[:, d:].astype(F32))
    ext[0:hb, :] = jnp.where(i > 0, halo, 0.0)
    ext[hb:, :] = a * sg
    for r0, c0 in _chunks(tm, d):
        cs = slice(c0, c0 + CH_C)
        acc = jnp.zeros((CH_R, CH_C), F32) + bias_ref[:, cs]
        for k in range(kw):
            acc = acc + w_ref[k:k + 1, cs] * ext[pl.ds(r0 + hb - (kw - 1) + k, CH_R), cs]
        u2_sc[r0:r0 + CH_R, cs] = acc
    return a, sg, u2_sc[...]


def _ln(u2):
    mu = jnp.mean(u2, axis=-1, keepdims=True)
    xc = u2 - mu
    rstd = lax.rsqrt(jnp.mean(xc * xc, axis=-1, keepdims=True) + EPS)
    return xc * rstd, rstd


def b_elem_fwd(p, w, bias, ln_g, ln_b):
    tp, d2 = p.shape
    d = d2 // 2
    kw = w.shape[0]
    tm = _pick(tp, CONV_TILE)
    hb = 32

    def body(p_ref, ph_ref, w_ref, bias_ref, g_ref, b_ref, y_ref, ext, u2_sc):
        i = pl.program_id(0)
        _, _, u2 = _b_u2(p_ref, ph_ref, w_ref, bias_ref, ext, u2_sc, i, tm, d, kw, hb)
        xhat, _ = _ln(u2)
        u3 = xhat * g_ref[...] + b_ref[...]
        y_ref[...] = (u3 * _sigmoid(u3)).astype(BF16)

    vec = pl.BlockSpec((1, d), lambda i: (0, 0))
    return pl.pallas_call(
        body, name="b_elem_fwd", grid=(tp // tm,), out_shape=jax.ShapeDtypeStruct((tp, d), BF16),
        in_specs=[pl.BlockSpec((tm, d2), lambda i: (i, 0)), pl.BlockSpec((hb, d2), _halo_before(tm, hb)),
                  pl.BlockSpec((kw, d), lambda i: (0, 0)), vec, vec, vec],
        out_specs=pl.BlockSpec((tm, d), lambda i: (i, 0)),
        scratch_shapes=[pltpu.VMEM((tm + hb, d), F32), pltpu.VMEM((tm, d), F32)],
        compiler_params=_cparams(1),
    )(p, p, w, bias, ln_g, ln_b)


def b_elem_bwd1(p, dy, w, bias, ln_g, ln_b):
    tp, d2 = p.shape
    d = d2 // 2
    kw = w.shape[0]
    tm = _pick(tp, CONV_TILE)
    hb = 32

    def body(p_ref, ph_ref, dy_ref, w_ref, bias_ref, g_ref, b_ref, du2_ref, st_ref, ext, u2_sc):
        i = pl.program_id(0)

        @pl.when(i == 0)
        def _():
            st_ref[...] = jnp.zeros_like(st_ref)

        _, _, u2 = _b_u2(p_ref, ph_ref, w_ref, bias_ref, ext, u2_sc, i, tm, d, kw, hb)
        xhat, rstd = _ln(u2)
        u3 = xhat * g_ref[...] + b_ref[...]
        s3 = _sigmoid(u3)
        du3 = dy_ref[...].astype(F32) * (s3 * (1.0 + u3 * (1.0 - s3)))
        dxh = du3 * g_ref[...]
        du2 = rstd * (dxh - jnp.mean(dxh, axis=-1, keepdims=True)
                      - xhat * jnp.mean(dxh * xhat, axis=-1, keepdims=True))
        du2_ref[...] = du2
        st_ref[0:1, :] += jnp.sum(du3 * xhat, axis=0, keepdims=True)
        st_ref[1:2, :] += jnp.sum(du3, axis=0, keepdims=True)
        st_ref[2:3, :] += jnp.sum(du2, axis=0, keepdims=True)

    vec = pl.BlockSpec((1, d), lambda i: (0, 0))
    return pl.pallas_call(
        body, name="b_elem_bwd1", grid=(tp // tm,),
        out_shape=[jax.ShapeDtypeStruct((tp, d), F32), jax.ShapeDtypeStruct((8, d), F32)],
        in_specs=[pl.BlockSpec((tm, d2), lambda i: (i, 0)), pl.BlockSpec((hb, d2), _halo_before(tm, hb)),
                  pl.BlockSpec((tm, d), lambda i: (i, 0)), pl.BlockSpec((kw, d), lambda i: (0, 0)), vec, vec, vec],
        out_specs=[pl.BlockSpec((tm, d), lambda i: (i, 0)), pl.BlockSpec((8, d), lambda i: (0, 0))],
        scratch_shapes=[pltpu.VMEM((tm + hb, d), F32), pltpu.VMEM((tm, d), F32)],
        compiler_params=_cparams(1),
    )(p, p, dy, w, bias, ln_g, ln_b)


def b_elem_bwd2(p, du2, w):
    tp, d2 = p.shape
    d = d2 // 2
    kw = w.shape[0]
    tm = _pick(tp, CONV_TILE)
    hb = 32
    n = tp // tm

    def body(p_ref, ph_ref, du2_ref, du2a_ref, w_ref, dp_ref, dw_ref, ext, dext, dwacc):
        i = pl.program_id(0)

        @pl.when(i == 0)
        def _():
            dwacc[...] = jnp.zeros_like(dwacc)

        halo = ph_ref[:, 0:d].astype(F32) * _sigmoid(ph_ref[:, d:].astype(F32))
        ext[0:hb, :] = jnp.where(i > 0, halo, 0.0)
        ext[hb:, :] = p_ref[:, 0:d].astype(F32) * _sigmoid(p_ref[:, d:].astype(F32))
        dext[0:tm, :] = du2_ref[...]
        dext[tm:, :] = jnp.where(i < n - 1, du2a_ref[...], 0.0)
        for r0, c0 in _chunks(tm, d):
            cs = slice(c0, c0 + CH_C)
            du2c = du2_ref[r0:r0 + CH_R, cs]
            du1 = jnp.zeros((CH_R, CH_C), F32)
            for k in range(kw):
                prod = du2c * ext[pl.ds(r0 + hb - (kw - 1) + k, CH_R), cs]
                part = prod[0:8]
                for r in range(8, CH_R, 8):
                    part = part + prod[r:r + 8]
                dwacc[8 * k:8 * k + 8, cs] += part
                du1 = du1 + w_ref[k:k + 1, cs] * dext[pl.ds(r0 + kw - 1 - k, CH_R), cs]
            a = p_ref[r0:r0 + CH_R, cs].astype(F32)
            sg = _sigmoid(p_ref[r0:r0 + CH_R, d + c0:d + c0 + CH_C].astype(F32))
            dp_ref[r0:r0 + CH_R, cs] = (du1 * sg).astype(BF16)
            dp_ref[r0:r0 + CH_R, d + c0:d + c0 + CH_C] = (du1 * a * sg * (1.0 - sg)).astype(BF16)

        @pl.when(i == n - 1)
        def _():
            dw_ref[...] = jnp.zeros_like(dw_ref)
            for k in range(kw):
                dw_ref[k:k + 1, :] = jnp.sum(dwacc[8 * k:8 * k + 8, :], axis=0, keepdims=True)

    nb = tp // hb
    return pl.pallas_call(
        body, name="b_elem_bwd2", grid=(n,),
        out_shape=[jax.ShapeDtypeStruct((tp, d2), BF16), jax.ShapeDtypeStruct((32, d), F32)],
        in_specs=[pl.BlockSpec((tm, d2), lambda i: (i, 0)), pl.BlockSpec((hb, d2), _halo_before(tm, hb)),
                  pl.BlockSpec((tm, d), lambda i: (i, 0)), pl.BlockSpec((hb, d), _halo_after(tm, hb, nb)),
                  pl.BlockSpec((kw, d), lambda i: (0, 0))],
        out_specs=[pl.BlockSpec((tm, d2), lambda i: (i, 0)), pl.BlockSpec((32, d), lambda i: (0, 0))],
        scratch_shapes=[pltpu.VMEM((tm + hb, d), F32), pltpu.VMEM((tm + hb, d), F32), pltpu.VMEM((8 * kw, d), F32)],
        compiler_params=_cparams(1),
    )(p, p, du2, du2, w)


ATT_TILE = 256


def _head_masks(d):
    c = lax.broadcasted_iota(jnp.int32, (d, LANES), 0)
    h = lax.broadcasted_iota(jnp.int32, (d, LANES), 1)
    seg = (c // HEAD_DIM == h).astype(BF16)
    fold = (c % HEAD_DIM == h).astype(BF16)
    return seg, seg.T, fold


def _dot2(x, m):
    hi = x.astype(BF16)
    lo = (x - hi.astype(F32)).astype(BF16)
    return jnp.dot(hi, m, preferred_element_type=F32) + jnp.dot(lo, m, preferred_element_type=F32)


L2E = 1.4426950408889634
LN2 = 0.6931471805599453
AUG = HEAD_DIM


def _split3(r):
    r1 = r.astype(BF16).astype(F32)
    r2 = (r - r1).astype(BF16).astype(F32)
    r3 = (r - r1 - r2).astype(BF16).astype(F32)
    return r1, r2, r3


def c_prep_slots(p, b_f, qg, kg):
    tp = p.shape[0]
    d = qg.shape[1]
    n_heads = d // HEAD_DIM
    tm = ATT_TILE
    seg, seg_t, _ = _head_masks(d)

    def body(p_ref, pf_ref, bf_ref, qg_ref, kg_ref, seg_ref, segt_ref, q_ref, k_ref, v_ref, cum_ref, carry):
        @pl.when(pl.program_id(0) == 0)
        def _():
            carry[...] = jnp.zeros_like(carry)

        def norm(x, g):
            ms = _dot2(x * x, seg_ref[...]) * (1.0 / HEAD_DIM)
            r = _dot2(lax.rsqrt(ms + EPS), segt_ref[...])
            return x * r * g

        qn = norm(p_ref[:, 0:d], qg_ref[...]) * (HEAD_DIM ** -0.5 * L2E)
        kn = norm(p_ref[:, d:2 * d], kg_ref[...])
        v_ref[...] = p_ref[:, 2 * d:].astype(BF16)
        xf = pf_ref[...] + bf_ref[...]
        logf = jnp.minimum(xf, 0.0) - jnp.log(1.0 + jnp.exp(-jnp.abs(xf)))
        r_ = lax.broadcasted_iota(jnp.int32, (tm, tm), 0)
        c_ = lax.broadcasted_iota(jnp.int32, (tm, tm), 1)
        cum = jnp.dot((c_ <= r_).astype(F32), logf, precision=HI, preferred_element_type=F32) + carry[0:1, :]
        cum_ref[...] = cum
        carry[0:1, :] += jnp.sum(logf, axis=0, keepdims=True)

        lane = lax.broadcasted_iota(jnp.int32, (1, LANES), 1)
        ones_q = jnp.where((lane >= AUG + 3) & (lane < AUG + 6), 1.0, 0.0)
        ones_k = jnp.where((lane >= AUG) & (lane < AUG + 3), 1.0, 0.0)
        for h in range(n_heads):
            c0 = LANES * (h // 2)
            src_q, src_k = qn[:, c0:c0 + LANES], kn[:, c0:c0 + LANES]
            if h % 2:
                src_q = pltpu.roll(src_q, HEAD_DIM, axis=1)
                src_k = pltpu.roll(src_k, HEAD_DIM, axis=1)
            c = cum[:, h:h + 1] * L2E
            a1, a2, a3 = _split3(c)
            b1, b2, b3 = _split3(-c)
            aug_q = jnp.where(lane == AUG, a1, jnp.where(lane == AUG + 1, a2, jnp.where(lane == AUG + 2, a3, ones_q)))
            aug_k = jnp.where(lane == AUG + 3, b1,
                              jnp.where(lane == AUG + 4, b2, jnp.where(lane == AUG + 5, b3, ones_k)))
            q_ref[:, LANES * h:LANES * (h + 1)] = jnp.where(lane < HEAD_DIM, src_q, aug_q).astype(BF16)
            k_ref[:, LANES * h:LANES * (h + 1)] = jnp.where(lane < HEAD_DIM, src_k, aug_k).astype(BF16)

    vec = pl.BlockSpec((1, d), lambda i: (0, 0))
    rowd = pl.BlockSpec((tm, d), lambda i: (i, 0))
    slots = pl.BlockSpec((tm, n_heads * LANES), lambda i: (i, 0))
    return pl.pallas_call(
        body, name="c_prep_slots", grid=(tp // tm,),
        out_shape=[jax.ShapeDtypeStruct((tp, n_heads * LANES), BF16)] * 2
        + [jax.ShapeDtypeStruct((tp, d), BF16), jax.ShapeDtypeStruct((tp, LANES), F32)],
        in_specs=[pl.BlockSpec((tm, 3 * d), lambda i: (i, 0)), pl.BlockSpec((tm, LANES), lambda i: (i, 3 * d // LANES)),
                  pl.BlockSpec((1, LANES), lambda i: (0, 0)), vec, vec,
                  pl.BlockSpec((d, LANES), lambda i: (0, 0)), pl.BlockSpec((LANES, d), lambda i: (0, 0))],
        out_specs=[slots, slots, rowd, pl.BlockSpec((tm, LANES), lambda i: (i, 0))],
        scratch_shapes=[pltpu.VMEM((8, LANES), F32)],
        compiler_params=_cparams(1),
    )(p, p, b_f, qg, kg, seg, seg_t)


ATT_GROUPS = (8, 2)


def _grouped_loop(n_blocks, body, carry):
    start = 0
    for g in ATT_GROUPS:
        count = (n_blocks - start) // g
        carry = lax.fori_loop(0, count, lambda i, c, g=g, start=start: body(start + i * g, g, c), carry)
        start = start + count * g
    return lax.fori_loop(start, n_blocks, lambda kj, c: body(kj, 1, c), carry)


def _diag_mask(t):
    return lax.broadcasted_iota(jnp.int32, (t, t), 0) <= lax.broadcasted_iota(jnp.int32, (t, t), 1)


def attn_fwd_t(qpt, kp, vt4):
    tp = kp.shape[0]
    t = ATT_TILE
    nb = tp // t
    npair = vt4.shape[0]

    def body(q_ref, k_ref, v_ref, o_ref, lse_ref):
        qi = pl.program_id(1)
        qts = (q_ref[0:LANES, :], q_ref[LANES:2 * LANES, :])
        mask = _diag_mask(t)

        def blocks(kj0, n, carry, masked=False):
            out = []
            for hd in range(2):
                m, l, acc = carry[3 * hd:3 * hd + 3]
                sts = []
                for g in range(n):
                    off = pl.multiple_of((kj0 + g) * t, t)
                    st = jnp.dot(k_ref[pl.ds(off, t), LANES * hd:LANES * (hd + 1)], qts[hd],
                                 preferred_element_type=F32)
                    sts.append(jnp.where(mask, st, NEG) if masked else st)
                m_new = m
                for st in sts:
                    m_new = jnp.maximum(m_new, jnp.max(st, axis=0, keepdims=True))
                alpha = jnp.exp2(m - m_new)
                l = alpha * l
                acc = alpha * acc
                for g, st in enumerate(sts):
                    pt = jnp.exp2(st - m_new)
                    l = l + jnp.sum(pt, axis=0, keepdims=True)
                    acc = acc + jnp.dot(v_ref[0, kj0 + g], pt.astype(BF16), preferred_element_type=F32)
                out += [m_new, l, acc]
            return tuple(out)

        init = (jnp.full((1, t), NEG, F32), jnp.zeros((1, t), F32), jnp.zeros((LANES, t), F32)) * 2
        carry = _grouped_loop(qi, blocks, init)
        ma, la, acca, mb, lb, accb = blocks(qi, 1, carry, masked=True)
        row = lax.broadcasted_iota(jnp.int32, (LANES, 1), 0)
        o_ref[...] = jnp.where(row < HEAD_DIM, acca / la, accb / lb).astype(BF16)
        lse_ref[0, 0:1, :] = ma + jnp.log(la) * L2E
        lse_ref[0, 1:2, :] = mb + jnp.log(lb) * L2E

    return pl.pallas_call(
        body, name="attn_fwd_t", grid=(npair, nb),
        out_shape=[jax.ShapeDtypeStruct((npair * LANES, tp), BF16), jax.ShapeDtypeStruct((npair, 2, tp), F32)],
        in_specs=[pl.BlockSpec((2 * LANES, t), lambda h, i: (h, i)), pl.BlockSpec((tp, 2 * LANES), lambda h, i: (0, h)),
                  pl.BlockSpec((1, nb, LANES, t), lambda h, i: (h, 0, 0, 0))],
        out_specs=[pl.BlockSpec((LANES, t), lambda h, i: (h, i)), pl.BlockSpec((1, 2, t), lambda h, i: (h, 0, i))],
        compiler_params=_cparams(2),
    )(qpt, kp, vt4)


def attn_bwd_t(qpt, qp, kp, kpt4, v, do, dot_, lse, cqt):
    tp = kp.shape[0]
    t = ATT_TILE
    nb = tp // t
    n_heads = kpt4.shape[0]
    d = v.shape[1]

    def body(qt_ref, q_ref, k_ref, kt_ref, v_ref, do_ref, dot_ref, lse_ref, cq_ref, dq_ref, dk_ref, dv_ref,
             p_sc, dp_sc, dc_sc):
        hd = pl.program_id(0) % 2
        qi = pl.program_id(1)

        @pl.when(qi == 0)
        def _():
            dk_ref[...] = jnp.zeros_like(dk_ref)
            dc_sc[...] = jnp.zeros_like(dc_sc)

        @pl.when((qi == 0) & (hd == 0))
        def _():
            dv_ref[...] = jnp.zeros_like(dv_ref)

        row = lax.broadcasted_iota(jnp.int32, (LANES, 1), 0)
        lane = lax.broadcasted_iota(jnp.int32, (1, LANES), 1)
        dot_h = jnp.where(row // HEAD_DIM == hd, dot_ref[...], jnp.zeros_like(dot_ref))
        do_h = jnp.where(lane // HEAD_DIM == hd, do_ref[...], jnp.zeros_like(do_ref))
        rr = cq_ref[0] - lse_ref[0]
        r1, r2, r3 = _split3(jnp.where(hd == 0, rr[0:1, :], rr[1:2, :]))
        qt = qt_ref[...].astype(F32)
        qt = jnp.where(row == AUG, r1, jnp.where(row == AUG + 1, r2, jnp.where(row == AUG + 2, r3, qt))).astype(BF16)
        mask = _diag_mask(t)

        def pass1(kj0, n, delta, masked=False):
            for g in range(n):
                off = pl.multiple_of((kj0 + g) * t, t)
                st = jnp.dot(k_ref[pl.ds(off, t), :], qt, preferred_element_type=F32)
                if masked:
                    st = jnp.where(mask, st, NEG)
                pt = jnp.exp2(st)
                dpt = jnp.dot(v_ref[pl.ds(off, t), :], dot_h, preferred_element_type=F32)
                p_sc[kj0 + g] = pt
                dp_sc[kj0 + g] = dpt
                delta = delta + jnp.sum(pt * dpt, axis=0, keepdims=True)
            return delta

        delta = _grouped_loop(qi, pass1, jnp.zeros((1, t), F32))
        delta = pass1(qi, 1, delta, masked=True)

        def pass2(kj0, n, dq):
            for g in range(n):
                off = pl.multiple_of((kj0 + g) * t, t)
                pt = p_sc[kj0 + g]
                ds32 = pt * (dp_sc[kj0 + g] - delta)
                ds = ds32.astype(BF16)
                dc_sc[pl.ds(off, t), :] += ds32[:, 0:LANES] + ds32[:, LANES:2 * LANES]
                dk_ref[pl.ds(off, t), :] += jnp.dot(ds, q_ref[...], preferred_element_type=F32)
                dv_ref[pl.ds(off, t), :] += jnp.dot(pt.astype(BF16), do_h, preferred_element_type=F32)
                dq = dq + jnp.dot(kt_ref[0, kj0 + g], ds, preferred_element_type=F32)
            return dq

        dq_ref[...] = _grouped_loop(qi + 1, pass2, jnp.zeros((LANES, t), F32))

        @pl.when(qi == nb - 1)
        def _():
            dk_ref[:, AUG + 3:AUG + 4] = jnp.sum(dc_sc[...], axis=1, keepdims=True)

    once = dict(pipeline_mode=pl.Buffered(1))
    pair_rows = pl.BlockSpec((1, 2, t), lambda h, i: (h // 2, 0, i))
    return pl.pallas_call(
        body, name="attn_bwd_t", grid=(n_heads, nb),
        out_shape=[jax.ShapeDtypeStruct((n_heads * LANES, tp), F32), jax.ShapeDtypeStruct((tp, n_heads * LANES), F32),
                   jax.ShapeDtypeStruct((tp, d), F32)],
        in_specs=[pl.BlockSpec((LANES, t), lambda h, i: (h, i)), pl.BlockSpec((t, LANES), lambda h, i: (i, h)),
                  pl.BlockSpec((tp, LANES), lambda h, i: (0, h), **once),
                  pl.BlockSpec((1, nb, LANES, t), lambda h, i: (h, 0, 0, 0), **once),
                  pl.BlockSpec((tp, LANES), lambda h, i: (0, h // 2), **once),
                  pl.BlockSpec((t, LANES), lambda h, i: (i, h // 2)), pl.BlockSpec((LANES, t), lambda h, i: (h // 2, i)),
                  pair_rows, pair_rows],
        out_specs=[pl.BlockSpec((LANES, t), lambda h, i: (h, i)), pl.BlockSpec((tp, LANES), lambda h, i: (0, h)),
                   pl.BlockSpec((tp, LANES), lambda h, i: (0, h // 2))],
        scratch_shapes=[pltpu.VMEM((nb, t, t), F32), pltpu.VMEM((nb, t, t), F32), pltpu.VMEM((tp, LANES), F32)],
        compiler_params=_cparams(2),
    )(qpt, qp, kp, kpt4, v, do, dot_, lse, cqt)


def rev_cumsum_rows(x):
    r, tp = x.shape
    t = ATT_TILE
    nb = tp // t

    def body(x_ref, o_ref, carry):
        @pl.when(pl.program_id(0) == 0)
        def _():
            carry[...] = jnp.zeros_like(carry)

        xv = x_ref[...]
        r_ = lax.broadcasted_iota(jnp.int32, (t, t), 0)
        c_ = lax.broadcasted_iota(jnp.int32, (t, t), 1)
        o_ref[...] = jnp.dot(xv, (r_ >= c_).astype(F32), precision=HI, preferred_element_type=F32) + carry[:, 0:1]
        carry[...] += jnp.sum(xv, axis=1, keepdims=True)

    return pl.pallas_call(
        body, name="rev_cumsum_rows", grid=(nb,), out_shape=jax.ShapeDtypeStruct((r, tp), F32),
        in_specs=[pl.BlockSpec((r, t), lambda i: (0, nb - 1 - i))],
        out_specs=pl.BlockSpec((r, t), lambda i: (0, nb - 1 - i)),
        scratch_shapes=[pltpu.VMEM((r, LANES), F32)],
        compiler_params=_cparams(1),
    )(x)


def c_elem_bwd(p, b_f, qg, kg, dq, dk, dv, dlogf):
    tp, n_out = p.shape
    d = qg.shape[1]
    tm = ATT_TILE
    n = tp // tm
    seg, seg_t, fold = _head_masks(d)

    def body(p_ref, pf_ref, bf_ref, qg_ref, kg_ref, dq_ref, dk_ref, dv_ref, dlf_ref, seg_ref, segt_ref, fold_ref,
             dp_ref, sm_ref, accq, acck, accf):
        i = pl.program_id(0)

        @pl.when(i == 0)
        def _():
            accq[...] = jnp.zeros_like(accq)
            acck[...] = jnp.zeros_like(acck)
            accf[...] = jnp.zeros_like(accf)

        def norm_bwd(x, g, dy, acc):
            ms = _dot2(x * x, seg_ref[...]) * (1.0 / HEAD_DIM)
            r = _dot2(lax.rsqrt(ms + EPS), segt_ref[...])
            xhat = x * r
            acc[0:1, :] += jnp.sum(dy * xhat, axis=0, keepdims=True)
            gy = dy * g
            mean = _dot2(_dot2(gy * xhat, seg_ref[...]), segt_ref[...]) * (1.0 / HEAD_DIM)
            return r * (gy - xhat * mean)

        dp_ref[:, 0:d] = norm_bwd(p_ref[:, 0:d], qg_ref[...], dq_ref[...] * (HEAD_DIM ** -0.5), accq).astype(BF16)
        dp_ref[:, d:2 * d] = norm_bwd(p_ref[:, d:2 * d], kg_ref[...], dk_ref[...] * LN2, acck).astype(BF16)
        dp_ref[:, 2 * d:3 * d] = dv_ref[...].astype(BF16)
        df = dlf_ref[...] * _sigmoid(-(pf_ref[...] + bf_ref[...]))
        accf[0:1, :] += jnp.sum(df, axis=0, keepdims=True)
        dp_ref[:, 3 * d:] = df.astype(BF16)

        @pl.when(i == n - 1)
        def _():
            sm_ref[...] = jnp.zeros_like(sm_ref)
            sm_ref[0:1, :] = jnp.dot(accq[0:1, :], fold_ref[...].astype(F32), precision=HI, preferred_element_type=F32)
            sm_ref[1:2, :] = jnp.dot(acck[0:1, :], fold_ref[...].astype(F32), precision=HI, preferred_element_type=F32)
            sm_ref[2:3, :] = accf[0:1, :]

    vec = pl.BlockSpec((1, d), lambda i: (0, 0))
    rowd = pl.BlockSpec((tm, d), lambda i: (i, 0))
    rowl = pl.BlockSpec((tm, LANES), lambda i: (i, 0))
    return pl.pallas_call(
        body, name="c_elem_bwd", grid=(n,),
        out_shape=[jax.ShapeDtypeStruct((tp, n_out), BF16), jax.ShapeDtypeStruct((8, LANES), F32)],
        in_specs=[pl.BlockSpec((tm, 2 * d), lambda i: (i, 0)), pl.BlockSpec((tm, LANES), lambda i: (i, 3 * d // LANES)),
                  pl.BlockSpec((1, LANES), lambda i: (0, 0)), vec, vec, rowd, rowd, rowd, rowl,
                  pl.BlockSpec((d, LANES), lambda i: (0, 0)), pl.BlockSpec((LANES, d), lambda i: (0, 0)),
                  pl.BlockSpec((d, LANES), lambda i: (0, 0))],
        out_specs=[pl.BlockSpec((tm, n_out), lambda i: (i, 0)), pl.BlockSpec((8, LANES), lambda i: (0, 0))],
        scratch_shapes=[pltpu.VMEM((8, d), F32), pltpu.VMEM((8, d), F32), pltpu.VMEM((8, LANES), F32)],
        compiler_params=_cparams(1),
    )(p, p, b_f, qg, kg, dq, dk, dv, dlogf, seg, seg_t, fold)


def loss_head(h, tgt, t_real):
    tp, d = h.shape
    tm = _pick(tp, (768, 512, 256))

    def body(h_ref, t_ref, dh_ref, l_ref):
        i = pl.program_id(0)

        @pl.when(i == 0)
        def _():
            l_ref[...] = jnp.zeros_like(l_ref)

        row = i * tm + lax.broadcasted_iota(jnp.int32, (tm, 1), 0)
        valid = (row >= N_META) & (row < t_real)
        e = jnp.where(valid, h_ref[...] - t_ref[...], 0.0)
        dh_ref[...] = e * (1.0 / d)
        per_row = jnp.sum(e * e, axis=-1, keepdims=True) * (1.0 / d)
        l_ref[...] += 0.5 * jnp.sum(per_row, axis=0, keepdims=True)

    return pl.pallas_call(
        body, name="loss_head", grid=(tp // tm,),
        out_shape=[jax.ShapeDtypeStruct((tp, d), F32), jax.ShapeDtypeStruct((8, LANES), F32)],
        in_specs=[pl.BlockSpec((tm, d), lambda i: (i, 0)), pl.BlockSpec((tm, d), lambda i: (i, 0))],
        out_specs=[pl.BlockSpec((tm, d), lambda i: (i, 0)), pl.BlockSpec((8, LANES), lambda i: (0, 0))],
        compiler_params=_cparams(1),
    )(h, tgt)


def sum_devices(x):
    _, r, c = x.shape

    def body(x_ref, o_ref):
        acc = x_ref[0]
        for dev in range(1, N_DEV):
            acc = acc + x_ref[dev]
        o_ref[...] = acc

    return pl.pallas_call(
        body, name="sum_devices", out_shape=jax.ShapeDtypeStruct((r, c), F32),
        in_specs=[pl.BlockSpec(memory_space=pltpu.VMEM)], out_specs=pl.BlockSpec(memory_space=pltpu.VMEM),
    )(x)


def _adamw_math(w, g, m, v):
    m = ADAM_B1 * m + (1.0 - ADAM_B1) * g
    v = ADAM_B2 * v + (1.0 - ADAM_B2) * (g * g)
    m_hat = m / (1.0 - ADAM_B1 ** ADAM_STEP)
    v_hat = v / (1.0 - ADAM_B2 ** ADAM_STEP)
    delta = -ADAM_LR * (m_hat / (jnp.sqrt(v_hat) + ADAM_EPS) + ADAM_WD * w)
    return delta, m, v


def adamw_small(w, g, m, v):
    def body(w_ref, g_ref, m_ref, v_ref, d_ref, nm_ref, nv_ref):
        d_ref[...], nm_ref[...], nv_ref[...] = _adamw_math(w_ref[...], g_ref[...], m_ref[...], v_ref[...])

    vm = pl.BlockSpec(memory_space=pltpu.VMEM)
    return pl.pallas_call(
        body, name="adamw_small", out_shape=[jax.ShapeDtypeStruct(w.shape, F32)] * 3,
        in_specs=[vm] * 4, out_specs=[vm] * 3,
    )(w, g, m, v)


def adamw_reduce(w, m, v, parts):
    r, c = w.shape
    n_parts = parts.shape[0]
    tr = _pick(r, (512, 384, 352, 256, 128, 8))

    def body(w_ref, m_ref, v_ref, p_ref, g_ref, d_ref, nm_ref, nv_ref):
        g = p_ref[0].astype(F32)
        for j in range(1, n_parts):
            g = g + p_ref[j].astype(F32)
        g_ref[...] = g
        d_ref[...], nm_ref[...], nv_ref[...] = _adamw_math(w_ref[...], g, m_ref[...], v_ref[...])

    blk = pl.BlockSpec((tr, c), lambda i: (i, 0))
    return pl.pallas_call(
        body, name="adamw_reduce", grid=(r // tr,), out_shape=[jax.ShapeDtypeStruct((r, c), F32)] * 4,
        in_specs=[blk, blk, blk, pl.BlockSpec((n_parts, tr, c), lambda i: (0, i, 0))],
        out_specs=[blk] * 4, compiler_params=_cparams(1),
    )(w, m, v, parts)


def _unshard(g, axis):
    g = jnp.moveaxis(g, 0, axis)
    return g.reshape(g.shape[:axis] + (g.shape[axis] * g.shape[axis + 1],) + g.shape[axis + 2:])


def _shard(full, axis):
    s = full.shape
    g = full.reshape(s[:axis] + (N_DEV, s[axis] // N_DEV) + s[axis + 1:])
    return jnp.moveaxis(g, axis, 0)


def _pad_lanes(a, n=LANES):
    flat = a.reshape(-1)
    pad = (-flat.shape[0]) % n
    return jnp.pad(flat, (0, pad)).reshape(-1, n)


BIG = ("ffn_w_gate", "ffn_w_up", "ffn_w_down", "a_w_in", "a_w_out", "b_w_in", "b_w_out", "c_w_in", "c_w_out")
SHARD_AXIS = {"ffn_w_gate": 3, "ffn_w_up": 3, "ffn_w_down": 2, "a_w_in": 2, "a_w_out": 1, "b_w_in": 2,
              "b_w_out": 1, "c_w_in": 2, "c_w_out": 1, "meta": 1, "ffn_norm": 2, "a_conv": 2, "b_conv": 2}
SMALL_SHARDED = ("meta", "ffn_norm", "a_conv", "b_conv")
SMALL_REPL = ("mix_norm", "b_conv_bias", "b_ln_g", "b_ln_b", "c_b_f", "c_q_norm", "c_k_norm")
WEIGHTS = ("meta", "ffn_norm", "ffn_w_gate", "ffn_w_up", "ffn_w_down", "mix_norm", "a_w_in", "a_conv", "a_w_out",
           "b_w_in", "b_conv", "b_conv_bias", "b_ln_g", "b_ln_b", "b_w_out", "c_w_in", "c_b_f", "c_q_norm",
           "c_k_norm", "c_w_out")
N_MIXERS = 3


def kernel(x, meta, ffn_norm, ffn_w_gate, ffn_w_up, ffn_w_down, mix_norm, a_w_in, a_conv, a_w_out, b_w_in, b_conv, b_conv_bias, b_ln_g, b_ln_b, b_w_out, c_w_in, c_b_f, c_q_norm, c_k_norm, c_w_out, loss_target, m_meta, m_ffn_norm, m_ffn_w_gate, m_ffn_w_up, m_ffn_w_down, m_mix_norm, m_a_w_in, m_a_conv, m_a_w_out, m_b_w_in, m_b_conv, m_b_conv_bias, m_b_ln_g, m_b_ln_b, m_b_w_out, m_c_w_in, m_c_b_f, m_c_q_norm, m_c_k_norm, m_c_w_out, v_meta, v_ffn_norm, v_ffn_w_gate, v_ffn_w_up, v_ffn_w_down, v_mix_norm, v_a_w_in, v_a_conv, v_a_w_out, v_b_w_in, v_b_conv, v_b_conv_bias, v_b_ln_g, v_b_ln_b, v_b_w_out, v_c_w_in, v_c_b_f, v_c_q_norm, v_c_k_norm, v_c_w_out):
    local = dict(locals())
    w = {n: local[n] for n in WEIGHTS}
    mom = {n: local["m_" + n] for n in WEIGHTS}
    var = {n: local["v_" + n] for n in WEIGHTS}
    d = x.shape[-1]
    depth = ffn_norm.shape[0]
    n_heads = d // HEAD_DIM
    seq = x.shape[1]
    t_real = N_META + seq
    tp = -(-t_real // ROW_ALIGN) * ROW_ALIGN
    nb = tp // ATT_TILE
    npair = d // LANES

    names = BIG + SMALL_SHARDED
    gathered = all_gather([w[n].astype(BF16) for n in BIG] + [w[n] for n in SMALL_SHARDED])
    full = {n: _unshard(g, SHARD_AXIS[n]) for n, g in zip(names, gathered)}
    n_c = full["c_w_in"].shape[-1]
    n_cp = 3 * d + LANES
    c_w_qkv = jnp.pad(full["c_w_in"], ((0, 0), (0, 0), (0, n_cp - n_c)))
    b_f_pad = jnp.pad(c_b_f, ((0, 0), (0, LANES - n_heads)))
    qg_t = jnp.tile(c_q_norm, (1, n_heads))
    kg_t = jnp.tile(c_k_norm, (1, n_heads))

    h = jnp.concatenate([full["meta"], x[0], jnp.zeros((tp - t_real, d), F32)], axis=0)
    saved = []
    for i in range(depth):
        mixer, j = i % N_MIXERS, i // N_MIXERS
        s = {"h0": h}
        h, s["gate_a"], s["up_a"] = ffn_fwd(h, full["ffn_norm"][i, 0:1], full["ffn_w_gate"][i, 0],
                                            full["ffn_w_up"][i, 0], full["ffn_w_down"][i, 0])
        s["h1"] = h
        g_mix = mix_norm[i:i + 1]
        if mixer == 0:
            s["p"], s["u"] = norm_matmul(h, g_mix, full["a_w_in"][j], BF16, True)
            s["y"] = a_elem_fwd(s["p"], full["a_conv"][j])
            h = res_matmul(h, s["y"], full["a_w_out"][j])
        elif mixer == 1:
            s["p"], s["u"] = norm_matmul(h, g_mix, full["b_w_in"][j], BF16, True)
            s["y"] = b_elem_fwd(s["p"], full["b_conv"][j], b_conv_bias[j:j + 1], b_ln_g[j:j + 1], b_ln_b[j:j + 1])
            h = res_matmul(h, s["y"], full["b_w_out"][j])
        else:
            s["p"], s["u"] = norm_matmul(h, g_mix, c_w_qkv[j], F32, True)
            s["qp"], s["kp"], s["v"], cum = c_prep_slots(s["p"], b_f_pad[j:j + 1], qg_t[j:j + 1], kg_t[j:j + 1])
            s["cqt"] = (cum[:, :n_heads].T * L2E).reshape(npair, 2, tp)
            s["qpt"] = s["qp"].T
            s["kpt4"] = s["kp"].T.reshape(n_heads, LANES, nb, ATT_TILE).transpose(0, 2, 1, 3)
            vt4 = s["v"].T.reshape(npair, LANES, nb, ATT_TILE).transpose(0, 2, 1, 3)
            ot, s["lse"] = attn_fwd_t(s["qpt"], s["kp"], vt4)
            s["y"] = ot.T
            h = res_matmul(h, s["y"], full["c_w_out"][j])
        s["h2"] = h
        h, s["gate_b"], s["up_b"] = ffn_fwd(h, full["ffn_norm"][i, 1:2], full["ffn_w_gate"][i, 1],
                                            full["ffn_w_up"][i, 1], full["ffn_w_down"][i, 1])
        saved.append(s)

    tgt = jnp.concatenate([jnp.zeros((N_META, d), F32), loss_target[0], jnp.zeros((tp - t_real, d), F32)], axis=0)
    dh, loss_part = loss_head(h, tgt, t_real)

    gfull = {n: [None] * full[n].shape[0] for n in ("a_w_in", "a_w_out", "b_w_in", "b_w_out", "c_w_in", "c_w_out")}
    gffn = {n: [[None, None] for _ in range(depth)] for n in ("ffn_w_gate", "ffn_w_up", "ffn_w_down")}
    g_ffn_norm = [[None, None] for _ in range(depth)]
    g_mix_norm = [None] * depth
    g_a_conv = [None] * a_conv.shape[0]
    g_b_conv = [None] * b_conv.shape[0]
    g_b_stats = [None] * b_conv.shape[0]
    g_c_small = [None] * c_b_f.shape[0]

    def ffn_backward(dh_out, i, half, h_in, gate, up):
        dh_in, hn, dgate, dup, act, dg = ffn_bwd(dh_out, h_in, full["ffn_norm"][i, half:half + 1], gate, up,
                                                 full["ffn_w_gate"][i, half], full["ffn_w_up"][i, half],
                                                 full["ffn_w_down"][i, half])
        gffn["ffn_w_gate"][i][half] = atb(hn, dgate)
        gffn["ffn_w_up"][i][half] = atb(hn, dup)
        gffn["ffn_w_down"][i][half] = atb(act, dh_out, 0.5)
        g_ffn_norm[i][half] = dg
        return dh_in

    for i in reversed(range(depth)):
        mixer, j = i % N_MIXERS, i // N_MIXERS
        s = saved[i]
        g_mix = mix_norm[i:i + 1]
        dh = ffn_backward(dh, i, 1, s["h2"], s["gate_b"], s["up_b"])
        if mixer == 0:
            dy = matmul_nt(dh, full["a_w_out"][j])
            dp, dwc = a_elem_bwd(s["p"], dy, full["a_conv"][j])
            g_a_conv[j] = dwc[:a_conv.shape[1]]
            gfull["a_w_out"][j] = atb(s["y"], dh)
            gfull["a_w_in"][j] = atb(s["u"], dp)
            dh, g_mix_norm[i] = matmul_nt_rms_bwd(dp, full["a_w_in"][j], s["h1"], g_mix, dh)
        elif mixer == 1:
            dy = matmul_nt(dh, full["b_w_out"][j])
            du2, g_b_stats[j] = b_elem_bwd1(s["p"], dy, full["b_conv"][j], b_conv_bias[j:j + 1],
                                            b_ln_g[j:j + 1], b_ln_b[j:j + 1])
            dp, dwc = b_elem_bwd2(s["p"], du2, full["b_conv"][j])
            g_b_conv[j] = dwc[:b_conv.shape[1]]
            gfull["b_w_out"][j] = atb(s["y"], dh)
            gfull["b_w_in"][j] = atb(s["u"], dp)
            dh, g_mix_norm[i] = matmul_nt_rms_bwd(dp, full["b_w_in"][j], s["h1"], g_mix, dh)
        else:
            do = matmul_nt(dh, full["c_w_out"][j])
            dqt, dkp, dv = attn_bwd_t(s["qpt"], s["qp"], s["kp"], s["kpt4"], s["v"], do, do.T, s["lse"], s["cqt"])
            dq = dqt.reshape(n_heads, LANES, tp)[:, :HEAD_DIM].reshape(d, tp).T
            dkp = dkp.reshape(tp, n_heads, LANES)
            dk = dkp[:, :, :HEAD_DIM].reshape(tp, d)
            dlogf = rev_cumsum_rows(-dkp[:, :, AUG + 3].T)
            dlogf = jnp.pad(dlogf.T, ((0, 0), (0, LANES - n_heads)))
            dp, g_c_small[j] = c_elem_bwd(s["p"], b_f_pad[j:j + 1], qg_t[j:j + 1], kg_t[j:j + 1], dq, dk, dv, dlogf)
            gfull["c_w_out"][j] = atb(s["y"], dh)
            gfull["c_w_in"][j] = atb(s["u"], dp)[:, :n_c]
            dh, g_mix_norm[i] = matmul_nt_rms_bwd(dp, c_w_qkv[j], s["h1"], g_mix, dh)
        dh = ffn_backward(dh, i, 0, s["h0"], s["gate_a"], s["up_a"])

    grad_x = dh[N_META:t_real][None]

    gbig = {n: jnp.stack([jnp.stack(r) for r in gffn[n]]) for n in gffn}
    gbig.update({n: jnp.stack(gfull[n]) for n in gfull})
    def by_core(n):
        g = _shard(gbig[n], SHARD_AXIS[n])
        return jnp.moveaxis(g.reshape((2, 2, 2) + g.shape[1:]), 2, 0).reshape((2, N_CHIP) + g.shape[1:])

    halves = [by_core(n) for n in BIG]
    chip_sums = []
    for hv, th in zip(halves, swap_with_sibling(halves)):
        cols = th.shape[-1]
        chip_sums.append(add_own_half(hv.reshape(2, -1, cols), th.reshape(-1, cols)).reshape(th.shape))
    parts = exchange_chips(chip_sums)
    out_g, out_d, out_m, out_v = {}, {}, {}, {}
    for n, prt in zip(BIG, parts):
        shp = w[n].shape
        flat = lambda a: a.reshape(-1, shp[-1])
        res = adamw_reduce(flat(w[n]), flat(mom[n]), flat(var[n]), prt.reshape(N_CHIP, -1, shp[-1]))
        out_g[n], out_d[n], out_m[n], out_v[n] = [r.reshape(shp) for r in res]

    def rows(a):
        return a.reshape(-1, d)

    def lane_rows(a):
        return jnp.pad(a.reshape(1, -1), ((0, 0), (0, d - a.size)))

    c_small = jnp.stack(g_c_small)
    pieces = [("meta", dh[:N_META]),
              ("ffn_norm", rows(jnp.stack([jnp.stack(r) for r in g_ffn_norm]))),
              ("mix_norm", rows(jnp.stack(g_mix_norm))),
              ("a_conv", rows(jnp.stack(g_a_conv))),
              ("b_conv", rows(jnp.stack(g_b_conv))),
              ("b_ln_g", rows(jnp.stack([st[0] for st in g_b_stats]))),
              ("b_ln_b", rows(jnp.stack([st[1] for st in g_b_stats]))),
              ("b_conv_bias", rows(jnp.stack([st[2] for st in g_b_stats]))),
              ("c_q_norm", lane_rows(c_small[:, 0, :HEAD_DIM])),
              ("c_k_norm", lane_rows(c_small[:, 1, :HEAD_DIM])),
              ("c_b_f", lane_rows(c_small[:, 2, :n_heads])),
              ("loss", lane_rows(loss_part[0:1, 0:1]))]
    packed = jnp.concatenate([pc for _, pc in pieces], axis=0)
    n_rows = packed.shape[0]
    packed = jnp.pad(packed, ((0, (-n_rows) % 8), (0, 0)))
    total = sum_devices(all_gather([packed])[0])
    small_g, r0 = {}, 0
    for n, pc in pieces:
        small_g[n] = total[r0:r0 + pc.shape[0]]
        r0 += pc.shape[0]
    loss = small_g.pop("loss")[0, 0]
    me = 4 * lax.axis_index("x") + 2 * lax.axis_index("y") + lax.axis_index("c")
    for n in SMALL_SHARDED:
        cols = w[n].shape[-1]
        g = lax.dynamic_slice_in_dim(small_g[n], me * cols, cols, axis=1)
        out_g[n] = g.reshape(w[n].shape)
    for n in SMALL_REPL:
        out_g[n] = small_g[n].reshape(-1)[:w[n].size].reshape(w[n].shape)
    small = SMALL_SHARDED + SMALL_REPL
    pack = lambda dct: jnp.concatenate([_pad_lanes(dct[n]) for n in small], axis=0)
    res = adamw_small(pack(w), pack(out_g), pack(mom), pack(var))
    r0 = 0
    for n in small:
        nr = -(-w[n].size // LANES)
        for dct, arr in zip((out_d, out_m, out_v), res):
            dct[n] = arr[r0:r0 + nr].reshape(-1)[:w[n].size].reshape(w[n].shape)
        r0 += nr

    return (loss, grad_x, *[out_g[n] for n in WEIGHTS], *[out_d[n] for n in WEIGHTS],
            *[out_m[n] for n in WEIGHTS], *[out_v[n] for n in WEIGHTS])
```

```python
import functools

import jax
import jax.numpy as jnp
from jax import lax
from jax.experimental import pallas as pl
from jax.experimental.pallas import tpu as pltpu

F32 = jnp.float32
BF16 = jnp.bfloat16
HI = lax.Precision.HIGHEST
EPS = 1e-6
NEG = -1e30
N_META = 16
HEAD_DIM = 64
LANES = 128
N_DEV = 8
ROW_ALIGN = 256
VMEM_LIMIT = 56 * 1024 * 1024
ADAM_LR, ADAM_B1, ADAM_B2, ADAM_EPS, ADAM_WD, ADAM_STEP = 0.001, 0.9, 0.999, 1e-08, 0.01, 10
MESH = pl.DeviceIdType.MESH
NT = (((1,), (1,)), ((), ()))
TN = (((0,), (0,)), ((), ()))


def _cparams(n_axes):
    return pltpu.CompilerParams(dimension_semantics=("arbitrary",) * n_axes, vmem_limit_bytes=VMEM_LIMIT)


def _pick(n, cands):
    for c in cands:
        if n % c == 0:
            return c
    raise ValueError(f"no tile for {n} among {cands}")


def _sigmoid(x):
    return 1.0 / (1.0 + jnp.exp(-x))


def _rms(x):
    r = lax.rsqrt(jnp.mean(x * x, axis=-1, keepdims=True) + EPS)
    return x * r, r


def _rms_bwd(xhat, r, g, dy):
    gy = dy * g
    dx = r * (gy - xhat * jnp.mean(gy * xhat, axis=-1, keepdims=True))
    return dx, jnp.sum(dy * xhat, axis=0, keepdims=True)


def _mesh_pos():
    return lax.axis_index("x"), lax.axis_index("y"), lax.axis_index("c")


def all_gather(arrs):
    n = len(arrs)
    hbm = pl.BlockSpec(memory_space=pltpu.HBM)

    def body(*refs):
        ins, outs = refs[:n], refs[n:2 * n]
        send_sems, recv_sems, local_sems = refs[2 * n:]
        x, y, c = _mesh_pos()
        me, sibling = (x, y, c), (x, y, 1 - c)
        xn, yn, dg = (1 - x, y), (x, 1 - y), (1 - x, 1 - y)

        def slot(a, px, py, pc):
            return outs[a].at[4 * px + 2 * py + pc]

        def copy(a, k, block, to, src=None):
            return pltpu.make_async_remote_copy(
                src_ref=slot(a, *block) if src is None else src, dst_ref=slot(a, *block),
                send_sem=send_sems.at[7 * a + k], recv_sem=recv_sems.at[7 * a + k],
                device_id=to, device_id_type=MESH)

        own, sent = [], []

        def start(cp):
            cp.start()
            sent.append(cp)

        for a in range(n):
            cp = pltpu.make_async_copy(ins[a], slot(a, *me), local_sems.at[a])
            cp.start()
            own.append(cp)
        for a in range(n):
            start(copy(a, 0, me, sibling, src=ins[a]))
            start(copy(a, 1, me, (*xn, c), src=ins[a]))
            start(copy(a, 2, me, (*yn, c), src=ins[a]))
        for a in range(n):
            via_x = a % 2 == 0
            copy(a, 1, (*xn, c), me).wait_recv()
            if via_x:
                start(copy(a, 3, (*xn, c), (*yn, c)))
            start(copy(a, 4, (*xn, c), sibling))
            copy(a, 2, (*yn, c), me).wait_recv()
            if not via_x:
                start(copy(a, 3, (*yn, c), (*xn, c)))
            start(copy(a, 5, (*yn, c), sibling))
        for a in range(n):
            copy(a, 3, (*dg, c), me).wait_recv()
            start(copy(a, 6, (*dg, c), sibling))
        for a in range(n):
            copy(a, 0, sibling, me).wait_recv()
            for k, chip in ((4, xn), (5, yn), (6, dg)):
                copy(a, k, (*chip, 1 - c), me).wait_recv()
        for cp in sent:
            cp.wait_send()
        for cp in own:
            cp.wait()

    return pl.pallas_call(
        body, name="all_gather",
        out_shape=[jax.ShapeDtypeStruct((N_DEV,) + a.shape, a.dtype) for a in arrs],
        in_specs=[hbm] * n, out_specs=[hbm] * n,
        scratch_shapes=[pltpu.SemaphoreType.DMA((7 * n,)), pltpu.SemaphoreType.DMA((7 * n,)),
                        pltpu.SemaphoreType.DMA((n,))],
    )(*arrs)


N_CHIP = 4


def swap_with_sibling(arrs):
    n = len(arrs)
    hbm = pl.BlockSpec(memory_space=pltpu.HBM)

    def body(*refs):
        ins, outs = refs[:n], refs[n:2 * n]
        send_sems, recv_sems = refs[2 * n:]
        x, y, c = _mesh_pos()
        copies = [pltpu.make_async_remote_copy(
            src_ref=ins[a].at[1 - c], dst_ref=outs[a], send_sem=send_sems.at[a], recv_sem=recv_sems.at[a],
            device_id=(x, y, 1 - c), device_id_type=MESH) for a in range(n)]
        for cp in copies:
            cp.start()
        for cp in copies:
            cp.wait()

    return pl.pallas_call(
        body, name="swap_with_sibling",
        out_shape=[jax.ShapeDtypeStruct(a.shape[1:], a.dtype) for a in arrs],
        in_specs=[hbm] * n, out_specs=[hbm] * n,
        scratch_shapes=[pltpu.SemaphoreType.DMA((n,)), pltpu.SemaphoreType.DMA((n,))],
    )(*arrs)


def exchange_chips(arrs):
    n = len(arrs)
    hbm = pl.BlockSpec(memory_space=pltpu.HBM)

    def body(*refs):
        ins, outs = refs[:n], refs[n:2 * n]
        send_sems, recv_sems, local_sems = refs[2 * n:]
        x, y, c = _mesh_pos()
        me = 2 * x + y
        own, sent = [], []
        for a in range(n):
            cp = pltpu.make_async_copy(ins[a].at[me], outs[a].at[me], local_sems.at[a])
            cp.start()
            own.append(cp)
        for k in range(1, N_CHIP):
            px = 1 - x if k & 2 else x
            py = 1 - y if k & 1 else y
            peer = 2 * px + py
            for a in range(n):
                cp = pltpu.make_async_remote_copy(
                    src_ref=ins[a].at[peer], dst_ref=outs[a].at[me],
                    send_sem=send_sems.at[3 * a + k - 1], recv_sem=recv_sems.at[3 * a + k - 1],
                    device_id=(px, py, c), device_id_type=MESH)
                cp.start()
                sent.append((cp, a, k, peer))
        for cp, a, k, peer in sent:
            pltpu.make_async_remote_copy(
                src_ref=ins[a].at[peer], dst_ref=outs[a].at[peer],
                send_sem=send_sems.at[3 * a + k - 1], recv_sem=recv_sems.at[3 * a + k - 1],
                device_id=(x, y, c), device_id_type=MESH).wait_recv()
        for cp, a, k, peer in sent:
            cp.wait_send()
        for cp in own:
            cp.wait()

    return pl.pallas_call(
        body, name="exchange_chips",
        out_shape=[jax.ShapeDtypeStruct(a.shape, a.dtype) for a in arrs],
        in_specs=[hbm] * n, out_specs=[hbm] * n,
        scratch_shapes=[pltpu.SemaphoreType.DMA((3 * n,)), pltpu.SemaphoreType.DMA((3 * n,)),
                        pltpu.SemaphoreType.DMA((n,))],
    )(*arrs)


def add_own_half(halves, theirs):
    _, r, c = halves.shape
    tr = _pick(r, (2048, 1408, 1024, 512, 256, 128, 8))

    def body(a_ref, b_ref, o_ref):
        mine = a_ref[lax.axis_index("c")]
        o_ref[...] = (mine.astype(F32) + b_ref[...].astype(F32)).astype(BF16)

    blk = pl.BlockSpec((tr, c), lambda i: (i, 0))
    return pl.pallas_call(
        body, name="add_own_half", grid=(r // tr,), out_shape=jax.ShapeDtypeStruct((r, c), BF16),
        in_specs=[pl.BlockSpec((2, tr, c), lambda i: (0, i, 0)), blk], out_specs=blk, compiler_params=_cparams(1),
    )(halves, theirs)


FFN_CHUNK = (1408, 1024, 512, 256, 128)


def _resident(shape):
    return pl.BlockSpec(shape, lambda i: (0,) * len(shape), pipeline_mode=pl.Buffered(1))


def ffn_fwd(h, g, wg, wu, wd):
    tp, d = h.shape
    f = wg.shape[1]
    tm = _pick(tp, (384, 256))
    tf = _pick(f, FFN_CHUNK)

    def body(h_ref, g_ref, wg_ref, wu_ref, wd_ref, ho_ref, gate_ref, up_ref):
        x = h_ref[...]
        xhat, _ = _rms(x)
        hn = (xhat * g_ref[...]).astype(BF16)
        acc = jnp.zeros((tm, d), F32)
        for c0 in range(0, f, tf):
            gate = jnp.dot(hn, wg_ref[:, c0:c0 + tf], preferred_element_type=F32)
            up = jnp.dot(hn, wu_ref[:, c0:c0 + tf], preferred_element_type=F32)
            gate_ref[:, c0:c0 + tf] = gate.astype(BF16)
            up_ref[:, c0:c0 + tf] = up.astype(BF16)
            act = (gate * _sigmoid(gate) * up).astype(BF16)
            acc = acc + jnp.dot(act, wd_ref[c0:c0 + tf, :], preferred_element_type=F32)
        ho_ref[...] = x + 0.5 * acc

    row = lambda n: pl.BlockSpec((tm, n), lambda i: (i, 0))
    return pl.pallas_call(
        body, name="ffn_fwd", grid=(tp // tm,),
        out_shape=[jax.ShapeDtypeStruct((tp, d), F32), jax.ShapeDtypeStruct((tp, f), BF16),
                   jax.ShapeDtypeStruct((tp, f), BF16)],
        in_specs=[row(d), _resident((1, d)), _resident((d, f)), _resident((d, f)), _resident((f, d))],
        out_specs=[row(d), row(f), row(f)],
        compiler_params=_cparams(1),
    )(h, g, wg, wu, wd)


def ffn_bwd(dho, h, g, gate, up, wg, wu, wd):
    tp, d = h.shape
    f = wg.shape[1]
    tm = _pick(tp, (256,))
    tf = _pick(f, FFN_CHUNK)

    def body(dho_ref, h_ref, g_ref, gate_ref, up_ref, wg_ref, wu_ref, wd_ref,
             dhi_ref, hn_ref, dgate_ref, dup_ref, act_ref, dg_ref):
        @pl.when(pl.program_id(0) == 0)
        def _():
            dg_ref[...] = jnp.zeros_like(dg_ref)

        dho_ = dho_ref[...]
        dout = (0.5 * dho_).astype(BF16)
        dhn = jnp.zeros((tm, d), F32)
        for c0 in range(0, f, tf):
            dact = lax.dot_general(dout, wd_ref[c0:c0 + tf, :], NT, preferred_element_type=F32)
            gt = gate_ref[:, c0:c0 + tf].astype(F32)
            u = up_ref[:, c0:c0 + tf].astype(F32)
            sig = _sigmoid(gt)
            silu = gt * sig
            act_ref[:, c0:c0 + tf] = (silu * u).astype(BF16)
            dup = (dact * silu).astype(BF16)
            dgate = (dact * u * (sig * (1.0 + gt * (1.0 - sig)))).astype(BF16)
            dup_ref[:, c0:c0 + tf] = dup
            dgate_ref[:, c0:c0 + tf] = dgate
            dhn = dhn + (lax.dot_general(dgate, wg_ref[:, c0:c0 + tf], NT, preferred_element_type=F32)
                         + lax.dot_general(dup, wu_ref[:, c0:c0 + tf], NT, preferred_element_type=F32))
        xhat, r = _rms(h_ref[...])
        gg = g_ref[...]
        dx, dgp = _rms_bwd(xhat, r, gg, dhn)
        dg_ref[...] += dgp
        dhi_ref[...] = dho_ + dx
        hn_ref[...] = (xhat * gg).astype(BF16)

    row = lambda n: pl.BlockSpec((tm, n), lambda i: (i, 0))
    return pl.pallas_call(
        body, name="ffn_bwd", grid=(tp // tm,),
        out_shape=[jax.ShapeDtypeStruct((tp, d), F32), jax.ShapeDtypeStruct((tp, d), BF16),
                   jax.ShapeDtypeStruct((tp, f), BF16), jax.ShapeDtypeStruct((tp, f), BF16),
                   jax.ShapeDtypeStruct((tp, f), BF16), jax.ShapeDtypeStruct((1, d), F32)],
        in_specs=[row(d), row(d), _resident((1, d)), row(f), row(f),
                  _resident((d, f)), _resident((d, f)), _resident((f, d))],
        out_specs=[row(d), row(d), row(f), row(f), row(f), pl.BlockSpec((1, d), lambda i: (0, 0))],
        compiler_params=_cparams(1),
    )(dho, h, g, gate, up, wg, wu, wd)


def atb(a, b, scale=1.0):
    tp, m = a.shape
    n = b.shape[1]
    tmm = _pick(m, (1024, 1408, 640, 512, 256, 128))
    tn = _pick(n, (1024, 1408, 640, 512, 256, 128))
    tk = _pick(tp, (1408, 768, 512, 256))
    nk = tp // tk

    def body(a_ref, b_ref, o_ref, acc):
        k = pl.program_id(2)

        @pl.when(k == 0)
        def _():
            acc[...] = jnp.zeros_like(acc)

        acc[...] += lax.dot_general(a_ref[...].astype(BF16), b_ref[...].astype(BF16), TN,
                                    preferred_element_type=F32)

        @pl.when(k == nk - 1)
        def _():
            o_ref[...] = (acc[...] * scale).astype(BF16)

    return pl.pallas_call(
        body, name="atb", grid=(m // tmm, n // tn, nk),
        out_shape=jax.ShapeDtypeStruct((m, n), BF16),
        in_specs=[pl.BlockSpec((tk, tmm), lambda i, j, k: (k, i)), pl.BlockSpec((tk, tn), lambda i, j, k: (k, j))],
        out_specs=pl.BlockSpec((tmm, tn), lambda i, j, k: (i, j)),
        scratch_shapes=[pltpu.VMEM((tmm, tn), F32)],
        compiler_params=_cparams(3),
    )(a, b)


def norm_matmul(h, g, w, out_dtype, emit_u):
    tp, d = h.shape
    n = w.shape[1]
    tm = _pick(tp, (768, 512, 256))
    tn = _pick(n, (1024, 768, 640, 512, 256, 128))

    def body(h_ref, g_ref, w_ref, o_ref, *rest):
        u_sc = rest[-1]

        @pl.when(pl.program_id(1) == 0)
        def _():
            xhat, _ = _rms(h_ref[...])
            u_sc[...] = (xhat * g_ref[...]).astype(BF16)
            if emit_u:
                rest[0][...] = u_sc[...]

        o_ref[...] = jnp.dot(u_sc[...], w_ref[...], preferred_element_type=F32).astype(out_dtype)

    out_shape = [jax.ShapeDtypeStruct((tp, n), out_dtype)]
    out_specs = [pl.BlockSpec((tm, tn), lambda i, j: (i, j))]
    if emit_u:
        out_shape.append(jax.ShapeDtypeStruct((tp, d), BF16))
        out_specs.append(pl.BlockSpec((tm, d), lambda i, j: (i, 0)))
    return pl.pallas_call(
        body, name="norm_matmul", grid=(tp // tm, n // tn), out_shape=out_shape,
        in_specs=[pl.BlockSpec((tm, d), lambda i, j: (i, 0)), pl.BlockSpec((1, d), lambda i, j: (0, 0)),
                  pl.BlockSpec((d, tn), lambda i, j: (0, j))],
        out_specs=out_specs, scratch_shapes=[pltpu.VMEM((tm, d), BF16)],
        compiler_params=_cparams(2),
    )(h, g, w)


def res_matmul(h, y, w):
    tp, d = h.shape
    k = y.shape[1]
    tm = _pick(tp, (768, 512, 256))

    def body(h_ref, y_ref, w_ref, o_ref):
        o_ref[...] = h_ref[...] + jnp.dot(y_ref[...], w_ref[...], preferred_element_type=F32)

    return pl.pallas_call(
        body, name="res_matmul", grid=(tp // tm,), out_shape=jax.ShapeDtypeStruct((tp, d), F32),
        in_specs=[pl.BlockSpec((tm, d), lambda i: (i, 0)), pl.BlockSpec((tm, k), lambda i: (i, 0)),
                  pl.BlockSpec((k, d), lambda i: (0, 0))],
        out_specs=pl.BlockSpec((tm, d), lambda i: (i, 0)),
        compiler_params=_cparams(1),
    )(h, y, w)


def matmul_nt(a, w):
    tp, d = a.shape
    k = w.shape[0]
    tm = _pick(tp, (768, 512, 256))

    def body(a_ref, w_ref, o_ref):
        o_ref[...] = lax.dot_general(a_ref[...].astype(BF16), w_ref[...], NT,
                                     preferred_element_type=F32).astype(BF16)

    return pl.pallas_call(
        body, name="matmul_nt", grid=(tp // tm,), out_shape=jax.ShapeDtypeStruct((tp, k), BF16),
        in_specs=[pl.BlockSpec((tm, d), lambda i: (i, 0)), pl.BlockSpec((k, d), lambda i: (0, 0))],
        out_specs=pl.BlockSpec((tm, k), lambda i: (i, 0)),
        compiler_params=_cparams(1),
    )(a, w)


def matmul_nt_rms_bwd(dp, w, h, g, dres):
    tp, d = h.shape
    n = w.shape[1]
    tm = _pick(tp, (768, 512, 256))

    def body(dp_ref, w_ref, h_ref, g_ref, dres_ref, dh_ref, dg_ref):
        @pl.when(pl.program_id(0) == 0)
        def _():
            dg_ref[...] = jnp.zeros_like(dg_ref)

        du = lax.dot_general(dp_ref[...], w_ref[...], NT, preferred_element_type=F32)
        xhat, r = _rms(h_ref[...])
        dx, dgp = _rms_bwd(xhat, r, g_ref[...], du)
        dg_ref[...] += dgp
        dh_ref[...] = dres_ref[...] + dx

    return pl.pallas_call(
        body, name="matmul_nt_rms_bwd", grid=(tp // tm,),
        out_shape=[jax.ShapeDtypeStruct((tp, d), F32), jax.ShapeDtypeStruct((1, d), F32)],
        in_specs=[pl.BlockSpec((tm, n), lambda i: (i, 0)), _resident((d, n)),
                  pl.BlockSpec((tm, d), lambda i: (i, 0)), _resident((1, d)),
                  pl.BlockSpec((tm, d), lambda i: (i, 0))],
        out_specs=[pl.BlockSpec((tm, d), lambda i: (i, 0)), pl.BlockSpec((1, d), lambda i: (0, 0))],
        compiler_params=_cparams(1),
    )(dp, w, h, g, dres)


CONV_TILE = (256,)


def _halo_before(tm, hb, col=0):
    return lambda i: (jnp.maximum(i * (tm // hb) - 1, 0), col)


def _halo_after(tm, hb, nblk, col=0):
    return lambda i: (jnp.minimum((i + 1) * (tm // hb), nblk - 1), col)


def a_elem_fwd(p, w):
    tp, d3 = p.shape
    d = d3 // 3
    kw = w.shape[0]
    tm = _pick(tp, CONV_TILE)
    hb = 16

    def body(p_ref, ph_ref, w_ref, y_ref, ext):
        i = pl.program_id(0)
        halo = ph_ref[:, d:2 * d].astype(F32) * ph_ref[:, 2 * d:].astype(F32)
        ext[0:hb, :] = jnp.where(i > 0, halo, 0.0)
        ext[hb:, :] = p_ref[:, d:2 * d].astype(F32) * p_ref[:, 2 * d:].astype(F32)
        z = jnp.zeros((tm, d), F32)
        for k in range(kw):
            z = z + w_ref[k:k + 1, :] * ext[pl.ds(hb - (kw - 1) + k, tm), :]
        y_ref[...] = (p_ref[:, 0:d].astype(F32) * z).astype(BF16)

    return pl.pallas_call(
        body, name="a_elem_fwd", grid=(tp // tm,), out_shape=jax.ShapeDtypeStruct((tp, d), BF16),
        in_specs=[pl.BlockSpec((tm, d3), lambda i: (i, 0)), pl.BlockSpec((hb, d3), _halo_before(tm, hb)),
                  pl.BlockSpec((kw, d), lambda i: (0, 0))],
        out_specs=pl.BlockSpec((tm, d), lambda i: (i, 0)),
        scratch_shapes=[pltpu.VMEM((tm + hb, d), F32)],
        compiler_params=_cparams(1),
    )(p, p, w)


def a_elem_bwd(p, dy, w):
    tp, d3 = p.shape
    d = d3 // 3
    kw = w.shape[0]
    tm = _pick(tp, CONV_TILE)
    hb = 16
    n = tp // tm

    def body(p_ref, ph_ref, pa_ref, dy_ref, dya_ref, w_ref, dp_ref, dw_ref, ext, dzext):
        i = pl.program_id(0)

        @pl.when(i == 0)
        def _():
            dw_ref[...] = jnp.zeros_like(dw_ref)

        b = p_ref[:, 0:d].astype(F32)
        c = p_ref[:, d:2 * d].astype(F32)
        v = p_ref[:, 2 * d:].astype(F32)
        dy_ = dy_ref[...].astype(F32)
        halo = ph_ref[:, d:2 * d].astype(F32) * ph_ref[:, 2 * d:].astype(F32)
        ext[0:hb, :] = jnp.where(i > 0, halo, 0.0)
        ext[hb:, :] = c * v
        dz = dy_ * b
        dzext[0:tm, :] = dz
        dzext[tm:, :] = jnp.where(i < n - 1, dya_ref[...].astype(F32) * pa_ref[:, 0:d].astype(F32), 0.0)
        z = jnp.zeros((tm, d), F32)
        dcv = jnp.zeros((tm, d), F32)
        for k in range(kw):
            sh = ext[pl.ds(hb - (kw - 1) + k, tm), :]
            z = z + w_ref[k:k + 1, :] * sh
            dw_ref[k:k + 1, :] += jnp.sum(dz * sh, axis=0, keepdims=True)
            dcv = dcv + w_ref[k:k + 1, :] * dzext[pl.ds(kw - 1 - k, tm), :]
        dp_ref[:, 0:d] = (dy_ * z).astype(BF16)
        dp_ref[:, d:2 * d] = (dcv * v).astype(BF16)
        dp_ref[:, 2 * d:] = (dcv * c).astype(BF16)

    nb = tp // hb
    return pl.pallas_call(
        body, name="a_elem_bwd", grid=(n,),
        out_shape=[jax.ShapeDtypeStruct((tp, d3), BF16), jax.ShapeDtypeStruct((8, d), F32)],
        in_specs=[pl.BlockSpec((tm, d3), lambda i: (i, 0)), pl.BlockSpec((hb, d3), _halo_before(tm, hb)),
                  pl.BlockSpec((hb, d3), _halo_after(tm, hb, nb)),
                  pl.BlockSpec((tm, d), lambda i: (i, 0)), pl.BlockSpec((hb, d), _halo_after(tm, hb, nb)),
                  pl.BlockSpec((kw, d), lambda i: (0, 0))],
        out_specs=[pl.BlockSpec((tm, d3), lambda i: (i, 0)), pl.BlockSpec((8, d), lambda i: (0, 0))],
        scratch_shapes=[pltpu.VMEM((tm + hb, d), F32), pltpu.VMEM((tm + hb, d), F32)],
        compiler_params=_cparams(1),
    )(p, p, p, dy, dy, w)


CH_R, CH_C = 128, 128


def _chunks(tm, d):
    return [(r0, c0) for c0 in range(0, d, CH_C) for r0 in range(0, tm, CH_R)]


def _b_u2(p_ref, ph_ref, w_ref, bias_ref, ext, u2_sc, i, tm, d, kw, hb):
    a = p_ref[:, 0:d].astype(F32)
    sg = _sigmoid(p_ref[:, d:].astype(F32))
    halo = ph_ref[:, 0:d].astype(F32) * _sigmoid(ph_ref[:, d:].astype(F32))
    ext[0:hb, :] = jnp.where(i > 0, halo, 0.0)
    ext[hb:, :] = a * sg
    for r0, c0 in _chunks(tm, d):
        cs = slice(c0, c0 + CH_C)
        acc = jnp.zeros((CH_R, CH_C), F32) + bias_ref[:, cs]
        for k in range(kw):
            acc = acc + w_ref[k:k + 1, cs] * ext[pl.ds(r0 + hb - (kw - 1) + k, CH_R), cs]
        u2_sc[r0:r0 + CH_R, cs] = acc
    return a, sg, u2_sc[...]


def _ln(u2):
    mu = jnp.mean(u2, axis=-1, keepdims=True)
    xc = u2 - mu
    rstd = lax.rsqrt(jnp.mean(xc * xc, axis=-1, keepdims=True) + EPS)
    return xc * rstd, rstd


def b_elem_fwd(p, w, bias, ln_g, ln_b):
    tp, d2 = p.shape
    d = d2 // 2
    kw = w.shape[0]
    tm = _pick(tp, CONV_TILE)
    hb = 32

    def body(p_ref, ph_ref, w_ref, bias_ref, g_ref, b_ref, y_ref, ext, u2_sc):
        i = pl.program_id(0)
        _, _, u2 = _b_u2(p_ref, ph_ref, w_ref, bias_ref, ext, u2_sc, i, tm, d, kw, hb)
        xhat, _ = _ln(u2)
        u3 = xhat * g_ref[...] + b_ref[...]
        y_ref[...] = (u3 * _sigmoid(u3)).astype(BF16)

    vec = pl.BlockSpec((1, d), lambda i: (0, 0))
    return pl.pallas_call(
        body, name="b_elem_fwd", grid=(tp // tm,), out_shape=jax.ShapeDtypeStruct((tp, d), BF16),
        in_specs=[pl.BlockSpec((tm, d2), lambda i: (i, 0)), pl.BlockSpec((hb, d2), _halo_before(tm, hb)),
                  pl.BlockSpec((kw, d), lambda i: (0, 0)), vec, vec, vec],
        out_specs=pl.BlockSpec((tm, d), lambda i: (i, 0)),
        scratch_shapes=[pltpu.VMEM((tm + hb, d), F32), pltpu.VMEM((tm, d), F32)],
        compiler_params=_cparams(1),
    )(p, p, w, bias, ln_g, ln_b)


def b_elem_bwd1(p, dy, w, bias, ln_g, ln_b):
    tp, d2 = p.shape
    d = d2 // 2
    kw = w.shape[0]
    tm = _pick(tp, CONV_TILE)
    hb = 32

    def body(p_ref, ph_ref, dy_ref, w_ref, bias_ref, g_ref, b_ref, du2_ref, st_ref, ext, u2_sc):
        i = pl.program_id(0)

        @pl.when(i == 0)
        def _():
            st_ref[...] = jnp.zeros_like(st_ref)

        _, _, u2 = _b_u2(p_ref, ph_ref, w_ref, bias_ref, ext, u2_sc, i, tm, d, kw, hb)
        xhat, rstd = _ln(u2)
        u3 = xhat * g_ref[...] + b_ref[...]
        s3 = _sigmoid(u3)
        du3 = dy_ref[...].astype(F32) * (s3 * (1.0 + u3 * (1.0 - s3)))
        dxh = du3 * g_ref[...]
        du2 = rstd * (dxh - jnp.mean(dxh, axis=-1, keepdims=True)
                      - xhat * jnp.mean(dxh * xhat, axis=-1, keepdims=True))
        du2_ref[...] = du2
        st_ref[0:1, :] += jnp.sum(du3 * xhat, axis=0, keepdims=True)
        st_ref[1:2, :] += jnp.sum(du3, axis=0, keepdims=True)
        st_ref[2:3, :] += jnp.sum(du2, axis=0, keepdims=True)

    vec = pl.BlockSpec((1, d), lambda i: (0, 0))
    return pl.pallas_call(
        body, name="b_elem_bwd1", grid=(tp // tm,),
        out_shape=[jax.ShapeDtypeStruct((tp, d), F32), jax.ShapeDtypeStruct((8, d), F32)],
        in_specs=[pl.BlockSpec((tm, d2), lambda i: (i, 0)), pl.BlockSpec((hb, d2), _halo_before(tm, hb)),
                  pl.BlockSpec((tm, d), lambda i: (i, 0)), pl.BlockSpec((kw, d), lambda i: (0, 0)), vec, vec, vec],
        out_specs=[pl.BlockSpec((tm, d), lambda i: (i, 0)), pl.BlockSpec((8, d), lambda i: (0, 0))],
        scratch_shapes=[pltpu.VMEM((tm + hb, d), F32), pltpu.VMEM((tm, d), F32)],
        compiler_params=_cparams(1),
    )(p, p, dy, w, bias, ln_g, ln_b)


def b_elem_bwd2(p, du2, w):
    tp, d2 = p.shape
    d = d2 // 2
    kw = w.shape[0]
    tm = _pick(tp, CONV_TILE)
    hb = 32
    n = tp // tm

    def body(p_ref, du2_ref, du2a_ref, w_ref, dp_ref, dw_ref, dext, dwacc):
        i = pl.program_id(0)

        @pl.when(i == 0)
        def _():
            dwacc[...] = jnp.zeros_like(dwacc)

        dext[0:tm, :] = du2_ref[...]
        dext[tm:, :] = jnp.where(i < n - 1, du2a_ref[...], 0.0)
        for r0, c0 in _chunks(tm, d):
            cs = slice(c0, c0 + CH_C)
            a = p_ref[r0:r0 + CH_R, cs].astype(F32)
            sg = _sigmoid(p_ref[r0:r0 + CH_R, d + c0:d + c0 + CH_C].astype(F32))
            u1 = a * sg
            du1 = jnp.zeros((CH_R, CH_C), F32)
            for k in range(kw):
                sh = dext[pl.ds(r0 + kw - 1 - k, CH_R), cs]
                du1 = du1 + w_ref[k:k + 1, cs] * sh
                prod = sh * u1
                part = prod[0:8]
                for r in range(8, CH_R, 8):
                    part = part + prod[r:r + 8]
                dwacc[8 * k:8 * k + 8, cs] += part
            dp_ref[r0:r0 + CH_R, cs] = (du1 * sg).astype(BF16)
            dp_ref[r0:r0 + CH_R, d + c0:d + c0 + CH_C] = (du1 * a * sg * (1.0 - sg)).astype(BF16)

        @pl.when(i == n - 1)
        def _():
            dw_ref[...] = jnp.zeros_like(dw_ref)
            for k in range(kw):
                dw_ref[k:k + 1, :] = jnp.sum(dwacc[8 * k:8 * k + 8, :], axis=0, keepdims=True)

    nb = tp // hb
    return pl.pallas_call(
        body, name="b_elem_bwd2", grid=(n,),
        out_shape=[jax.ShapeDtypeStruct((tp, d2), BF16), jax.ShapeDtypeStruct((32, d), F32)],
        in_specs=[pl.BlockSpec((tm, d2), lambda i: (i, 0)),
                  pl.BlockSpec((tm, d), lambda i: (i, 0)), pl.BlockSpec((hb, d), _halo_after(tm, hb, nb)),
                  pl.BlockSpec((kw, d), lambda i: (0, 0))],
        out_specs=[pl.BlockSpec((tm, d2), lambda i: (i, 0)), pl.BlockSpec((32, d), lambda i: (0, 0))],
        scratch_shapes=[pltpu.VMEM((tm + hb, d), F32), pltpu.VMEM((8 * kw, d), F32)],
        compiler_params=_cparams(1),
    )(p, du2, du2, w)


ATT_TILE = 256


def _head_masks(d):
    c = lax.broadcasted_iota(jnp.int32, (d, LANES), 0)
    h = lax.broadcasted_iota(jnp.int32, (d, LANES), 1)
    seg = (c // HEAD_DIM == h).astype(BF16)
    fold = (c % HEAD_DIM == h).astype(BF16)
    return seg, seg.T, fold


def _dot2(x, m):
    hi = x.astype(BF16)
    lo = (x - hi.astype(F32)).astype(BF16)
    return jnp.dot(hi, m, preferred_element_type=F32) + jnp.dot(lo, m, preferred_element_type=F32)


L2E = 1.4426950408889634
LN2 = 0.6931471805599453
AUG = HEAD_DIM


def _split3(r):
    r1 = r.astype(BF16).astype(F32)
    r2 = (r - r1).astype(BF16).astype(F32)
    r3 = (r - r1 - r2).astype(BF16).astype(F32)
    return r1, r2, r3


def c_prep_slots(p, b_f, qg, kg):
    tp = p.shape[0]
    d = qg.shape[1]
    n_heads = d // HEAD_DIM
    tm = ATT_TILE
    seg, seg_t, _ = _head_masks(d)

    def body(p_ref, pf_ref, bf_ref, qg_ref, kg_ref, seg_ref, segt_ref, q_ref, k_ref, v_ref, cum_ref, carry):
        @pl.when(pl.program_id(0) == 0)
        def _():
            carry[...] = jnp.zeros_like(carry)

        def norm(x, g):
            ms = _dot2(x * x, seg_ref[...]) * (1.0 / HEAD_DIM)
            r = _dot2(lax.rsqrt(ms + EPS), segt_ref[...])
            return x * r * g

        qn = norm(p_ref[:, 0:d], qg_ref[...]) * (HEAD_DIM ** -0.5 * L2E)
        kn = norm(p_ref[:, d:2 * d], kg_ref[...])
        v_ref[...] = p_ref[:, 2 * d:].astype(BF16)
        xf = pf_ref[...] + bf_ref[...]
        logf = jnp.minimum(xf, 0.0) - jnp.log(1.0 + jnp.exp(-jnp.abs(xf)))
        r_ = lax.broadcasted_iota(jnp.int32, (tm, tm), 0)
        c_ = lax.broadcasted_iota(jnp.int32, (tm, tm), 1)
        cum = jnp.dot((c_ <= r_).astype(F32), logf, precision=HI, preferred_element_type=F32) + carry[0:1, :]
        cum_ref[...] = cum
        carry[0:1, :] += jnp.sum(logf, axis=0, keepdims=True)

        lane = lax.broadcasted_iota(jnp.int32, (1, LANES), 1)
        ones_q = jnp.where((lane >= AUG + 3) & (lane < AUG + 6), 1.0, 0.0)
        ones_k = jnp.where((lane >= AUG) & (lane < AUG + 3), 1.0, 0.0)
        for h in range(n_heads):
            c0 = LANES * (h // 2)
            src_q, src_k = qn[:, c0:c0 + LANES], kn[:, c0:c0 + LANES]
            if h % 2:
                src_q = pltpu.roll(src_q, HEAD_DIM, axis=1)
                src_k = pltpu.roll(src_k, HEAD_DIM, axis=1)
            c = cum[:, h:h + 1] * L2E
            a1, a2, a3 = _split3(c)
            b1, b2, b3 = _split3(-c)
            aug_q = jnp.where(lane == AUG, a1, jnp.where(lane == AUG + 1, a2, jnp.where(lane == AUG + 2, a3, ones_q)))
            aug_k = jnp.where(lane == AUG + 3, b1,
                              jnp.where(lane == AUG + 4, b2, jnp.where(lane == AUG + 5, b3, ones_k)))
            q_ref[:, LANES * h:LANES * (h + 1)] = jnp.where(lane < HEAD_DIM, src_q, aug_q).astype(BF16)
            k_ref[:, LANES * h:LANES * (h + 1)] = jnp.where(lane < HEAD_DIM, src_k, aug_k).astype(BF16)

    vec = pl.BlockSpec((1, d), lambda i: (0, 0))
    rowd = pl.BlockSpec((tm, d), lambda i: (i, 0))
    slots = pl.BlockSpec((tm, n_heads * LANES), lambda i: (i, 0))
    return pl.pallas_call(
        body, name="c_prep_slots", grid=(tp // tm,),
        out_shape=[jax.ShapeDtypeStruct((tp, n_heads * LANES), BF16)] * 2
        + [jax.ShapeDtypeStruct((tp, d), BF16), jax.ShapeDtypeStruct((tp, LANES), F32)],
        in_specs=[pl.BlockSpec((tm, 3 * d), lambda i: (i, 0)), pl.BlockSpec((tm, LANES), lambda i: (i, 3 * d // LANES)),
                  pl.BlockSpec((1, LANES), lambda i: (0, 0)), vec, vec,
                  pl.BlockSpec((d, LANES), lambda i: (0, 0)), pl.BlockSpec((LANES, d), lambda i: (0, 0))],
        out_specs=[slots, slots, rowd, pl.BlockSpec((tm, LANES), lambda i: (i, 0))],
        scratch_shapes=[pltpu.VMEM((8, LANES), F32)],
        compiler_params=_cparams(1),
    )(p, p, b_f, qg, kg, seg, seg_t)


ATT_GROUPS = (8, 2)


def _grouped_loop(n_blocks, body, carry):
    start = 0
    for g in ATT_GROUPS:
        count = (n_blocks - start) // g
        carry = lax.fori_loop(0, count, lambda i, c, g=g, start=start: body(start + i * g, g, c), carry)
        start = start + count * g
    return lax.fori_loop(start, n_blocks, lambda kj, c: body(kj, 1, c), carry)


def _diag_mask(t):
    return lax.broadcasted_iota(jnp.int32, (t, t), 0) <= lax.broadcasted_iota(jnp.int32, (t, t), 1)


def attn_fwd_t(qpt, kp, vt4):
    tp = kp.shape[0]
    t = ATT_TILE
    nb = tp // t
    npair = vt4.shape[0]

    def body(q_ref, k_ref, v_ref, o_ref, lse_ref):
        qi = pl.program_id(1)
        qts = (q_ref[0:LANES, :], q_ref[LANES:2 * LANES, :])
        mask = _diag_mask(t)

        def blocks(kj0, n, carry, masked=False):
            out = []
            for hd in range(2):
                m, l, acc = carry[3 * hd:3 * hd + 3]
                sts = []
                for g in range(n):
                    off = pl.multiple_of((kj0 + g) * t, t)
                    st = jnp.dot(k_ref[pl.ds(off, t), LANES * hd:LANES * (hd + 1)], qts[hd],
                                 preferred_element_type=F32)
                    sts.append(jnp.where(mask, st, NEG) if masked else st)
                m_new = m
                for st in sts:
                    m_new = jnp.maximum(m_new, jnp.max(st, axis=0, keepdims=True))
                alpha = jnp.exp2(m - m_new)
                l = alpha * l
                acc = alpha * acc
                for g, st in enumerate(sts):
                    pt = jnp.exp2(st - m_new)
                    l = l + jnp.sum(pt, axis=0, keepdims=True)
                    acc = acc + jnp.dot(v_ref[0, kj0 + g], pt.astype(BF16), preferred_element_type=F32)
                out += [m_new, l, acc]
            return tuple(out)

        init = (jnp.full((1, t), NEG, F32), jnp.zeros((1, t), F32), jnp.zeros((LANES, t), F32)) * 2
        carry = _grouped_loop(qi, blocks, init)
        ma, la, acca, mb, lb, accb = blocks(qi, 1, carry, masked=True)
        row = lax.broadcasted_iota(jnp.int32, (LANES, 1), 0)
        o_ref[...] = jnp.where(row < HEAD_DIM, acca / la, accb / lb).astype(BF16)
        lse_ref[0, 0:1, :] = ma + jnp.log(la) * L2E
        lse_ref[0, 1:2, :] = mb + jnp.log(lb) * L2E

    return pl.pallas_call(
        body, name="attn_fwd_t", grid=(npair, nb),
        out_shape=[jax.ShapeDtypeStruct((npair * LANES, tp), BF16), jax.ShapeDtypeStruct((npair, 2, tp), F32)],
        in_specs=[pl.BlockSpec((2 * LANES, t), lambda h, i: (h, i)), pl.BlockSpec((tp, 2 * LANES), lambda h, i: (0, h)),
                  pl.BlockSpec((1, nb, LANES, t), lambda h, i: (h, 0, 0, 0))],
        out_specs=[pl.BlockSpec((LANES, t), lambda h, i: (h, i)), pl.BlockSpec((1, 2, t), lambda h, i: (h, 0, i))],
        compiler_params=_cparams(2),
    )(qpt, kp, vt4)


def attn_bwd_t(qpt, qp, kp, kpt4, v, do, dot_, lse, cqt):
    tp = kp.shape[0]
    t = ATT_TILE
    nb = tp // t
    n_heads = kpt4.shape[0]
    d = v.shape[1]

    def body(qt_ref, q_ref, k_ref, kt_ref, v_ref, do_ref, dot_ref, lse_ref, cq_ref, dq_ref, dk_ref, dv_ref,
             p_sc, dp_sc, dc_sc):
        hd = pl.program_id(0) % 2
        qi = pl.program_id(1)

        @pl.when(qi == 0)
        def _():
            dk_ref[...] = jnp.zeros_like(dk_ref)
            dc_sc[...] = jnp.zeros_like(dc_sc)

        @pl.when((qi == 0) & (hd == 0))
        def _():
            dv_ref[...] = jnp.zeros_like(dv_ref)

        row = lax.broadcasted_iota(jnp.int32, (LANES, 1), 0)
        lane = lax.broadcasted_iota(jnp.int32, (1, LANES), 1)
        dot_h = jnp.where(row // HEAD_DIM == hd, dot_ref[...], jnp.zeros_like(dot_ref))
        do_h = jnp.where(lane // HEAD_DIM == hd, do_ref[...], jnp.zeros_like(do_ref))
        rr = cq_ref[0] - lse_ref[0]
        r1, r2, r3 = _split3(jnp.where(hd == 0, rr[0:1, :], rr[1:2, :]))
        qt = qt_ref[...].astype(F32)
        qt = jnp.where(row == AUG, r1, jnp.where(row == AUG + 1, r2, jnp.where(row == AUG + 2, r3, qt))).astype(BF16)
        mask = _diag_mask(t)

        def pass1(kj0, n, delta, masked=False):
            for g in range(n):
                off = pl.multiple_of((kj0 + g) * t, t)
                st = jnp.dot(k_ref[pl.ds(off, t), :], qt, preferred_element_type=F32)
                if masked:
                    st = jnp.where(mask, st, NEG)
                pt = jnp.exp2(st)
                dpt = jnp.dot(v_ref[pl.ds(off, t), :], dot_h, preferred_element_type=F32)
                p_sc[kj0 + g] = pt
                dp_sc[kj0 + g] = dpt
                delta = delta + jnp.sum(pt * dpt, axis=0, keepdims=True)
            return delta

        delta = _grouped_loop(qi, pass1, jnp.zeros((1, t), F32))
        delta = pass1(qi, 1, delta, masked=True)

        def pass2(kj0, n, dq):
            for g in range(n):
                off = pl.multiple_of((kj0 + g) * t, t)
                pt = p_sc[kj0 + g]
                ds32 = pt * (dp_sc[kj0 + g] - delta)
                ds = ds32.astype(BF16)
                dc_sc[pl.ds(off, t), :] += ds32[:, 0:LANES] + ds32[:, LANES:2 * LANES]
                dk_ref[pl.ds(off, t), :] += jnp.dot(ds, q_ref[...], preferred_element_type=F32)
                dv_ref[pl.ds(off, t), :] += jnp.dot(pt.astype(BF16), do_h, preferred_element_type=F32)
                dq = dq + jnp.dot(kt_ref[0, kj0 + g], ds, preferred_element_type=F32)
            return dq

        dq_ref[...] = _grouped_loop(qi + 1, pass2, jnp.zeros((LANES, t), F32))

        @pl.when(qi == nb - 1)
        def _():
            dk_ref[:, AUG + 3:AUG + 4] = jnp.sum(dc_sc[...], axis=1, keepdims=True)

    once = dict(pipeline_mode=pl.Buffered(1))
    pair_rows = pl.BlockSpec((1, 2, t), lambda h, i: (h // 2, 0, i))
    return pl.pallas_call(
        body, name="attn_bwd_t", grid=(n_heads, nb),
        out_shape=[jax.ShapeDtypeStruct((n_heads * LANES, tp), F32), jax.ShapeDtypeStruct((tp, n_heads * LANES), F32),
                   jax.ShapeDtypeStruct((tp, d), F32)],
        in_specs=[pl.BlockSpec((LANES, t), lambda h, i: (h, i)), pl.BlockSpec((t, LANES), lambda h, i: (i, h)),
                  pl.BlockSpec((tp, LANES), lambda h, i: (0, h), **once),
                  pl.BlockSpec((1, nb, LANES, t), lambda h, i: (h, 0, 0, 0), **once),
                  pl.BlockSpec((tp, LANES), lambda h, i: (0, h // 2), **once),
                  pl.BlockSpec((t, LANES), lambda h, i: (i, h // 2)), pl.BlockSpec((LANES, t), lambda h, i: (h // 2, i)),
                  pair_rows, pair_rows],
        out_specs=[pl.BlockSpec((LANES, t), lambda h, i: (h, i)), pl.BlockSpec((tp, LANES), lambda h, i: (0, h)),
                   pl.BlockSpec((tp, LANES), lambda h, i: (0, h // 2))],
        scratch_shapes=[pltpu.VMEM((nb, t, t), F32), pltpu.VMEM((nb, t, t), F32), pltpu.VMEM((tp, LANES), F32)],
        compiler_params=_cparams(2),
    )(qpt, qp, kp, kpt4, v, do, dot_, lse, cqt)


def rev_cumsum_rows(x):
    r, tp = x.shape
    t = ATT_TILE
    nb = tp // t

    def body(x_ref, o_ref, carry):
        @pl.when(pl.program_id(0) == 0)
        def _():
            carry[...] = jnp.zeros_like(carry)

        xv = x_ref[...]
        r_ = lax.broadcasted_iota(jnp.int32, (t, t), 0)
        c_ = lax.broadcasted_iota(jnp.int32, (t, t), 1)
        o_ref[...] = jnp.dot(xv, (r_ >= c_).astype(F32), precision=HI, preferred_element_type=F32) + carry[:, 0:1]
        carry[...] += jnp.sum(xv, axis=1, keepdims=True)

    return pl.pallas_call(
        body, name="rev_cumsum_rows", grid=(nb,), out_shape=jax.ShapeDtypeStruct((r, tp), F32),
        in_specs=[pl.BlockSpec((r, t), lambda i: (0, nb - 1 - i))],
        out_specs=pl.BlockSpec((r, t), lambda i: (0, nb - 1 - i)),
        scratch_shapes=[pltpu.VMEM((r, LANES), F32)],
        compiler_params=_cparams(1),
    )(x)


def c_elem_bwd(p, b_f, qg, kg, dq, dk, dv, dlogf):
    tp, n_out = p.shape
    d = qg.shape[1]
    tm = ATT_TILE
    n = tp // tm
    seg, seg_t, fold = _head_masks(d)

    def body(p_ref, pf_ref, bf_ref, qg_ref, kg_ref, dq_ref, dk_ref, dv_ref, dlf_ref, seg_ref, segt_ref, fold_ref,
             dp_ref, sm_ref, accq, acck, accf):
        i = pl.program_id(0)

        @pl.when(i == 0)
        def _():
            accq[...] = jnp.zeros_like(accq)
            acck[...] = jnp.zeros_like(acck)
            accf[...] = jnp.zeros_like(accf)

        def norm_bwd(x, g, dy, acc):
            ms = _dot2(x * x, seg_ref[...]) * (1.0 / HEAD_DIM)
            r = _dot2(lax.rsqrt(ms + EPS), segt_ref[...])
            xhat = x * r
            acc[0:1, :] += jnp.sum(dy * xhat, axis=0, keepdims=True)
            gy = dy * g
            mean = _dot2(_dot2(gy * xhat, seg_ref[...]), segt_ref[...]) * (1.0 / HEAD_DIM)
            return r * (gy - xhat * mean)

        dp_ref[:, 0:d] = norm_bwd(p_ref[:, 0:d], qg_ref[...], dq_ref[...] * (HEAD_DIM ** -0.5), accq).astype(BF16)
        dp_ref[:, d:2 * d] = norm_bwd(p_ref[:, d:2 * d], kg_ref[...], dk_ref[...] * LN2, acck).astype(BF16)
        dp_ref[:, 2 * d:3 * d] = dv_ref[...].astype(BF16)
        df = dlf_ref[...] * _sigmoid(-(pf_ref[...] + bf_ref[...]))
        accf[0:1, :] += jnp.sum(df, axis=0, keepdims=True)
        dp_ref[:, 3 * d:] = df.astype(BF16)

        @pl.when(i == n - 1)
        def _():
            sm_ref[...] = jnp.zeros_like(sm_ref)
            sm_ref[0:1, :] = jnp.dot(accq[0:1, :], fold_ref[...].astype(F32), precision=HI, preferred_element_type=F32)
            sm_ref[1:2, :] = jnp.dot(acck[0:1, :], fold_ref[...].astype(F32), precision=HI, preferred_element_type=F32)
            sm_ref[2:3, :] = accf[0:1, :]

    vec = pl.BlockSpec((1, d), lambda i: (0, 0))
    rowd = pl.BlockSpec((tm, d), lambda i: (i, 0))
    rowl = pl.BlockSpec((tm, LANES), lambda i: (i, 0))
    return pl.pallas_call(
        body, name="c_elem_bwd", grid=(n,),
        out_shape=[jax.ShapeDtypeStruct((tp, n_out), BF16), jax.ShapeDtypeStruct((8, LANES), F32)],
        in_specs=[pl.BlockSpec((tm, 2 * d), lambda i: (i, 0)), pl.BlockSpec((tm, LANES), lambda i: (i, 3 * d // LANES)),
                  pl.BlockSpec((1, LANES), lambda i: (0, 0)), vec, vec, rowd, rowd, rowd, rowl,
                  pl.BlockSpec((d, LANES), lambda i: (0, 0)), pl.BlockSpec((LANES, d), lambda i: (0, 0)),
                  pl.BlockSpec((d, LANES), lambda i: (0, 0))],
        out_specs=[pl.BlockSpec((tm, n_out), lambda i: (i, 0)), pl.BlockSpec((8, LANES), lambda i: (0, 0))],
        scratch_shapes=[pltpu.VMEM((8, d), F32), pltpu.VMEM((8, d), F32), pltpu.VMEM((8, LANES), F32)],
        compiler_params=_cparams(1),
    )(p, p, b_f, qg, kg, dq, dk, dv, dlogf, seg, seg_t, fold)


def loss_head(h, tgt, t_real):
    tp, d = h.shape
    tm = _pick(tp, (768, 512, 256))

    def body(h_ref, t_ref, dh_ref, l_ref):
        i = pl.program_id(0)

        @pl.when(i == 0)
        def _():
            l_ref[...] = jnp.zeros_like(l_ref)

        row = i * tm + lax.broadcasted_iota(jnp.int32, (tm, 1), 0)
        valid = (row >= N_META) & (row < t_real)
        e = jnp.where(valid, h_ref[...] - t_ref[...], 0.0)
        dh_ref[...] = e * (1.0 / d)
        per_row = jnp.sum(e * e, axis=-1, keepdims=True) * (1.0 / d)
        l_ref[...] += 0.5 * jnp.sum(per_row, axis=0, keepdims=True)

    return pl.pallas_call(
        body, name="loss_head", grid=(tp // tm,),
        out_shape=[jax.ShapeDtypeStruct((tp, d), F32), jax.ShapeDtypeStruct((8, LANES), F32)],
        in_specs=[pl.BlockSpec((tm, d), lambda i: (i, 0)), pl.BlockSpec((tm, d), lambda i: (i, 0))],
        out_specs=[pl.BlockSpec((tm, d), lambda i: (i, 0)), pl.BlockSpec((8, LANES), lambda i: (0, 0))],
        compiler_params=_cparams(1),
    )(h, tgt)


def sum_devices(x):
    _, r, c = x.shape

    def body(x_ref, o_ref):
        acc = x_ref[0]
        for dev in range(1, N_DEV):
            acc = acc + x_ref[dev]
        o_ref[...] = acc

    return pl.pallas_call(
        body, name="sum_devices", out_shape=jax.ShapeDtypeStruct((r, c), F32),
        in_specs=[pl.BlockSpec(memory_space=pltpu.VMEM)], out_specs=pl.BlockSpec(memory_space=pltpu.VMEM),
    )(x)


def _adamw_math(w, g, m, v):
    m = ADAM_B1 * m + (1.0 - ADAM_B1) * g
    v = ADAM_B2 * v + (1.0 - ADAM_B2) * (g * g)
    m_hat = m / (1.0 - ADAM_B1 ** ADAM_STEP)
    v_hat = v / (1.0 - ADAM_B2 ** ADAM_STEP)
    delta = -ADAM_LR * (m_hat / (jnp.sqrt(v_hat) + ADAM_EPS) + ADAM_WD * w)
    return delta, m, v


def adamw_small(w, g, m, v):
    def body(w_ref, g_ref, m_ref, v_ref, d_ref, nm_ref, nv_ref):
        d_ref[...], nm_ref[...], nv_ref[...] = _adamw_math(w_ref[...], g_ref[...], m_ref[...], v_ref[...])

    vm = pl.BlockSpec(memory_space=pltpu.VMEM)
    return pl.pallas_call(
        body, name="adamw_small", out_shape=[jax.ShapeDtypeStruct(w.shape, F32)] * 3,
        in_specs=[vm] * 4, out_specs=[vm] * 3,
    )(w, g, m, v)


def adamw_reduce(w, m, v, parts):
    r, c = w.shape
    n_parts = parts.shape[0]
    tr = _pick(r, (512, 384, 352, 256, 128, 8))

    def body(w_ref, m_ref, v_ref, p_ref, g_ref, d_ref, nm_ref, nv_ref):
        g = p_ref[0].astype(F32)
        for j in range(1, n_parts):
            g = g + p_ref[j].astype(F32)
        g_ref[...] = g
        d_ref[...], nm_ref[...], nv_ref[...] = _adamw_math(w_ref[...], g, m_ref[...], v_ref[...])

    blk = pl.BlockSpec((tr, c), lambda i: (i, 0))
    return pl.pallas_call(
        body, name="adamw_reduce", grid=(r // tr,), out_shape=[jax.ShapeDtypeStruct((r, c), F32)] * 4,
        in_specs=[blk, blk, blk, pl.BlockSpec((n_parts, tr, c), lambda i: (0, i, 0))],
        out_specs=[blk] * 4, compiler_params=_cparams(1),
    )(w, m, v, parts)


def _unshard(g, axis):
    g = jnp.moveaxis(g, 0, axis)
    return g.reshape(g.shape[:axis] + (g.shape[axis] * g.shape[axis + 1],) + g.shape[axis + 2:])


def _shard(full, axis):
    s = full.shape
    g = full.reshape(s[:axis] + (N_DEV, s[axis] // N_DEV) + s[axis + 1:])
    return jnp.moveaxis(g, axis, 0)


def _pad_lanes(a, n=LANES):
    flat = a.reshape(-1)
    pad = (-flat.shape[0]) % n
    return jnp.pad(flat, (0, pad)).reshape(-1, n)


BIG = ("ffn_w_gate", "ffn_w_up", "ffn_w_down", "a_w_in", "a_w_out", "b_w_in", "b_w_out", "c_w_in", "c_w_out")
SHARD_AXIS = {"ffn_w_gate": 3, "ffn_w_up": 3, "ffn_w_down": 2, "a_w_in": 2, "a_w_out": 1, "b_w_in": 2,
              "b_w_out": 1, "c_w_in": 2, "c_w_out": 1, "meta": 1, "ffn_norm": 2, "a_conv": 2, "b_conv": 2}
SMALL_SHARDED = ("meta", "ffn_norm", "a_conv", "b_conv")
SMALL_REPL = ("mix_norm", "b_conv_bias", "b_ln_g", "b_ln_b", "c_b_f", "c_q_norm", "c_k_norm")
WEIGHTS = ("meta", "ffn_norm", "ffn_w_gate", "ffn_w_up", "ffn_w_down", "mix_norm", "a_w_in", "a_conv", "a_w_out",
           "b_w_in", "b_conv", "b_conv_bias", "b_ln_g", "b_ln_b", "b_w_out", "c_w_in", "c_b_f", "c_q_norm",
           "c_k_norm", "c_w_out")
N_MIXERS = 3


def kernel(x, meta, ffn_norm, ffn_w_gate, ffn_w_up, ffn_w_down, mix_norm, a_w_in, a_conv, a_w_out, b_w_in, b_conv, b_conv_bias, b_ln_g, b_ln_b, b_w_out, c_w_in, c_b_f, c_q_norm, c_k_norm, c_w_out, loss_target, m_meta, m_ffn_norm, m_ffn_w_gate, m_ffn_w_up, m_ffn_w_down, m_mix_norm, m_a_w_in, m_a_conv, m_a_w_out, m_b_w_in, m_b_conv, m_b_conv_bias, m_b_ln_g, m_b_ln_b, m_b_w_out, m_c_w_in, m_c_b_f, m_c_q_norm, m_c_k_norm, m_c_w_out, v_meta, v_ffn_norm, v_ffn_w_gate, v_ffn_w_up, v_ffn_w_down, v_mix_norm, v_a_w_in, v_a_conv, v_a_w_out, v_b_w_in, v_b_conv, v_b_conv_bias, v_b_ln_g, v_b_ln_b, v_b_w_out, v_c_w_in, v_c_b_f, v_c_q_norm, v_c_k_norm, v_c_w_out):
    local = dict(locals())
    w = {n: local[n] for n in WEIGHTS}
    mom = {n: local["m_" + n] for n in WEIGHTS}
    var = {n: local["v_" + n] for n in WEIGHTS}
    d = x.shape[-1]
    depth = ffn_norm.shape[0]
    n_heads = d // HEAD_DIM
    seq = x.shape[1]
    t_real = N_META + seq
    tp = -(-t_real // ROW_ALIGN) * ROW_ALIGN
    nb = tp // ATT_TILE
    npair = d // LANES

    names = BIG + SMALL_SHARDED
    gathered = all_gather([w[n].astype(BF16) for n in BIG] + [w[n] for n in SMALL_SHARDED])
    full = {n: _unshard(g, SHARD_AXIS[n]) for n, g in zip(names, gathered)}
    n_c = full["c_w_in"].shape[-1]
    n_cp = 3 * d + LANES
    c_w_qkv = jnp.pad(full["c_w_in"], ((0, 0), (0, 0), (0, n_cp - n_c)))
    b_f_pad = jnp.pad(c_b_f, ((0, 0), (0, LANES - n_heads)))
    qg_t = jnp.tile(c_q_norm, (1, n_heads))
    kg_t = jnp.tile(c_k_norm, (1, n_heads))

    h = jnp.concatenate([full["meta"], x[0], jnp.zeros((tp - t_real, d), F32)], axis=0)
    saved = []
    for i in range(depth):
        mixer, j = i % N_MIXERS, i // N_MIXERS
        s = {"h0": h}
        h, s["gate_a"], s["up_a"] = ffn_fwd(h, full["ffn_norm"][i, 0:1], full["ffn_w_gate"][i, 0],
                                            full["ffn_w_up"][i, 0], full["ffn_w_down"][i, 0])
        s["h1"] = h
        g_mix = mix_norm[i:i + 1]
        if mixer == 0:
            s["p"], s["u"] = norm_matmul(h, g_mix, full["a_w_in"][j], BF16, True)
            s["y"] = a_elem_fwd(s["p"], full["a_conv"][j])
            h = res_matmul(h, s["y"], full["a_w_out"][j])
        elif mixer == 1:
            s["p"], s["u"] = norm_matmul(h, g_mix, full["b_w_in"][j], BF16, True)
            s["y"] = b_elem_fwd(s["p"], full["b_conv"][j], b_conv_bias[j:j + 1], b_ln_g[j:j + 1], b_ln_b[j:j + 1])
            h = res_matmul(h, s["y"], full["b_w_out"][j])
        else:
            s["p"], s["u"] = norm_matmul(h, g_mix, c_w_qkv[j], F32, True)
            s["qp"], s["kp"], s["v"], cum = c_prep_slots(s["p"], b_f_pad[j:j + 1], qg_t[j:j + 1], kg_t[j:j + 1])
            s["cqt"] = (cum[:, :n_heads].T * L2E).reshape(npair, 2, tp)
            s["qpt"] = s["qp"].T
            s["kpt4"] = s["kp"].T.reshape(n_heads, LANES, nb, ATT_TILE).transpose(0, 2, 1, 3)
            vt4 = s["v"].T.reshape(npair, LANES, nb, ATT_TILE).transpose(0, 2, 1, 3)
            ot, s["lse"] = attn_fwd_t(s["qpt"], s["kp"], vt4)
            s["y"] = ot.T
            h = res_matmul(h, s["y"], full["c_w_out"][j])
        s["h2"] = h
        h, s["gate_b"], s["up_b"] = ffn_fwd(h, full["ffn_norm"][i, 1:2], full["ffn_w_gate"][i, 1],
                                            full["ffn_w_up"][i, 1], full["ffn_w_down"][i, 1])
        saved.append(s)

    tgt = jnp.concatenate([jnp.zeros((N_META, d), F32), loss_target[0], jnp.zeros((tp - t_real, d), F32)], axis=0)
    dh, loss_part = loss_head(h, tgt, t_real)

    gfull = {n: [None] * full[n].shape[0] for n in ("a_w_in", "a_w_out", "b_w_in", "b_w_out", "c_w_in", "c_w_out")}
    gffn = {n: [[None, None] for _ in range(depth)] for n in ("ffn_w_gate", "ffn_w_up", "ffn_w_down")}
    g_ffn_norm = [[None, None] for _ in range(depth)]
    g_mix_norm = [None] * depth
    g_a_conv = [None] * a_conv.shape[0]
    g_b_conv = [None] * b_conv.shape[0]
    g_b_stats = [None] * b_conv.shape[0]
    g_c_small = [None] * c_b_f.shape[0]

    def ffn_backward(dh_out, i, half, h_in, gate, up):
        dh_in, hn, dgate, dup, act, dg = ffn_bwd(dh_out, h_in, full["ffn_norm"][i, half:half + 1], gate, up,
                                                 full["ffn_w_gate"][i, half], full["ffn_w_up"][i, half],
                                                 full["ffn_w_down"][i, half])
        gffn["ffn_w_gate"][i][half] = atb(hn, dgate)
        gffn["ffn_w_up"][i][half] = atb(hn, dup)
        gffn["ffn_w_down"][i][half] = atb(act, dh_out, 0.5)
        g_ffn_norm[i][half] = dg
        return dh_in

    for i in reversed(range(depth)):
        mixer, j = i % N_MIXERS, i // N_MIXERS
        s = saved[i]
        g_mix = mix_norm[i:i + 1]
        dh = ffn_backward(dh, i, 1, s["h2"], s["gate_b"], s["up_b"])
        if mixer == 0:
            dy = matmul_nt(dh, full["a_w_out"][j])
            dp, dwc = a_elem_bwd(s["p"], dy, full["a_conv"][j])
            g_a_conv[j] = dwc[:a_conv.shape[1]]
            gfull["a_w_out"][j] = atb(s["y"], dh)
            gfull["a_w_in"][j] = atb(s["u"], dp)
            dh, g_mix_norm[i] = matmul_nt_rms_bwd(dp, full["a_w_in"][j], s["h1"], g_mix, dh)
        elif mixer == 1:
            dy = matmul_nt(dh, full["b_w_out"][j])
            du2, g_b_stats[j] = b_elem_bwd1(s["p"], dy, full["b_conv"][j], b_conv_bias[j:j + 1],
                                            b_ln_g[j:j + 1], b_ln_b[j:j + 1])
            dp, dwc = b_elem_bwd2(s["p"], du2, full["b_conv"][j])
            g_b_conv[j] = dwc[:b_conv.shape[1]]
            gfull["b_w_out"][j] = atb(s["y"], dh)
            gfull["b_w_in"][j] = atb(s["u"], dp)
            dh, g_mix_norm[i] = matmul_nt_rms_bwd(dp, full["b_w_in"][j], s["h1"], g_mix, dh)
        else:
            do = matmul_nt(dh, full["c_w_out"][j])
            dqt, dkp, dv = attn_bwd_t(s["qpt"], s["qp"], s["kp"], s["kpt4"], s["v"], do, do.T, s["lse"], s["cqt"])
            dq = dqt.reshape(n_heads, LANES, tp)[:, :HEAD_DIM].reshape(d, tp).T
            dkp = dkp.reshape(tp, n_heads, LANES)
            dk = dkp[:, :, :HEAD_DIM].reshape(tp, d)
            dlogf = rev_cumsum_rows(-dkp[:, :, AUG + 3].T)
            dlogf = jnp.pad(dlogf.T, ((0, 0), (0, LANES - n_heads)))
            dp, g_c_small[j] = c_elem_bwd(s["p"], b_f_pad[j:j + 1], qg_t[j:j + 1], kg_t[j:j + 1], dq, dk, dv, dlogf)
            gfull["c_w_out"][j] = atb(s["y"], dh)
            gfull["c_w_in"][j] = atb(s["u"], dp)[:, :n_c]
            dh, g_mix_norm[i] = matmul_nt_rms_bwd(dp, c_w_qkv[j], s["h1"], g_mix, dh)
        dh = ffn_backward(dh, i, 0, s["h0"], s["gate_a"], s["up_a"])

    grad_x = dh[N_META:t_real][None]

    gbig = {n: jnp.stack([jnp.stack(r) for r in gffn[n]]) for n in gffn}
    gbig.update({n: jnp.stack(gfull[n]) for n in gfull})
    def by_core(n):
        g = _shard(gbig[n], SHARD_AXIS[n])
        return jnp.moveaxis(g.reshape((2, 2, 2) + g.shape[1:]), 2, 0).reshape((2, N_CHIP) + g.shape[1:])

    halves = [by_core(n) for n in BIG]
    chip_sums = []
    for hv, th in zip(halves, swap_with_sibling(halves)):
        cols = th.shape[-1]
        chip_sums.append(add_own_half(hv.reshape(2, -1, cols), th.reshape(-1, cols)).reshape(th.shape))
    parts = exchange_chips(chip_sums)
    out_g, out_d, out_m, out_v = {}, {}, {}, {}
    for n, prt in zip(BIG, parts):
        shp = w[n].shape
        flat = lambda a: a.reshape(-1, shp[-1])
        res = adamw_reduce(flat(w[n]), flat(mom[n]), flat(var[n]), prt.reshape(N_CHIP, -1, shp[-1]))
        out_g[n], out_d[n], out_m[n], out_v[n] = [r.reshape(shp) for r in res]

    def rows(a):
        return a.reshape(-1, d)

    def lane_rows(a):
        return jnp.pad(a.reshape(1, -1), ((0, 0), (0, d - a.size)))

    c_small = jnp.stack(g_c_small)
    pieces = [("meta", dh[:N_META]),
              ("ffn_norm", rows(jnp.stack([jnp.stack(r) for r in g_ffn_norm]))),
              ("mix_norm", rows(jnp.stack(g_mix_norm))),
              ("a_conv", rows(jnp.stack(g_a_conv))),
              ("b_conv", rows(jnp.stack(g_b_conv))),
              ("b_ln_g", rows(jnp.stack([st[0] for st in g_b_stats]))),
              ("b_ln_b", rows(jnp.stack([st[1] for st in g_b_stats]))),
              ("b_conv_bias", rows(jnp.stack([st[2] for st in g_b_stats]))),
              ("c_q_norm", lane_rows(c_small[:, 0, :HEAD_DIM])),
              ("c_k_norm", lane_rows(c_small[:, 1, :HEAD_DIM])),
              ("c_b_f", lane_rows(c_small[:, 2, :n_heads])),
              ("loss", lane_rows(loss_part[0:1, 0:1]))]
    packed = jnp.concatenate([pc for _, pc in pieces], axis=0)
    n_rows = packed.shape[0]
    packed = jnp.pad(packed, ((0, (-n_rows) % 8), (0, 0)))
    total = sum_devices(all_gather([packed])[0])
    small_g, r0 = {}, 0
    for n, pc in pieces:
        small_g[n] = total[r0:r0 + pc.shape[0]]
        r0 += pc.shape[0]
    loss = small_g.pop("loss")[0, 0]
    me = 4 * lax.axis_index("x") + 2 * lax.axis_index("y") + lax.axis_index("c")
    for n in SMALL_SHARDED:
        cols = w[n].shape[-1]
        g = lax.dynamic_slice_in_dim(small_g[n], me * cols, cols, axis=1)
        out_g[n] = g.reshape(w[n].shape)
    for n in SMALL_REPL:
        out_g[n] = small_g[n].reshape(-1)[:w[n].size].reshape(w[n].shape)
    small = SMALL_SHARDED + SMALL_REPL
    pack = lambda dct: jnp.concatenate([_pad_lanes(dct[n]) for n in small], axis=0)
    res = adamw_small(pack(w), pack(out_g), pack(mom), pack(var))
    r0 = 0
    for n in small:
        nr = -(-w[n].size // LANES)
        for dct, arr in zip((out_d, out_m, out_v), res):
            dct[n] = arr[r0:r0 + nr].reshape(-1)[:w[n].size].reshape(w[n].shape)
        r0 += nr

    return (loss, grad_x, *[out_g[n] for n in WEIGHTS], *[out_d[n] for n in WEIGHTS],
            *[out_m[n] for n in WEIGHTS], *[out_v[n] for n in WEIGHTS])
```

```python
import functools

import jax
import jax.numpy as jnp
from jax import lax
from jax.experimental import pallas as pl
from jax.experimental.pallas import tpu as pltpu

F32 = jnp.float32
BF16 = jnp.bfloat16
HI = lax.Precision.HIGHEST
EPS = 1e-6
NEG = -1e30
N_META = 16
HEAD_DIM = 64
LANES = 128
N_DEV = 8
ROW_ALIGN = 256
VMEM_LIMIT = 56 * 1024 * 1024
ADAM_LR, ADAM_B1, ADAM_B2, ADAM_EPS, ADAM_WD, ADAM_STEP = 0.001, 0.9, 0.999, 1e-08, 0.01, 10
MESH = pl.DeviceIdType.MESH
NT = (((1,), (1,)), ((), ()))
TN = (((0,), (0,)), ((), ()))


def _cparams(n_axes):
    return pltpu.CompilerParams(dimension_semantics=("arbitrary",) * n_axes, vmem_limit_bytes=VMEM_LIMIT)


def _pick(n, cands):
    for c in cands:
        if n % c == 0:
            return c
    raise ValueError(f"no tile for {n} among {cands}")


def _sigmoid(x):
    return 1.0 / (1.0 + jnp.exp(-x))


def _rms(x):
    r = lax.rsqrt(jnp.mean(x * x, axis=-1, keepdims=True) + EPS)
    return x * r, r


def _rms_bwd(xhat, r, g, dy):
    gy = dy * g
    dx = r * (gy - xhat * jnp.mean(gy * xhat, axis=-1, keepdims=True))
    return dx, jnp.sum(dy * xhat, axis=0, keepdims=True)


def _mesh_pos():
    return lax.axis_index("x"), lax.axis_index("y"), lax.axis_index("c")


def all_gather(arrs):
    n = len(arrs)
    hbm = pl.BlockSpec(memory_space=pltpu.HBM)

    def body(*refs):
        ins, outs = refs[:n], refs[n:2 * n]
        send_sems, recv_sems, local_sems = refs[2 * n:]
        x, y, c = _mesh_pos()
        me, sibling = (x, y, c), (x, y, 1 - c)
        xn, yn, dg = (1 - x, y), (x, 1 - y), (1 - x, 1 - y)

        def slot(a, px, py, pc):
            return outs[a].at[4 * px + 2 * py + pc]

        def copy(a, k, block, to, src=None):
            return pltpu.make_async_remote_copy(
                src_ref=slot(a, *block) if src is None else src, dst_ref=slot(a, *block),
                send_sem=send_sems.at[7 * a + k], recv_sem=recv_sems.at[7 * a + k],
                device_id=to, device_id_type=MESH)

        own, sent = [], []

        def start(cp):
            cp.start()
            sent.append(cp)

        for a in range(n):
            cp = pltpu.make_async_copy(ins[a], slot(a, *me), local_sems.at[a])
            cp.start()
            own.append(cp)
        for a in range(n):
            start(copy(a, 0, me, sibling, src=ins[a]))
            start(copy(a, 1, me, (*xn, c), src=ins[a]))
            start(copy(a, 2, me, (*yn, c), src=ins[a]))
        for a in range(n):
            via_x = a % 2 == 0
            copy(a, 1, (*xn, c), me).wait_recv()
            if via_x:
                start(copy(a, 3, (*xn, c), (*yn, c)))
            start(copy(a, 4, (*xn, c), sibling))
            copy(a, 2, (*yn, c), me).wait_recv()
            if not via_x:
                start(copy(a, 3, (*yn, c), (*xn, c)))
            start(copy(a, 5, (*yn, c), sibling))
        for a in range(n):
            copy(a, 3, (*dg, c), me).wait_recv()
            start(copy(a, 6, (*dg, c), sibling))
        for a in range(n):
            copy(a, 0, sibling, me).wait_recv()
            for k, chip in ((4, xn), (5, yn), (6, dg)):
                copy(a, k, (*chip, 1 - c), me).wait_recv()
        for cp in sent:
            cp.wait_send()
        for cp in own:
            cp.wait()

    return pl.pallas_call(
        body, name="all_gather",
        out_shape=[jax.ShapeDtypeStruct((N_DEV,) + a.shape, a.dtype) for a in arrs],
        in_specs=[hbm] * n, out_specs=[hbm] * n,
        scratch_shapes=[pltpu.SemaphoreType.DMA((7 * n,)), pltpu.SemaphoreType.DMA((7 * n,)),
                        pltpu.SemaphoreType.DMA((n,))],
    )(*arrs)


N_CHIP = 4


def swap_with_sibling(arrs):
    n = len(arrs)
    hbm = pl.BlockSpec(memory_space=pltpu.HBM)

    def body(*refs):
        ins, outs = refs[:n], refs[n:2 * n]
        send_sems, recv_sems = refs[2 * n:]
        x, y, c = _mesh_pos()
        copies = [pltpu.make_async_remote_copy(
            src_ref=ins[a].at[1 - c], dst_ref=outs[a], send_sem=send_sems.at[a], recv_sem=recv_sems.at[a],
            device_id=(x, y, 1 - c), device_id_type=MESH) for a in range(n)]
        for cp in copies:
            cp.start()
        for cp in copies:
            cp.wait()

    return pl.pallas_call(
        body, name="swap_with_sibling",
        out_shape=[jax.ShapeDtypeStruct(a.shape[1:], a.dtype) for a in arrs],
        in_specs=[hbm] * n, out_specs=[hbm] * n,
        scratch_shapes=[pltpu.SemaphoreType.DMA((n,)), pltpu.SemaphoreType.DMA((n,))],
    )(*arrs)


def exchange_chips(arrs):
    n = len(arrs)
    hbm = pl.BlockSpec(memory_space=pltpu.HBM)

    def body(*refs):
        ins, outs = refs[:n], refs[n:2 * n]
        send_sems, recv_sems, local_sems = refs[2 * n:]
        x, y, c = _mesh_pos()
        me = 2 * x + y
        own, sent = [], []
        for a in range(n):
            cp = pltpu.make_async_copy(ins[a].at[me], outs[a].at[me], local_sems.at[a])
            cp.start()
            own.append(cp)
        for k in range(1, N_CHIP):
            px = 1 - x if k & 2 else x
            py = 1 - y if k & 1 else y
            peer = 2 * px + py
            for a in range(n):
                cp = pltpu.make_async_remote_copy(
                    src_ref=ins[a].at[peer], dst_ref=outs[a].at[me],
                    send_sem=send_sems.at[3 * a + k - 1], recv_sem=recv_sems.at[3 * a + k - 1],
                    device_id=(px, py, c), device_id_type=MESH)
                cp.start()
                sent.append((cp, a, k, peer))
        for cp, a, k, peer in sent:
            pltpu.make_async_remote_copy(
                src_ref=ins[a].at[peer], dst_ref=outs[a].at[peer],
                send_sem=send_sems.at[3 * a + k - 1], recv_sem=recv_sems.at[3 * a + k - 1],
                device_id=(x, y, c), device_id_type=MESH).wait_recv()
        for cp, a, k, peer in sent:
            cp.wait_send()
        for cp in own:
            cp.wait()

    return pl.pallas_call(
        body, name="exchange_chips",
        out_shape=[jax.ShapeDtypeStruct(a.shape, a.dtype) for a in arrs],
        in_specs=[hbm] * n, out_specs=[hbm] * n,
        scratch_shapes=[pltpu.SemaphoreType.DMA((3 * n,)), pltpu.SemaphoreType.DMA((3 * n,)),
                        pltpu.SemaphoreType.DMA((n,))],
    )(*arrs)


def _near_far(a, x, y):
    xn, yn = (1 - x, y), (x, 1 - y)
    return (xn, yn) if a % 2 == 0 else (yn, xn)


def exchange_near(arrs):
    n = len(arrs)
    hbm = pl.BlockSpec(memory_space=pltpu.HBM)

    def body(*refs):
        ins, direct, relay = refs[:n], refs[n:2 * n], refs[2 * n:3 * n]
        send_sems, recv_sems = refs[3 * n:]
        x, y, c = _mesh_pos()
        copies = []
        for a in range(n):
            (nx, ny), _ = _near_far(a, x, y)
            for k, (slab, dst) in enumerate(((2 * nx + ny, direct[a]), (2 * (1 - x) + (1 - y), relay[a]))):
                copies.append(pltpu.make_async_remote_copy(
                    src_ref=ins[a].at[slab], dst_ref=dst, send_sem=send_sems.at[2 * a + k],
                    recv_sem=recv_sems.at[2 * a + k], device_id=(nx, ny, c), device_id_type=MESH))
        for cp in copies:
            cp.start()
        for cp in copies:
            cp.wait()

    return pl.pallas_call(
        body, name="exchange_near",
        out_shape=[jax.ShapeDtypeStruct(a.shape[1:], a.dtype) for a in arrs] * 2,
        in_specs=[hbm] * n, out_specs=[hbm] * (2 * n),
        scratch_shapes=[pltpu.SemaphoreType.DMA((2 * n,)), pltpu.SemaphoreType.DMA((2 * n,))],
    )(*arrs)


def merge_far(slabs, relay, a):
    _, r, c = slabs.shape
    tr = _pick(r, (1024, 704, 512, 256, 128, 8))

    def body(s_ref, r_ref, m_ref, o_ref):
        x, y = lax.axis_index("x"), lax.axis_index("y")
        _, (fx, fy) = _near_far(a, x, y)
        m_ref[...] = (s_ref[2 * fx + fy].astype(F32) + r_ref[...].astype(F32)).astype(BF16)
        o_ref[...] = s_ref[2 * x + y]

    blk = pl.BlockSpec((tr, c), lambda i: (i, 0))
    return pl.pallas_call(
        body, name="merge_far", grid=(r // tr,), out_shape=[jax.ShapeDtypeStruct((r, c), BF16)] * 2,
        in_specs=[pl.BlockSpec((N_CHIP, tr, c), lambda i: (0, i, 0)), blk], out_specs=[blk, blk],
        compiler_params=_cparams(1),
    )(slabs, relay)


def exchange_far(arrs):
    n = len(arrs)
    hbm = pl.BlockSpec(memory_space=pltpu.HBM)

    def body(*refs):
        ins, outs = refs[:n], refs[n:2 * n]
        send_sems, recv_sems = refs[2 * n:]
        x, y, c = _mesh_pos()
        copies = []
        for a in range(n):
            _, (fx, fy) = _near_far(a, x, y)
            copies.append(pltpu.make_async_remote_copy(
                src_ref=ins[a], dst_ref=outs[a], send_sem=send_sems.at[a], recv_sem=recv_sems.at[a],
                device_id=(fx, fy, c), device_id_type=MESH))
        for cp in copies:
            cp.start()
        for cp in copies:
            cp.wait()

    return pl.pallas_call(
        body, name="exchange_far",
        out_shape=[jax.ShapeDtypeStruct(a.shape, a.dtype) for a in arrs],
        in_specs=[hbm] * n, out_specs=[hbm] * n,
        scratch_shapes=[pltpu.SemaphoreType.DMA((n,)), pltpu.SemaphoreType.DMA((n,))],
    )(*arrs)


def add_own_half(halves, theirs):
    _, r, c = halves.shape
    tr = _pick(r, (2048, 1408, 1024, 512, 256, 128, 8))

    def body(a_ref, b_ref, o_ref):
        mine = a_ref[lax.axis_index("c")]
        o_ref[...] = (mine.astype(F32) + b_ref[...].astype(F32)).astype(BF16)

    blk = pl.BlockSpec((tr, c), lambda i: (i, 0))
    return pl.pallas_call(
        body, name="add_own_half", grid=(r // tr,), out_shape=jax.ShapeDtypeStruct((r, c), BF16),
        in_specs=[pl.BlockSpec((2, tr, c), lambda i: (0, i, 0)), blk], out_specs=blk, compiler_params=_cparams(1),
    )(halves, theirs)


FFN_CHUNK = (1408, 1024, 512, 256, 128)


def _resident(shape):
    return pl.BlockSpec(shape, lambda i: (0,) * len(shape), pipeline_mode=pl.Buffered(1))


def ffn_fwd(h, g, wg, wu, wd):
    tp, d = h.shape
    f = wg.shape[1]
    tm = _pick(tp, (384, 256))
    tf = _pick(f, FFN_CHUNK)

    def body(h_ref, g_ref, wg_ref, wu_ref, wd_ref, ho_ref, gate_ref, up_ref):
        x = h_ref[...]
        xhat, _ = _rms(x)
        hn = (xhat * g_ref[...]).astype(BF16)
        acc = jnp.zeros((tm, d), F32)
        for c0 in range(0, f, tf):
            gate = jnp.dot(hn, wg_ref[:, c0:c0 + tf], preferred_element_type=F32)
            up = jnp.dot(hn, wu_ref[:, c0:c0 + tf], preferred_element_type=F32)
            gate_ref[:, c0:c0 + tf] = gate.astype(BF16)
            up_ref[:, c0:c0 + tf] = up.astype(BF16)
            act = (gate * _sigmoid(gate) * up).astype(BF16)
            acc = acc + jnp.dot(act, wd_ref[c0:c0 + tf, :], preferred_element_type=F32)
        ho_ref[...] = x + 0.5 * acc

    row = lambda n: pl.BlockSpec((tm, n), lambda i: (i, 0))
    return pl.pallas_call(
        body, name="ffn_fwd", grid=(tp // tm,),
        out_shape=[jax.ShapeDtypeStruct((tp, d), F32), jax.ShapeDtypeStruct((tp, f), BF16),
                   jax.ShapeDtypeStruct((tp, f), BF16)],
        in_specs=[row(d), _resident((1, d)), _resident((d, f)), _resident((d, f)), _resident((f, d))],
        out_specs=[row(d), row(f), row(f)],
        compiler_params=_cparams(1),
    )(h, g, wg, wu, wd)


def ffn_bwd(dho, h, g, gate, up, wg, wu, wd):
    tp, d = h.shape
    f = wg.shape[1]
    tm = _pick(tp, (256,))
    tf = _pick(f, FFN_CHUNK)

    def body(dho_ref, h_ref, g_ref, gate_ref, up_ref, wg_ref, wu_ref, wd_ref,
             dhi_ref, hn_ref, dgate_ref, dup_ref, act_ref, dg_ref):
        @pl.when(pl.program_id(0) == 0)
        def _():
            dg_ref[...] = jnp.zeros_like(dg_ref)

        dho_ = dho_ref[...]
        dout = (0.5 * dho_).astype(BF16)
        dhn = jnp.zeros((tm, d), F32)
        for c0 in range(0, f, tf):
            dact = lax.dot_general(dout, wd_ref[c0:c0 + tf, :], NT, preferred_element_type=F32)
            gt = gate_ref[:, c0:c0 + tf].astype(F32)
            u = up_ref[:, c0:c0 + tf].astype(F32)
            sig = _sigmoid(gt)
            silu = gt * sig
            act_ref[:, c0:c0 + tf] = (silu * u).astype(BF16)
            dup = (dact * silu).astype(BF16)
            dgate = (dact * u * (sig * (1.0 + gt * (1.0 - sig)))).astype(BF16)
            dup_ref[:, c0:c0 + tf] = dup
            dgate_ref[:, c0:c0 + tf] = dgate
            dhn = dhn + (lax.dot_general(dgate, wg_ref[:, c0:c0 + tf], NT, preferred_element_type=F32)
                         + lax.dot_general(dup, wu_ref[:, c0:c0 + tf], NT, preferred_element_type=F32))
        xhat, r = _rms(h_ref[...])
        gg = g_ref[...]
        dx, dgp = _rms_bwd(xhat, r, gg, dhn)
        dg_ref[...] += dgp
        dhi_ref[...] = dho_ + dx
        hn_ref[...] = (xhat * gg).astype(BF16)

    row = lambda n: pl.BlockSpec((tm, n), lambda i: (i, 0))
    return pl.pallas_call(
        body, name="ffn_bwd", grid=(tp // tm,),
        out_shape=[jax.ShapeDtypeStruct((tp, d), F32), jax.ShapeDtypeStruct((tp, d), BF16),
                   jax.ShapeDtypeStruct((tp, f), BF16), jax.ShapeDtypeStruct((tp, f), BF16),
                   jax.ShapeDtypeStruct((tp, f), BF16), jax.ShapeDtypeStruct((1, d), F32)],
        in_specs=[row(d), row(d), _resident((1, d)), row(f), row(f),
                  _resident((d, f)), _resident((d, f)), _resident((f, d))],
        out_specs=[row(d), row(d), row(f), row(f), row(f), pl.BlockSpec((1, d), lambda i: (0, 0))],
        compiler_params=_cparams(1),
    )(dho, h, g, gate, up, wg, wu, wd)


def atb(a, b, scale=1.0):
    tp, m = a.shape
    n = b.shape[1]
    tmm = _pick(m, (1024, 1408, 640, 512, 256, 128))
    tn = _pick(n, (1024, 1408, 640, 512, 256, 128))
    tk = _pick(tp, (1408, 768, 512, 256))
    nk = tp // tk

    def body(a_ref, b_ref, o_ref, acc):
        k = pl.program_id(2)

        @pl.when(k == 0)
        def _():
            acc[...] = jnp.zeros_like(acc)

        acc[...] += lax.dot_general(a_ref[...].astype(BF16), b_ref[...].astype(BF16), TN,
                                    preferred_element_type=F32)

        @pl.when(k == nk - 1)
        def _():
            o_ref[...] = (acc[...] * scale).astype(BF16)

    return pl.pallas_call(
        body, name="atb", grid=(m // tmm, n // tn, nk),
        out_shape=jax.ShapeDtypeStruct((m, n), BF16),
        in_specs=[pl.BlockSpec((tk, tmm), lambda i, j, k: (k, i)), pl.BlockSpec((tk, tn), lambda i, j, k: (k, j))],
        out_specs=pl.BlockSpec((tmm, tn), lambda i, j, k: (i, j)),
        scratch_shapes=[pltpu.VMEM((tmm, tn), F32)],
        compiler_params=_cparams(3),
    )(a, b)


def norm_matmul(h, g, w, out_dtype, emit_u):
    tp, d = h.shape
    n = w.shape[1]
    tm = _pick(tp, (768, 512, 256))
    tn = _pick(n, (1024, 768, 640, 512, 256, 128))

    def body(h_ref, g_ref, w_ref, o_ref, *rest):
        u_sc = rest[-1]

        @pl.when(pl.program_id(1) == 0)
        def _():
            xhat, _ = _rms(h_ref[...])
            u_sc[...] = (xhat * g_ref[...]).astype(BF16)
            if emit_u:
                rest[0][...] = u_sc[...]

        o_ref[...] = jnp.dot(u_sc[...], w_ref[...], preferred_element_type=F32).astype(out_dtype)

    out_shape = [jax.ShapeDtypeStruct((tp, n), out_dtype)]
    out_specs = [pl.BlockSpec((tm, tn), lambda i, j: (i, j))]
    if emit_u:
        out_shape.append(jax.ShapeDtypeStruct((tp, d), BF16))
        out_specs.append(pl.BlockSpec((tm, d), lambda i, j: (i, 0)))
    return pl.pallas_call(
        body, name="norm_matmul", grid=(tp // tm, n // tn), out_shape=out_shape,
        in_specs=[pl.BlockSpec((tm, d), lambda i, j: (i, 0)), pl.BlockSpec((1, d), lambda i, j: (0, 0)),
                  pl.BlockSpec((d, tn), lambda i, j: (0, j))],
        out_specs=out_specs, scratch_shapes=[pltpu.VMEM((tm, d), BF16)],
        compiler_params=_cparams(2),
    )(h, g, w)


def res_matmul(h, y, w):
    tp, d = h.shape
    k = y.shape[1]
    tm = _pick(tp, (768, 512, 256))

    def body(h_ref, y_ref, w_ref, o_ref):
        o_ref[...] = h_ref[...] + jnp.dot(y_ref[...], w_ref[...], preferred_element_type=F32)

    return pl.pallas_call(
        body, name="res_matmul", grid=(tp // tm,), out_shape=jax.ShapeDtypeStruct((tp, d), F32),
        in_specs=[pl.BlockSpec((tm, d), lambda i: (i, 0)), pl.BlockSpec((tm, k), lambda i: (i, 0)),
                  pl.BlockSpec((k, d), lambda i: (0, 0))],
        out_specs=pl.BlockSpec((tm, d), lambda i: (i, 0)),
        compiler_params=_cparams(1),
    )(h, y, w)


def matmul_nt(a, w):
    tp, d = a.shape
    k = w.shape[0]
    tm = _pick(tp, (768, 512, 256))

    def body(a_ref, w_ref, o_ref):
        o_ref[...] = lax.dot_general(a_ref[...].astype(BF16), w_ref[...], NT,
                                     preferred_element_type=F32).astype(BF16)

    return pl.pallas_call(
        body, name="matmul_nt", grid=(tp // tm,), out_shape=jax.ShapeDtypeStruct((tp, k), BF16),
        in_specs=[pl.BlockSpec((tm, d), lambda i: (i, 0)), pl.BlockSpec((k, d), lambda i: (0, 0))],
        out_specs=pl.BlockSpec((tm, k), lambda i: (i, 0)),
        compiler_params=_cparams(1),
    )(a, w)


def matmul_nt_rms_bwd(dp, w, h, g, dres):
    tp, d = h.shape
    n = w.shape[1]
    tm = _pick(tp, (768, 512, 256))

    def body(dp_ref, w_ref, h_ref, g_ref, dres_ref, dh_ref, dg_ref):
        @pl.when(pl.program_id(0) == 0)
        def _():
            dg_ref[...] = jnp.zeros_like(dg_ref)

        du = lax.dot_general(dp_ref[...], w_ref[...], NT, preferred_element_type=F32)
        xhat, r = _rms(h_ref[...])
        dx, dgp = _rms_bwd(xhat, r, g_ref[...], du)
        dg_ref[...] += dgp
        dh_ref[...] = dres_ref[...] + dx

    return pl.pallas_call(
        body, name="matmul_nt_rms_bwd", grid=(tp // tm,),
        out_shape=[jax.ShapeDtypeStruct((tp, d), F32), jax.ShapeDtypeStruct((1, d), F32)],
        in_specs=[pl.BlockSpec((tm, n), lambda i: (i, 0)), _resident((d, n)),
                  pl.BlockSpec((tm, d), lambda i: (i, 0)), _resident((1, d)),
                  pl.BlockSpec((tm, d), lambda i: (i, 0))],
        out_specs=[pl.BlockSpec((tm, d), lambda i: (i, 0)), pl.BlockSpec((1, d), lambda i: (0, 0))],
        compiler_params=_cparams(1),
    )(dp, w, h, g, dres)


CONV_TILE = (256,)


def _halo_before(tm, hb, col=0):
    return lambda i: (jnp.maximum(i * (tm // hb) - 1, 0), col)


def _halo_after(tm, hb, nblk, col=0):
    return lambda i: (jnp.minimum((i + 1) * (tm // hb), nblk - 1), col)


def a_elem_fwd(p, w):
    tp, d3 = p.shape
    d = d3 // 3
    kw = w.shape[0]
    tm = _pick(tp, CONV_TILE)
    hb = 16

    def body(p_ref, ph_ref, w_ref, y_ref, ext):
        i = pl.program_id(0)
        halo = ph_ref[:, d:2 * d].astype(F32) * ph_ref[:, 2 * d:].astype(F32)
        ext[0:hb, :] = jnp.where(i > 0, halo, 0.0)
        ext[hb:, :] = p_ref[:, d:2 * d].astype(F32) * p_ref[:, 2 * d:].astype(F32)
        z = jnp.zeros((tm, d), F32)
        for k in range(kw):
            z = z + w_ref[k:k + 1, :] * ext[pl.ds(hb - (kw - 1) + k, tm), :]
        y_ref[...] = (p_ref[:, 0:d].astype(F32) * z).astype(BF16)

    return pl.pallas_call(
        body, name="a_elem_fwd", grid=(tp // tm,), out_shape=jax.ShapeDtypeStruct((tp, d), BF16),
        in_specs=[pl.BlockSpec((tm, d3), lambda i: (i, 0)), pl.BlockSpec((hb, d3), _halo_before(tm, hb)),
                  pl.BlockSpec((kw, d), lambda i: (0, 0))],
        out_specs=pl.BlockSpec((tm, d), lambda i: (i, 0)),
        scratch_shapes=[pltpu.VMEM((tm + hb, d), F32)],
        compiler_params=_cparams(1),
    )(p, p, w)


def a_elem_bwd(p, dy, w):
    tp, d3 = p.shape
    d = d3 // 3
    kw = w.shape[0]
    tm = _pick(tp, CONV_TILE)
    hb = 16
    n = tp // tm

    def body(p_ref, ph_ref, pa_ref, dy_ref, dya_ref, w_ref, dp_ref, dw_ref, ext, dzext):
        i = pl.program_id(0)

        @pl.when(i == 0)
        def _():
            dw_ref[...] = jnp.zeros_like(dw_ref)

        b = p_ref[:, 0:d].astype(F32)
        c = p_ref[:, d:2 * d].astype(F32)
        v = p_ref[:, 2 * d:].astype(F32)
        dy_ = dy_ref[...].astype(F32)
        halo = ph_ref[:, d:2 * d].astype(F32) * ph_ref[:, 2 * d:].astype(F32)
        ext[0:hb, :] = jnp.where(i > 0, halo, 0.0)
        ext[hb:, :] = c * v
        dz = dy_ * b
        dzext[0:tm, :] = dz
        dzext[tm:, :] = jnp.where(i < n - 1, dya_ref[...].astype(F32) * pa_ref[:, 0:d].astype(F32), 0.0)
        z = jnp.zeros((tm, d), F32)
        dcv = jnp.zeros((tm, d), F32)
        for k in range(kw):
            sh = ext[pl.ds(hb - (kw - 1) + k, tm), :]
            z = z + w_ref[k:k + 1, :] * sh
            dw_ref[k:k + 1, :] += jnp.sum(dz * sh, axis=0, keepdims=True)
            dcv = dcv + w_ref[k:k + 1, :] * dzext[pl.ds(kw - 1 - k, tm), :]
        dp_ref[:, 0:d] = (dy_ * z).astype(BF16)
        dp_ref[:, d:2 * d] = (dcv * v).astype(BF16)
        dp_ref[:, 2 * d:] = (dcv * c).astype(BF16)

    nb = tp // hb
    return pl.pallas_call(
        body, name="a_elem_bwd", grid=(n,),
        out_shape=[jax.ShapeDtypeStruct((tp, d3), BF16), jax.ShapeDtypeStruct((8, d), F32)],
        in_specs=[pl.BlockSpec((tm, d3), lambda i: (i, 0)), pl.BlockSpec((hb, d3), _halo_before(tm, hb)),
                  pl.BlockSpec((hb, d3), _halo_after(tm, hb, nb)),
                  pl.BlockSpec((tm, d), lambda i: (i, 0)), pl.BlockSpec((hb, d), _halo_after(tm, hb, nb)),
                  pl.BlockSpec((kw, d), lambda i: (0, 0))],
        out_specs=[pl.BlockSpec((tm, d3), lambda i: (i, 0)), pl.BlockSpec((8, d), lambda i: (0, 0))],
        scratch_shapes=[pltpu.VMEM((tm + hb, d), F32), pltpu.VMEM((tm + hb, d), F32)],
        compiler_params=_cparams(1),
    )(p, p, p, dy, dy, w)


CH_R, CH_C = 128, 128


def _chunks(tm, d):
    return [(r0, c0) for c0 in range(0, d, CH_C) for r0 in range(0, tm, CH_R)]


def _b_u2(p_ref, ph_ref, w_ref, bias_ref, ext, u2_sc, i, tm, d, kw, hb):
    a = p_ref[:, 0:d].astype(F32)
    sg = _sigmoid(p_ref[:, d:].astype(F32))
    halo = ph_ref[:, 0:d].astype(F32) * _sigmoid(ph_ref[:, d:].astype(F32))
    ext[0:hb, :] = jnp.where(i > 0, halo, 0.0)
    ext[hb:, :] = a * sg
    for r0, c0 in _chunks(tm, d):
        cs = slice(c0, c0 + CH_C)
        acc = jnp.zeros((CH_R, CH_C), F32) + bias_ref[:, cs]
        for k in range(kw):
            acc = acc + w_ref[k:k + 1, cs] * ext[pl.ds(r0 + hb - (kw - 1) + k, CH_R), cs]
        u2_sc[r0:r0 + CH_R, cs] = acc
    return a, sg, u2_sc[...]


def _ln(u2):
    mu = jnp.mean(u2, axis=-1, keepdims=True)
    xc = u2 - mu
    rstd = lax.rsqrt(jnp.mean(xc * xc, axis=-1, keepdims=True) + EPS)
    return xc * rstd, rstd


def b_elem_fwd(p, w, bias, ln_g, ln_b):
    tp, d2 = p.shape
    d = d2 // 2
    kw = w.shape[0]
    tm = _pick(tp, CONV_TILE)
    hb = 32

    def body(p_ref, ph_ref, w_ref, bias_ref, g_ref, b_ref, y_ref, ext, u2_sc):
        i = pl.program_id(0)
        _, _, u2 = _b_u2(p_ref, ph_ref, w_ref, bias_ref, ext, u2_sc, i, tm, d, kw, hb)
        xhat, _ = _ln(u2)
        u3 = xhat * g_ref[...] + b_ref[...]
        y_ref[...] = (u3 * _sigmoid(u3)).astype(BF16)

    vec = pl.BlockSpec((1, d), lambda i: (0, 0))
    return pl.pallas_call(
        body, name="b_elem_fwd", grid=(tp // tm,), out_shape=jax.ShapeDtypeStruct((tp, d), BF16),
        in_specs=[pl.BlockSpec((tm, d2), lambda i: (i, 0)), pl.BlockSpec((hb, d2), _halo_before(tm, hb)),
                  pl.BlockSpec((kw, d), lambda i: (0, 0)), vec, vec, vec],
        out_specs=pl.BlockSpec((tm, d), lambda i: (i, 0)),
        scratch_shapes=[pltpu.VMEM((tm + hb, d), F32), pltpu.VMEM((tm, d), F32)],
        compiler_params=_cparams(1),
    )(p, p, w, bias, ln_g, ln_b)


def b_elem_bwd1(p, dy, w, bias, ln_g, ln_b):
    tp, d2 = p.shape
    d = d2 // 2
    kw = w.shape[0]
    tm = _pick(tp, CONV_TILE)
    hb = 32

    def body(p_ref, ph_ref, dy_ref, w_ref, bias_ref, g_ref, b_ref, du2_ref, st_ref, ext, u2_sc):
        i = pl.program_id(0)

        @pl.when(i == 0)
        def _():
            st_ref[...] = jnp.zeros_like(st_ref)

        _, _, u2 = _b_u2(p_ref, ph_ref, w_ref, bias_ref, ext, u2_sc, i, tm, d, kw, hb)
        xhat, rstd = _ln(u2)
        u3 = xhat * g_ref[...] + b_ref[...]
        s3 = _sigmoid(u3)
        du3 = dy_ref[...].astype(F32) * (s3 * (1.0 + u3 * (1.0 - s3)))
        dxh = du3 * g_ref[...]
        du2 = rstd * (dxh - jnp.mean(dxh, axis=-1, keepdims=True)
                      - xhat * jnp.mean(dxh * xhat, axis=-1, keepdims=True))
        du2_ref[...] = du2
        st_ref[0:1, :] += jnp.sum(du3 * xhat, axis=0, keepdims=True)
        st_ref[1:2, :] += jnp.sum(du3, axis=0, keepdims=True)
        st_ref[2:3, :] += jnp.sum(du2, axis=0, keepdims=True)

    vec = pl.BlockSpec((1, d), lambda i: (0, 0))
    return pl.pallas_call(
        body, name="b_elem_bwd1", grid=(tp // tm,),
        out_shape=[jax.ShapeDtypeStruct((tp, d), F32), jax.ShapeDtypeStruct((8, d), F32)],
        in_specs=[pl.BlockSpec((tm, d2), lambda i: (i, 0)), pl.BlockSpec((hb, d2), _halo_before(tm, hb)),
                  pl.BlockSpec((tm, d), lambda i: (i, 0)), pl.BlockSpec((kw, d), lambda i: (0, 0)), vec, vec, vec],
        out_specs=[pl.BlockSpec((tm, d), lambda i: (i, 0)), pl.BlockSpec((8, d), lambda i: (0, 0))],
        scratch_shapes=[pltpu.VMEM((tm + hb, d), F32), pltpu.VMEM((tm, d), F32)],
        compiler_params=_cparams(1),
    )(p, p, dy, w, bias, ln_g, ln_b)


def b_elem_bwd2(p, du2, w):
    tp, d2 = p.shape
    d = d2 // 2
    kw = w.shape[0]
    tm = _pick(tp, CONV_TILE)
    hb = 32
    n = tp // tm

    def body(p_ref, du2_ref, du2a_ref, w_ref, dp_ref, dw_ref, dext, dwacc):
        i = pl.program_id(0)

        @pl.when(i == 0)
        def _():
            dwacc[...] = jnp.zeros_like(dwacc)

        dext[0:tm, :] = du2_ref[...]
        dext[tm:, :] = jnp.where(i < n - 1, du2a_ref[...], 0.0)
        for r0, c0 in _chunks(tm, d):
            cs = slice(c0, c0 + CH_C)
            a = p_ref[r0:r0 + CH_R, cs].astype(F32)
            sg = _sigmoid(p_ref[r0:r0 + CH_R, d + c0:d + c0 + CH_C].astype(F32))
            u1 = a * sg
            du1 = jnp.zeros((CH_R, CH_C), F32)
            for k in range(kw):
                sh = dext[pl.ds(r0 + kw - 1 - k, CH_R), cs]
                du1 = du1 + w_ref[k:k + 1, cs] * sh
                prod = sh * u1
                part = prod[0:8]
                for r in range(8, CH_R, 8):
                    part = part + prod[r:r + 8]
                dwacc[8 * k:8 * k + 8, cs] += part
            dp_ref[r0:r0 + CH_R, cs] = (du1 * sg).astype(BF16)
            dp_ref[r0:r0 + CH_R, d + c0:d + c0 + CH_C] = (du1 * a * sg * (1.0 - sg)).astype(BF16)

        @pl.when(i == n - 1)
        def _():
            dw_ref[...] = jnp.zeros_like(dw_ref)
            for k in range(kw):
                dw_ref[k:k + 1, :] = jnp.sum(dwacc[8 * k:8 * k + 8, :], axis=0, keepdims=True)

    nb = tp // hb
    return pl.pallas_call(
        body, name="b_elem_bwd2", grid=(n,),
        out_shape=[jax.ShapeDtypeStruct((tp, d2), BF16), jax.ShapeDtypeStruct((32, d), F32)],
        in_specs=[pl.BlockSpec((tm, d2), lambda i: (i, 0)),
                  pl.BlockSpec((tm, d), lambda i: (i, 0)), pl.BlockSpec((hb, d), _halo_after(tm, hb, nb)),
                  pl.BlockSpec((kw, d), lambda i: (0, 0))],
        out_specs=[pl.BlockSpec((tm, d2), lambda i: (i, 0)), pl.BlockSpec((32, d), lambda i: (0, 0))],
        scratch_shapes=[pltpu.VMEM((tm + hb, d), F32), pltpu.VMEM((8 * kw, d), F32)],
        compiler_params=_cparams(1),
    )(p, du2, du2, w)


ATT_TILE = 256


def _head_masks(d):
    c = lax.broadcasted_iota(jnp.int32, (d, LANES), 0)
    h = lax.broadcasted_iota(jnp.int32, (d, LANES), 1)
    seg = (c // HEAD_DIM == h).astype(BF16)
    fold = (c % HEAD_DIM == h).astype(BF16)
    return seg, seg.T, fold


def _dot2(x, m):
    hi = x.astype(BF16)
    lo = (x - hi.astype(F32)).astype(BF16)
    return jnp.dot(hi, m, preferred_element_type=F32) + jnp.dot(lo, m, preferred_element_type=F32)


L2E = 1.4426950408889634
LN2 = 0.6931471805599453
AUG = HEAD_DIM


def _split3(r):
    r1 = r.astype(BF16).astype(F32)
    r2 = (r - r1).astype(BF16).astype(F32)
    r3 = (r - r1 - r2).astype(BF16).astype(F32)
    return r1, r2, r3


def c_prep_slots(p, b_f, qg, kg):
    tp = p.shape[0]
    d = qg.shape[1]
    n_heads = d // HEAD_DIM
    tm = ATT_TILE
    seg, seg_t, _ = _head_masks(d)

    def body(p_ref, pf_ref, bf_ref, qg_ref, kg_ref, seg_ref, segt_ref, q_ref, k_ref, v_ref, cum_ref, carry):
        @pl.when(pl.program_id(0) == 0)
        def _():
            carry[...] = jnp.zeros_like(carry)

        def norm(x, g):
            ms = _dot2(x * x, seg_ref[...]) * (1.0 / HEAD_DIM)
            r = _dot2(lax.rsqrt(ms + EPS), segt_ref[...])
            return x * r * g

        qn = norm(p_ref[:, 0:d], qg_ref[...]) * (HEAD_DIM ** -0.5 * L2E)
        kn = norm(p_ref[:, d:2 * d], kg_ref[...])
        v_ref[...] = p_ref[:, 2 * d:].astype(BF16)
        xf = pf_ref[...] + bf_ref[...]
        logf = jnp.minimum(xf, 0.0) - jnp.log(1.0 + jnp.exp(-jnp.abs(xf)))
        r_ = lax.broadcasted_iota(jnp.int32, (tm, tm), 0)
        c_ = lax.broadcasted_iota(jnp.int32, (tm, tm), 1)
        cum = jnp.dot((c_ <= r_).astype(F32), logf, precision=HI, preferred_element_type=F32) + carry[0:1, :]
        cum_ref[...] = cum
        carry[0:1, :] += jnp.sum(logf, axis=0, keepdims=True)

        lane = lax.broadcasted_iota(jnp.int32, (1, LANES), 1)
        ones_q = jnp.where((lane >= AUG + 3) & (lane < AUG + 6), 1.0, 0.0)
        ones_k = jnp.where((lane >= AUG) & (lane < AUG + 3), 1.0, 0.0)
        for h in range(n_heads):
            c0 = LANES * (h // 2)
            src_q, src_k = qn[:, c0:c0 + LANES], kn[:, c0:c0 + LANES]
            if h % 2:
                src_q = pltpu.roll(src_q, HEAD_DIM, axis=1)
                src_k = pltpu.roll(src_k, HEAD_DIM, axis=1)
            c = cum[:, h:h + 1] * L2E
            a1, a2, a3 = _split3(c)
            b1, b2, b3 = _split3(-c)
            aug_q = jnp.where(lane == AUG, a1, jnp.where(lane == AUG + 1, a2, jnp.where(lane == AUG + 2, a3, ones_q)))
            aug_k = jnp.where(lane == AUG + 3, b1,
                              jnp.where(lane == AUG + 4, b2, jnp.where(lane == AUG + 5, b3, ones_k)))
            q_ref[:, LANES * h:LANES * (h + 1)] = jnp.where(lane < HEAD_DIM, src_q, aug_q).astype(BF16)
            k_ref[:, LANES * h:LANES * (h + 1)] = jnp.where(lane < HEAD_DIM, src_k, aug_k).astype(BF16)

    vec = pl.BlockSpec((1, d), lambda i: (0, 0))
    rowd = pl.BlockSpec((tm, d), lambda i: (i, 0))
    slots = pl.BlockSpec((tm, n_heads * LANES), lambda i: (i, 0))
    return pl.pallas_call(
        body, name="c_prep_slots", grid=(tp // tm,),
        out_shape=[jax.ShapeDtypeStruct((tp, n_heads * LANES), BF16)] * 2
        + [jax.ShapeDtypeStruct((tp, d), BF16), jax.ShapeDtypeStruct((tp, LANES), F32)],
        in_specs=[pl.BlockSpec((tm, 3 * d), lambda i: (i, 0)), pl.BlockSpec((tm, LANES), lambda i: (i, 3 * d // LANES)),
                  pl.BlockSpec((1, LANES), lambda i: (0, 0)), vec, vec,
                  pl.BlockSpec((d, LANES), lambda i: (0, 0)), pl.BlockSpec((LANES, d), lambda i: (0, 0))],
        out_specs=[slots, slots, rowd, pl.BlockSpec((tm, LANES), lambda i: (i, 0))],
        scratch_shapes=[pltpu.VMEM((8, LANES), F32)],
        compiler_params=_cparams(1),
    )(p, p, b_f, qg, kg, seg, seg_t)


ATT_GROUPS = (8, 2)


def _grouped_loop(n_blocks, body, carry):
    start = 0
    for g in ATT_GROUPS:
        count = (n_blocks - start) // g
        carry = lax.fori_loop(0, count, lambda i, c, g=g, start=start: body(start + i * g, g, c), carry)
        start = start + count * g
    return lax.fori_loop(start, n_blocks, lambda kj, c: body(kj, 1, c), carry)


def _diag_mask(t):
    return lax.broadcasted_iota(jnp.int32, (t, t), 0) <= lax.broadcasted_iota(jnp.int32, (t, t), 1)


def attn_fwd_t(qpt, kp, vt4):
    tp = kp.shape[0]
    t = ATT_TILE
    nb = tp // t
    npair = vt4.shape[0]

    def body(q_ref, k_ref, v_ref, o_ref, lse_ref):
        qi = pl.program_id(1)
        qts = (q_ref[0:LANES, :], q_ref[LANES:2 * LANES, :])
        mask = _diag_mask(t)

        def blocks(kj0, n, carry, masked=False):
            out = []
            for hd in range(2):
                m, l, acc = carry[3 * hd:3 * hd + 3]
                sts = []
                for g in range(n):
                    off = pl.multiple_of((kj0 + g) * t, t)
                    st = jnp.dot(k_ref[pl.ds(off, t), LANES * hd:LANES * (hd + 1)], qts[hd],
                                 preferred_element_type=F32)
                    sts.append(jnp.where(mask, st, NEG) if masked else st)
                m_new = m
                for st in sts:
                    m_new = jnp.maximum(m_new, jnp.max(st, axis=0, keepdims=True))
                alpha = jnp.exp2(m - m_new)
                l = alpha * l
                acc = alpha * acc
                for g, st in enumerate(sts):
                    pt = jnp.exp2(st - m_new)
                    l = l + jnp.sum(pt, axis=0, keepdims=True)
                    acc = acc + jnp.dot(v_ref[0, kj0 + g], pt.astype(BF16), preferred_element_type=F32)
                out += [m_new, l, acc]
            return tuple(out)

        init = (jnp.full((1, t), NEG, F32), jnp.zeros((1, t), F32), jnp.zeros((LANES, t), F32)) * 2
        carry = _grouped_loop(qi, blocks, init)
        ma, la, acca, mb, lb, accb = blocks(qi, 1, carry, masked=True)
        row = lax.broadcasted_iota(jnp.int32, (LANES, 1), 0)
        o_ref[...] = jnp.where(row < HEAD_DIM, acca / la, accb / lb).astype(BF16)
        lse_ref[0, 0:1, :] = ma + jnp.log(la) * L2E
        lse_ref[0, 1:2, :] = mb + jnp.log(lb) * L2E

    return pl.pallas_call(
        body, name="attn_fwd_t", grid=(npair, nb),
        out_shape=[jax.ShapeDtypeStruct((npair * LANES, tp), BF16), jax.ShapeDtypeStruct((npair, 2, tp), F32)],
        in_specs=[pl.BlockSpec((2 * LANES, t), lambda h, i: (h, i)), pl.BlockSpec((tp, 2 * LANES), lambda h, i: (0, h)),
                  pl.BlockSpec((1, nb, LANES, t), lambda h, i: (h, 0, 0, 0))],
        out_specs=[pl.BlockSpec((LANES, t), lambda h, i: (h, i)), pl.BlockSpec((1, 2, t), lambda h, i: (h, 0, i))],
        compiler_params=_cparams(2),
    )(qpt, kp, vt4)


def attn_bwd_t(qpt, qp, kp, kpt4, v, do, dot_, lse, cqt):
    tp = kp.shape[0]
    t = ATT_TILE
    nb = tp // t
    n_heads = kpt4.shape[0]
    d = v.shape[1]

    def body(qt_ref, q_ref, k_ref, kt_ref, v_ref, do_ref, dot_ref, lse_ref, cq_ref, dq_ref, dk_ref, dv_ref,
             p_sc, dp_sc, dc_sc):
        hd = pl.program_id(0) % 2
        qi = pl.program_id(1)

        @pl.when(qi == 0)
        def _():
            dk_ref[...] = jnp.zeros_like(dk_ref)
            dc_sc[...] = jnp.zeros_like(dc_sc)

        @pl.when((qi == 0) & (hd == 0))
        def _():
            dv_ref[...] = jnp.zeros_like(dv_ref)

        row = lax.broadcasted_iota(jnp.int32, (LANES, 1), 0)
        lane = lax.broadcasted_iota(jnp.int32, (1, LANES), 1)
        dot_h = jnp.where(row // HEAD_DIM == hd, dot_ref[...], jnp.zeros_like(dot_ref))
        do_h = jnp.where(lane // HEAD_DIM == hd, do_ref[...], jnp.zeros_like(do_ref))
        rr = cq_ref[0] - lse_ref[0]
        r1, r2, r3 = _split3(jnp.where(hd == 0, rr[0:1, :], rr[1:2, :]))
        qt = qt_ref[...].astype(F32)
        qt = jnp.where(row == AUG, r1, jnp.where(row == AUG + 1, r2, jnp.where(row == AUG + 2, r3, qt))).astype(BF16)
        mask = _diag_mask(t)

        def pass1(kj0, n, delta, masked=False):
            for g in range(n):
                off = pl.multiple_of((kj0 + g) * t, t)
                st = jnp.dot(k_ref[pl.ds(off, t), :], qt, preferred_element_type=F32)
                if masked:
                    st = jnp.where(mask, st, NEG)
                pt = jnp.exp2(st)
                dpt = jnp.dot(v_ref[pl.ds(off, t), :], dot_h, preferred_element_type=F32)
                p_sc[kj0 + g] = pt
                dp_sc[kj0 + g] = dpt
                delta = delta + jnp.sum(pt * dpt, axis=0, keepdims=True)
            return delta

        delta = _grouped_loop(qi, pass1, jnp.zeros((1, t), F32))
        delta = pass1(qi, 1, delta, masked=True)

        def pass2(kj0, n, dq):
            for g in range(n):
                off = pl.multiple_of((kj0 + g) * t, t)
                pt = p_sc[kj0 + g]
                ds32 = pt * (dp_sc[kj0 + g] - delta)
                ds = ds32.astype(BF16)
                dc_sc[pl.ds(off, t), :] += ds32[:, 0:LANES] + ds32[:, LANES:2 * LANES]
                dk_ref[pl.ds(off, t), :] += jnp.dot(ds, q_ref[...], preferred_element_type=F32)
                dv_ref[pl.ds(off, t), :] += jnp.dot(pt.astype(BF16), do_h, preferred_element_type=F32)
                dq = dq + jnp.dot(kt_ref[0, kj0 + g], ds, preferred_element_type=F32)
            return dq

        dq_ref[...] = _grouped_loop(qi + 1, pass2, jnp.zeros((LANES, t), F32))

        @pl.when(qi == nb - 1)
        def _():
            dk_ref[:, AUG + 3:AUG + 4] = jnp.sum(dc_sc[...], axis=1, keepdims=True)

    once = dict(pipeline_mode=pl.Buffered(1))
    pair_rows = pl.BlockSpec((1, 2, t), lambda h, i: (h // 2, 0, i))
    return pl.pallas_call(
        body, name="attn_bwd_t", grid=(n_heads, nb),
        out_shape=[jax.ShapeDtypeStruct((n_heads * LANES, tp), F32), jax.ShapeDtypeStruct((tp, n_heads * LANES), F32),
                   jax.ShapeDtypeStruct((tp, d), F32)],
        in_specs=[pl.BlockSpec((LANES, t), lambda h, i: (h, i)), pl.BlockSpec((t, LANES), lambda h, i: (i, h)),
                  pl.BlockSpec((tp, LANES), lambda h, i: (0, h), **once),
                  pl.BlockSpec((1, nb, LANES, t), lambda h, i: (h, 0, 0, 0), **once),
                  pl.BlockSpec((tp, LANES), lambda h, i: (0, h // 2), **once),
                  pl.BlockSpec((t, LANES), lambda h, i: (i, h // 2)), pl.BlockSpec((LANES, t), lambda h, i: (h // 2, i)),
                  pair_rows, pair_rows],
        out_specs=[pl.BlockSpec((LANES, t), lambda h, i: (h, i)), pl.BlockSpec((tp, LANES), lambda h, i: (0, h)),
                   pl.BlockSpec((tp, LANES), lambda h, i: (0, h // 2))],
        scratch_shapes=[pltpu.VMEM((nb, t, t), F32), pltpu.VMEM((nb, t, t), F32), pltpu.VMEM((tp, LANES), F32)],
        compiler_params=_cparams(2),
    )(qpt, qp, kp, kpt4, v, do, dot_, lse, cqt)


def rev_cumsum_rows(x):
    r, tp = x.shape
    t = ATT_TILE
    nb = tp // t

    def body(x_ref, o_ref, carry):
        @pl.when(pl.program_id(0) == 0)
        def _():
            carry[...] = jnp.zeros_like(carry)

        xv = x_ref[...]
        r_ = lax.broadcasted_iota(jnp.int32, (t, t), 0)
        c_ = lax.broadcasted_iota(jnp.int32, (t, t), 1)
        o_ref[...] = jnp.dot(xv, (r_ >= c_).astype(F32), precision=HI, preferred_element_type=F32) + carry[:, 0:1]
        carry[...] += jnp.sum(xv, axis=1, keepdims=True)

    return pl.pallas_call(
        body, name="rev_cumsum_rows", grid=(nb,), out_shape=jax.ShapeDtypeStruct((r, tp), F32),
        in_specs=[pl.BlockSpec((r, t), lambda i: (0, nb - 1 - i))],
        out_specs=pl.BlockSpec((r, t), lambda i: (0, nb - 1 - i)),
        scratch_shapes=[pltpu.VMEM((r, LANES), F32)],
        compiler_params=_cparams(1),
    )(x)


def c_elem_bwd(p, b_f, qg, kg, dq, dk, dv, dlogf):
    tp, n_out = p.shape
    d = qg.shape[1]
    tm = ATT_TILE
    n = tp // tm
    seg, seg_t, fold = _head_masks(d)

    def body(p_ref, pf_ref, bf_ref, qg_ref, kg_ref, dq_ref, dk_ref, dv_ref, dlf_ref, seg_ref, segt_ref, fold_ref,
             dp_ref, sm_ref, accq, acck, accf):
        i = pl.program_id(0)

        @pl.when(i == 0)
        def _():
            accq[...] = jnp.zeros_like(accq)
            acck[...] = jnp.zeros_like(acck)
            accf[...] = jnp.zeros_like(accf)

        def norm_bwd(x, g, dy, acc):
            ms = _dot2(x * x, seg_ref[...]) * (1.0 / HEAD_DIM)
            r = _dot2(lax.rsqrt(ms + EPS), segt_ref[...])
            xhat = x * r
            acc[0:1, :] += jnp.sum(dy * xhat, axis=0, keepdims=True)
            gy = dy * g
            mean = _dot2(_dot2(gy * xhat, seg_ref[...]), segt_ref[...]) * (1.0 / HEAD_DIM)
            return r * (gy - xhat * mean)

        dp_ref[:, 0:d] = norm_bwd(p_ref[:, 0:d], qg_ref[...], dq_ref[...] * (HEAD_DIM ** -0.5), accq).astype(BF16)
        dp_ref[:, d:2 * d] = norm_bwd(p_ref[:, d:2 * d], kg_ref[...], dk_ref[...] * LN2, acck).astype(BF16)
        dp_ref[:, 2 * d:3 * d] = dv_ref[...].astype(BF16)
        df = dlf_ref[...] * _sigmoid(-(pf_ref[...] + bf_ref[...]))
        accf[0:1, :] += jnp.sum(df, axis=0, keepdims=True)
        dp_ref[:, 3 * d:] = df.astype(BF16)

        @pl.when(i == n - 1)
        def _():
            sm_ref[...] = jnp.zeros_like(sm_ref)
            sm_ref[0:1, :] = jnp.dot(accq[0:1, :], fold_ref[...].astype(F32), precision=HI, preferred_element_type=F32)
            sm_ref[1:2, :] = jnp.dot(acck[0:1, :], fold_ref[...].astype(F32), precision=HI, preferred_element_type=F32)
            sm_ref[2:3, :] = accf[0:1, :]

    vec = pl.BlockSpec((1, d), lambda i: (0, 0))
    rowd = pl.BlockSpec((tm, d), lambda i: (i, 0))
    rowl = pl.BlockSpec((tm, LANES), lambda i: (i, 0))
    return pl.pallas_call(
        body, name="c_elem_bwd", grid=(n,),
        out_shape=[jax.ShapeDtypeStruct((tp, n_out), BF16), jax.ShapeDtypeStruct((8, LANES), F32)],
        in_specs=[pl.BlockSpec((tm, 2 * d), lambda i: (i, 0)), pl.BlockSpec((tm, LANES), lambda i: (i, 3 * d // LANES)),
                  pl.BlockSpec((1, LANES), lambda i: (0, 0)), vec, vec, rowd, rowd, rowd, rowl,
                  pl.BlockSpec((d, LANES), lambda i: (0, 0)), pl.BlockSpec((LANES, d), lambda i: (0, 0)),
                  pl.BlockSpec((d, LANES), lambda i: (0, 0))],
        out_specs=[pl.BlockSpec((tm, n_out), lambda i: (i, 0)), pl.BlockSpec((8, LANES), lambda i: (0, 0))],
        scratch_shapes=[pltpu.VMEM((8, d), F32), pltpu.VMEM((8, d), F32), pltpu.VMEM((8, LANES), F32)],
        compiler_params=_cparams(1),
    )(p, p, b_f, qg, kg, dq, dk, dv, dlogf, seg, seg_t, fold)


def loss_head(h, tgt, t_real):
    tp, d = h.shape
    tm = _pick(tp, (768, 512, 256))

    def body(h_ref, t_ref, dh_ref, l_ref):
        i = pl.program_id(0)

        @pl.when(i == 0)
        def _():
            l_ref[...] = jnp.zeros_like(l_ref)

        row = i * tm + lax.broadcasted_iota(jnp.int32, (tm, 1), 0)
        valid = (row >= N_META) & (row < t_real)
        e = jnp.where(valid, h_ref[...] - t_ref[...], 0.0)
        dh_ref[...] = e * (1.0 / d)
        per_row = jnp.sum(e * e, axis=-1, keepdims=True) * (1.0 / d)
        l_ref[...] += 0.5 * jnp.sum(per_row, axis=0, keepdims=True)

    return pl.pallas_call(
        body, name="loss_head", grid=(tp // tm,),
        out_shape=[jax.ShapeDtypeStruct((tp, d), F32), jax.ShapeDtypeStruct((8, LANES), F32)],
        in_specs=[pl.BlockSpec((tm, d), lambda i: (i, 0)), pl.BlockSpec((tm, d), lambda i: (i, 0))],
        out_specs=[pl.BlockSpec((tm, d), lambda i: (i, 0)), pl.BlockSpec((8, LANES), lambda i: (0, 0))],
        compiler_params=_cparams(1),
    )(h, tgt)


def sum_devices(x):
    _, r, c = x.shape

    def body(x_ref, o_ref):
        acc = x_ref[0]
        for dev in range(1, N_DEV):
            acc = acc + x_ref[dev]
        o_ref[...] = acc

    return pl.pallas_call(
        body, name="sum_devices", out_shape=jax.ShapeDtypeStruct((r, c), F32),
        in_specs=[pl.BlockSpec(memory_space=pltpu.VMEM)], out_specs=pl.BlockSpec(memory_space=pltpu.VMEM),
    )(x)


def _adamw_math(w, g, m, v):
    m = ADAM_B1 * m + (1.0 - ADAM_B1) * g
    v = ADAM_B2 * v + (1.0 - ADAM_B2) * (g * g)
    m_hat = m / (1.0 - ADAM_B1 ** ADAM_STEP)
    v_hat = v / (1.0 - ADAM_B2 ** ADAM_STEP)
    delta = -ADAM_LR * (m_hat / (jnp.sqrt(v_hat) + ADAM_EPS) + ADAM_WD * w)
    return delta, m, v


def adamw_small(w, g, m, v):
    def body(w_ref, g_ref, m_ref, v_ref, d_ref, nm_ref, nv_ref):
        d_ref[...], nm_ref[...], nv_ref[...] = _adamw_math(w_ref[...], g_ref[...], m_ref[...], v_ref[...])

    vm = pl.BlockSpec(memory_space=pltpu.VMEM)
    return pl.pallas_call(
        body, name="adamw_small", out_shape=[jax.ShapeDtypeStruct(w.shape, F32)] * 3,
        in_specs=[vm] * 4, out_specs=[vm] * 3,
    )(w, g, m, v)


def adamw_reduce(w, m, v, parts):
    r, c = w.shape
    n_parts = len(parts)
    tr = _pick(r, (512, 384, 352, 256, 128, 8))

    def body(w_ref, m_ref, v_ref, *rest):
        p_refs, (g_ref, d_ref, nm_ref, nv_ref) = rest[:n_parts], rest[n_parts:]
        g = p_refs[0][...].astype(F32)
        for p_ref in p_refs[1:]:
            g = g + p_ref[...].astype(F32)
        g_ref[...] = g
        d_ref[...], nm_ref[...], nv_ref[...] = _adamw_math(w_ref[...], g, m_ref[...], v_ref[...])

    blk = pl.BlockSpec((tr, c), lambda i: (i, 0))
    return pl.pallas_call(
        body, name="adamw_reduce", grid=(r // tr,), out_shape=[jax.ShapeDtypeStruct((r, c), F32)] * 4,
        in_specs=[blk] * (3 + n_parts), out_specs=[blk] * 4, compiler_params=_cparams(1),
    )(w, m, v, *parts)


def _unshard(g, axis):
    g = jnp.moveaxis(g, 0, axis)
    return g.reshape(g.shape[:axis] + (g.shape[axis] * g.shape[axis + 1],) + g.shape[axis + 2:])


def _shard(full, axis):
    s = full.shape
    g = full.reshape(s[:axis] + (N_DEV, s[axis] // N_DEV) + s[axis + 1:])
    return jnp.moveaxis(g, axis, 0)


def _pad_lanes(a, n=LANES):
    flat = a.reshape(-1)
    pad = (-flat.shape[0]) % n
    return jnp.pad(flat, (0, pad)).reshape(-1, n)


BIG = ("ffn_w_gate", "ffn_w_up", "ffn_w_down", "a_w_in", "a_w_out", "b_w_in", "b_w_out", "c_w_in", "c_w_out")
SHARD_AXIS = {"ffn_w_gate": 3, "ffn_w_up": 3, "ffn_w_down": 2, "a_w_in": 2, "a_w_out": 1, "b_w_in": 2,
              "b_w_out": 1, "c_w_in": 2, "c_w_out": 1, "meta": 1, "ffn_norm": 2, "a_conv": 2, "b_conv": 2}
SMALL_SHARDED = ("meta", "ffn_norm", "a_conv", "b_conv")
SMALL_REPL = ("mix_norm", "b_conv_bias", "b_ln_g", "b_ln_b", "c_b_f", "c_q_norm", "c_k_norm")
WEIGHTS = ("meta", "ffn_norm", "ffn_w_gate", "ffn_w_up", "ffn_w_down", "mix_norm", "a_w_in", "a_conv", "a_w_out",
           "b_w_in", "b_conv", "b_conv_bias", "b_ln_g", "b_ln_b", "b_w_out", "c_w_in", "c_b_f", "c_q_norm",
           "c_k_norm", "c_w_out")
N_MIXERS = 3


def kernel(x, meta, ffn_norm, ffn_w_gate, ffn_w_up, ffn_w_down, mix_norm, a_w_in, a_conv, a_w_out, b_w_in, b_conv, b_conv_bias, b_ln_g, b_ln_b, b_w_out, c_w_in, c_b_f, c_q_norm, c_k_norm, c_w_out, loss_target, m_meta, m_ffn_norm, m_ffn_w_gate, m_ffn_w_up, m_ffn_w_down, m_mix_norm, m_a_w_in, m_a_conv, m_a_w_out, m_b_w_in, m_b_conv, m_b_conv_bias, m_b_ln_g, m_b_ln_b, m_b_w_out, m_c_w_in, m_c_b_f, m_c_q_norm, m_c_k_norm, m_c_w_out, v_meta, v_ffn_norm, v_ffn_w_gate, v_ffn_w_up, v_ffn_w_down, v_mix_norm, v_a_w_in, v_a_conv, v_a_w_out, v_b_w_in, v_b_conv, v_b_conv_bias, v_b_ln_g, v_b_ln_b, v_b_w_out, v_c_w_in, v_c_b_f, v_c_q_norm, v_c_k_norm, v_c_w_out):
    local = dict(locals())
    w = {n: local[n] for n in WEIGHTS}
    mom = {n: local["m_" + n] for n in WEIGHTS}
    var = {n: local["v_" + n] for n in WEIGHTS}
    d = x.shape[-1]
    depth = ffn_norm.shape[0]
    n_heads = d // HEAD_DIM
    seq = x.shape[1]
    t_real = N_META + seq
    tp = -(-t_real // ROW_ALIGN) * ROW_ALIGN
    nb = tp // ATT_TILE
    npair = d // LANES

    names = BIG + SMALL_SHARDED
    gathered = all_gather([w[n].astype(BF16) for n in BIG] + [w[n] for n in SMALL_SHARDED])
    full = {n: _unshard(g, SHARD_AXIS[n]) for n, g in zip(names, gathered)}
    n_c = full["c_w_in"].shape[-1]
    n_cp = 3 * d + LANES
    c_w_qkv = jnp.pad(full["c_w_in"], ((0, 0), (0, 0), (0, n_cp - n_c)))
    b_f_pad = jnp.pad(c_b_f, ((0, 0), (0, LANES - n_heads)))
    qg_t = jnp.tile(c_q_norm, (1, n_heads))
    kg_t = jnp.tile(c_k_norm, (1, n_heads))

    h = jnp.concatenate([full["meta"], x[0], jnp.zeros((tp - t_real, d), F32)], axis=0)
    saved = []
    for i in range(depth):
        mixer, j = i % N_MIXERS, i // N_MIXERS
        s = {"h0": h}
        h, s["gate_a"], s["up_a"] = ffn_fwd(h, full["ffn_norm"][i, 0:1], full["ffn_w_gate"][i, 0],
                                            full["ffn_w_up"][i, 0], full["ffn_w_down"][i, 0])
        s["h1"] = h
        g_mix = mix_norm[i:i + 1]
        if mixer == 0:
            s["p"], s["u"] = norm_matmul(h, g_mix, full["a_w_in"][j], BF16, True)
            s["y"] = a_elem_fwd(s["p"], full["a_conv"][j])
            h = res_matmul(h, s["y"], full["a_w_out"][j])
        elif mixer == 1:
            s["p"], s["u"] = norm_matmul(h, g_mix, full["b_w_in"][j], BF16, True)
            s["y"] = b_elem_fwd(s["p"], full["b_conv"][j], b_conv_bias[j:j + 1], b_ln_g[j:j + 1], b_ln_b[j:j + 1])
            h = res_matmul(h, s["y"], full["b_w_out"][j])
        else:
            s["p"], s["u"] = norm_matmul(h, g_mix, c_w_qkv[j], F32, True)
            s["qp"], s["kp"], s["v"], cum = c_prep_slots(s["p"], b_f_pad[j:j + 1], qg_t[j:j + 1], kg_t[j:j + 1])
            s["cqt"] = (cum[:, :n_heads].T * L2E).reshape(npair, 2, tp)
            s["qpt"] = s["qp"].T
            s["kpt4"] = s["kp"].T.reshape(n_heads, LANES, nb, ATT_TILE).transpose(0, 2, 1, 3)
            vt4 = s["v"].T.reshape(npair, LANES, nb, ATT_TILE).transpose(0, 2, 1, 3)
            ot, s["lse"] = attn_fwd_t(s["qpt"], s["kp"], vt4)
            s["y"] = ot.T
            h = res_matmul(h, s["y"], full["c_w_out"][j])
        s["h2"] = h
        h, s["gate_b"], s["up_b"] = ffn_fwd(h, full["ffn_norm"][i, 1:2], full["ffn_w_gate"][i, 1],
                                            full["ffn_w_up"][i, 1], full["ffn_w_down"][i, 1])
        saved.append(s)

    tgt = jnp.concatenate([jnp.zeros((N_META, d), F32), loss_target[0], jnp.zeros((tp - t_real, d), F32)], axis=0)
    dh, loss_part = loss_head(h, tgt, t_real)

    gfull = {n: [None] * full[n].shape[0] for n in ("a_w_in", "a_w_out", "b_w_in", "b_w_out", "c_w_in", "c_w_out")}
    gffn = {n: [[None, None] for _ in range(depth)] for n in ("ffn_w_gate", "ffn_w_up", "ffn_w_down")}
    g_ffn_norm = [[None, None] for _ in range(depth)]
    g_mix_norm = [None] * depth
    g_a_conv = [None] * a_conv.shape[0]
    g_b_conv = [None] * b_conv.shape[0]
    g_b_stats = [None] * b_conv.shape[0]
    g_c_small = [None] * c_b_f.shape[0]

    def ffn_backward(dh_out, i, half, h_in, gate, up):
        dh_in, hn, dgate, dup, act, dg = ffn_bwd(dh_out, h_in, full["ffn_norm"][i, half:half + 1], gate, up,
                                                 full["ffn_w_gate"][i, half], full["ffn_w_up"][i, half],
                                                 full["ffn_w_down"][i, half])
        gffn["ffn_w_gate"][i][half] = atb(hn, dgate)
        gffn["ffn_w_up"][i][half] = atb(hn, dup)
        gffn["ffn_w_down"][i][half] = atb(act, dh_out, 0.5)
        g_ffn_norm[i][half] = dg
        return dh_in

    for i in reversed(range(depth)):
        mixer, j = i % N_MIXERS, i // N_MIXERS
        s = saved[i]
        g_mix = mix_norm[i:i + 1]
        dh = ffn_backward(dh, i, 1, s["h2"], s["gate_b"], s["up_b"])
        if mixer == 0:
            dy = matmul_nt(dh, full["a_w_out"][j])
            dp, dwc = a_elem_bwd(s["p"], dy, full["a_conv"][j])
            g_a_conv[j] = dwc[:a_conv.shape[1]]
            gfull["a_w_out"][j] = atb(s["y"], dh)
            gfull["a_w_in"][j] = atb(s["u"], dp)
            dh, g_mix_norm[i] = matmul_nt_rms_bwd(dp, full["a_w_in"][j], s["h1"], g_mix, dh)
        elif mixer == 1:
            dy = matmul_nt(dh, full["b_w_out"][j])
            du2, g_b_stats[j] = b_elem_bwd1(s["p"], dy, full["b_conv"][j], b_conv_bias[j:j + 1],
                                            b_ln_g[j:j + 1], b_ln_b[j:j + 1])
            dp, dwc = b_elem_bwd2(s["p"], du2, full["b_conv"][j])
            g_b_conv[j] = dwc[:b_conv.shape[1]]
            gfull["b_w_out"][j] = atb(s["y"], dh)
            gfull["b_w_in"][j] = atb(s["u"], dp)
            dh, g_mix_norm[i] = matmul_nt_rms_bwd(dp, full["b_w_in"][j], s["h1"], g_mix, dh)
        else:
            do = matmul_nt(dh, full["c_w_out"][j])
            dqt, dkp, dv = attn_bwd_t(s["qpt"], s["qp"], s["kp"], s["kpt4"], s["v"], do, do.T, s["lse"], s["cqt"])
            dq = dqt.reshape(n_heads, LANES, tp)[:, :HEAD_DIM].reshape(d, tp).T
            dkp = dkp.reshape(tp, n_heads, LANES)
            dk = dkp[:, :, :HEAD_DIM].reshape(tp, d)
            dlogf = rev_cumsum_rows(-dkp[:, :, AUG + 3].T)
            dlogf = jnp.pad(dlogf.T, ((0, 0), (0, LANES - n_heads)))
            dp, g_c_small[j] = c_elem_bwd(s["p"], b_f_pad[j:j + 1], qg_t[j:j + 1], kg_t[j:j + 1], dq, dk, dv, dlogf)
            gfull["c_w_out"][j] = atb(s["y"], dh)
            gfull["c_w_in"][j] = atb(s["u"], dp)[:, :n_c]
            dh, g_mix_norm[i] = matmul_nt_rms_bwd(dp, c_w_qkv[j], s["h1"], g_mix, dh)
        dh = ffn_backward(dh, i, 0, s["h0"], s["gate_a"], s["up_a"])

    grad_x = dh[N_META:t_real][None]

    gbig = {n: jnp.stack([jnp.stack(r) for r in gffn[n]]) for n in gffn}
    gbig.update({n: jnp.stack(gfull[n]) for n in gfull})
    def by_core(n):
        g = _shard(gbig[n], SHARD_AXIS[n])
        return jnp.moveaxis(g.reshape((2, 2, 2) + g.shape[1:]), 2, 0).reshape((2, N_CHIP) + g.shape[1:])

    halves = [by_core(n) for n in BIG]
    chip_sums = []
    for hv, th in zip(halves, swap_with_sibling(halves)):
        cols = th.shape[-1]
        chip_sums.append(add_own_half(hv.reshape(2, -1, cols), th.reshape(-1, cols)).reshape(th.shape))
    near = exchange_near(chip_sums)
    direct, relay = near[:len(BIG)], near[len(BIG):]
    merged, own = [], []
    for a, (cs, rl) in enumerate(zip(chip_sums, relay)):
        cols = cs.shape[-1]
        mg, ow = merge_far(cs.reshape(N_CHIP, -1, cols), rl.reshape(-1, cols), a)
        merged.append(mg)
        own.append(ow)
    far = exchange_far(merged)
    out_g, out_d, out_m, out_v = {}, {}, {}, {}
    for a, n in enumerate(BIG):
        shp = w[n].shape
        flat = lambda a_: a_.reshape(-1, shp[-1])
        res = adamw_reduce(flat(w[n]), flat(mom[n]), flat(var[n]), [own[a], flat(direct[a]), far[a]])
        out_g[n], out_d[n], out_m[n], out_v[n] = [r.reshape(shp) for r in res]

    def rows(a):
        return a.reshape(-1, d)

    def lane_rows(a):
        return jnp.pad(a.reshape(1, -1), ((0, 0), (0, d - a.size)))

    c_small = jnp.stack(g_c_small)
    pieces = [("meta", dh[:N_META]),
              ("ffn_norm", rows(jnp.stack([jnp.stack(r) for r in g_ffn_norm]))),
              ("mix_norm", rows(jnp.stack(g_mix_norm))),
              ("a_conv", rows(jnp.stack(g_a_conv))),
              ("b_conv", rows(jnp.stack(g_b_conv))),
              ("b_ln_g", rows(jnp.stack([st[0] for st in g_b_stats]))),
              ("b_ln_b", rows(jnp.stack([st[1] for st in g_b_stats]))),
              ("b_conv_bias", rows(jnp.stack([st[2] for st in g_b_stats]))),
              ("c_q_norm", lane_rows(c_small[:, 0, :HEAD_DIM])),
              ("c_k_norm", lane_rows(c_small[:, 1, :HEAD_DIM])),
              ("c_b_f", lane_rows(c_small[:, 2, :n_heads])),
              ("loss", lane_rows(loss_part[0:1, 0:1]))]
    packed = jnp.concatenate([pc for _, pc in pieces], axis=0)
    n_rows = packed.shape[0]
    packed = jnp.pad(packed, ((0, (-n_rows) % 8), (0, 0)))
    total = sum_devices(all_gather([packed])[0])
    small_g, r0 = {}, 0
    for n, pc in pieces:
        small_g[n] = total[r0:r0 + pc.shape[0]]
        r0 += pc.shape[0]
    loss = small_g.pop("loss")[0, 0]
    me = 4 * lax.axis_index("x") + 2 * lax.axis_index("y") + lax.axis_index("c")
    for n in SMALL_SHARDED:
        cols = w[n].shape[-1]
        g = lax.dynamic_slice_in_dim(small_g[n], me * cols, cols, axis=1)
        out_g[n] = g.reshape(w[n].shape)
    for n in SMALL_REPL:
        out_g[n] = small_g[n].reshape(-1)[:w[n].size].reshape(w[n].shape)
    small = SMALL_SHARDED + SMALL_REPL
    pack = lambda dct: jnp.concatenate([_pad_lanes(dct[n]) for n in small], axis=0)
    res = adamw_small(pack(w), pack(out_g), pack(mom), pack(var))
    r0 = 0
    for n in small:
        nr = -(-w[n].size // LANES)
        for dct, arr in zip((out_d, out_m, out_v), res):
            dct[n] = arr[r0:r0 + nr].reshape(-1)[:w[n].size].reshape(w[n].shape)
        r0 += nr

    return (loss, grad_x, *[out_g[n] for n in WEIGHTS], *[out_d[n] for n in WEIGHTS],
            *[out_m[n] for n in WEIGHTS], *[out_v[n] for n in WEIGHTS])
```
